```python
import math
import jax, jax.numpy as jnp
from jax import lax
import numpy as np

D_MODEL = 1024
BATCH = 16
SEQ = 256
DEPTH = 1
DEC_BATCH = 4
DEC_SEQ = 2048
PAST_LEN = 256

GRID_W = 64
HD_A = 64
DV_A = 2 * HD_A
H_A = D_MODEL // DV_A
W_A = H_A * DV_A
N_B = 64
H_B = D_MODEL // N_B
W_B = H_B * N_B
LORA_W = 64
LORA_A = 64
LORA_G = 128
N_DIR = 2
N_BRANCH = 2
N_EXPERTS = 32
TOP_K = 4
D_EXPERT = D_MODEL
SWIGLU_LIMIT = 7.0
SWIGLU_ALPHA = 1.702
ROPE_THETA = 10000.0
ROPE_AXIS = HD_A // 2
ROPE_FREQS = HD_A // 4
Q_BLOCK = 128
NORM_EPS = 1e-6
GN_EPS = 64e-5
L2_EPS = 1e-12
ATTN_SCALE = HD_A ** -0.5
N_MOD = 6
C_QA = H_A * 2 * HD_A
C_KA = H_A * 2 * HD_A
C_VA = W_A
C_RWKV = 3 * W_B + LORA_W + LORA_A + LORA_G
C_GATE = N_BRANCH * D_MODEL
C_IN = C_QA + C_KA + C_VA + C_RWKV + C_GATE

kernel_name = 'hybrid_diffattn_rwkv7_moe_dit_step'


def rms_norm(x, g):
    xf = x.astype(jnp.float32)
    y = xf * lax.rsqrt(jnp.mean(xf * xf, axis=-1, keepdims=True) + NORM_EPS)
    return (y * g.astype(jnp.float32)).astype(x.dtype)


def modulation(cond, w_ada, b_ada):
    m = jax.nn.silu(cond) @ w_ada + b_ada
    return [t[:, None, :] for t in jnp.split(m, N_MOD, axis=-1)]


def token_shift(y, mu_prev, mu_next):
    zeros = jnp.zeros_like(y[:, :1])
    prev = jnp.concatenate([zeros, y[:, :-1]], axis=1)
    nxt = jnp.concatenate([y[:, 1:], zeros], axis=1)
    return y + mu_prev * (prev - y) + mu_next * (nxt - y)


def axial_rope_tables(n_tokens):
    rows = n_tokens // GRID_W
    row = jnp.repeat(jnp.arange(rows, dtype=jnp.float32), GRID_W)
    col = jnp.tile(jnp.arange(GRID_W, dtype=jnp.float32), rows)
    inv = ROPE_THETA ** (-jnp.arange(ROPE_FREQS, dtype=jnp.float32) / ROPE_FREQS)
    ang_r = row[:, None] * inv[None, :]
    ang_c = col[:, None] * inv[None, :]
    ang = jnp.concatenate([ang_r, ang_r, ang_c, ang_c], axis=-1)
    return jnp.cos(ang), jnp.sin(ang)


def apply_rope(x, cos, sin):
    xf = x.astype(jnp.float32)
    a, b = xf[..., :ROPE_AXIS], xf[..., ROPE_AXIS:]
    rot = jnp.concatenate([-a[..., ROPE_FREQS:], a[..., :ROPE_FREQS],
                           -b[..., ROPE_FREQS:], b[..., :ROPE_FREQS]], axis=-1)
    c = cos[None, :, None, None, :]
    s = sin[None, :, None, None, :]
    return (xf * c + rot * s).astype(x.dtype)


def diff_attend(q, k, v, lam):
    B, Sq = q.shape[0], q.shape[1]
    nblk = Sq // Q_BLOCK
    kf = k.astype(jnp.float32)
    vf = v.astype(jnp.float32)
    qb = jnp.moveaxis(q.reshape(B, nblk, Q_BLOCK, H_A, 2, HD_A), 1, 0)

    def block(qi):
        s = jnp.einsum('bqhmd,bkhmd->bhmqk', qi.astype(jnp.float32), kf) * ATTN_SCALE
        p = jax.nn.softmax(s, axis=-1)
        pd = p[:, :, 0] - lam * p[:, :, 1]
        return jnp.einsum('bhqk,bkhd->bqhd', pd, vf)

    o = lax.map(block, qb)
    return jnp.moveaxis(o, 0, 1).reshape(B, Sq, H_A, DV_A)


def rwkv_scan(s0, r, w, k, v, kk, a, reverse):
    xs = tuple(jnp.moveaxis(t, 1, 0) for t in (r, w, k, v, kk, a))

    def step(S, inp):
        r_t, w_t, k_t, v_t, kk_t, a_t = inp
        sa = jnp.einsum('bhvk,bhk->bhv', S, -kk_t)
        S = (S * w_t[:, :, None, :] + sa[..., None] * (kk_t * a_t)[:, :, None, :]
             + v_t[..., None] * k_t[:, :, None, :])
        return S, jnp.einsum('bhvk,bhk->bhv', S, r_t)

    s_fin, ys = lax.scan(step, s0.astype(jnp.float32), xs, reverse=reverse)
    return s_fin, jnp.moveaxis(ys, 0, 1)


def rwkv_branch(cols, s0, lp):
    B, S = cols.shape[0], cols.shape[1]
    cols = token_shift(cols, lp['mu_prev'], lp['mu_next'])
    r, k, v, xw, xa, xg = jnp.split(
        cols, [W_B, 2 * W_B, 3 * W_B, 3 * W_B + LORA_W, 3 * W_B + LORA_W + LORA_A], axis=-1)

    def heads(t):
        return t.astype(jnp.float32).reshape(B, S, H_B, N_B)

    rh, kh, vh = heads(r), heads(k), heads(v)
    kk = heads(k * lp['k_k'])
    kk = kk / jnp.maximum(jnp.sqrt(jnp.sum(kk * kk, axis=-1, keepdims=True)), L2_EPS)
    ka = lp['k_a'].astype(jnp.float32).reshape(H_B, N_B)
    rk = lp['r_k'].astype(jnp.float32)
    gate_out = (jax.nn.sigmoid(xg) @ lp['g_up']).astype(jnp.float32)
    ys, bonuses, finals = [], [], []
    for d in range(N_DIR):
        wlog = lp['w0'][d] + jnp.tanh(xw) @ lp['w_up'][d]
        decay = jnp.exp(-math.exp(-0.5) * jax.nn.sigmoid(heads(wlog)))
        iclr = jax.nn.sigmoid(heads(lp['a0'][d] + xa @ lp['a_up'][d]))
        kd = kh * (1.0 + (iclr - 1.0) * ka)
        s_fin, yd = rwkv_scan(s0[:, d], rh, decay, kd, vh, kk, iclr, reverse=(d == 1))
        ys.append(yd)
        bonuses.append(jnp.sum(rh * kd * rk, axis=-1, keepdims=True) * vh)
        finals.append(s_fin)
    y = ys[0] + ys[1]
    mu = jnp.mean(y, axis=-1, keepdims=True)
    var = jnp.mean(jnp.square(y - mu), axis=-1, keepdims=True)
    yn = ((y - mu) * lax.rsqrt(var + GN_EPS)).reshape(B, S, W_B)
    yn = yn * lp['gn_w'].astype(jnp.float32) + lp['gn_b'].astype(jnp.float32)
    out = (yn + (bonuses[0] + bonuses[1]).reshape(B, S, W_B)) * gate_out
    return out, jnp.stack(finals, axis=1)


def mixer(h, rope_tab, ctx_k, ctx_v, s0, lam_init, lp):
    B, S = h.shape[0], h.shape[1]
    proj = h @ lp['w_in']
    qa, ka, va, cols, gl = jnp.split(
        proj, [C_QA, C_QA + C_KA, C_QA + C_KA + C_VA, C_QA + C_KA + C_VA + C_RWKV], axis=-1)
    q = qa.reshape(B, S, H_A, 2, HD_A)
    k = ka.reshape(B, S, H_A, 2, HD_A)
    v = va.reshape(B, S, H_A, DV_A)
    k_new, v_new = k, v
    if rope_tab is not None:
        q = apply_rope(q, rope_tab[0], rope_tab[1])
        k = apply_rope(k, rope_tab[0], rope_tab[1])
    if ctx_k is not None:
        k = jnp.concatenate([k, ctx_k.astype(k.dtype)], axis=1)
        v = jnp.concatenate([v, ctx_v.astype(v.dtype)], axis=1)
    lq = lp['lam'].astype(jnp.float32)
    lam = jnp.exp(jnp.sum(lq[0] * lq[1])) - jnp.exp(jnp.sum(lq[2] * lq[3])) + lam_init
    oa = diff_attend(q, k, v, lam)
    oa = oa * lax.rsqrt(jnp.mean(oa * oa, axis=-1, keepdims=True) + NORM_EPS)
    oa = oa.reshape(B, S, W_A) * lp['g_subln'].astype(jnp.float32) * (1.0 - lam_init)
    if s0 is None:
        s0 = jnp.zeros((B, N_DIR, H_B, N_B, N_B), jnp.float32)
    ob, s_fin = rwkv_branch(cols, s0, lp)
    g = jax.nn.sigmoid(gl.astype(jnp.float32))
    merged = g[..., :D_MODEL] * oa + g[..., D_MODEL:] * ob
    out = merged.astype(h.dtype) @ lp['w_out']
    return out, k_new, v_new, s_fin


def moe(tokens, lp):
    logits = (tokens @ lp['w_router'] + lp['b_router']).astype(jnp.float32)
    top_v, top_i = lax.top_k(logits, TOP_K)
    wts = jax.nn.softmax(top_v, axis=-1)
    comb = jnp.sum(jax.nn.one_hot(top_i, N_EXPERTS, dtype=jnp.float32) * wts[..., None], axis=1)
    out = jnp.zeros(tokens.shape, jnp.float32)
    for e in range(N_EXPERTS):
        gu = tokens @ lp['w_gate_up'][e] + lp['b_gate_up'][e]
        gate = jnp.minimum(gu[:, :D_EXPERT], SWIGLU_LIMIT)
        up = jnp.clip(gu[:, D_EXPERT:], -SWIGLU_LIMIT, SWIGLU_LIMIT)
        act = (up + 1.0) * gate * jax.nn.sigmoid(SWIGLU_ALPHA * gate)
        y = act @ lp['w_down'][e] + lp['b_down'][e]
        out = out + comb[:, e:e + 1] * y.astype(jnp.float32)
    return out.astype(tokens.dtype)


def setup_inputs(seed: int = 0) -> dict:
    key = jax.random.key(seed)
    ks = iter(jax.random.split(key, 48))
    L = DEPTH

    def nrm(shape, scale=1.0):
        return scale * jax.random.normal(next(ks), shape, jnp.float32)

    def gain(shape):
        return 1.0 + 0.05 * nrm(shape)

    def unif(shape, lo, hi):
        return jax.random.uniform(next(ks), shape, jnp.float32, lo, hi)

    inp = {}
    inp['x_prompt'] = nrm((BATCH, SEQ, D_MODEL))
    inp['x_sample'] = nrm((DEC_BATCH, DEC_SEQ, D_MODEL))
    inp['cache_k'] = nrm((DEC_BATCH, L, PAST_LEN, H_A, 2, HD_A))
    inp['cache_v'] = nrm((DEC_BATCH, L, PAST_LEN, H_A, DV_A))
    inp['state_rwkv'] = nrm((DEC_BATCH, L, N_DIR, H_B, N_B, N_B), 0.5)
    inp['c'] = nrm((DEC_BATCH, D_MODEL))
    inp['c_ctx'] = nrm((D_MODEL,))
    inp['w_ada'] = nrm((L, D_MODEL, N_MOD * D_MODEL), D_MODEL ** -0.5)
    inp['b_ada'] = nrm((L, N_MOD * D_MODEL), 0.02)
    inp['g_pre_mix'] = gain((L, D_MODEL))
    inp['g_post_mix'] = gain((L, D_MODEL))
    inp['g_pre_ffn'] = gain((L, D_MODEL))
    inp['g_post_ffn'] = gain((L, D_MODEL))
    inp['w_in'] = nrm((L, D_MODEL, C_IN), D_MODEL ** -0.5)
    inp['mu_prev'] = unif((L, C_RWKV), 0.0, 0.5)
    inp['mu_next'] = unif((L, C_RWKV), 0.0, 0.5)
    inp['lam'] = nrm((L, 4, HD_A), 0.1)
    inp['g_subln'] = gain((L, W_A))
    inp['k_k'] = 0.85 + 0.05 * nrm((L, W_B))
    inp['k_a'] = gain((L, W_B))
    inp['r_k'] = nrm((L, H_B, N_B), 0.1)
    inp['w0'] = 0.5 + 0.5 * nrm((L, N_DIR, W_B))
    inp['w_up'] = nrm((L, N_DIR, LORA_W, W_B), 0.5 * LORA_W ** -0.5)
    inp['a0'] = nrm((L, N_DIR, W_B), 0.5)
    inp['a_up'] = nrm((L, N_DIR, LORA_A, W_B), 0.5 * LORA_A ** -0.5)
    inp['g_up'] = nrm((L, LORA_G, W_B), LORA_G ** -0.5)
    inp['gn_w'] = gain((L, W_B))
    inp['gn_b'] = nrm((L, W_B), 0.02)
    inp['w_out'] = nrm((L, D_MODEL, D_MODEL), D_MODEL ** -0.5)
    inp['w_router'] = nrm((L, D_MODEL, N_EXPERTS), D_MODEL ** -0.5)
    inp['b_router'] = nrm((L, N_EXPERTS), 0.01)
    inp['w_gate_up'] = nrm((L, N_EXPERTS, D_MODEL, 2 * D_EXPERT), D_MODEL ** -0.5)
    inp['b_gate_up'] = nrm((L, N_EXPERTS, 2 * D_EXPERT), 0.02)
    inp['w_down'] = nrm((L, N_EXPERTS, D_EXPERT, D_MODEL), D_EXPERT ** -0.5)
    inp['b_down'] = nrm((L, N_EXPERTS, D_MODEL), 0.02)
    return inp


def reference(x_prompt, x_sample, cache_k, cache_v, state_rwkv, c, c_ctx,
              w_ada, b_ada, g_pre_mix, g_post_mix, g_pre_ffn, g_post_ffn,
              w_in, mu_prev, mu_next, lam, g_subln, k_k, k_a, r_k,
              w0, w_up, a0, a_up, g_up, gn_w, gn_b, w_out,
              w_router, b_router, w_gate_up, b_gate_up, w_down, b_down):
    rope_tab = axial_rope_tables(x_sample.shape[1])
    xp, xs = x_prompt, x_sample
    n_prompt = xp.shape[0] * xp.shape[1]
    new_k, new_v, new_s = [], [], []
    for l in range(DEPTH):
        lp = {'w_in': w_in[l], 'mu_prev': mu_prev[l], 'mu_next': mu_next[l], 'lam': lam[l],
              'g_subln': g_subln[l], 'k_k': k_k[l], 'k_a': k_a[l], 'r_k': r_k[l],
              'w0': w0[l], 'w_up': w_up[l], 'a0': a0[l], 'a_up': a_up[l], 'g_up': g_up[l],
              'gn_w': gn_w[l], 'gn_b': gn_b[l], 'w_out': w_out[l],
              'w_router': w_router[l], 'b_router': b_router[l],
              'w_gate_up': w_gate_up[l], 'b_gate_up': b_gate_up[l],
              'w_down': w_down[l], 'b_down': b_down[l]}
        lam_init = 0.8 - 0.6 * math.exp(-0.3 * l)
        mp = modulation(c_ctx[None, :], w_ada[l], b_ada[l])
        ms = modulation(c, w_ada[l], b_ada[l])
        hp = rms_norm(xp, g_pre_mix[l]) * (1.0 + mp[1]) + mp[0]
        hs = rms_norm(xs, g_pre_mix[l]) * (1.0 + ms[1]) + ms[0]
        op, kp, vp, sp = mixer(hp, None, None, None, None, lam_init, lp)
        os_, _, _, _ = mixer(hs, rope_tab, cache_k[:, l], cache_v[:, l], state_rwkv[:, l], lam_init, lp)
        xp = xp + mp[2] * rms_norm(op, g_post_mix[l])
        xs = xs + ms[2] * rms_norm(os_, g_post_mix[l])
        hp2 = rms_norm(xp, g_pre_ffn[l]) * (1.0 + mp[4]) + mp[3]
        hs2 = rms_norm(xs, g_pre_ffn[l]) * (1.0 + ms[4]) + ms[3]
        f = moe(jnp.concatenate([hp2.reshape(-1, D_MODEL), hs2.reshape(-1, D_MODEL)], axis=0), lp)
        fp = f[:n_prompt].reshape(xp.shape)
        fs = f[n_prompt:].reshape(xs.shape)
        xp = xp + mp[5] * rms_norm(fp, g_post_ffn[l])
        xs = xs + ms[5] * rms_norm(fs, g_post_ffn[l])
        new_k.append(kp)
        new_v.append(vp)
        new_s.append(sp.astype(x_prompt.dtype))
    new_cache_k = jnp.stack(new_k, axis=1)
    new_cache_v = jnp.stack(new_v, axis=1)
    new_state_rwkv = jnp.stack(new_s, axis=1)
    return (xp, xs, new_cache_k, new_cache_v, new_state_rwkv)
```

```python
import functools
import math

import jax
import jax.numpy as jnp
from jax import lax
from jax.experimental import pallas as pl
from jax.experimental.pallas import tpu as pltpu

F32 = jnp.float32
BF16 = jnp.bfloat16

D_MODEL = 1024
GRID_W = 64
HD_A = 64
DV_A = 2 * HD_A
H_A = D_MODEL // DV_A
N_B = 64
H_B = D_MODEL // N_B
LORA_W = 64
LORA_A = 64
LORA_G = 128
N_EXPERTS = 32
TOP_K = 4
SWIGLU_LIMIT = 7.0
SWIGLU_ALPHA = 1.702
ROPE_THETA = 10000.0
ROPE_FREQS = HD_A // 4
NORM_EPS = 1e-6
GN_EPS = 64e-5
L2_EPS = 1e-12
ATTN_SCALE = HD_A ** -0.5
N_MOD = 6
LAM_INIT = 0.8 - 0.6 * math.exp(-0.3 * 0)
DECAY_SCALE = -math.exp(-0.5)

LANES = 128
SUBLANES = 8
CHUNK = 64
VMEM_LIMIT = 56 * 1024 * 1024

COL_Q, COL_K, COL_V, COL_G0, COL_G1, COL_R, COL_KR, COL_VR, COL_LORA = (
    0, 1024, 2048, 3072, 4096, 5120, 6144, 7168, 8192)
C_LORA = LORA_W + LORA_A + LORA_G
C_IN = COL_LORA + C_LORA


def _sigmoid(x):
    return 1.0 / (1.0 + jnp.exp(-x))


def _dot(a, b):
    return jnp.dot(a.astype(BF16), b.astype(BF16), preferred_element_type=F32)


def _dot_nt(a, b):
    return lax.dot_general(a.astype(BF16), b.astype(BF16), (((1,), (1,)), ((), ())),
                           preferred_element_type=F32)


def _dot_tn(a, b):
    return lax.dot_general(a.astype(BF16), b.astype(BF16), (((0,), (0,)), ((), ())),
                           preferred_element_type=F32)


def _split3(x):
    hi = x.astype(BF16)
    r1 = x - hi.astype(F32)
    mid = r1.astype(BF16)
    lo = (r1 - mid.astype(F32)).astype(BF16)
    return hi, mid, lo


def _headsum(x, g):
    hi = x.astype(BF16)
    lo = (x - hi.astype(F32)).astype(BF16)
    return (jnp.dot(hi, g, preferred_element_type=F32)
            + jnp.dot(lo, g, preferred_element_type=F32))


def _rms(x, g):
    return x * lax.rsqrt(jnp.mean(x * x, axis=-1, keepdims=True) + NORM_EPS) * g


def _mod_kernel(c_ref, w_ref, b_ref, o_ref):
    c = c_ref[...]
    s = c * _sigmoid(c)
    o_ref[...] = _dot(s, w_ref[...]) + b_ref[...]


def _modulation(cond, w_ada, b_ada):
    rows, d = cond.shape
    n = w_ada.shape[1]
    tn = 768
    return pl.pallas_call(
        _mod_kernel,
        grid=(n // tn,),
        in_specs=[pl.BlockSpec((rows, d), lambda j: (0, 0)),
                  pl.BlockSpec((d, tn), lambda j: (0, j)),
                  pl.BlockSpec((1, tn), lambda j: (0, j))],
        out_specs=pl.BlockSpec((rows, tn), lambda j: (0, j)),
        out_shape=jax.ShapeDtypeStruct((rows, n), F32),
        name="modulation",
    )(cond, w_ada, b_ada)


def _proj_kernel(x_ref, g_ref, sc_ref, sh_ref, w_ref, o_ref, h_scr):
    @pl.when(pl.program_id(2) == 0)
    def _():
        h = _rms(x_ref[0], g_ref[...]) * (1.0 + sc_ref[0]) + sh_ref[0]
        h_scr[...] = h.astype(BF16)

    o_ref[0] = jnp.dot(h_scr[...], w_ref[...], preferred_element_type=F32)


def _norm_proj(x, g, scale, shift, w, tm, tn):
    b, s, d = x.shape
    n = w.shape[1]
    per_batch = scale.shape[0] > 1
    mod_map = (lambda bi, i, j: (bi, 0, 0)) if per_batch else (lambda bi, i, j: (0, 0, 0))
    return pl.pallas_call(
        _proj_kernel,
        grid=(b, s // tm, n // tn),
        in_specs=[pl.BlockSpec((1, tm, d), lambda bi, i, j: (bi, i, 0)),
                  pl.BlockSpec((1, d), lambda bi, i, j: (0, 0)),
                  pl.BlockSpec((1, 1, d), mod_map),
                  pl.BlockSpec((1, 1, d), mod_map),
                  pl.BlockSpec((d, tn), lambda bi, i, j: (0, j))],
        out_specs=pl.BlockSpec((1, tm, tn), lambda bi, i, j: (bi, i, j)),
        out_shape=jax.ShapeDtypeStruct((b, s, n), F32),
        scratch_shapes=[pltpu.VMEM((tm, d), BF16)],
        compiler_params=pltpu.CompilerParams(
            dimension_semantics=("arbitrary", "arbitrary", "arbitrary"),
            vmem_limit_bytes=VMEM_LIMIT),
        name="norm_proj",
    )(x, g, scale, shift, w)


def _rope(x, cos, sin):
    lane = lax.broadcasted_iota(jnp.int32, x.shape, 1)
    even = (lane // ROPE_FREQS) % 2 == 0
    rot = jnp.where(even, -pltpu.roll(x, LANES - ROPE_FREQS, 1), pltpu.roll(x, ROPE_FREQS, 1))
    return x * cos + rot * sin


def _attn_kernel(*refs, use_rope, use_ctx, s_new):
    it = iter(refs)
    lam_ref = next(it)
    q_ref, k_ref, v_ref = next(it), next(it), next(it)
    if use_rope:
        cq_ref, sq_ref, ck_ref, sk_ref = next(it), next(it), next(it), next(it)
    if use_ctx:
        ctxk_ref, ctxv_ref = next(it), next(it)
    g_ref = next(it)
    o_ref = next(it)
    k_scr, v_scr = next(it), next(it)

    @pl.when(pl.program_id(2) == 0)
    def _():
        k = k_ref[0]
        if use_rope:
            k = _rope(k, ck_ref[...], sk_ref[...])
        k_scr[0:s_new, :] = k.astype(BF16)
        v_scr[0:s_new, :] = v_ref[0].astype(BF16)
        if use_ctx:
            k_scr[s_new:, :] = ctxk_ref[0].astype(BF16)
            v_scr[s_new:, :] = ctxv_ref[0].astype(BF16)

    lam = lam_ref[0]
    q = q_ref[0]
    if use_rope:
        q = _rope(q, cq_ref[...], sq_ref[...])
    q = q * ATTN_SCALE
    lane = lax.broadcasted_iota(jnp.int32, q.shape, 1)
    kk = k_scr[...]
    probs = []
    for m in range(2):
        in_map = (lane >= HD_A) if m else (lane < HD_A)
        qm = jnp.where(in_map, q, 0.0)
        s = _dot_nt(qm, kk)
        e = jnp.exp(s - jnp.max(s, axis=-1, keepdims=True))
        probs.append((e, jnp.sum(e, axis=-1, keepdims=True)))
    (e0, l0), (e1, l1) = probs
    pd = e0 * (1.0 / l0) - e1 * (lam / l1)
    o = jnp.dot(pd.astype(BF16), v_scr[...], preferred_element_type=F32)
    o = o * lax.rsqrt(jnp.mean(o * o, axis=-1, keepdims=True) + NORM_EPS)
    o_ref[0] = o * g_ref[...] * (1.0 - LAM_INIT)


def _diff_attention(proj, lam, g_subln, rope=None, ctx=None, tq=256):
    b, s, _ = proj.shape
    use_rope, use_ctx = rope is not None, ctx is not None
    s_tot = s + (ctx[0].shape[1] if use_ctx else 0)
    qmap = lambda bi, h, i: (bi, i, COL_Q // LANES + h)
    in_specs = [pl.BlockSpec(memory_space=pltpu.SMEM),
                pl.BlockSpec((1, tq, DV_A), qmap),
                pl.BlockSpec((1, s, DV_A), lambda bi, h, i: (bi, 0, COL_K // LANES + h)),
                pl.BlockSpec((1, s, DV_A), lambda bi, h, i: (bi, 0, COL_V // LANES + h))]
    args = [lam, proj, proj, proj]
    if use_rope:
        cos, sin = rope
        in_specs += [pl.BlockSpec((tq, DV_A), lambda bi, h, i: (i, 0)),
                     pl.BlockSpec((tq, DV_A), lambda bi, h, i: (i, 0)),
                     pl.BlockSpec((s, DV_A), lambda bi, h, i: (0, 0)),
                     pl.BlockSpec((s, DV_A), lambda bi, h, i: (0, 0))]
        args += [cos, sin, cos, sin]
    if use_ctx:
        p = ctx[0].shape[1]
        in_specs += [pl.BlockSpec((1, p, DV_A), lambda bi, h, i: (bi, 0, h)),
                     pl.BlockSpec((1, p, DV_A), lambda bi, h, i: (bi, 0, h))]
        args += [ctx[0], ctx[1]]
    in_specs.append(pl.BlockSpec((1, DV_A), lambda bi, h, i: (0, h)))
    args.append(g_subln)
    return pl.pallas_call(
        functools.partial(_attn_kernel, use_rope=use_rope, use_ctx=use_ctx, s_new=s),
        grid=(b, H_A, s // tq),
        in_specs=in_specs,
        out_specs=pl.BlockSpec((1, tq, DV_A), lambda bi, h, i: (bi, i, h)),
        out_shape=jax.ShapeDtypeStruct((b, s, D_MODEL), F32),
        scratch_shapes=[pltpu.VMEM((s_tot, DV_A), BF16), pltpu.VMEM((s_tot, DV_A), BF16)],
        compiler_params=pltpu.CompilerParams(
            dimension_semantics=("arbitrary", "arbitrary", "arbitrary"),
            vmem_limit_bytes=VMEM_LIMIT),
        name="diff_attention",
    )(*args)


def _shifted(x_ref, p_ref, n_ref, mup, mun, first, last):
    x = x_ref[0]
    ts = x.shape[0]
    row = lax.broadcasted_iota(jnp.int32, x.shape, 0)
    prev_row = p_ref[0][SUBLANES - 1:SUBLANES, :] * first
    next_row = n_ref[0][0:1, :] * last
    prev = jnp.where(row == 0, prev_row, pltpu.roll(x, 1, 0))
    nxt = jnp.where(row == ts - 1, next_row, pltpu.roll(x, ts - 1, 0))
    return x + mup * (prev - x) + mun * (nxt - x)


def _prep_kernel(r_ref, rp_ref, rn_ref, k_ref, kp_ref, kn_ref, v_ref, vp_ref, vn_ref,
                 l_ref, lp_ref, ln_ref, mup_ref, mun_ref, mupl_ref, munl_ref,
                 kk_ref, ka_ref, rk_ref, w0_ref, wup_ref, a0_ref, aup_ref, gup_ref, g_ref,
                 ro_ref, vo_ref, kko_ref, kd_ref, ba_ref, lw_ref, bonus_ref, gate_ref):
    i = pl.program_id(1)
    first = (i > 0).astype(F32)
    last = (i < pl.num_programs(1) - 1).astype(F32)
    mup, mun = mup_ref[...], mun_ref[...]
    d = D_MODEL
    r = _shifted(r_ref, rp_ref, rn_ref, mup[:, 0:d], mun[:, 0:d], first, last)
    k = _shifted(k_ref, kp_ref, kn_ref, mup[:, d:2 * d], mun[:, d:2 * d], first, last)
    v = _shifted(v_ref, vp_ref, vn_ref, mup[:, 2 * d:3 * d], mun[:, 2 * d:3 * d], first, last)
    lo = _shifted(l_ref, lp_ref, ln_ref, mupl_ref[...], munl_ref[...], first, last)
    xw = lo[:, 0:LORA_W]
    xa = lo[:, LORA_W:LORA_W + LORA_A]
    xg = lo[:, LORA_W + LORA_A:]
    g = g_ref[...]

    kk = k * kk_ref[...]
    nrm = jnp.sqrt(_headsum(kk * kk, g))
    kkn = kk / jnp.maximum(nrm, L2_EPS)
    ro_ref[0] = r
    vo_ref[0] = v
    kko_ref[0] = kkn
    gate_ref[0] = _dot(_sigmoid(xg), gup_ref[...])

    wlog = w0_ref[...] + _dot(jnp.tanh(xw), wup_ref[...])
    alog = a0_ref[...] + _dot(xa, aup_ref[...])
    ka = ka_ref[...]
    rrk = r * rk_ref[...]
    dots = None
    for dr in range(2):
        sl = slice(dr * d, (dr + 1) * d)
        lw_ref[dr, 0] = DECAY_SCALE * _sigmoid(wlog[:, sl])
        a = _sigmoid(alog[:, sl])
        kd = k * (1.0 + (a - 1.0) * ka)
        kd_ref[dr, 0] = kd
        ba_ref[dr, 0] = kkn * a
        t = rrk * kd
        dots = t if dots is None else dots + t
    bonus_ref[0] = _headsum(dots, g) * v


def _rwkv_prep(proj, p, ts):
    b, s, _ = proj.shape
    d = D_MODEL
    nt = s // ts
    hb = ts // SUBLANES
    nhb = s // SUBLANES

    def main(col, w):
        return pl.BlockSpec((1, ts, w), lambda bi, i: (bi, i, col // w))

    def prev(col, w):
        return pl.BlockSpec((1, SUBLANES, w), lambda bi, i: (bi, jnp.maximum(i * hb - 1, 0), col // w))

    def nxt(col, w):
        return pl.BlockSpec((1, SUBLANES, w),
                            lambda bi, i: (bi, jnp.minimum((i + 1) * hb, nhb - 1), col // w))

    def full(a):
        return pl.BlockSpec(a.shape, lambda bi, i: (0,) * a.ndim)

    in_specs, args = [], []
    for col, w in ((COL_R, d), (COL_KR, d), (COL_VR, d), (COL_LORA, C_LORA)):
        in_specs += [main(col, w), prev(col, w), nxt(col, w)]
        args += [proj, proj, proj]
    consts = [p['mu_prev_main'], p['mu_next_main'], p['mu_prev_lora'], p['mu_next_lora'],
              p['k_k'], p['k_a'], p['r_k'], p['w0'], p['w_up'], p['a0'], p['a_up'], p['g_up'],
              p['head_ones']]
    in_specs += [full(a) for a in consts]
    args += consts
    tok = pl.BlockSpec((1, ts, d), lambda bi, i: (bi, i, 0))
    tok2 = pl.BlockSpec((2, 1, ts, d), lambda bi, i: (0, bi, i, 0))
    one = jax.ShapeDtypeStruct((b, s, d), F32)
    two = jax.ShapeDtypeStruct((2, b, s, d), F32)
    return pl.pallas_call(
        _prep_kernel,
        grid=(b, nt),
        in_specs=in_specs,
        out_specs=[tok, tok, tok, tok2, tok2, tok2, tok, tok],
        out_shape=[one, one, one, two, two, two, one, one],
        compiler_params=pltpu.CompilerParams(
            dimension_semantics=("arbitrary", "arbitrary"),
            vmem_limit_bytes=VMEM_LIMIT),
        name="rwkv_prep",
    )(*args)


def _scan_kernel(*refs, reverse, has_s0):
    it = iter(refs)
    r_ref, v_ref, kk_ref, kd_ref, ba_ref, lw_ref = (next(it) for _ in range(6))
    s0_ref = next(it) if has_s0 else None
    y_ref, sf_ref = next(it), next(it)
    s_scr = next(it)
    c = pl.program_id(1)

    @pl.when(c == 0)
    def _():
        if has_s0:
            s_scr[...] = s0_ref[0]
        else:
            s_scr[...] = jnp.zeros_like(s_scr)

    n = CHUNK
    tt = lax.broadcasted_iota(jnp.int32, (n, n), 0)
    ss = lax.broadcasted_iota(jnp.int32, (n, n), 1)
    hi, lo = (ss, tt) if reverse else (tt, ss)
    strict = hi > lo
    incl = hi >= lo
    eye = (tt == ss).astype(F32)

    lw = lw_ref[0, 0]
    tri = jnp.where(incl, 1.0, 0.0).astype(BF16)
    cum = sum(jnp.dot(tri, part, preferred_element_type=F32) for part in _split3(lw))
    tot = cum[0:1, :] if reverse else cum[n - 1:n, :]
    r, v, kkn = r_ref[0], v_ref[0], kk_ref[0]
    kd, ba = kd_ref[0, 0], ba_ref[0, 0]
    g_in = jnp.exp(cum)
    g_inv = jnp.exp(-cum)
    g_rest = jnp.exp(tot - cum)
    a_t = (kkn * jnp.exp(cum - lw)).astype(BF16)
    r_t = (r * g_in).astype(BF16)
    b_t = (ba * g_inv).astype(BF16)
    k_t = (kd * g_inv).astype(BF16)
    b_h = (ba * g_rest).astype(BF16)
    k_h = (kd * g_rest).astype(BF16)
    g_tot = jnp.exp(tot)
    vb = v.astype(BF16)

    for h in range(H_B):
        sl = slice(h * N_B, (h + 1) * N_B)
        s_h = s_scr[h]
        s_b = s_h.astype(BF16)
        ah, rh, bh, kh, vh = a_t[:, sl], r_t[:, sl], b_t[:, sl], k_t[:, sl], vb[:, sl]
        nmat = jnp.where(strict, _dot_nt(ah, bh), 0.0)
        mak = jnp.where(strict, _dot_nt(ah, kh), 0.0)
        prb = jnp.where(incl, _dot_nt(rh, bh), 0.0)
        prk = jnp.where(incl, _dot_nt(rh, kh), 0.0)
        z = _dot_nt(ah, s_b) + _dot(mak, vh)
        x = eye - jnp.where((hi // 2 == lo // 2) & strict, nmat, 0.0)
        blk = 2
        while blk < n:
            m = (hi // (2 * blk) == lo // (2 * blk)) & ((hi // blk) % 2 == 1) & ((lo // blk) % 2 == 0)
            x = x - _dot(_dot(x, jnp.where(m, nmat, 0.0)), x)
            blk *= 2
        u = -_dot(x, z)
        y = _dot_nt(rh, s_b) + _dot(prb, u) + _dot(prk, vh)
        y_ref[0, :, sl] = y
        s_scr[h] = s_h * g_tot[:, sl] + _dot_tn(u, b_h[:, sl]) + _dot_tn(vh, k_h[:, sl])

    @pl.when(c == pl.num_programs(1) - 1)
    def _():
        sf_ref[0] = s_scr[...]


def _rwkv_scan(r, v, kkn, kd, ba, lw, s0, dr):
    b, s, d = r.shape
    nc = s // CHUNK
    reverse = dr == 1
    tmap = (lambda bi, c: (bi, nc - 1 - c, 0)) if reverse else (lambda bi, c: (bi, c, 0))
    dmap = (lambda bi, c: (dr, bi, nc - 1 - c, 0)) if reverse else (lambda bi, c: (dr, bi, c, 0))
    tok = pl.BlockSpec((1, CHUNK, d), tmap)
    tok2 = pl.BlockSpec((1, 1, CHUNK, d), dmap)
    in_specs = [tok, tok, tok, tok2, tok2, tok2]
    args = [r, v, kkn, kd, ba, lw]
    if s0 is not None:
        in_specs.append(pl.BlockSpec((1, None, H_B, N_B, N_B), lambda bi, c: (bi, dr, 0, 0, 0)))
        args.append(s0)
    return pl.pallas_call(
        functools.partial(_scan_kernel, reverse=reverse, has_s0=s0 is not None),
        grid=(b, nc),
        in_specs=in_specs,
        out_specs=[tok, pl.BlockSpec((1, H_B, N_B, N_B), lambda bi, c: (bi, 0, 0, 0))],
        out_shape=[jax.ShapeDtypeStruct((b, s, d), F32),
                   jax.ShapeDtypeStruct((b, H_B, N_B, N_B), F32)],
        scratch_shapes=[pltpu.VMEM((H_B, N_B, N_B), F32)],
        compiler_params=pltpu.CompilerParams(
            dimension_semantics=("arbitrary", "arbitrary"),
            vmem_limit_bytes=VMEM_LIMIT),
        name="rwkv_scan",
    )(*args)


def _merge_kernel(x_ref, oa_ref, y0_ref, y1_ref, bonus_ref, gate_ref, g0_ref, g1_ref,
                  mg_ref, sc_ref, sh_ref, gnw_ref, gnb_ref, hones_ref, wout_ref,
                  gpost_ref, gpre_ref, wr_ref, br_ref,
                  x1_ref, h2_ref, comb_ref):
    g = hones_ref[...]
    y = y0_ref[0] + y1_ref[0]
    mu = _headsum(y, g) * (1.0 / N_B)
    yc = y - mu
    var = _headsum(yc * yc, g) * (1.0 / N_B)
    yn = yc * lax.rsqrt(var + GN_EPS) * gnw_ref[...] + gnb_ref[...]
    ob = (yn + bonus_ref[0]) * gate_ref[0]
    merged = _sigmoid(g0_ref[0]) * oa_ref[0] + _sigmoid(g1_ref[0]) * ob
    out = _dot(merged, wout_ref[...])
    x1 = x_ref[0] + mg_ref[0] * _rms(out, gpost_ref[...])
    x1_ref[0] = x1
    h2 = _rms(x1, gpre_ref[...]) * (1.0 + sc_ref[0]) + sh_ref[0]
    h2_ref[0] = h2.astype(BF16)

    logits = jnp.dot(h2, wr_ref[...], precision=lax.Precision.HIGHEST,
                     preferred_element_type=F32) + br_ref[...]
    lane = lax.broadcasted_iota(jnp.int32, logits.shape, 1)
    work = logits
    top = None
    picks = []
    for _ in range(TOP_K):
        mx = jnp.max(work, axis=-1, keepdims=True)
        idx = jnp.min(jnp.where(work == mx, lane, LANES), axis=-1, keepdims=True)
        hit = lane == idx
        if top is None:
            top = mx
        picks.append((hit, jnp.exp(mx - top)))
        work = jnp.where(hit, -jnp.inf, work)
    denom = sum(e for _, e in picks)
    comb = jnp.zeros_like(logits)
    for hit, e in picks:
        comb = comb + jnp.where(hit, e / denom, 0.0)
    comb_ref[0] = comb


def _merge_out(x, oa, y0, y1, bonus, gate, proj, mods, p, tm):
    b, s, d = x.shape
    per_batch = mods[0].shape[0] > 1
    mod_map = (lambda bi, i: (bi, 0, 0)) if per_batch else (lambda bi, i: (0, 0, 0))
    tok = pl.BlockSpec((1, tm, d), lambda bi, i: (bi, i, 0))

    def full(a):
        return pl.BlockSpec(a.shape, lambda bi, i: (0,) * a.ndim)

    consts = [p['gn_w'], p['gn_b'], p['head_ones'], p['w_out'], p['g_post_mix'], p['g_pre_ffn'],
              p['w_router'], p['b_router']]
    in_specs = ([tok] * 6
                + [pl.BlockSpec((1, tm, d), lambda bi, i: (bi, i, COL_G0 // d)),
                   pl.BlockSpec((1, tm, d), lambda bi, i: (bi, i, COL_G1 // d))]
                + [pl.BlockSpec((1, 1, d), mod_map)] * 3
                + [full(a) for a in consts])
    return pl.pallas_call(
        _merge_kernel,
        grid=(b, s // tm),
        in_specs=in_specs,
        out_specs=[tok, tok, pl.BlockSpec((1, tm, LANES), lambda bi, i: (bi, i, 0))],
        out_shape=[jax.ShapeDtypeStruct((b, s, d), F32),
                   jax.ShapeDtypeStruct((b, s, d), BF16),
                   jax.ShapeDtypeStruct((b, s, LANES), F32)],
        compiler_params=pltpu.CompilerParams(
            dimension_semantics=("arbitrary", "arbitrary"),
            vmem_limit_bytes=VMEM_LIMIT),
        name="merge_out",
    )(x, oa, y0, y1, bonus, gate, proj, proj, *mods, *consts)


def _moe_kernel(h_ref, comb_ref, x1_ref, mg_ref, wgu_ref, bgu_ref, wd_ref, bd_ref, gpost_ref,
                o_ref, acc_scr):
    e = pl.program_id(2)

    @pl.when(e == 0)
    def _():
        acc_scr[...] = jnp.zeros_like(acc_scr)

    gu = jnp.dot(h_ref[0], wgu_ref[0], preferred_element_type=F32) + bgu_ref[0]
    gate = jnp.minimum(gu[:, :D_MODEL], SWIGLU_LIMIT)
    up = jnp.clip(gu[:, D_MODEL:], -SWIGLU_LIMIT, SWIGLU_LIMIT)
    act = (up + 1.0) * gate * _sigmoid(SWIGLU_ALPHA * gate)
    y = _dot(act, wd_ref[0]) + bd_ref[0]
    comb = comb_ref[0]
    lane = lax.broadcasted_iota(jnp.int32, comb.shape, 1)
    w = jnp.sum(jnp.where(lane == e, comb, 0.0), axis=-1, keepdims=True)
    acc_scr[...] += w * y

    @pl.when(e == pl.num_programs(2) - 1)
    def _():
        o_ref[0] = x1_ref[0] + mg_ref[0] * _rms(acc_scr[...], gpost_ref[...])


def _moe(h2, comb, x1, mod_gate, p, tm):
    b, s, d = h2.shape
    per_batch = mod_gate.shape[0] > 1
    mod_map = (lambda bi, i, e: (bi, 0, 0)) if per_batch else (lambda bi, i, e: (0, 0, 0))
    tok = pl.BlockSpec((1, tm, d), lambda bi, i, e: (bi, i, 0))
    return pl.pallas_call(
        _moe_kernel,
        grid=(b, s // tm, N_EXPERTS),
        in_specs=[tok,
                  pl.BlockSpec((1, tm, LANES), lambda bi, i, e: (bi, i, 0)),
                  tok,
                  pl.BlockSpec((1, 1, d), mod_map),
                  pl.BlockSpec((1, d, 2 * d), lambda bi, i, e: (e, 0, 0)),
                  pl.BlockSpec((1, 1, 2 * d), lambda bi, i, e: (e, 0, 0)),
                  pl.BlockSpec((1, d, d), lambda bi, i, e: (e, 0, 0)),
                  pl.BlockSpec((1, 1, d), lambda bi, i, e: (e, 0, 0)),
                  pl.BlockSpec((1, d), lambda bi, i, e: (0, 0))],
        out_specs=tok,
        out_shape=jax.ShapeDtypeStruct((b, s, d), F32),
        scratch_shapes=[pltpu.VMEM((tm, d), F32)],
        compiler_params=pltpu.CompilerParams(
            dimension_semantics=("arbitrary", "arbitrary", "arbitrary"),
            vmem_limit_bytes=VMEM_LIMIT),
        name="moe_dense",
    )(h2, comb, x1, mod_gate, p['w_gate_up'], p['b_gate_up'], p['w_down'], p['b_down'],
      p['g_post_ffn'])


def _rope_tables(n_tokens):
    rows = n_tokens // GRID_W
    row = jnp.repeat(jnp.arange(rows, dtype=F32), GRID_W)
    col = jnp.tile(jnp.arange(GRID_W, dtype=F32), rows)
    inv = ROPE_THETA ** (-jnp.arange(ROPE_FREQS, dtype=F32) / ROPE_FREQS)
    ang_r = row[:, None] * inv[None, :]
    ang_c = col[:, None] * inv[None, :]
    ang = jnp.concatenate([ang_r, ang_r, ang_c, ang_c] * 2, axis=-1)
    return jnp.cos(ang), jnp.sin(ang)


def _stream(x, mods, proj_w, lam, p, rope, ctx, s0, tm_proj, tm_tok):
    b, s, d = x.shape
    xf = x if mods[0].shape[0] > 1 else x.reshape(1, b * s, d)
    proj = _norm_proj(xf, p['g_pre_mix'], mods[1], mods[0], proj_w, tm_proj, 768)
    proj = proj.reshape(b, s, C_IN)
    oa = _diff_attention(proj, lam, p['g_subln'], rope, ctx)
    r, v, kkn, kd, ba, lw, bonus, gate = _rwkv_prep(proj, p, min(s, 256))
    y0, sf0 = _rwkv_scan(r, v, kkn, kd, ba, lw, s0, 0)
    y1, sf1 = _rwkv_scan(r, v, kkn, kd, ba, lw, s0, 1)
    tb, tsq = xf.shape[0], xf.shape[1]
    x1, h2, comb = _merge_out(*(a.reshape(tb, tsq, a.shape[-1]) for a in (x, oa, y0, y1, bonus, gate, proj)),
                              (mods[2], mods[4], mods[3]), p, tm_tok)
    out = _moe(h2, comb, x1, mods[5], p, tm_tok)
    return out.reshape(b, s, d), proj, jnp.stack([sf0, sf1], axis=1)


def kernel(x_prompt, x_sample, cache_k, cache_v, state_rwkv, c, c_ctx, w_ada, b_ada, g_pre_mix, g_post_mix, g_pre_ffn, g_post_ffn, w_in, mu_prev, mu_next, lam, g_subln, k_k, k_a, r_k, w0, w_up, a0, a_up, g_up, gn_w, gn_b, w_out, w_router, b_router, w_gate_up, b_gate_up, w_down, b_down):
    l = 0
    d = D_MODEL
    bp, sp, _ = x_prompt.shape
    bs, ss, _ = x_sample.shape

    n_cond = 1 + bs
    rows = -(-n_cond // SUBLANES) * SUBLANES
    cond = jnp.concatenate([c_ctx[None, :], c, jnp.zeros((rows - n_cond, d), F32)], axis=0)
    mod = _modulation(cond, w_ada[l], b_ada[l][None, :])
    mods_p = [mod[0:1, i * d:(i + 1) * d].reshape(1, 1, d) for i in range(N_MOD)]
    mods_s = [mod[1:1 + bs, i * d:(i + 1) * d].reshape(bs, 1, d) for i in range(N_MOD)]

    w = w_in[l]
    o_rwkv = 3 * d
    o_gate = o_rwkv + 3 * d + C_LORA
    w_perm = jnp.concatenate([w[:, :3 * d], w[:, o_gate:o_gate + 2 * d],
                              w[:, o_rwkv:o_rwkv + 3 * d], w[:, o_rwkv + 3 * d:o_gate]],
                             axis=1).astype(BF16)

    lq = lam[l]
    lam_val = (jnp.exp(jnp.sum(lq[0] * lq[1])) - jnp.exp(jnp.sum(lq[2] * lq[3])) + LAM_INIT).reshape(1)

    head = jnp.arange(d) // N_B
    wr = jnp.concatenate([w_router[l], jnp.zeros((d, LANES - N_EXPERTS), F32)], axis=1)
    br = jnp.concatenate([b_router[l], jnp.full((LANES - N_EXPERTS,), -jnp.inf, F32)])[None, :]
    mup, mun = mu_prev[l][None, :], mu_next[l][None, :]
    p = {
        'g_pre_mix': g_pre_mix[l][None, :], 'g_post_mix': g_post_mix[l][None, :],
        'g_pre_ffn': g_pre_ffn[l][None, :], 'g_post_ffn': g_post_ffn[l][None, :],
        'g_subln': g_subln[l][None, :],
        'mu_prev_main': mup[:, :3 * d], 'mu_next_main': mun[:, :3 * d],
        'mu_prev_lora': mup[:, 3 * d:], 'mu_next_lora': mun[:, 3 * d:],
        'k_k': k_k[l][None, :], 'k_a': k_a[l][None, :], 'r_k': r_k[l].reshape(1, d),
        'w0': w0[l].reshape(1, 2 * d),
        'w_up': jnp.concatenate([w_up[l, 0], w_up[l, 1]], axis=1).astype(BF16),
        'a0': a0[l].reshape(1, 2 * d),
        'a_up': jnp.concatenate([a_up[l, 0], a_up[l, 1]], axis=1).astype(BF16),
        'g_up': g_up[l].astype(BF16),
        'head_ones': (head[:, None] == head[None, :]).astype(BF16),
        'gn_w': gn_w[l][None, :], 'gn_b': gn_b[l][None, :],
        'w_out': w_out[l].astype(BF16),
        'w_router': wr, 'b_router': br,
        'w_gate_up': w_gate_up[l].astype(BF16), 'b_gate_up': b_gate_up[l][:, None, :],
        'w_down': w_down[l].astype(BF16), 'b_down': b_down[l][:, None, :],
    }

    rope = _rope_tables(ss)
    ctx = (cache_k[:, l].reshape(bs, -1, d), cache_v[:, l].reshape(bs, -1, d))

    yp, proj_p, sfin = _stream(x_prompt, mods_p, w_perm, lam_val, p, None, None, None, 1024, 512)
    ys, _, _ = _stream(x_sample, mods_s, w_perm, lam_val, p, rope, ctx, state_rwkv[:, l], 1024, 512)

    new_k = proj_p[:, :, COL_K:COL_K + d].reshape(bp, 1, sp, H_A, 2, HD_A)
    new_v = proj_p[:, :, COL_V:COL_V + d].reshape(bp, 1, sp, H_A, DV_A)
    return (yp, ys, new_k, new_v, sfin[:, None])
```

```python
import functools
import math

import jax
import jax.numpy as jnp
from jax import lax
from jax.experimental import pallas as pl
from jax.experimental.pallas import tpu as pltpu

F32 = jnp.float32
BF16 = jnp.bfloat16

D_MODEL = 1024
GRID_W = 64
HD_A = 64
DV_A = 2 * HD_A
H_A = D_MODEL // DV_A
N_B = 64
H_B = D_MODEL // N_B
LORA_W = 64
LORA_A = 64
LORA_G = 128
N_EXPERTS = 32
TOP_K = 4
SWIGLU_LIMIT = 7.0
SWIGLU_ALPHA = 1.702
ROPE_THETA = 10000.0
ROPE_FREQS = HD_A // 4
NORM_EPS = 1e-6
GN_EPS = 64e-5
L2_EPS = 1e-12
ATTN_SCALE = HD_A ** -0.5
N_MOD = 6
LAM_INIT = 0.8 - 0.6 * math.exp(-0.3 * 0)
DECAY_SCALE = -math.exp(-0.5)

LANES = 128
SUBLANES = 8
CHUNK = 64
MOE_TM = 256
COMB_TB = 256
VMEM_LIMIT = 56 * 1024 * 1024

COL_Q, COL_K, COL_V, COL_G0, COL_G1, COL_R, COL_KR, COL_VR, COL_LORA = (
    0, 1024, 2048, 3072, 4096, 5120, 6144, 7168, 8192)
C_LORA = LORA_W + LORA_A + LORA_G
C_IN = COL_LORA + C_LORA


def _sigmoid(x):
    return 1.0 / (1.0 + jnp.exp(-x))


def _dot(a, b):
    return jnp.dot(a.astype(BF16), b.astype(BF16), preferred_element_type=F32)


def _dot_nt(a, b):
    return lax.dot_general(a.astype(BF16), b.astype(BF16), (((1,), (1,)), ((), ())),
                           preferred_element_type=F32)


def _dot_tn(a, b):
    return lax.dot_general(a.astype(BF16), b.astype(BF16), (((0,), (0,)), ((), ())),
                           preferred_element_type=F32)


def _split3(x):
    hi = x.astype(BF16)
    r1 = x - hi.astype(F32)
    mid = r1.astype(BF16)
    lo = (r1 - mid.astype(F32)).astype(BF16)
    return hi, mid, lo


def _headsum(x, g):
    hi = x.astype(BF16)
    lo = (x - hi.astype(F32)).astype(BF16)
    return (jnp.dot(hi, g, preferred_element_type=F32)
            + jnp.dot(lo, g, preferred_element_type=F32))


def _rms(x, g):
    return x * lax.rsqrt(jnp.mean(x * x, axis=-1, keepdims=True) + NORM_EPS) * g


def _mod_kernel(c_ref, w_ref, b_ref, o_ref):
    c = c_ref[...]
    s = c * _sigmoid(c)
    o_ref[...] = _dot(s, w_ref[...]) + b_ref[...]


def _modulation(cond, w_ada, b_ada):
    rows, d = cond.shape
    n = w_ada.shape[1]
    tn = 768
    return pl.pallas_call(
        _mod_kernel,
        grid=(n // tn,),
        in_specs=[pl.BlockSpec((rows, d), lambda j: (0, 0)),
                  pl.BlockSpec((d, tn), lambda j: (0, j)),
                  pl.BlockSpec((1, tn), lambda j: (0, j))],
        out_specs=pl.BlockSpec((rows, tn), lambda j: (0, j)),
        out_shape=jax.ShapeDtypeStruct((rows, n), F32),
        name="modulation",
    )(cond, w_ada, b_ada)


def _proj_kernel(x_ref, g_ref, sc_ref, sh_ref, w_ref, o_ref, h_scr):
    @pl.when(pl.program_id(2) == 0)
    def _():
        h = _rms(x_ref[0], g_ref[...]) * (1.0 + sc_ref[0]) + sh_ref[0]
        h_scr[...] = h.astype(BF16)

    o_ref[0] = jnp.dot(h_scr[...], w_ref[...], preferred_element_type=F32)


def _norm_proj(x, g, scale, shift, w, tm, tn):
    b, s, d = x.shape
    n = w.shape[1]
    per_batch = scale.shape[0] > 1
    mod_map = (lambda bi, i, j: (bi, 0, 0)) if per_batch else (lambda bi, i, j: (0, 0, 0))
    return pl.pallas_call(
        _proj_kernel,
        grid=(b, s // tm, n // tn),
        in_specs=[pl.BlockSpec((1, tm, d), lambda bi, i, j: (bi, i, 0)),
                  pl.BlockSpec((1, d), lambda bi, i, j: (0, 0)),
                  pl.BlockSpec((1, 1, d), mod_map),
                  pl.BlockSpec((1, 1, d), mod_map),
                  pl.BlockSpec((d, tn), lambda bi, i, j: (0, j))],
        out_specs=pl.BlockSpec((1, tm, tn), lambda bi, i, j: (bi, i, j)),
        out_shape=jax.ShapeDtypeStruct((b, s, n), F32),
        scratch_shapes=[pltpu.VMEM((tm, d), BF16)],
        compiler_params=pltpu.CompilerParams(
            dimension_semantics=("arbitrary", "arbitrary", "arbitrary"),
            vmem_limit_bytes=VMEM_LIMIT),
        name="norm_proj",
    )(x, g, scale, shift, w)


def _rope(x, cos, sin):
    lane = lax.broadcasted_iota(jnp.int32, x.shape, 1)
    even = (lane // ROPE_FREQS) % 2 == 0
    rot = jnp.where(even, -pltpu.roll(x, LANES - ROPE_FREQS, 1), pltpu.roll(x, ROPE_FREQS, 1))
    return x * cos + rot * sin


def _attn_kernel(*refs, use_rope, use_ctx, s_new):
    it = iter(refs)
    lam_ref = next(it)
    q_ref, k_ref, v_ref = next(it), next(it), next(it)
    if use_rope:
        cq_ref, sq_ref, ck_ref, sk_ref = next(it), next(it), next(it), next(it)
    if use_ctx:
        ctxk_ref, ctxv_ref = next(it), next(it)
    g_ref = next(it)
    o_ref = next(it)
    k_scr, v_scr = next(it), next(it)

    @pl.when(pl.program_id(2) == 0)
    def _():
        k = k_ref[0]
        if use_rope:
            k = _rope(k, ck_ref[...], sk_ref[...])
        k_scr[0:s_new, :] = k.astype(BF16)
        v_scr[0:s_new, :] = v_ref[0].astype(BF16)
        if use_ctx:
            k_scr[s_new:, :] = ctxk_ref[0].astype(BF16)
            v_scr[s_new:, :] = ctxv_ref[0].astype(BF16)

    lam = lam_ref[0]
    q = q_ref[0]
    if use_rope:
        q = _rope(q, cq_ref[...], sq_ref[...])
    q = q * ATTN_SCALE
    lane = lax.broadcasted_iota(jnp.int32, q.shape, 1)
    kk = k_scr[...]
    probs = []
    for m in range(2):
        in_map = (lane >= HD_A) if m else (lane < HD_A)
        qm = jnp.where(in_map, q, 0.0)
        s = _dot_nt(qm, kk)
        e = jnp.exp(s - jnp.max(s, axis=-1, keepdims=True))
        probs.append((e, jnp.sum(e, axis=-1, keepdims=True)))
    (e0, l0), (e1, l1) = probs
    pd = e0 * (1.0 / l0) - e1 * (lam / l1)
    o = jnp.dot(pd.astype(BF16), v_scr[...], preferred_element_type=F32)
    o = o * lax.rsqrt(jnp.mean(o * o, axis=-1, keepdims=True) + NORM_EPS)
    o_ref[0] = o * g_ref[...] * (1.0 - LAM_INIT)


def _diff_attention(proj, lam, g_subln, rope=None, ctx=None, tq=256):
    b, s, _ = proj.shape
    use_rope, use_ctx = rope is not None, ctx is not None
    s_tot = s + (ctx[0].shape[1] if use_ctx else 0)
    qmap = lambda bi, h, i: (bi, i, COL_Q // LANES + h)
    in_specs = [pl.BlockSpec(memory_space=pltpu.SMEM),
                pl.BlockSpec((1, tq, DV_A), qmap),
                pl.BlockSpec((1, s, DV_A), lambda bi, h, i: (bi, 0, COL_K // LANES + h)),
                pl.BlockSpec((1, s, DV_A), lambda bi, h, i: (bi, 0, COL_V // LANES + h))]
    args = [lam, proj, proj, proj]
    if use_rope:
        cos, sin = rope
        in_specs += [pl.BlockSpec((tq, DV_A), lambda bi, h, i: (i, 0)),
                     pl.BlockSpec((tq, DV_A), lambda bi, h, i: (i, 0)),
                     pl.BlockSpec((s, DV_A), lambda bi, h, i: (0, 0)),
                     pl.BlockSpec((s, DV_A), lambda bi, h, i: (0, 0))]
        args += [cos, sin, cos, sin]
    if use_ctx:
        p = ctx[0].shape[1]
        in_specs += [pl.BlockSpec((1, p, DV_A), lambda bi, h, i: (bi, 0, h)),
                     pl.BlockSpec((1, p, DV_A), lambda bi, h, i: (bi, 0, h))]
        args += [ctx[0], ctx[1]]
    in_specs.append(pl.BlockSpec((1, DV_A), lambda bi, h, i: (0, h)))
    args.append(g_subln)
    return pl.pallas_call(
        functools.partial(_attn_kernel, use_rope=use_rope, use_ctx=use_ctx, s_new=s),
        grid=(b, H_A, s // tq),
        in_specs=in_specs,
        out_specs=pl.BlockSpec((1, tq, DV_A), lambda bi, h, i: (bi, i, h)),
        out_shape=jax.ShapeDtypeStruct((b, s, D_MODEL), F32),
        scratch_shapes=[pltpu.VMEM((s_tot, DV_A), BF16), pltpu.VMEM((s_tot, DV_A), BF16)],
        compiler_params=pltpu.CompilerParams(
            dimension_semantics=("arbitrary", "arbitrary", "arbitrary"),
            vmem_limit_bytes=VMEM_LIMIT),
        name="diff_attention",
    )(*args)


def _shifted(x_ref, p_ref, n_ref, mup, mun, first, last):
    x = x_ref[0]
    ts = x.shape[0]
    row = lax.broadcasted_iota(jnp.int32, x.shape, 0)
    prev_row = p_ref[0][SUBLANES - 1:SUBLANES, :] * first
    next_row = n_ref[0][0:1, :] * last
    prev = jnp.where(row == 0, prev_row, pltpu.roll(x, 1, 0))
    nxt = jnp.where(row == ts - 1, next_row, pltpu.roll(x, ts - 1, 0))
    return x + mup * (prev - x) + mun * (nxt - x)


def _prep_kernel(r_ref, rp_ref, rn_ref, k_ref, kp_ref, kn_ref, v_ref, vp_ref, vn_ref,
                 l_ref, lp_ref, ln_ref, mup_ref, mun_ref, mupl_ref, munl_ref,
                 kk_ref, ka_ref, rk_ref, w0_ref, wup_ref, a0_ref, aup_ref, gup_ref, g_ref,
                 ro_ref, vo_ref, kko_ref, kd_ref, ba_ref, lw_ref, bonus_ref, gate_ref, *, n_tiles):
    i = pl.program_id(1)
    first = (i > 0).astype(F32)
    last = (i < n_tiles - 1).astype(F32)
    mup, mun = mup_ref[...], mun_ref[...]
    d = D_MODEL
    r = _shifted(r_ref, rp_ref, rn_ref, mup[:, 0:d], mun[:, 0:d], first, last)
    k = _shifted(k_ref, kp_ref, kn_ref, mup[:, d:2 * d], mun[:, d:2 * d], first, last)
    v = _shifted(v_ref, vp_ref, vn_ref, mup[:, 2 * d:3 * d], mun[:, 2 * d:3 * d], first, last)
    lo = _shifted(l_ref, lp_ref, ln_ref, mupl_ref[...], munl_ref[...], first, last)
    xw = lo[:, 0:LORA_W]
    xa = lo[:, LORA_W:LORA_W + LORA_A]
    xg = lo[:, LORA_W + LORA_A:]
    g = g_ref[...]

    kk = k * kk_ref[...]
    nrm = jnp.sqrt(_headsum(kk * kk, g))
    kkn = kk / jnp.maximum(nrm, L2_EPS)
    ro_ref[0] = r
    vo_ref[0] = v
    kko_ref[0] = kkn
    gate_ref[0] = _dot(_sigmoid(xg), gup_ref[...])

    wlog = w0_ref[...] + _dot(jnp.tanh(xw), wup_ref[...])
    alog = a0_ref[...] + _dot(xa, aup_ref[...])
    ka = ka_ref[...]
    rrk = r * rk_ref[...]
    dots = None
    for dr in range(2):
        sl = slice(dr * d, (dr + 1) * d)
        lw_ref[dr, 0] = DECAY_SCALE * _sigmoid(wlog[:, sl])
        a = _sigmoid(alog[:, sl])
        kd = k * (1.0 + (a - 1.0) * ka)
        kd_ref[dr, 0] = kd
        ba_ref[dr, 0] = kkn * a
        t = rrk * kd
        dots = t if dots is None else dots + t
    bonus_ref[0] = _headsum(dots, g) * v


def _rwkv_prep(proj, p, ts):
    b, s, _ = proj.shape
    d = D_MODEL
    nt = s // ts
    hb = ts // SUBLANES
    nhb = s // SUBLANES

    def main(col, w):
        return pl.BlockSpec((1, ts, w), lambda bi, i: (bi, i, col // w))

    def prev(col, w):
        return pl.BlockSpec((1, SUBLANES, w), lambda bi, i: (bi, jnp.maximum(i * hb - 1, 0), col // w))

    def nxt(col, w):
        return pl.BlockSpec((1, SUBLANES, w),
                            lambda bi, i: (bi, jnp.minimum((i + 1) * hb, nhb - 1), col // w))

    def full(a):
        return pl.BlockSpec(a.shape, lambda bi, i: (0,) * a.ndim)

    in_specs, args = [], []
    for col, w in ((COL_R, d), (COL_KR, d), (COL_VR, d), (COL_LORA, C_LORA)):
        in_specs += [main(col, w), prev(col, w), nxt(col, w)]
        args += [proj, proj, proj]
    consts = [p['mu_prev_main'], p['mu_next_main'], p['mu_prev_lora'], p['mu_next_lora'],
              p['k_k'], p['k_a'], p['r_k'], p['w0'], p['w_up'], p['a0'], p['a_up'], p['g_up'],
              p['head_ones']]
    in_specs += [full(a) for a in consts]
    args += consts
    tok = pl.BlockSpec((1, ts, d), lambda bi, i: (bi, i, 0))
    tok2 = pl.BlockSpec((2, 1, ts, d), lambda bi, i: (0, bi, i, 0))
    one = jax.ShapeDtypeStruct((b, s, d), F32)
    two = jax.ShapeDtypeStruct((2, b, s, d), F32)
    return pl.pallas_call(
        functools.partial(_prep_kernel, n_tiles=nt),
        grid=(b, nt),
        in_specs=in_specs,
        out_specs=[tok, tok, tok, tok2, tok2, tok2, tok, tok],
        out_shape=[one, one, one, two, two, two, one, one],
        compiler_params=pltpu.CompilerParams(
            dimension_semantics=("arbitrary", "arbitrary"),
            vmem_limit_bytes=VMEM_LIMIT),
        name="rwkv_prep",
    )(*args)


def _scan_kernel(*refs, reverse, has_s0, n_chunks):
    it = iter(refs)
    r_ref, v_ref, kk_ref, kd_ref, ba_ref, lw_ref = (next(it) for _ in range(6))
    s0_ref = next(it) if has_s0 else None
    y_ref, sf_ref = next(it), next(it)
    s_scr = next(it)
    c = pl.program_id(1)

    @pl.when(c == 0)
    def _():
        if has_s0:
            s_scr[...] = s0_ref[0]
        else:
            s_scr[...] = jnp.zeros_like(s_scr)

    n = CHUNK
    tt = lax.broadcasted_iota(jnp.int32, (n, n), 0)
    ss = lax.broadcasted_iota(jnp.int32, (n, n), 1)
    hi, lo = (ss, tt) if reverse else (tt, ss)
    strict = hi > lo
    incl = hi >= lo
    eye = (tt == ss).astype(F32)

    lw = lw_ref[0, 0]
    tri = jnp.where(incl, 1.0, 0.0).astype(BF16)
    cum = sum(jnp.dot(tri, part, preferred_element_type=F32) for part in _split3(lw))
    tot = cum[0:1, :] if reverse else cum[n - 1:n, :]
    r, v, kkn = r_ref[0], v_ref[0], kk_ref[0]
    kd, ba = kd_ref[0, 0], ba_ref[0, 0]
    g_in = jnp.exp(cum)
    g_inv = jnp.exp(-cum)
    g_rest = jnp.exp(tot - cum)
    a_t = (kkn * jnp.exp(cum - lw)).astype(BF16)
    r_t = (r * g_in).astype(BF16)
    b_t = (ba * g_inv).astype(BF16)
    k_t = (kd * g_inv).astype(BF16)
    b_h = (ba * g_rest).astype(BF16)
    k_h = (kd * g_rest).astype(BF16)
    g_tot = jnp.exp(tot)
    vb = v.astype(BF16)

    heads = range(H_B)
    sls = [slice(h * N_B, (h + 1) * N_B) for h in heads]
    s_old = [s_scr[h] for h in heads]
    ar = [jnp.concatenate([a_t[:, sl], r_t[:, sl]], axis=0) for sl in sls]
    bk = [jnp.concatenate([b_t[:, sl], k_t[:, sl]], axis=0) for sl in sls]
    bkh = [jnp.concatenate([b_h[:, sl], k_h[:, sl]], axis=0) for sl in sls]
    vh = [vb[:, sl] for sl in sls]
    tt2 = lax.broadcasted_iota(jnp.int32, (n, 2 * n), 0)
    col2 = lax.broadcasted_iota(jnp.int32, (n, 2 * n), 1)
    ss2 = col2 % n
    hi2, lo2 = (ss2, tt2) if reverse else (tt2, ss2)
    strict_r = (hi2 > lo2) & (col2 >= n)
    incl2 = hi2 >= lo2

    gram = [_dot_nt(ar[h], bk[h]) for h in heads]
    ars = [_dot_nt(ar[h], s_old[h]) for h in heads]
    nmat = [jnp.where(strict, gram[h][0:n, 0:n], 0.0) for h in heads]
    mak = [jnp.where(strict_r, gram[h][0:n, :], 0.0) for h in heads]
    pr = [jnp.where(incl2, gram[h][n:, :], 0.0) for h in heads]
    z = [ars[h][0:n] + _dot(mak[h], jnp.concatenate([vh[h], vh[h]], axis=0)) for h in heads]
    x = [eye - jnp.where((hi // 2 == lo // 2) & strict, nmat[h], 0.0) for h in heads]
    blk = 2
    while blk < n:
        m = (hi // (2 * blk) == lo // (2 * blk)) & ((hi // blk) % 2 == 1) & ((lo // blk) % 2 == 0)
        xl = [_dot(x[h], jnp.where(m, nmat[h], 0.0)) for h in heads]
        x = [x[h] - _dot(xl[h], x[h]) for h in heads]
        blk *= 2
    u = [-_dot(x[h], z[h]) for h in heads]
    uv = [jnp.concatenate([u[h].astype(BF16), vh[h]], axis=0) for h in heads]
    y = [ars[h][n:] + _dot(pr[h], uv[h]) for h in heads]
    s_new = [s_old[h] * g_tot[:, sls[h]] + _dot_tn(uv[h], bkh[h]) for h in heads]
    for h in heads:
        y_ref[0, :, sls[h]] = y[h]
    for h in heads:
        s_scr[h] = s_new[h]

    @pl.when(c == n_chunks - 1)
    def _():
        sf_ref[0] = s_scr[...]


def _rwkv_scan(r, v, kkn, kd, ba, lw, s0, dr):
    b, s, d = r.shape
    nc = s // CHUNK
    reverse = dr == 1
    tmap = (lambda bi, c: (bi, nc - 1 - c, 0)) if reverse else (lambda bi, c: (bi, c, 0))
    dmap = (lambda bi, c: (dr, bi, nc - 1 - c, 0)) if reverse else (lambda bi, c: (dr, bi, c, 0))
    tok = pl.BlockSpec((1, CHUNK, d), tmap)
    tok2 = pl.BlockSpec((1, 1, CHUNK, d), dmap)
    in_specs = [tok, tok, tok, tok2, tok2, tok2]
    args = [r, v, kkn, kd, ba, lw]
    if s0 is not None:
        in_specs.append(pl.BlockSpec((1, None, H_B, N_B, N_B), lambda bi, c: (bi, dr, 0, 0, 0)))
        args.append(s0)
    return pl.pallas_call(
        functools.partial(_scan_kernel, reverse=reverse, has_s0=s0 is not None, n_chunks=nc),
        grid=(b, nc),
        in_specs=in_specs,
        out_specs=[tok, pl.BlockSpec((1, H_B, N_B, N_B), lambda bi, c: (bi, 0, 0, 0))],
        out_shape=[jax.ShapeDtypeStruct((b, s, d), F32),
                   jax.ShapeDtypeStruct((b, H_B, N_B, N_B), F32)],
        scratch_shapes=[pltpu.VMEM((H_B, N_B, N_B), F32)],
        compiler_params=pltpu.CompilerParams(
            dimension_semantics=("arbitrary", "arbitrary"),
            vmem_limit_bytes=VMEM_LIMIT),
        name="rwkv_scan",
    )(*args)


def _merge_kernel(x_ref, oa_ref, y0_ref, y1_ref, bonus_ref, gate_ref, g0_ref, g1_ref,
                  mg_ref, sc_ref, sh_ref, gnw_ref, gnb_ref, hones_ref, wout_ref,
                  gpost_ref, gpre_ref, wr_ref, br_ref,
                  x1_ref, h2_ref, eid_ref, ew_ref):
    g = hones_ref[...]
    y = y0_ref[0] + y1_ref[0]
    mu = _headsum(y, g) * (1.0 / N_B)
    yc = y - mu
    var = _headsum(yc * yc, g) * (1.0 / N_B)
    yn = yc * lax.rsqrt(var + GN_EPS) * gnw_ref[...] + gnb_ref[...]
    ob = (yn + bonus_ref[0]) * gate_ref[0]
    merged = _sigmoid(g0_ref[0]) * oa_ref[0] + _sigmoid(g1_ref[0]) * ob
    out = _dot(merged, wout_ref[...])
    x1 = x_ref[0] + mg_ref[0] * _rms(out, gpost_ref[...])
    x1_ref[0] = x1
    h2 = _rms(x1, gpre_ref[...]) * (1.0 + sc_ref[0]) + sh_ref[0]
    h2_ref[0] = h2

    logits = jnp.dot(h2, wr_ref[...], precision=lax.Precision.HIGHEST,
                     preferred_element_type=F32) + br_ref[...]
    lane = lax.broadcasted_iota(jnp.int32, logits.shape, 1)
    work = logits
    top = None
    picks = []
    for _ in range(TOP_K):
        mx = jnp.max(work, axis=-1, keepdims=True)
        idx = jnp.min(jnp.where(work == mx, lane, LANES), axis=-1, keepdims=True)
        if top is None:
            top = mx
        picks.append((idx, jnp.exp(mx - top)))
        work = jnp.where(lane == idx, -jnp.inf, work)
    denom = sum(e for _, e in picks)
    eid = jnp.zeros(logits.shape, jnp.int32)
    ew = jnp.zeros_like(logits)
    for j, (idx, e) in enumerate(picks):
        eid = jnp.where(lane == j, idx, eid)
        ew = jnp.where(lane == j, e / denom, ew)
    eid_ref[0] = eid
    ew_ref[0] = ew


def _merge_out(x, oa, y0, y1, bonus, gate, proj, mods, p, tm):
    b, s, d = x.shape
    per_batch = mods[0].shape[0] > 1
    mod_map = (lambda bi, i: (bi, 0, 0)) if per_batch else (lambda bi, i: (0, 0, 0))
    tok = pl.BlockSpec((1, tm, d), lambda bi, i: (bi, i, 0))

    def full(a):
        return pl.BlockSpec(a.shape, lambda bi, i: (0,) * a.ndim)

    consts = [p['gn_w'], p['gn_b'], p['head_ones'], p['w_out'], p['g_post_mix'], p['g_pre_ffn'],
              p['w_router'], p['b_router']]
    in_specs = ([tok] * 6
                + [pl.BlockSpec((1, tm, d), lambda bi, i: (bi, i, COL_G0 // d)),
                   pl.BlockSpec((1, tm, d), lambda bi, i: (bi, i, COL_G1 // d))]
                + [pl.BlockSpec((1, 1, d), mod_map)] * 3
                + [full(a) for a in consts])
    return pl.pallas_call(
        _merge_kernel,
        grid=(b, s // tm),
        in_specs=in_specs,
        out_specs=[tok, tok, pl.BlockSpec((1, tm, LANES), lambda bi, i: (bi, i, 0)),
                   pl.BlockSpec((1, tm, LANES), lambda bi, i: (bi, i, 0))],
        out_shape=[jax.ShapeDtypeStruct((b, s, d), F32),
                   jax.ShapeDtypeStruct((b, s, d), F32),
                   jax.ShapeDtypeStruct((b, s, LANES), jnp.int32),
                   jax.ShapeDtypeStruct((b, s, LANES), F32)],
        compiler_params=pltpu.CompilerParams(
            dimension_semantics=("arbitrary", "arbitrary"),
            vmem_limit_bytes=VMEM_LIMIT),
        name="merge_out",
    )(x, oa, y0, y1, bonus, gate, proj, proj, *mods, *consts)


def _route_metadata(eid, ew, n_tiles):
    n = eid.shape[0]
    onehot = (eid[:, :, None] == jnp.arange(N_EXPERTS, dtype=jnp.int32)[None, None, :]).any(axis=1)
    onehot = onehot.astype(jnp.int32)
    csum = jnp.cumsum(onehot, axis=0)
    counts = csum[-1]
    padded = (counts + MOE_TM - 1) // MOE_TM * MOE_TM
    ends = jnp.cumsum(padded)
    rank = jnp.take_along_axis(csum - onehot, eid, axis=1)
    pos = (ends - padded)[eid] + rank
    flat = pos.reshape(-1)
    row_token = jnp.zeros((n_tiles * MOE_TM,), jnp.int32).at[flat].set(
        jnp.repeat(jnp.arange(n, dtype=jnp.int32), TOP_K))
    row_w = jnp.zeros((n_tiles * MOE_TM,), F32).at[flat].set(ew.reshape(-1))
    tile_start = jnp.arange(n_tiles, dtype=jnp.int32) * MOE_TM
    valid = tile_start < ends[-1]
    te = jnp.minimum(jnp.searchsorted(ends, tile_start, side='right'), N_EXPERTS - 1).astype(jnp.int32)
    te = jnp.where(valid, te, te[ends[-1] // MOE_TM - 1])
    first = valid & jnp.concatenate([jnp.ones((1,), bool), te[1:] != te[:-1]])
    return pos, row_token, row_w, te, valid.astype(jnp.int32), first.astype(jnp.int32)


def _experts_kernel(te_ref, tv_ref, tf_ref, idx_ref, idxn_ref, h_hbm, roww_ref,
                    wgu_ref, bgu_ref, wd_ref, bd_ref, ys_ref, xbuf, wgu_scr, wd_scr, sem, *, n):
    i = pl.program_id(0)
    slot = i % 2

    def row_copy(tok, r, slot_):
        return pltpu.make_async_copy(h_hbm.at[pl.ds(tok, 1)], xbuf.at[slot_, pl.ds(r, 1)], sem.at[slot_])

    def gather(src_ref, slot_):
        def body(r, carry):
            row_copy(src_ref[0, 0, r], r, slot_).start()
            return carry
        lax.fori_loop(0, MOE_TM, body, 0, unroll=8)

    @pl.when(i == 0)
    def _():
        gather(idx_ref, 0)

    @pl.when((i + 1 < n) & (tv_ref[jnp.minimum(i + 1, n - 1)] == 1))
    def _():
        gather(idxn_ref, 1 - slot)

    @pl.when(tf_ref[i] == 1)
    def _():
        wgu_scr[...] = wgu_ref[0].astype(BF16)
        wd_scr[...] = wd_ref[0].astype(BF16)

    @pl.when(tv_ref[i] == 1)
    def _():
        pltpu.make_async_copy(h_hbm.at[pl.ds(0, MOE_TM)], xbuf.at[slot], sem.at[slot]).wait()
        x = xbuf[slot].astype(BF16)
        gu = jnp.dot(x, wgu_scr[...], preferred_element_type=F32) + bgu_ref[0]
        gate = jnp.minimum(gu[:, :D_MODEL], SWIGLU_LIMIT)
        up = jnp.clip(gu[:, D_MODEL:], -SWIGLU_LIMIT, SWIGLU_LIMIT)
        act = (up + 1.0) * gate * _sigmoid(SWIGLU_ALPHA * gate)
        y = jnp.dot(act.astype(BF16), wd_scr[...], preferred_element_type=F32) + bd_ref[0]
        ys_ref[...] = y * roww_ref[...]

    @pl.when(tv_ref[i] == 0)
    def _():
        ys_ref[...] = jnp.zeros_like(ys_ref)


def _experts(h2, row_token, row_w, te, tv, tf, p, n_tiles):
    d = D_MODEL
    idx = row_token.reshape(n_tiles, 1, MOE_TM)
    smem_tile = lambda f: pl.BlockSpec((1, 1, MOE_TM), f, memory_space=pltpu.SMEM)
    grid_spec = pltpu.PrefetchScalarGridSpec(
        num_scalar_prefetch=3,
        grid=(n_tiles,),
        in_specs=[smem_tile(lambda i, te, tv, tf: (i, 0, 0)),
                  smem_tile(lambda i, te, tv, tf: (jnp.minimum(i + 1, n_tiles - 1), 0, 0)),
                  pl.BlockSpec(memory_space=pl.ANY),
                  pl.BlockSpec((MOE_TM, 1), lambda i, te, tv, tf: (i, 0)),
                  pl.BlockSpec((1, d, 2 * d), lambda i, te, tv, tf: (te[i], 0, 0)),
                  pl.BlockSpec((1, 1, 2 * d), lambda i, te, tv, tf: (te[i], 0, 0)),
                  pl.BlockSpec((1, d, d), lambda i, te, tv, tf: (te[i], 0, 0)),
                  pl.BlockSpec((1, 1, d), lambda i, te, tv, tf: (te[i], 0, 0))],
        out_specs=pl.BlockSpec((MOE_TM, d), lambda i, te, tv, tf: (i, 0)),
        scratch_shapes=[pltpu.VMEM((2, MOE_TM, d), F32),
                        pltpu.VMEM((d, 2 * d), BF16),
                        pltpu.VMEM((d, d), BF16),
                        pltpu.SemaphoreType.DMA((2,))])
    return pl.pallas_call(
        functools.partial(_experts_kernel, n=n_tiles),
        grid_spec=grid_spec,
        out_shape=jax.ShapeDtypeStruct((n_tiles * MOE_TM, d), F32),
        compiler_params=pltpu.CompilerParams(
            dimension_semantics=("arbitrary",), vmem_limit_bytes=VMEM_LIMIT),
        name="moe_experts",
    )(te, tv, tf, idx, idx, h2, row_w.reshape(-1, 1),
      p['w_gate_up'], p['b_gate_up'], p['w_down'], p['b_down'])


def _combine_kernel(pos_ref, posn_ref, ys_hbm, x1_ref, mg_ref, gpost_ref, o_ref, buf, sem, *, n):
    i = pl.program_id(0)
    slot = i % 2

    def row_copy(row, j, t, slot_):
        return pltpu.make_async_copy(ys_hbm.at[pl.ds(row, 1)], buf.at[slot_, j, pl.ds(t, 1)], sem.at[slot_])

    def gather(src_ref, slot_):
        for j in range(TOP_K):
            def body(t, carry, j=j):
                row_copy(src_ref[0, 0, j * COMB_TB + t], j, t, slot_).start()
                return carry
            lax.fori_loop(0, COMB_TB, body, 0, unroll=8)

    @pl.when(i == 0)
    def _():
        gather(pos_ref, 0)

    @pl.when(i + 1 < n)
    def _():
        gather(posn_ref, 1 - slot)

    for j in range(TOP_K):
        pltpu.make_async_copy(ys_hbm.at[pl.ds(0, COMB_TB)], buf.at[slot, j], sem.at[slot]).wait()
    f = (buf[slot, 0] + buf[slot, 1]) + (buf[slot, 2] + buf[slot, 3])
    o_ref[...] = x1_ref[...] + mg_ref[0] * _rms(f, gpost_ref[...])


def _combine(ys, pos, x1, mod_gate, p):
    b, s, d = x1.shape
    n_tok = b * s
    n_tiles = n_tok // COMB_TB
    per_batch = mod_gate.shape[0] > 1
    if per_batch:
        mod_gate = jnp.repeat(mod_gate, s // COMB_TB, axis=0)
    mod_map = (lambda i: (i, 0, 0)) if per_batch else (lambda i: (0, 0, 0))
    pos_t = pos.reshape(n_tiles, COMB_TB, TOP_K).transpose(0, 2, 1).reshape(n_tiles, 1, TOP_K * COMB_TB)
    smem_tile = lambda f: pl.BlockSpec((1, 1, TOP_K * COMB_TB), f, memory_space=pltpu.SMEM)
    out = pl.pallas_call(
        functools.partial(_combine_kernel, n=n_tiles),
        grid=(n_tiles,),
        in_specs=[smem_tile(lambda i: (i, 0, 0)),
                  smem_tile(lambda i: (jnp.minimum(i + 1, n_tiles - 1), 0, 0)),
                  pl.BlockSpec(memory_space=pl.ANY),
                  pl.BlockSpec((COMB_TB, d), lambda i: (i, 0)),
                  pl.BlockSpec((1, 1, d), mod_map),
                  pl.BlockSpec((1, d), lambda i: (0, 0))],
        out_specs=pl.BlockSpec((COMB_TB, d), lambda i: (i, 0)),
        out_shape=jax.ShapeDtypeStruct((n_tok, d), F32),
        scratch_shapes=[pltpu.VMEM((2, TOP_K, COMB_TB, d), F32),
                        pltpu.SemaphoreType.DMA((2,))],
        compiler_params=pltpu.CompilerParams(
            dimension_semantics=("arbitrary",), vmem_limit_bytes=VMEM_LIMIT),
        name="moe_combine",
    )(pos_t, pos_t, ys, x1.reshape(n_tok, d), mod_gate, p['g_post_ffn'])
    return out.reshape(b, s, d)


def _rope_tables(n_tokens):
    rows = n_tokens // GRID_W
    row = jnp.repeat(jnp.arange(rows, dtype=F32), GRID_W)
    col = jnp.tile(jnp.arange(GRID_W, dtype=F32), rows)
    inv = ROPE_THETA ** (-jnp.arange(ROPE_FREQS, dtype=F32) / ROPE_FREQS)
    ang_r = row[:, None] * inv[None, :]
    ang_c = col[:, None] * inv[None, :]
    ang = jnp.concatenate([ang_r, ang_r, ang_c, ang_c] * 2, axis=-1)
    return jnp.cos(ang), jnp.sin(ang)


def _stream(x, mods, proj_w, lam, p, rope, ctx, s0, tm_proj, tm_tok):
    b, s, d = x.shape
    xf = x if mods[0].shape[0] > 1 else x.reshape(1, b * s, d)
    proj = _norm_proj(xf, p['g_pre_mix'], mods[1], mods[0], proj_w, tm_proj, 768)
    proj = proj.reshape(b, s, C_IN)
    oa = _diff_attention(proj, lam, p['g_subln'], rope, ctx)
    r, v, kkn, kd, ba, lw, bonus, gate = _rwkv_prep(proj, p, min(s, 256))
    y0, sf0 = _rwkv_scan(r, v, kkn, kd, ba, lw, s0, 0)
    y1, sf1 = _rwkv_scan(r, v, kkn, kd, ba, lw, s0, 1)
    tb, tsq = xf.shape[0], xf.shape[1]
    x1, h2, eid, ew = _merge_out(*(a.reshape(tb, tsq, a.shape[-1]) for a in (x, oa, y0, y1, bonus, gate, proj)),
                                 (mods[2], mods[4], mods[3]), p, tm_tok)
    return x1, h2, eid, ew, proj, jnp.stack([sf0, sf1], axis=1)


def kernel(x_prompt, x_sample, cache_k, cache_v, state_rwkv, c, c_ctx, w_ada, b_ada, g_pre_mix, g_post_mix, g_pre_ffn, g_post_ffn, w_in, mu_prev, mu_next, lam, g_subln, k_k, k_a, r_k, w0, w_up, a0, a_up, g_up, gn_w, gn_b, w_out, w_router, b_router, w_gate_up, b_gate_up, w_down, b_down):
    l = 0
    d = D_MODEL
    bp, sp, _ = x_prompt.shape
    bs, ss, _ = x_sample.shape

    n_cond = 1 + bs
    rows = -(-n_cond // SUBLANES) * SUBLANES
    cond = jnp.concatenate([c_ctx[None, :], c, jnp.zeros((rows - n_cond, d), F32)], axis=0)
    mod = _modulation(cond, w_ada[l], b_ada[l][None, :])
    mods_p = [mod[0:1, i * d:(i + 1) * d].reshape(1, 1, d) for i in range(N_MOD)]
    mods_s = [mod[1:1 + bs, i * d:(i + 1) * d].reshape(bs, 1, d) for i in range(N_MOD)]

    w = w_in[l]
    o_rwkv = 3 * d
    o_gate = o_rwkv + 3 * d + C_LORA
    w_perm = jnp.concatenate([w[:, :3 * d], w[:, o_gate:o_gate + 2 * d],
                              w[:, o_rwkv:o_rwkv + 3 * d], w[:, o_rwkv + 3 * d:o_gate]],
                             axis=1).astype(BF16)

    lq = lam[l]
    lam_val = (jnp.exp(jnp.sum(lq[0] * lq[1])) - jnp.exp(jnp.sum(lq[2] * lq[3])) + LAM_INIT).reshape(1)

    head = jnp.arange(d) // N_B
    wr = jnp.concatenate([w_router[l], jnp.zeros((d, LANES - N_EXPERTS), F32)], axis=1)
    br = jnp.concatenate([b_router[l], jnp.full((LANES - N_EXPERTS,), -jnp.inf, F32)])[None, :]
    mup, mun = mu_prev[l][None, :], mu_next[l][None, :]
    p = {
        'g_pre_mix': g_pre_mix[l][None, :], 'g_post_mix': g_post_mix[l][None, :],
        'g_pre_ffn': g_pre_ffn[l][None, :], 'g_post_ffn': g_post_ffn[l][None, :],
        'g_subln': g_subln[l][None, :],
        'mu_prev_main': mup[:, :3 * d], 'mu_next_main': mun[:, :3 * d],
        'mu_prev_lora': mup[:, 3 * d:], 'mu_next_lora': mun[:, 3 * d:],
        'k_k': k_k[l][None, :], 'k_a': k_a[l][None, :], 'r_k': r_k[l].reshape(1, d),
        'w0': w0[l].reshape(1, 2 * d),
        'w_up': jnp.concatenate([w_up[l, 0], w_up[l, 1]], axis=1).astype(BF16),
        'a0': a0[l].reshape(1, 2 * d),
        'a_up': jnp.concatenate([a_up[l, 0], a_up[l, 1]], axis=1).astype(BF16),
        'g_up': g_up[l].astype(BF16),
        'head_ones': (head[:, None] == head[None, :]).astype(BF16),
        'gn_w': gn_w[l][None, :], 'gn_b': gn_b[l][None, :],
        'w_out': w_out[l].astype(BF16),
        'w_router': wr, 'b_router': br,
        'w_gate_up': w_gate_up[l], 'b_gate_up': b_gate_up[l][:, None, :],
        'w_down': w_down[l], 'b_down': b_down[l][:, None, :],
    }

    rope = _rope_tables(ss)
    ctx = (cache_k[:, l].reshape(bs, -1, d), cache_v[:, l].reshape(bs, -1, d))

    x1p, h2p, eidp, ewp, proj_p, sfin = _stream(x_prompt, mods_p, w_perm, lam_val, p, None, None, None, 1024, 512)
    x1s, h2s, eids, ews, _, _ = _stream(x_sample, mods_s, w_perm, lam_val, p, rope, ctx, state_rwkv[:, l],
                                        1024, 512)

    n_p, n_s = bp * sp, bs * ss
    n_tok = n_p + n_s
    h2 = jnp.concatenate([h2p.reshape(n_p, d), h2s.reshape(n_s, d)], axis=0)
    eid = jnp.concatenate([eidp.reshape(n_p, LANES), eids.reshape(n_s, LANES)], axis=0)[:, :TOP_K]
    ew = jnp.concatenate([ewp.reshape(n_p, LANES), ews.reshape(n_s, LANES)], axis=0)[:, :TOP_K]
    n_tiles = -(-(n_tok * TOP_K + N_EXPERTS * (MOE_TM - 1)) // MOE_TM)
    pos, row_token, row_w, te, tv, tf = _route_metadata(eid, ew, n_tiles)
    rows = _experts(h2, row_token, row_w, te, tv, tf, p, n_tiles)
    yp = _combine(rows, pos[:n_p], x1p.reshape(bp, sp, d), mods_p[5], p)
    ys = _combine(rows, pos[n_p:], x1s, mods_s[5], p)

    new_k = proj_p[:, :, COL_K:COL_K + d].reshape(bp, 1, sp, H_A, 2, HD_A)
    new_v = proj_p[:, :, COL_V:COL_V + d].reshape(bp, 1, sp, H_A, DV_A)
    return (yp, ys, new_k, new_v, sfin[:, None])
```

```python
import functools
import math

import jax
import jax.numpy as jnp
from jax import lax
from jax.experimental import pallas as pl
from jax.experimental.pallas import tpu as pltpu

F32 = jnp.float32
BF16 = jnp.bfloat16

D_MODEL = 1024
GRID_W = 64
HD_A = 64
DV_A = 2 * HD_A
H_A = D_MODEL // DV_A
N_B = 64
H_B = D_MODEL // N_B
LORA_W = 64
LORA_A = 64
LORA_G = 128
N_EXPERTS = 32
TOP_K = 4
SWIGLU_LIMIT = 7.0
SWIGLU_ALPHA = 1.702
ROPE_THETA = 10000.0
ROPE_FREQS = HD_A // 4
NORM_EPS = 1e-6
GN_EPS = 64e-5
L2_EPS = 1e-12
ATTN_SCALE = HD_A ** -0.5
N_MOD = 6
LAM_INIT = 0.8 - 0.6 * math.exp(-0.3 * 0)
DECAY_SCALE = -math.exp(-0.5)

LANES = 128
SUBLANES = 8
CHUNK = 64
MOE_TM = 256
COMB_TB = 256
VMEM_LIMIT = 56 * 1024 * 1024

COL_Q, COL_K, COL_V, COL_G0, COL_G1, COL_R, COL_KR, COL_VR, COL_LORA = (
    0, 1024, 2048, 3072, 4096, 5120, 6144, 7168, 8192)
C_LORA = LORA_W + LORA_A + LORA_G
C_IN = COL_LORA + C_LORA


def _sigmoid(x):
    return 1.0 / (1.0 + jnp.exp(-x))


def _dot(a, b):
    return jnp.dot(a.astype(BF16), b.astype(BF16), preferred_element_type=F32)


def _dot_nt(a, b):
    return lax.dot_general(a.astype(BF16), b.astype(BF16), (((1,), (1,)), ((), ())),
                           preferred_element_type=F32)


def _dot_tn(a, b):
    return lax.dot_general(a.astype(BF16), b.astype(BF16), (((0,), (0,)), ((), ())),
                           preferred_element_type=F32)


def _split3(x):
    hi = x.astype(BF16)
    r1 = x - hi.astype(F32)
    mid = r1.astype(BF16)
    lo = (r1 - mid.astype(F32)).astype(BF16)
    return hi, mid, lo


def _headsum(x, g):
    hi = x.astype(BF16)
    lo = (x - hi.astype(F32)).astype(BF16)
    return (jnp.dot(hi, g, preferred_element_type=F32)
            + jnp.dot(lo, g, preferred_element_type=F32))


def _rms(x, g):
    return x * lax.rsqrt(jnp.mean(x * x, axis=-1, keepdims=True) + NORM_EPS) * g


def _mod_kernel(c_ref, w_ref, b_ref, o_ref):
    c = c_ref[...]
    s = c * _sigmoid(c)
    o_ref[...] = _dot(s, w_ref[...]) + b_ref[...]


def _modulation(cond, w_ada, b_ada):
    rows, d = cond.shape
    n = w_ada.shape[1]
    tn = 768
    return pl.pallas_call(
        _mod_kernel,
        grid=(n // tn,),
        in_specs=[pl.BlockSpec((rows, d), lambda j: (0, 0)),
                  pl.BlockSpec((d, tn), lambda j: (0, j)),
                  pl.BlockSpec((1, tn), lambda j: (0, j))],
        out_specs=pl.BlockSpec((rows, tn), lambda j: (0, j)),
        out_shape=jax.ShapeDtypeStruct((rows, n), F32),
        name="modulation",
    )(cond, w_ada, b_ada)


def _proj_kernel(x_ref, g_ref, sc_ref, sh_ref, w_ref, o_ref, h_scr):
    @pl.when(pl.program_id(2) == 0)
    def _():
        h = _rms(x_ref[0], g_ref[...]) * (1.0 + sc_ref[0]) + sh_ref[0]
        h_scr[...] = h.astype(BF16)

    o_ref[0] = jnp.dot(h_scr[...], w_ref[...], preferred_element_type=F32)


def _norm_proj(x, g, scale, shift, w, tm, tn):
    b, s, d = x.shape
    n = w.shape[1]
    per_batch = scale.shape[0] > 1
    mod_map = (lambda bi, i, j: (bi, 0, 0)) if per_batch else (lambda bi, i, j: (0, 0, 0))
    return pl.pallas_call(
        _proj_kernel,
        grid=(b, s // tm, n // tn),
        in_specs=[pl.BlockSpec((1, tm, d), lambda bi, i, j: (bi, i, 0)),
                  pl.BlockSpec((1, d), lambda bi, i, j: (0, 0)),
                  pl.BlockSpec((1, 1, d), mod_map),
                  pl.BlockSpec((1, 1, d), mod_map),
                  pl.BlockSpec((d, tn), lambda bi, i, j: (0, j))],
        out_specs=pl.BlockSpec((1, tm, tn), lambda bi, i, j: (bi, i, j)),
        out_shape=jax.ShapeDtypeStruct((b, s, n), F32),
        scratch_shapes=[pltpu.VMEM((tm, d), BF16)],
        compiler_params=pltpu.CompilerParams(
            dimension_semantics=("arbitrary", "arbitrary", "arbitrary"),
            vmem_limit_bytes=VMEM_LIMIT),
        name="norm_proj",
    )(x, g, scale, shift, w)


def _rope(x, cos, sin):
    lane = lax.broadcasted_iota(jnp.int32, x.shape, 1)
    even = (lane // ROPE_FREQS) % 2 == 0
    rot = jnp.where(even, -pltpu.roll(x, LANES - ROPE_FREQS, 1), pltpu.roll(x, ROPE_FREQS, 1))
    return x * cos + rot * sin


def _attn_kernel(*refs, use_rope, use_ctx, s_new):
    it = iter(refs)
    lam_ref = next(it)
    q_ref, k_ref, v_ref = next(it), next(it), next(it)
    if use_rope:
        cq_ref, sq_ref, ck_ref, sk_ref = next(it), next(it), next(it), next(it)
    if use_ctx:
        ctxk_ref, ctxv_ref = next(it), next(it)
    g_ref = next(it)
    o_ref = next(it)
    k_scr, v_scr = next(it), next(it)

    @pl.when(pl.program_id(2) == 0)
    def _():
        k = k_ref[0]
        if use_rope:
            k = _rope(k, ck_ref[...], sk_ref[...])
        k_scr[0:s_new, :] = k.astype(BF16)
        v_scr[0:s_new, :] = v_ref[0].astype(BF16)
        if use_ctx:
            k_scr[s_new:, :] = ctxk_ref[0].astype(BF16)
            v_scr[s_new:, :] = ctxv_ref[0].astype(BF16)

    lam = lam_ref[0]
    q = q_ref[0]
    if use_rope:
        q = _rope(q, cq_ref[...], sq_ref[...])
    q = q * ATTN_SCALE
    lane = lax.broadcasted_iota(jnp.int32, q.shape, 1)
    kk = k_scr[...]
    outs = []
    for m in range(2):
        in_map = (lane >= HD_A) if m else (lane < HD_A)
        qm = jnp.where(in_map, q, 0.0)
        s = _dot_nt(qm, kk)
        e = jnp.exp(s - jnp.max(s, axis=-1, keepdims=True))
        l = jnp.sum(e, axis=-1, keepdims=True)
        outs.append(jnp.dot(e.astype(BF16), v_scr[...], preferred_element_type=F32) / l)
    o = outs[0] - lam * outs[1]
    o = o * lax.rsqrt(jnp.mean(o * o, axis=-1, keepdims=True) + NORM_EPS)
    o_ref[0] = o * g_ref[...] * (1.0 - LAM_INIT)


def _diff_attention(proj, lam, g_subln, rope=None, ctx=None, tq=256):
    b, s, _ = proj.shape
    use_rope, use_ctx = rope is not None, ctx is not None
    s_tot = s + (ctx[0].shape[1] if use_ctx else 0)
    qmap = lambda bi, h, i: (bi, i, COL_Q // LANES + h)
    in_specs = [pl.BlockSpec(memory_space=pltpu.SMEM),
                pl.BlockSpec((1, tq, DV_A), qmap),
                pl.BlockSpec((1, s, DV_A), lambda bi, h, i: (bi, 0, COL_K // LANES + h)),
                pl.BlockSpec((1, s, DV_A), lambda bi, h, i: (bi, 0, COL_V // LANES + h))]
    args = [lam, proj, proj, proj]
    if use_rope:
        cos, sin = rope
        in_specs += [pl.BlockSpec((tq, DV_A), lambda bi, h, i: (i, 0)),
                     pl.BlockSpec((tq, DV_A), lambda bi, h, i: (i, 0)),
                     pl.BlockSpec((s, DV_A), lambda bi, h, i: (0, 0)),
                     pl.BlockSpec((s, DV_A), lambda bi, h, i: (0, 0))]
        args += [cos, sin, cos, sin]
    if use_ctx:
        p = ctx[0].shape[1]
        in_specs += [pl.BlockSpec((1, p, DV_A), lambda bi, h, i: (bi, 0, h)),
                     pl.BlockSpec((1, p, DV_A), lambda bi, h, i: (bi, 0, h))]
        args += [ctx[0], ctx[1]]
    in_specs.append(pl.BlockSpec((1, DV_A), lambda bi, h, i: (0, h)))
    args.append(g_subln)
    return pl.pallas_call(
        functools.partial(_attn_kernel, use_rope=use_rope, use_ctx=use_ctx, s_new=s),
        grid=(b, H_A, s // tq),
        in_specs=in_specs,
        out_specs=pl.BlockSpec((1, tq, DV_A), lambda bi, h, i: (bi, i, h)),
        out_shape=jax.ShapeDtypeStruct((b, s, D_MODEL), F32),
        scratch_shapes=[pltpu.VMEM((s_tot, DV_A), BF16), pltpu.VMEM((s_tot, DV_A), BF16)],
        compiler_params=pltpu.CompilerParams(
            dimension_semantics=("arbitrary", "arbitrary", "arbitrary"),
            vmem_limit_bytes=VMEM_LIMIT),
        name="diff_attention",
    )(*args)


def _shifted(x_ref, p_ref, n_ref, mup, mun, first, last):
    x = x_ref[0]
    ts = x.shape[0]
    row = lax.broadcasted_iota(jnp.int32, x.shape, 0)
    prev_row = p_ref[0][SUBLANES - 1:SUBLANES, :] * first
    next_row = n_ref[0][0:1, :] * last
    prev = jnp.where(row == 0, prev_row, pltpu.roll(x, 1, 0))
    nxt = jnp.where(row == ts - 1, next_row, pltpu.roll(x, ts - 1, 0))
    return x + mup * (prev - x) + mun * (nxt - x)


def _prep_kernel(r_ref, rp_ref, rn_ref, k_ref, kp_ref, kn_ref, v_ref, vp_ref, vn_ref,
                 l_ref, lp_ref, ln_ref, mup_ref, mun_ref, mupl_ref, munl_ref,
                 kk_ref, ka_ref, rk_ref, w0_ref, wup_ref, a0_ref, aup_ref, gup_ref, g_ref,
                 ro_ref, vo_ref, kko_ref, kd_ref, ba_ref, lw_ref, bonus_ref, gate_ref, *, n_tiles):
    i = pl.program_id(1)
    first = (i > 0).astype(F32)
    last = (i < n_tiles - 1).astype(F32)
    mup, mun = mup_ref[...], mun_ref[...]
    d = D_MODEL
    r = _shifted(r_ref, rp_ref, rn_ref, mup[:, 0:d], mun[:, 0:d], first, last)
    k = _shifted(k_ref, kp_ref, kn_ref, mup[:, d:2 * d], mun[:, d:2 * d], first, last)
    v = _shifted(v_ref, vp_ref, vn_ref, mup[:, 2 * d:3 * d], mun[:, 2 * d:3 * d], first, last)
    lo = _shifted(l_ref, lp_ref, ln_ref, mupl_ref[...], munl_ref[...], first, last)
    xw = lo[:, 0:LORA_W]
    xa = lo[:, LORA_W:LORA_W + LORA_A]
    xg = lo[:, LORA_W + LORA_A:]
    g = g_ref[...]

    kk = k * kk_ref[...]
    nrm = jnp.sqrt(_headsum(kk * kk, g))
    kkn = kk / jnp.maximum(nrm, L2_EPS)
    ro_ref[0] = r
    vo_ref[0] = v
    kko_ref[0] = kkn
    gate_ref[0] = _dot(_sigmoid(xg), gup_ref[...])

    wlog = w0_ref[...] + _dot(jnp.tanh(xw), wup_ref[...])
    alog = a0_ref[...] + _dot(xa, aup_ref[...])
    ka = ka_ref[...]
    rrk = r * rk_ref[...]
    dots = None
    for dr in range(2):
        sl = slice(dr * d, (dr + 1) * d)
        lw_ref[dr, 0] = DECAY_SCALE * _sigmoid(wlog[:, sl])
        a = _sigmoid(alog[:, sl])
        kd = k * (1.0 + (a - 1.0) * ka)
        kd_ref[dr, 0] = kd
        ba_ref[dr, 0] = kkn * a
        t = rrk * kd
        dots = t if dots is None else dots + t
    bonus_ref[0] = _headsum(dots, g) * v


def _rwkv_prep(proj, p, ts):
    b, s, _ = proj.shape
    d = D_MODEL
    nt = s // ts
    hb = ts // SUBLANES
    nhb = s // SUBLANES

    def main(col, w):
        return pl.BlockSpec((1, ts, w), lambda bi, i: (bi, i, col // w))

    def prev(col, w):
        return pl.BlockSpec((1, SUBLANES, w), lambda bi, i: (bi, jnp.maximum(i * hb - 1, 0), col // w))

    def nxt(col, w):
        return pl.BlockSpec((1, SUBLANES, w),
                            lambda bi, i: (bi, jnp.minimum((i + 1) * hb, nhb - 1), col // w))

    def full(a):
        return pl.BlockSpec(a.shape, lambda bi, i: (0,) * a.ndim)

    in_specs, args = [], []
    for col, w in ((COL_R, d), (COL_KR, d), (COL_VR, d), (COL_LORA, C_LORA)):
        in_specs += [main(col, w), prev(col, w), nxt(col, w)]
        args += [proj, proj, proj]
    consts = [p['mu_prev_main'], p['mu_next_main'], p['mu_prev_lora'], p['mu_next_lora'],
              p['k_k'], p['k_a'], p['r_k'], p['w0'], p['w_up'], p['a0'], p['a_up'], p['g_up'],
              p['head_ones']]
    in_specs += [full(a) for a in consts]
    args += consts
    tok = pl.BlockSpec((1, ts, d), lambda bi, i: (bi, i, 0))
    tok2 = pl.BlockSpec((2, 1, ts, d), lambda bi, i: (0, bi, i, 0))
    one = jax.ShapeDtypeStruct((b, s, d), F32)
    two = jax.ShapeDtypeStruct((2, b, s, d), F32)
    return pl.pallas_call(
        functools.partial(_prep_kernel, n_tiles=nt),
        grid=(b, nt),
        in_specs=in_specs,
        out_specs=[tok, tok, tok, tok2, tok2, tok2, tok, tok],
        out_shape=[one, one, one, two, two, two, one, one],
        compiler_params=pltpu.CompilerParams(
            dimension_semantics=("arbitrary", "arbitrary"),
            vmem_limit_bytes=VMEM_LIMIT),
        name="rwkv_prep",
    )(*args)


def _scan_kernel(*refs, reverse, has_s0, n_chunks):
    it = iter(refs)
    r_ref, v_ref, kk_ref, kd_ref, ba_ref, lw_ref = (next(it) for _ in range(6))
    s0_ref = next(it) if has_s0 else None
    y_ref, sf_ref = next(it), next(it)
    s_scr = next(it)
    c = pl.program_id(1)

    @pl.when(c == 0)
    def _():
        if has_s0:
            s_scr[...] = s0_ref[0]
        else:
            s_scr[...] = jnp.zeros_like(s_scr)

    n = CHUNK
    tt = lax.broadcasted_iota(jnp.int32, (n, n), 0)
    ss = lax.broadcasted_iota(jnp.int32, (n, n), 1)
    hi, lo = (ss, tt) if reverse else (tt, ss)
    strict = hi > lo
    incl = hi >= lo
    eye = (tt == ss).astype(F32)

    lw = lw_ref[0, 0]
    tri = jnp.where(incl, 1.0, 0.0).astype(BF16)
    cum = sum(jnp.dot(tri, part, preferred_element_type=F32) for part in _split3(lw))
    tot = cum[0:1, :] if reverse else cum[n - 1:n, :]
    r, v, kkn = r_ref[0], v_ref[0], kk_ref[0]
    kd, ba = kd_ref[0, 0], ba_ref[0, 0]
    g_in = jnp.exp(cum)
    g_inv = jnp.exp(-cum)
    g_rest = jnp.exp(tot - cum)
    a_t = (kkn * jnp.exp(cum - lw)).astype(BF16)
    r_t = (r * g_in).astype(BF16)
    b_t = (ba * g_inv).astype(BF16)
    k_t = (kd * g_inv).astype(BF16)
    b_h = (ba * g_rest).astype(BF16)
    k_h = (kd * g_rest).astype(BF16)
    g_tot = jnp.exp(tot)
    vb = v.astype(BF16)

    heads = range(H_B)
    sls = [slice(h * N_B, (h + 1) * N_B) for h in heads]
    s_old = [s_scr[h] for h in heads]
    ar = [jnp.concatenate([a_t[:, sl], r_t[:, sl]], axis=0) for sl in sls]
    bk = [jnp.concatenate([b_t[:, sl], k_t[:, sl]], axis=0) for sl in sls]
    bkh = [jnp.concatenate([b_h[:, sl], k_h[:, sl]], axis=0) for sl in sls]
    vh = [vb[:, sl] for sl in sls]
    tt2 = lax.broadcasted_iota(jnp.int32, (n, 2 * n), 0)
    col2 = lax.broadcasted_iota(jnp.int32, (n, 2 * n), 1)
    ss2 = col2 % n
    hi2, lo2 = (ss2, tt2) if reverse else (tt2, ss2)
    strict_r = (hi2 > lo2) & (col2 >= n)
    incl2 = hi2 >= lo2

    gram = [_dot_nt(ar[h], bk[h]) for h in heads]
    ars = [_dot_nt(ar[h], s_old[h]) for h in heads]
    nmat = [jnp.where(strict, gram[h][0:n, 0:n], 0.0) for h in heads]
    mak = [jnp.where(strict_r, gram[h][0:n, :], 0.0) for h in heads]
    pr = [jnp.where(incl2, gram[h][n:, :], 0.0) for h in heads]
    z = [ars[h][0:n] + _dot(mak[h], jnp.concatenate([vh[h], vh[h]], axis=0)) for h in heads]
    x = [eye - jnp.where((hi // 2 == lo // 2) & strict, nmat[h], 0.0) for h in heads]
    blk = 2
    while blk < n:
        m = (hi // (2 * blk) == lo // (2 * blk)) & ((hi // blk) % 2 == 1) & ((lo // blk) % 2 == 0)
        xl = [_dot(x[h], jnp.where(m, nmat[h], 0.0)) for h in heads]
        x = [x[h] - _dot(xl[h], x[h]) for h in heads]
        blk *= 2
    u = [-_dot(x[h], z[h]) for h in heads]
    uv = [jnp.concatenate([u[h].astype(BF16), vh[h]], axis=0) for h in heads]
    y = [ars[h][n:] + _dot(pr[h], uv[h]) for h in heads]
    s_new = [s_old[h] * g_tot[:, sls[h]] + _dot_tn(uv[h], bkh[h]) for h in heads]
    for h in heads:
        y_ref[0, :, sls[h]] = y[h]
    for h in heads:
        s_scr[h] = s_new[h]

    @pl.when(c == n_chunks - 1)
    def _():
        sf_ref[0] = s_scr[...]


def _rwkv_scan(r, v, kkn, kd, ba, lw, s0, dr):
    b, s, d = r.shape
    nc = s // CHUNK
    reverse = dr == 1
    tmap = (lambda bi, c: (bi, nc - 1 - c, 0)) if reverse else (lambda bi, c: (bi, c, 0))
    dmap = (lambda bi, c: (dr, bi, nc - 1 - c, 0)) if reverse else (lambda bi, c: (dr, bi, c, 0))
    tok = pl.BlockSpec((1, CHUNK, d), tmap)
    tok2 = pl.BlockSpec((1, 1, CHUNK, d), dmap)
    in_specs = [tok, tok, tok, tok2, tok2, tok2]
    args = [r, v, kkn, kd, ba, lw]
    if s0 is not None:
        in_specs.append(pl.BlockSpec((1, None, H_B, N_B, N_B), lambda bi, c: (bi, dr, 0, 0, 0)))
        args.append(s0)
    return pl.pallas_call(
        functools.partial(_scan_kernel, reverse=reverse, has_s0=s0 is not None, n_chunks=nc),
        grid=(b, nc),
        in_specs=in_specs,
        out_specs=[tok, pl.BlockSpec((1, H_B, N_B, N_B), lambda bi, c: (bi, 0, 0, 0))],
        out_shape=[jax.ShapeDtypeStruct((b, s, d), F32),
                   jax.ShapeDtypeStruct((b, H_B, N_B, N_B), F32)],
        scratch_shapes=[pltpu.VMEM((H_B, N_B, N_B), F32)],
        compiler_params=pltpu.CompilerParams(
            dimension_semantics=("arbitrary", "arbitrary"),
            vmem_limit_bytes=VMEM_LIMIT),
        name="rwkv_scan",
    )(*args)


def _merge_kernel(x_ref, oa_ref, y0_ref, y1_ref, bonus_ref, gate_ref, g0_ref, g1_ref,
                  mg_ref, sc_ref, sh_ref, gnw_ref, gnb_ref, hones_ref, wout_ref,
                  gpost_ref, gpre_ref, wr_ref, br_ref, cnt0_ref,
                  x1_ref, h2_ref, route_ref, ew_ref, cnt_ref, cnt_scr):
    @pl.when((pl.program_id(0) == 0) & (pl.program_id(1) == 0))
    def _():
        cnt_scr[...] = cnt0_ref[...]

    g = hones_ref[...]
    y = y0_ref[0] + y1_ref[0]
    mu = _headsum(y, g) * (1.0 / N_B)
    yc = y - mu
    var = _headsum(yc * yc, g) * (1.0 / N_B)
    yn = yc * lax.rsqrt(var + GN_EPS) * gnw_ref[...] + gnb_ref[...]
    ob = (yn + bonus_ref[0]) * gate_ref[0]
    merged = _sigmoid(g0_ref[0]) * oa_ref[0] + _sigmoid(g1_ref[0]) * ob
    out = _dot(merged, wout_ref[...])
    x1 = x_ref[0] + mg_ref[0] * _rms(out, gpost_ref[...])
    x1_ref[0] = x1
    h2 = _rms(x1, gpre_ref[...]) * (1.0 + sc_ref[0]) + sh_ref[0]
    h2_ref[0] = h2

    logits = jnp.dot(h2, wr_ref[...], precision=lax.Precision.HIGHEST,
                     preferred_element_type=F32) + br_ref[...]
    lane = lax.broadcasted_iota(jnp.int32, logits.shape, 1)
    work = logits
    top = None
    picks = []
    for _ in range(TOP_K):
        mx = jnp.max(work, axis=-1, keepdims=True)
        idx = jnp.min(jnp.where(work == mx, lane, LANES), axis=-1, keepdims=True)
        if top is None:
            top = mx
        picks.append((idx, jnp.exp(mx - top)))
        work = jnp.where(lane == idx, -jnp.inf, work)
    denom = sum(e for _, e in picks)
    tm = logits.shape[0]
    onehot = jnp.zeros_like(logits)
    for idx, _ in picks:
        onehot = onehot + jnp.where(lane == idx, 1.0, 0.0)
    rr = lax.broadcasted_iota(jnp.int32, (tm, tm), 0)
    cc = lax.broadcasted_iota(jnp.int32, (tm, tm), 1)
    before = cnt_scr[...] + _dot(jnp.where(rr > cc, 1.0, 0.0), onehot)
    route = jnp.zeros(logits.shape, jnp.int32)
    ew = jnp.zeros_like(logits)
    for j, (idx, e) in enumerate(picks):
        rank = jnp.sum(jnp.where(lane == idx, before, 0.0), axis=-1, keepdims=True)
        route = jnp.where(lane == j, idx, route)
        route = jnp.where(lane == TOP_K + j, rank.astype(jnp.int32), route)
        ew = jnp.where(lane == j, e / denom, ew)
    route_ref[0] = route
    ew_ref[0] = ew
    cnt_scr[...] = cnt_scr[...] + jnp.sum(onehot, axis=0, keepdims=True)
    cnt_ref[...] = cnt_scr[...]


def _merge_out(x, oa, y0, y1, bonus, gate, proj, mods, p, cnt0, tm):
    b, s, d = x.shape
    per_batch = mods[0].shape[0] > 1
    mod_map = (lambda bi, i: (bi, 0, 0)) if per_batch else (lambda bi, i: (0, 0, 0))
    tok = pl.BlockSpec((1, tm, d), lambda bi, i: (bi, i, 0))

    def full(a):
        return pl.BlockSpec(a.shape, lambda bi, i: (0,) * a.ndim)

    consts = [p['gn_w'], p['gn_b'], p['head_ones'], p['w_out'], p['g_post_mix'], p['g_pre_ffn'],
              p['w_router'], p['b_router']]
    in_specs = ([tok] * 6
                + [pl.BlockSpec((1, tm, d), lambda bi, i: (bi, i, COL_G0 // d)),
                   pl.BlockSpec((1, tm, d), lambda bi, i: (bi, i, COL_G1 // d))]
                + [pl.BlockSpec((1, 1, d), mod_map)] * 3
                + [full(a) for a in consts] + [full(cnt0)])
    return pl.pallas_call(
        _merge_kernel,
        grid=(b, s // tm),
        in_specs=in_specs,
        out_specs=[tok, tok, pl.BlockSpec((1, tm, LANES), lambda bi, i: (bi, i, 0)),
                   pl.BlockSpec((1, tm, LANES), lambda bi, i: (bi, i, 0)),
                   pl.BlockSpec((1, LANES), lambda bi, i: (0, 0))],
        out_shape=[jax.ShapeDtypeStruct((b, s, d), F32),
                   jax.ShapeDtypeStruct((b, s, d), F32),
                   jax.ShapeDtypeStruct((b, s, LANES), jnp.int32),
                   jax.ShapeDtypeStruct((b, s, LANES), F32),
                   jax.ShapeDtypeStruct((1, LANES), F32)],
        scratch_shapes=[pltpu.VMEM((1, LANES), F32)],
        compiler_params=pltpu.CompilerParams(
            dimension_semantics=("arbitrary", "arbitrary"),
            vmem_limit_bytes=VMEM_LIMIT),
        name="merge_out",
    )(x, oa, y0, y1, bonus, gate, proj, proj, *mods, *consts, cnt0)


def _route_metadata(eid, rank, counts, n_tiles):
    padded = (counts + MOE_TM - 1) // MOE_TM * MOE_TM
    ends = jnp.cumsum(padded)
    offs = ends - padded
    experts = jnp.arange(N_EXPERTS, dtype=jnp.int32)
    pos = jnp.sum(jnp.where(eid[:, :, None] == experts, offs, 0), axis=-1) + rank
    idx = jnp.arange(n_tiles, dtype=jnp.int32)
    valid = idx * MOE_TM < ends[-1]
    tile = jnp.where(valid, idx, ends[-1] // MOE_TM - 1)
    te = jnp.sum((ends[None, :] <= (tile * MOE_TM)[:, None]).astype(jnp.int32), axis=1)
    first = valid & jnp.concatenate([jnp.ones((1,), bool), te[1:] != te[:-1]])
    fill = jnp.concatenate([jnp.where(padded > counts, ends - MOE_TM, -1),
                            jnp.where(valid, -1, idx * MOE_TM)[eid.size // MOE_TM:]])
    return pos, tile, te, valid.astype(jnp.int32), first.astype(jnp.int32), fill.astype(jnp.int32)


def _dispatch_kernel(fill_ref, pos_ref, hp_hbm, hs_hbm, xs_hbm, zbuf, sem, fsem, *, n, n_p_tiles, n_fill):
    i = pl.program_id(0)

    def fill_copy(e):
        start = pl.multiple_of(fill_ref[e], MOE_TM)
        return pltpu.make_async_copy(zbuf, xs_hbm.at[pl.ds(start, MOE_TM)], fsem)

    @pl.when(i == 0)
    def _():
        zbuf[...] = jnp.zeros_like(zbuf)
        for e in range(n_fill):
            @pl.when(fill_ref[e] >= 0)
            def _(e=e):
                fill_copy(e).start()
        for e in range(n_fill):
            @pl.when(fill_ref[e] >= 0)
            def _(e=e):
                fill_copy(e).wait()

    def scatter(h_hbm, base):
        for j in range(TOP_K):
            def body(t, carry, j=j):
                pltpu.make_async_copy(h_hbm.at[pl.ds(base + t, 1)],
                                      xs_hbm.at[pl.ds(pos_ref[0, 0, j * COMB_TB + t], 1)], sem).start()
                return carry
            lax.fori_loop(0, COMB_TB, body, 0, unroll=8)

    @pl.when(i < n_p_tiles)
    def _():
        scatter(hp_hbm, i * COMB_TB)

    @pl.when(i >= n_p_tiles)
    def _():
        scatter(hs_hbm, (i - n_p_tiles) * COMB_TB)

    def wait_tile():
        for _ in range(TOP_K):
            pltpu.make_async_copy(hp_hbm.at[pl.ds(0, COMB_TB)], xs_hbm.at[pl.ds(0, COMB_TB)], sem).wait()

    @pl.when(i > 0)
    def _():
        wait_tile()

    @pl.when(i == n - 1)
    def _():
        wait_tile()


def _dispatch(pos, fill, h2p, h2s, n_tiles):
    d = D_MODEL
    n_p, n_s = h2p.shape[0], h2s.shape[0]
    n = (n_p + n_s) // COMB_TB
    pos_t = pos.reshape(n, COMB_TB, TOP_K).transpose(0, 2, 1).reshape(n, 1, TOP_K * COMB_TB)
    grid_spec = pltpu.PrefetchScalarGridSpec(
        num_scalar_prefetch=1,
        grid=(n,),
        in_specs=[pl.BlockSpec((1, 1, TOP_K * COMB_TB), lambda i, fill: (i, 0, 0), memory_space=pltpu.SMEM),
                  pl.BlockSpec(memory_space=pl.ANY),
                  pl.BlockSpec(memory_space=pl.ANY)],
        out_specs=pl.BlockSpec(memory_space=pl.ANY),
        scratch_shapes=[pltpu.VMEM((MOE_TM, d), F32),
                        pltpu.SemaphoreType.DMA(()),
                        pltpu.SemaphoreType.DMA(())])
    return pl.pallas_call(
        functools.partial(_dispatch_kernel, n=n, n_p_tiles=n_p // COMB_TB, n_fill=fill.shape[0]),
        grid_spec=grid_spec,
        out_shape=jax.ShapeDtypeStruct((n_tiles * MOE_TM, d), F32),
        compiler_params=pltpu.CompilerParams(
            dimension_semantics=("arbitrary",), vmem_limit_bytes=VMEM_LIMIT),
        name="moe_dispatch",
    )(fill, pos_t, h2p, h2s)


def _experts_kernel(tile_ref, te_ref, tv_ref, tf_ref, x_ref,
                    wgu_ref, bgu_ref, wd_ref, bd_ref, ys_ref, wgu_scr, wd_scr):
    i = pl.program_id(0)

    @pl.when(tf_ref[i] == 1)
    def _():
        wgu_scr[...] = wgu_ref[0].astype(BF16)
        wd_scr[...] = wd_ref[0].astype(BF16)

    @pl.when(tv_ref[i] == 0)
    def _():
        ys_ref[...] = jnp.zeros_like(ys_ref)

    @pl.when(tv_ref[i] == 1)
    def _():
        gu = jnp.dot(x_ref[...].astype(BF16), wgu_scr[...], preferred_element_type=F32) + bgu_ref[0]
        gate = jnp.minimum(gu[:, :D_MODEL], SWIGLU_LIMIT)
        up = jnp.clip(gu[:, D_MODEL:], -SWIGLU_LIMIT, SWIGLU_LIMIT)
        act = (up + 1.0) * gate * _sigmoid(SWIGLU_ALPHA * gate)
        ys_ref[...] = jnp.dot(act.astype(BF16), wd_scr[...], preferred_element_type=F32) + bd_ref[0]


def _experts(xs, tile, te, tv, tf, p, n_tiles):
    d = D_MODEL
    grid_spec = pltpu.PrefetchScalarGridSpec(
        num_scalar_prefetch=4,
        grid=(n_tiles,),
        in_specs=[pl.BlockSpec((MOE_TM, d), lambda i, tile, te, tv, tf: (tile[i], 0)),
                  pl.BlockSpec((1, d, 2 * d), lambda i, tile, te, tv, tf: (te[i], 0, 0)),
                  pl.BlockSpec((1, 1, 2 * d), lambda i, tile, te, tv, tf: (te[i], 0, 0)),
                  pl.BlockSpec((1, d, d), lambda i, tile, te, tv, tf: (te[i], 0, 0)),
                  pl.BlockSpec((1, 1, d), lambda i, tile, te, tv, tf: (te[i], 0, 0))],
        out_specs=pl.BlockSpec((MOE_TM, d), lambda i, tile, te, tv, tf: (i, 0)),
        scratch_shapes=[pltpu.VMEM((d, 2 * d), BF16),
                        pltpu.VMEM((d, d), BF16)])
    return pl.pallas_call(
        _experts_kernel,
        grid_spec=grid_spec,
        out_shape=jax.ShapeDtypeStruct((n_tiles * MOE_TM, d), F32),
        compiler_params=pltpu.CompilerParams(
            dimension_semantics=("arbitrary",), vmem_limit_bytes=VMEM_LIMIT),
        name="moe_experts",
    )(tile, te, tv, tf, xs, p['w_gate_up'], p['b_gate_up'], p['w_down'], p['b_down'])


def _combine_kernel(pos_ref, posn_ref, ys_hbm, ew_ref, x1_ref, mg_ref, gpost_ref, o_ref, buf, sem, *, n):
    i = pl.program_id(0)
    slot = i % 2

    def row_copy(row, j, t, slot_):
        return pltpu.make_async_copy(ys_hbm.at[pl.ds(row, 1)], buf.at[slot_, j, pl.ds(t, 1)], sem.at[slot_])

    def gather(src_ref, slot_):
        for j in range(TOP_K):
            def body(t, carry, j=j):
                row_copy(src_ref[0, 0, j * COMB_TB + t], j, t, slot_).start()
                return carry
            lax.fori_loop(0, COMB_TB, body, 0, unroll=8)

    @pl.when(i == 0)
    def _():
        gather(pos_ref, 0)

    @pl.when(i + 1 < n)
    def _():
        gather(posn_ref, 1 - slot)

    for j in range(TOP_K):
        pltpu.make_async_copy(ys_hbm.at[pl.ds(0, COMB_TB)], buf.at[slot, j], sem.at[slot]).wait()
    ew = ew_ref[...]
    f = sum(ew[:, j:j + 1] * buf[slot, j] for j in range(TOP_K))
    o_ref[...] = x1_ref[...] + mg_ref[0] * _rms(f, gpost_ref[...])


def _combine(ys, pos, ew, x1, mod_gate, p):
    b, s, d = x1.shape
    n_tok = b * s
    n_tiles = n_tok // COMB_TB
    per_batch = mod_gate.shape[0] > 1
    if per_batch:
        mod_gate = jnp.repeat(mod_gate, s // COMB_TB, axis=0)
    mod_map = (lambda i: (i, 0, 0)) if per_batch else (lambda i: (0, 0, 0))
    pos_t = pos.reshape(n_tiles, COMB_TB, TOP_K).transpose(0, 2, 1).reshape(n_tiles, 1, TOP_K * COMB_TB)
    smem_tile = lambda f: pl.BlockSpec((1, 1, TOP_K * COMB_TB), f, memory_space=pltpu.SMEM)
    out = pl.pallas_call(
        functools.partial(_combine_kernel, n=n_tiles),
        grid=(n_tiles,),
        in_specs=[smem_tile(lambda i: (i, 0, 0)),
                  smem_tile(lambda i: (jnp.minimum(i + 1, n_tiles - 1), 0, 0)),
                  pl.BlockSpec(memory_space=pl.ANY),
                  pl.BlockSpec((COMB_TB, LANES), lambda i: (i, 0)),
                  pl.BlockSpec((COMB_TB, d), lambda i: (i, 0)),
                  pl.BlockSpec((1, 1, d), mod_map),
                  pl.BlockSpec((1, d), lambda i: (0, 0))],
        out_specs=pl.BlockSpec((COMB_TB, d), lambda i: (i, 0)),
        out_shape=jax.ShapeDtypeStruct((n_tok, d), F32),
        scratch_shapes=[pltpu.VMEM((2, TOP_K, COMB_TB, d), F32),
                        pltpu.SemaphoreType.DMA((2,))],
        compiler_params=pltpu.CompilerParams(
            dimension_semantics=("arbitrary",), vmem_limit_bytes=VMEM_LIMIT),
        name="moe_combine",
    )(pos_t, pos_t, ys, ew.reshape(n_tok, LANES), x1.reshape(n_tok, d), mod_gate, p['g_post_ffn'])
    return out.reshape(b, s, d)


def _rope_tables(n_tokens):
    rows = n_tokens // GRID_W
    row = jnp.repeat(jnp.arange(rows, dtype=F32), GRID_W)
    col = jnp.tile(jnp.arange(GRID_W, dtype=F32), rows)
    inv = ROPE_THETA ** (-jnp.arange(ROPE_FREQS, dtype=F32) / ROPE_FREQS)
    ang_r = row[:, None] * inv[None, :]
    ang_c = col[:, None] * inv[None, :]
    ang = jnp.concatenate([ang_r, ang_r, ang_c, ang_c] * 2, axis=-1)
    return jnp.cos(ang), jnp.sin(ang)


def _stream(x, mods, proj_w, lam, p, rope, ctx, s0, cnt0, tm_proj, tm_tok):
    b, s, d = x.shape
    xf = x if mods[0].shape[0] > 1 else x.reshape(1, b * s, d)
    proj = _norm_proj(xf, p['g_pre_mix'], mods[1], mods[0], proj_w, tm_proj, 768)
    proj = proj.reshape(b, s, C_IN)
    oa = _diff_attention(proj, lam, p['g_subln'], rope, ctx, tq=min(s, 512))
    r, v, kkn, kd, ba, lw, bonus, gate = _rwkv_prep(proj, p, min(s, 256))
    y0, sf0 = _rwkv_scan(r, v, kkn, kd, ba, lw, s0, 0)
    y1, sf1 = _rwkv_scan(r, v, kkn, kd, ba, lw, s0, 1)
    tb, tsq = xf.shape[0], xf.shape[1]
    x1, h2, route, ew, cnt = _merge_out(
        *(a.reshape(tb, tsq, a.shape[-1]) for a in (x, oa, y0, y1, bonus, gate, proj)),
        (mods[2], mods[4], mods[3]), p, cnt0, tm_tok)
    n = b * s
    return (x1.reshape(b, s, d), h2.reshape(n, d), route.reshape(n, LANES), ew.reshape(n, LANES), cnt,
            proj, jnp.stack([sf0, sf1], axis=1))


def kernel(x_prompt, x_sample, cache_k, cache_v, state_rwkv, c, c_ctx, w_ada, b_ada, g_pre_mix, g_post_mix, g_pre_ffn, g_post_ffn, w_in, mu_prev, mu_next, lam, g_subln, k_k, k_a, r_k, w0, w_up, a0, a_up, g_up, gn_w, gn_b, w_out, w_router, b_router, w_gate_up, b_gate_up, w_down, b_down):
    l = 0
    d = D_MODEL
    bp, sp, _ = x_prompt.shape
    bs, ss, _ = x_sample.shape

    n_cond = 1 + bs
    rows = -(-n_cond // SUBLANES) * SUBLANES
    cond = jnp.concatenate([c_ctx[None, :], c, jnp.zeros((rows - n_cond, d), F32)], axis=0)
    mod = _modulation(cond, w_ada[l], b_ada[l][None, :])
    mods_p = [mod[0:1, i * d:(i + 1) * d].reshape(1, 1, d) for i in range(N_MOD)]
    mods_s = [mod[1:1 + bs, i * d:(i + 1) * d].reshape(bs, 1, d) for i in range(N_MOD)]

    w = w_in[l]
    o_rwkv = 3 * d
    o_gate = o_rwkv + 3 * d + C_LORA
    w_perm = jnp.concatenate([w[:, :3 * d], w[:, o_gate:o_gate + 2 * d],
                              w[:, o_rwkv:o_rwkv + 3 * d], w[:, o_rwkv + 3 * d:o_gate]],
                             axis=1).astype(BF16)

    lq = lam[l]
    lam_val = (jnp.exp(jnp.sum(lq[0] * lq[1])) - jnp.exp(jnp.sum(lq[2] * lq[3])) + LAM_INIT).reshape(1)

    head = jnp.arange(d) // N_B
    wr = jnp.concatenate([w_router[l], jnp.zeros((d, LANES - N_EXPERTS), F32)], axis=1)
    br = jnp.concatenate([b_router[l], jnp.full((LANES - N_EXPERTS,), -jnp.inf, F32)])[None, :]
    mup, mun = mu_prev[l][None, :], mu_next[l][None, :]
    p = {
        'g_pre_mix': g_pre_mix[l][None, :], 'g_post_mix': g_post_mix[l][None, :],
        'g_pre_ffn': g_pre_ffn[l][None, :], 'g_post_ffn': g_post_ffn[l][None, :],
        'g_subln': g_subln[l][None, :],
        'mu_prev_main': mup[:, :3 * d], 'mu_next_main': mun[:, :3 * d],
        'mu_prev_lora': mup[:, 3 * d:], 'mu_next_lora': mun[:, 3 * d:],
        'k_k': k_k[l][None, :], 'k_a': k_a[l][None, :], 'r_k': r_k[l].reshape(1, d),
        'w0': w0[l].reshape(1, 2 * d),
        'w_up': jnp.concatenate([w_up[l, 0], w_up[l, 1]], axis=1).astype(BF16),
        'a0': a0[l].reshape(1, 2 * d),
        'a_up': jnp.concatenate([a_up[l, 0], a_up[l, 1]], axis=1).astype(BF16),
        'g_up': g_up[l].astype(BF16),
        'head_ones': (head[:, None] == head[None, :]).astype(BF16),
        'gn_w': gn_w[l][None, :], 'gn_b': gn_b[l][None, :],
        'w_out': w_out[l].astype(BF16),
        'w_router': wr, 'b_router': br,
        'w_gate_up': w_gate_up[l], 'b_gate_up': b_gate_up[l][:, None, :],
        'w_down': w_down[l], 'b_down': b_down[l][:, None, :],
    }

    rope = _rope_tables(ss)
    ctx = (cache_k[:, l].reshape(bs, -1, d), cache_v[:, l].reshape(bs, -1, d))

    cnt0 = jnp.zeros((1, LANES), F32)
    x1p, h2p, route_p, ewp, cnt_p, proj_p, sfin = _stream(
        x_prompt, mods_p, w_perm, lam_val, p, None, None, None, cnt0, 1024, 512)
    x1s, h2s, route_s, ews, cnt_s, _, _ = _stream(
        x_sample, mods_s, w_perm, lam_val, p, rope, ctx, state_rwkv[:, l], cnt_p, 1024, 512)

    n_p, n_s = bp * sp, bs * ss
    n_tok = n_p + n_s
    route = jnp.concatenate([route_p[:, :2 * TOP_K], route_s[:, :2 * TOP_K]], axis=0)
    counts = cnt_s[0, :N_EXPERTS].astype(jnp.int32)
    n_tiles = -(-(n_tok * TOP_K + N_EXPERTS * (MOE_TM - 1)) // MOE_TM)
    pos, tile, te, tv, tf, fill = _route_metadata(route[:, :TOP_K], route[:, TOP_K:], counts, n_tiles)
    xs = _dispatch(pos, fill, h2p, h2s, n_tiles)
    rows = _experts(xs, tile, te, tv, tf, p, n_tiles)
    yp = _combine(rows, pos[:n_p], ewp, x1p, mods_p[5], p)
    ys = _combine(rows, pos[n_p:], ews, x1s, mods_s[5], p)

    new_k = proj_p[:, :, COL_K:COL_K + d].reshape(bp, 1, sp, H_A, 2, HD_A)
    new_v = proj_p[:, :, COL_V:COL_V + d].reshape(bp, 1, sp, H_A, DV_A)
    return (yp, ys, new_k, new_v, sfin[:, None])
```

```python
import functools
import math

import jax
import jax.numpy as jnp
from jax import lax
from jax.experimental import pallas as pl
from jax.experimental.pallas import tpu as pltpu

F32 = jnp.float32
BF16 = jnp.bfloat16

D_MODEL = 1024
GRID_W = 64
HD_A = 64
DV_A = 2 * HD_A
H_A = D_MODEL // DV_A
N_B = 64
H_B = D_MODEL // N_B
LORA_W = 64
LORA_A = 64
LORA_G = 128
N_EXPERTS = 32
TOP_K = 4
SWIGLU_LIMIT = 7.0
SWIGLU_ALPHA = 1.702
ROPE_THETA = 10000.0
ROPE_FREQS = HD_A // 4
NORM_EPS = 1e-6
GN_EPS = 64e-5
L2_EPS = 1e-12
ATTN_SCALE = HD_A ** -0.5
N_MOD = 6
LAM_INIT = 0.8 - 0.6 * math.exp(-0.3 * 0)
DECAY_SCALE = -math.exp(-0.5)

LANES = 128
SUBLANES = 8
CHUNK = 64
MOE_TM = 256
COMB_TB = 256
ATT_KC = 256
LOG2_E = math.log2(math.e)
VMEM_LIMIT = 56 * 1024 * 1024

COL_Q, COL_K, COL_V, COL_G0, COL_G1, COL_R, COL_KR, COL_VR, COL_LORA = (
    0, 1024, 2048, 3072, 4096, 5120, 6144, 7168, 8192)
C_LORA = LORA_W + LORA_A + LORA_G
C_IN = COL_LORA + C_LORA


def _sigmoid(x):
    return 1.0 / (1.0 + jnp.exp(-x))


def _dot(a, b):
    return jnp.dot(a.astype(BF16), b.astype(BF16), preferred_element_type=F32)


def _dot_nt(a, b):
    return lax.dot_general(a.astype(BF16), b.astype(BF16), (((1,), (1,)), ((), ())),
                           preferred_element_type=F32)


def _dot_tn(a, b):
    return lax.dot_general(a.astype(BF16), b.astype(BF16), (((0,), (0,)), ((), ())),
                           preferred_element_type=F32)


def _split3(x):
    hi = x.astype(BF16)
    r1 = x - hi.astype(F32)
    mid = r1.astype(BF16)
    lo = (r1 - mid.astype(F32)).astype(BF16)
    return hi, mid, lo


def _headsum(x, g):
    hi = x.astype(BF16)
    lo = (x - hi.astype(F32)).astype(BF16)
    return (jnp.dot(hi, g, preferred_element_type=F32)
            + jnp.dot(lo, g, preferred_element_type=F32))


def _rms(x, g):
    return x * lax.rsqrt(jnp.mean(x * x, axis=-1, keepdims=True) + NORM_EPS) * g


def _mod_kernel(c_ref, w_ref, b_ref, o_ref):
    c = c_ref[...]
    s = c * _sigmoid(c)
    o_ref[...] = _dot(s, w_ref[...]) + b_ref[...]


def _modulation(cond, w_ada, b_ada):
    rows, d = cond.shape
    n = w_ada.shape[1]
    tn = 768
    return pl.pallas_call(
        _mod_kernel,
        grid=(n // tn,),
        in_specs=[pl.BlockSpec((rows, d), lambda j: (0, 0)),
                  pl.BlockSpec((d, tn), lambda j: (0, j)),
                  pl.BlockSpec((1, tn), lambda j: (0, j))],
        out_specs=pl.BlockSpec((rows, tn), lambda j: (0, j)),
        out_shape=jax.ShapeDtypeStruct((rows, n), F32),
        name="modulation",
    )(cond, w_ada, b_ada)


def _proj_kernel(x_ref, g_ref, sc_ref, sh_ref, w_ref, o_ref, h_scr):
    @pl.when(pl.program_id(2) == 0)
    def _():
        h = _rms(x_ref[0], g_ref[...]) * (1.0 + sc_ref[0]) + sh_ref[0]
        h_scr[...] = h.astype(BF16)

    o_ref[0] = jnp.dot(h_scr[...], w_ref[...], preferred_element_type=F32)


def _norm_proj(x, g, scale, shift, w, tm, tn):
    b, s, d = x.shape
    n = w.shape[1]
    per_batch = scale.shape[0] > 1
    mod_map = (lambda bi, i, j: (bi, 0, 0)) if per_batch else (lambda bi, i, j: (0, 0, 0))
    return pl.pallas_call(
        _proj_kernel,
        grid=(b, s // tm, n // tn),
        in_specs=[pl.BlockSpec((1, tm, d), lambda bi, i, j: (bi, i, 0)),
                  pl.BlockSpec((1, d), lambda bi, i, j: (0, 0)),
                  pl.BlockSpec((1, 1, d), mod_map),
                  pl.BlockSpec((1, 1, d), mod_map),
                  pl.BlockSpec((d, tn), lambda bi, i, j: (0, j))],
        out_specs=pl.BlockSpec((1, tm, tn), lambda bi, i, j: (bi, i, j)),
        out_shape=jax.ShapeDtypeStruct((b, s, n), F32),
        scratch_shapes=[pltpu.VMEM((tm, d), BF16)],
        compiler_params=pltpu.CompilerParams(
            dimension_semantics=("arbitrary", "arbitrary", "arbitrary"),
            vmem_limit_bytes=VMEM_LIMIT),
        name="norm_proj",
    )(x, g, scale, shift, w)


def _rope(x, cos, sin):
    lane = lax.broadcasted_iota(jnp.int32, x.shape, 1)
    even = (lane // ROPE_FREQS) % 2 == 0
    rot = jnp.where(even, -pltpu.roll(x, LANES - ROPE_FREQS, 1), pltpu.roll(x, ROPE_FREQS, 1))
    return x * cos + rot * sin


def _attn_kernel(*refs, use_rope, use_ctx, s_new):
    it = iter(refs)
    lam_ref = next(it)
    q_ref, k_ref, v_ref = next(it), next(it), next(it)
    if use_rope:
        cq_ref, sq_ref, ck_ref, sk_ref = next(it), next(it), next(it), next(it)
    if use_ctx:
        ctxk_ref, ctxv_ref = next(it), next(it)
    g_ref = next(it)
    o_ref = next(it)
    k_scr, vt_scr = next(it), next(it)

    @pl.when(pl.program_id(2) == 0)
    def _():
        k = k_ref[0]
        if use_rope:
            k = _rope(k, ck_ref[...], sk_ref[...])
        k_scr[0:s_new, :] = k.astype(BF16)
        vt_scr[:, 0:s_new] = v_ref[0].T.astype(BF16)
        if use_ctx:
            k_scr[s_new:, :] = ctxk_ref[0].astype(BF16)
            vt_scr[:, s_new:] = ctxv_ref[0].T.astype(BF16)

    lam = lam_ref[0]
    q = q_ref[0]
    if use_rope:
        q = _rope(q, cq_ref[...], sq_ref[...])
    q = q * ATTN_SCALE
    lane = lax.broadcasted_iota(jnp.int32, q.shape, 1)
    sts = []
    for m in range(2):
        in_map = (lane >= HD_A) if m else (lane < HD_A)
        qm = jnp.where(in_map, q, 0.0) * LOG2_E
        sts.append(_dot_nt(k_scr[...], qm))
    outs = []
    for st in sts:
        et = jnp.exp2(st - jnp.max(st, axis=0, keepdims=True))
        l = jnp.sum(et, axis=0, keepdims=True)
        ot = jnp.dot(vt_scr[...], et.astype(BF16), preferred_element_type=F32)
        outs.append(ot / l)
    o = (outs[0] - lam * outs[1]).T
    o = o * lax.rsqrt(jnp.mean(o * o, axis=-1, keepdims=True) + NORM_EPS)
    o_ref[0] = o * g_ref[...] * (1.0 - LAM_INIT)


def _diff_attention(proj, lam, g_subln, rope=None, ctx=None, tq=256):
    b, s, _ = proj.shape
    use_rope, use_ctx = rope is not None, ctx is not None
    s_tot = s + (ctx[0].shape[1] if use_ctx else 0)
    qmap = lambda bi, h, i: (bi, i, COL_Q // LANES + h)
    in_specs = [pl.BlockSpec(memory_space=pltpu.SMEM),
                pl.BlockSpec((1, tq, DV_A), qmap),
                pl.BlockSpec((1, s, DV_A), lambda bi, h, i: (bi, 0, COL_K // LANES + h)),
                pl.BlockSpec((1, s, DV_A), lambda bi, h, i: (bi, 0, COL_V // LANES + h))]
    args = [lam, proj, proj, proj]
    if use_rope:
        cos, sin = rope
        in_specs += [pl.BlockSpec((tq, DV_A), lambda bi, h, i: (i, 0)),
                     pl.BlockSpec((tq, DV_A), lambda bi, h, i: (i, 0)),
                     pl.BlockSpec((s, DV_A), lambda bi, h, i: (0, 0)),
                     pl.BlockSpec((s, DV_A), lambda bi, h, i: (0, 0))]
        args += [cos, sin, cos, sin]
    if use_ctx:
        p = ctx[0].shape[1]
        in_specs += [pl.BlockSpec((1, p, DV_A), lambda bi, h, i: (bi, 0, h)),
                     pl.BlockSpec((1, p, DV_A), lambda bi, h, i: (bi, 0, h))]
        args += [ctx[0], ctx[1]]
    in_specs.append(pl.BlockSpec((1, DV_A), lambda bi, h, i: (0, h)))
    args.append(g_subln)
    return pl.pallas_call(
        functools.partial(_attn_kernel, use_rope=use_rope, use_ctx=use_ctx, s_new=s),
        grid=(b, H_A, s // tq),
        in_specs=in_specs,
        out_specs=pl.BlockSpec((1, tq, DV_A), lambda bi, h, i: (bi, i, h)),
        out_shape=jax.ShapeDtypeStruct((b, s, D_MODEL), F32),
        scratch_shapes=[pltpu.VMEM((s_tot, DV_A), BF16), pltpu.VMEM((DV_A, s_tot), BF16)],
        compiler_params=pltpu.CompilerParams(
            dimension_semantics=("arbitrary", "arbitrary", "arbitrary"),
            vmem_limit_bytes=VMEM_LIMIT),
        name="diff_attention",
    )(*args)


def _shifted(x_ref, p_ref, n_ref, mup, mun, first, last):
    x = x_ref[0]
    ts = x.shape[0]
    row = lax.broadcasted_iota(jnp.int32, x.shape, 0)
    prev_row = p_ref[0][SUBLANES - 1:SUBLANES, :] * first
    next_row = n_ref[0][0:1, :] * last
    prev = jnp.where(row == 0, prev_row, pltpu.roll(x, 1, 0))
    nxt = jnp.where(row == ts - 1, next_row, pltpu.roll(x, ts - 1, 0))
    return x + mup * (prev - x) + mun * (nxt - x)


def _prep_kernel(r_ref, rp_ref, rn_ref, k_ref, kp_ref, kn_ref, v_ref, vp_ref, vn_ref,
                 l_ref, lp_ref, ln_ref, mup_ref, mun_ref, mupl_ref, munl_ref,
                 kk_ref, ka_ref, rk_ref, w0_ref, wup_ref, a0_ref, aup_ref, gup_ref, g_ref,
                 ro_ref, vo_ref, kko_ref, kd_ref, ba_ref, lw_ref, bonus_ref, gate_ref, *, n_tiles):
    i = pl.program_id(1)
    first = (i > 0).astype(F32)
    last = (i < n_tiles - 1).astype(F32)
    mup, mun = mup_ref[...], mun_ref[...]
    d = D_MODEL
    r = _shifted(r_ref, rp_ref, rn_ref, mup[:, 0:d], mun[:, 0:d], first, last)
    k = _shifted(k_ref, kp_ref, kn_ref, mup[:, d:2 * d], mun[:, d:2 * d], first, last)
    v = _shifted(v_ref, vp_ref, vn_ref, mup[:, 2 * d:3 * d], mun[:, 2 * d:3 * d], first, last)
    lo = _shifted(l_ref, lp_ref, ln_ref, mupl_ref[...], munl_ref[...], first, last)
    xw = lo[:, 0:LORA_W]
    xa = lo[:, LORA_W:LORA_W + LORA_A]
    xg = lo[:, LORA_W + LORA_A:]
    g = g_ref[...]

    kk = k * kk_ref[...]
    nrm = jnp.sqrt(_headsum(kk * kk, g))
    kkn = kk / jnp.maximum(nrm, L2_EPS)
    ro_ref[0] = r
    vo_ref[0] = v
    kko_ref[0] = kkn
    gate_ref[0] = _dot(_sigmoid(xg), gup_ref[...])

    wlog = w0_ref[...] + _dot(jnp.tanh(xw), wup_ref[...])
    alog = a0_ref[...] + _dot(xa, aup_ref[...])
    ka = ka_ref[...]
    rrk = r * rk_ref[...]
    dots = None
    for dr in range(2):
        sl = slice(dr * d, (dr + 1) * d)
        lw_ref[dr, 0] = DECAY_SCALE * _sigmoid(wlog[:, sl])
        a = _sigmoid(alog[:, sl])
        kd = k * (1.0 + (a - 1.0) * ka)
        kd_ref[dr, 0] = kd
        ba_ref[dr, 0] = kkn * a
        t = rrk * kd
        dots = t if dots is None else dots + t
    bonus_ref[0] = _headsum(dots, g) * v


def _rwkv_prep(proj, p, ts):
    b, s, _ = proj.shape
    d = D_MODEL
    nt = s // ts
    hb = ts // SUBLANES
    nhb = s // SUBLANES

    def main(col, w):
        return pl.BlockSpec((1, ts, w), lambda bi, i: (bi, i, col // w))

    def prev(col, w):
        return pl.BlockSpec((1, SUBLANES, w), lambda bi, i: (bi, jnp.maximum(i * hb - 1, 0), col // w))

    def nxt(col, w):
        return pl.BlockSpec((1, SUBLANES, w),
                            lambda bi, i: (bi, jnp.minimum((i + 1) * hb, nhb - 1), col // w))

    def full(a):
        return pl.BlockSpec(a.shape, lambda bi, i: (0,) * a.ndim)

    in_specs, args = [], []
    for col, w in ((COL_R, d), (COL_KR, d), (COL_VR, d), (COL_LORA, C_LORA)):
        in_specs += [main(col, w), prev(col, w), nxt(col, w)]
        args += [proj, proj, proj]
    consts = [p['mu_prev_main'], p['mu_next_main'], p['mu_prev_lora'], p['mu_next_lora'],
              p['k_k'], p['k_a'], p['r_k'], p['w0'], p['w_up'], p['a0'], p['a_up'], p['g_up'],
              p['head_ones']]
    in_specs += [full(a) for a in consts]
    args += consts
    tok = pl.BlockSpec((1, ts, d), lambda bi, i: (bi, i, 0))
    tok2 = pl.BlockSpec((2, 1, ts, d), lambda bi, i: (0, bi, i, 0))
    one = jax.ShapeDtypeStruct((b, s, d), F32)
    two = jax.ShapeDtypeStruct((2, b, s, d), F32)
    return pl.pallas_call(
        functools.partial(_prep_kernel, n_tiles=nt),
        grid=(b, nt),
        in_specs=in_specs,
        out_specs=[tok, tok, tok, tok2, tok2, tok2, tok, tok],
        out_shape=[one, one, one, two, two, two, one, one],
        compiler_params=pltpu.CompilerParams(
            dimension_semantics=("arbitrary", "arbitrary"),
            vmem_limit_bytes=VMEM_LIMIT),
        name="rwkv_prep",
    )(*args)


def _scan_kernel(*refs, reverse, has_s0, n_chunks):
    it = iter(refs)
    r_ref, v_ref, kk_ref, kd_ref, ba_ref, lw_ref = (next(it) for _ in range(6))
    s0_ref = next(it) if has_s0 else None
    y_ref, sf_ref = next(it), next(it)
    s_scr = next(it)
    c = pl.program_id(1)

    @pl.when(c == 0)
    def _():
        if has_s0:
            s_scr[...] = s0_ref[0]
        else:
            s_scr[...] = jnp.zeros_like(s_scr)

    n = CHUNK
    tt = lax.broadcasted_iota(jnp.int32, (n, n), 0)
    ss = lax.broadcasted_iota(jnp.int32, (n, n), 1)
    hi, lo = (ss, tt) if reverse else (tt, ss)
    strict = hi > lo
    incl = hi >= lo
    eye = (tt == ss).astype(F32)

    lw = lw_ref[0, 0]
    tri = jnp.where(incl, 1.0, 0.0).astype(BF16)
    cum = sum(jnp.dot(tri, part, preferred_element_type=F32) for part in _split3(lw))
    tot = cum[0:1, :] if reverse else cum[n - 1:n, :]
    r, v, kkn = r_ref[0], v_ref[0], kk_ref[0]
    kd, ba = kd_ref[0, 0], ba_ref[0, 0]
    g_in = jnp.exp(cum)
    g_inv = jnp.exp(-cum)
    g_rest = jnp.exp(tot - cum)
    a_t = (kkn * jnp.exp(cum - lw)).astype(BF16)
    r_t = (r * g_in).astype(BF16)
    b_t = (ba * g_inv).astype(BF16)
    k_t = (kd * g_inv).astype(BF16)
    b_h = (ba * g_rest).astype(BF16)
    k_h = (kd * g_rest).astype(BF16)
    g_tot = jnp.exp(tot)
    vb = v.astype(BF16)

    heads = range(H_B)
    sls = [slice(h * N_B, (h + 1) * N_B) for h in heads]
    s_old = [s_scr[h] for h in heads]
    ar = [jnp.concatenate([a_t[:, sl], r_t[:, sl]], axis=0) for sl in sls]
    bk = [jnp.concatenate([b_t[:, sl], k_t[:, sl]], axis=0) for sl in sls]
    bkh = [jnp.concatenate([b_h[:, sl], k_h[:, sl]], axis=0) for sl in sls]
    vh = [vb[:, sl] for sl in sls]
    tt2 = lax.broadcasted_iota(jnp.int32, (n, 2 * n), 0)
    col2 = lax.broadcasted_iota(jnp.int32, (n, 2 * n), 1)
    ss2 = col2 % n
    hi2, lo2 = (ss2, tt2) if reverse else (tt2, ss2)
    strict_r = (hi2 > lo2) & (col2 >= n)
    incl2 = hi2 >= lo2

    gram = [_dot_nt(ar[h], bk[h]) for h in heads]
    ars = [_dot_nt(ar[h], s_old[h]) for h in heads]
    nmat = [jnp.where(strict, gram[h][0:n, 0:n], 0.0) for h in heads]
    mak = [jnp.where(strict_r, gram[h][0:n, :], 0.0) for h in heads]
    pr = [jnp.where(incl2, gram[h][n:, :], 0.0) for h in heads]
    z = [ars[h][0:n] + _dot(mak[h], jnp.concatenate([vh[h], vh[h]], axis=0)) for h in heads]
    x = [eye - jnp.where((hi // 2 == lo // 2) & strict, nmat[h], 0.0) for h in heads]
    blk = 2
    while blk < n:
        m = (hi // (2 * blk) == lo // (2 * blk)) & ((hi // blk) % 2 == 1) & ((lo // blk) % 2 == 0)
        xl = [_dot(x[h], jnp.where(m, nmat[h], 0.0)) for h in heads]
        x = [x[h] - _dot(xl[h], x[h]) for h in heads]
        blk *= 2
    u = [-_dot(x[h], z[h]) for h in heads]
    uv = [jnp.concatenate([u[h].astype(BF16), vh[h]], axis=0) for h in heads]
    y = [ars[h][n:] + _dot(pr[h], uv[h]) for h in heads]
    s_new = [s_old[h] * g_tot[:, sls[h]] + _dot_tn(uv[h], bkh[h]) for h in heads]
    for h in heads:
        y_ref[0, :, sls[h]] = y[h]
    for h in heads:
        s_scr[h] = s_new[h]

    @pl.when(c == n_chunks - 1)
    def _():
        sf_ref[0] = s_scr[...]


def _rwkv_scan(r, v, kkn, kd, ba, lw, s0, dr):
    b, s, d = r.shape
    nc = s // CHUNK
    reverse = dr == 1
    tmap = (lambda bi, c: (bi, nc - 1 - c, 0)) if reverse else (lambda bi, c: (bi, c, 0))
    dmap = (lambda bi, c: (dr, bi, nc - 1 - c, 0)) if reverse else (lambda bi, c: (dr, bi, c, 0))
    tok = pl.BlockSpec((1, CHUNK, d), tmap)
    tok2 = pl.BlockSpec((1, 1, CHUNK, d), dmap)
    in_specs = [tok, tok, tok, tok2, tok2, tok2]
    args = [r, v, kkn, kd, ba, lw]
    if s0 is not None:
        in_specs.append(pl.BlockSpec((1, None, H_B, N_B, N_B), lambda bi, c: (bi, dr, 0, 0, 0)))
        args.append(s0)
    return pl.pallas_call(
        functools.partial(_scan_kernel, reverse=reverse, has_s0=s0 is not None, n_chunks=nc),
        grid=(b, nc),
        in_specs=in_specs,
        out_specs=[tok, pl.BlockSpec((1, H_B, N_B, N_B), lambda bi, c: (bi, 0, 0, 0))],
        out_shape=[jax.ShapeDtypeStruct((b, s, d), F32),
                   jax.ShapeDtypeStruct((b, H_B, N_B, N_B), F32)],
        scratch_shapes=[pltpu.VMEM((H_B, N_B, N_B), F32)],
        compiler_params=pltpu.CompilerParams(
            dimension_semantics=("arbitrary", "arbitrary"),
            vmem_limit_bytes=VMEM_LIMIT),
        name="rwkv_scan",
    )(*args)


def _merge_kernel(x_ref, oa_ref, y0_ref, y1_ref, bonus_ref, gate_ref, g0_ref, g1_ref,
                  mg_ref, sc_ref, sh_ref, gnw_ref, gnb_ref, hones_ref, wout_ref,
                  gpost_ref, gpre_ref, wr_ref, br_ref, cnt0_ref,
                  x1_ref, h2_ref, route_ref, ew_ref, cnt_ref, cnt_scr):
    @pl.when((pl.program_id(0) == 0) & (pl.program_id(1) == 0))
    def _():
        cnt_scr[...] = cnt0_ref[...]

    g = hones_ref[...]
    y = y0_ref[0] + y1_ref[0]
    mu = _headsum(y, g) * (1.0 / N_B)
    yc = y - mu
    var = _headsum(yc * yc, g) * (1.0 / N_B)
    yn = yc * lax.rsqrt(var + GN_EPS) * gnw_ref[...] + gnb_ref[...]
    ob = (yn + bonus_ref[0]) * gate_ref[0]
    merged = _sigmoid(g0_ref[0]) * oa_ref[0] + _sigmoid(g1_ref[0]) * ob
    out = _dot(merged, wout_ref[...])
    x1 = x_ref[0] + mg_ref[0] * _rms(out, gpost_ref[...])
    x1_ref[0] = x1
    h2 = _rms(x1, gpre_ref[...]) * (1.0 + sc_ref[0]) + sh_ref[0]
    h2_ref[0] = h2

    logits = jnp.dot(h2, wr_ref[...], precision=lax.Precision.HIGHEST,
                     preferred_element_type=F32) + br_ref[...]
    lane = lax.broadcasted_iota(jnp.int32, logits.shape, 1)
    work = logits
    top = None
    picks = []
    for _ in range(TOP_K):
        mx = jnp.max(work, axis=-1, keepdims=True)
        idx = jnp.min(jnp.where(work == mx, lane, LANES), axis=-1, keepdims=True)
        if top is None:
            top = mx
        picks.append((idx, jnp.exp(mx - top)))
        work = jnp.where(lane == idx, -jnp.inf, work)
    denom = sum(e for _, e in picks)
    tm = logits.shape[0]
    onehot = jnp.zeros_like(logits)
    for idx, _ in picks:
        onehot = onehot + jnp.where(lane == idx, 1.0, 0.0)
    rr = lax.broadcasted_iota(jnp.int32, (tm, tm), 0)
    cc = lax.broadcasted_iota(jnp.int32, (tm, tm), 1)
    before = cnt_scr[...] + _dot(jnp.where(rr > cc, 1.0, 0.0), onehot)
    route = jnp.zeros(logits.shape, jnp.int32)
    ew = jnp.zeros_like(logits)
    for j, (idx, e) in enumerate(picks):
        rank = jnp.sum(jnp.where(lane == idx, before, 0.0), axis=-1, keepdims=True)
        route = jnp.where(lane == j, idx, route)
        route = jnp.where(lane == TOP_K + j, rank.astype(jnp.int32), route)
        ew = jnp.where(lane == j, e / denom, ew)
    route_ref[0] = route
    ew_ref[0] = ew
    cnt_scr[...] = cnt_scr[...] + jnp.sum(onehot, axis=0, keepdims=True)
    cnt_ref[...] = cnt_scr[...]


def _merge_out(x, oa, y0, y1, bonus, gate, proj, mods, p, cnt0, tm):
    b, s, d = x.shape
    per_batch = mods[0].shape[0] > 1
    mod_map = (lambda bi, i: (bi, 0, 0)) if per_batch else (lambda bi, i: (0, 0, 0))
    tok = pl.BlockSpec((1, tm, d), lambda bi, i: (bi, i, 0))

    def full(a):
        return pl.BlockSpec(a.shape, lambda bi, i: (0,) * a.ndim)

    consts = [p['gn_w'], p['gn_b'], p['head_ones'], p['w_out'], p['g_post_mix'], p['g_pre_ffn'],
              p['w_router'], p['b_router']]
    in_specs = ([tok] * 6
                + [pl.BlockSpec((1, tm, d), lambda bi, i: (bi, i, COL_G0 // d)),
                   pl.BlockSpec((1, tm, d), lambda bi, i: (bi, i, COL_G1 // d))]
                + [pl.BlockSpec((1, 1, d), mod_map)] * 3
                + [full(a) for a in consts] + [full(cnt0)])
    return pl.pallas_call(
        _merge_kernel,
        grid=(b, s // tm),
        in_specs=in_specs,
        out_specs=[tok, tok, pl.BlockSpec((1, tm, LANES), lambda bi, i: (bi, i, 0)),
                   pl.BlockSpec((1, tm, LANES), lambda bi, i: (bi, i, 0)),
                   pl.BlockSpec((1, LANES), lambda bi, i: (0, 0))],
        out_shape=[jax.ShapeDtypeStruct((b, s, d), F32),
                   jax.ShapeDtypeStruct((b, s, d), F32),
                   jax.ShapeDtypeStruct((b, s, LANES), jnp.int32),
                   jax.ShapeDtypeStruct((b, s, LANES), F32),
                   jax.ShapeDtypeStruct((1, LANES), F32)],
        scratch_shapes=[pltpu.VMEM((1, LANES), F32)],
        compiler_params=pltpu.CompilerParams(
            dimension_semantics=("arbitrary", "arbitrary"),
            vmem_limit_bytes=VMEM_LIMIT),
        name="merge_out",
    )(x, oa, y0, y1, bonus, gate, proj, proj, *mods, *consts, cnt0)


def _route_metadata(eid, rank, counts, n_tiles):
    padded = (counts + MOE_TM - 1) // MOE_TM * MOE_TM
    ends = jnp.cumsum(padded)
    offs = ends - padded
    experts = jnp.arange(N_EXPERTS, dtype=jnp.int32)
    pos = jnp.sum(jnp.where(eid[:, :, None] == experts, offs, 0), axis=-1) + rank
    idx = jnp.arange(n_tiles, dtype=jnp.int32)
    valid = idx * MOE_TM < ends[-1]
    tile = jnp.where(valid, idx, ends[-1] // MOE_TM - 1)
    te = jnp.sum((ends[None, :] <= (tile * MOE_TM)[:, None]).astype(jnp.int32), axis=1)
    first = valid & jnp.concatenate([jnp.ones((1,), bool), te[1:] != te[:-1]])
    fill = jnp.concatenate([jnp.where(padded > counts, ends - MOE_TM, -1),
                            jnp.where(valid, -1, idx * MOE_TM)[eid.size // MOE_TM:]])
    return pos, tile, te, valid.astype(jnp.int32), first.astype(jnp.int32), fill.astype(jnp.int32)


def _dispatch_kernel(fill_ref, pos_ref, hp_ref, hs_ref, xs_hbm, zbuf, sem, fsem, *, n_p_tiles, n_fill):
    i = pl.program_id(0)

    def fill_copy(e):
        start = pl.multiple_of(fill_ref[e], MOE_TM)
        return pltpu.make_async_copy(zbuf, xs_hbm.at[pl.ds(start, MOE_TM)], fsem)

    @pl.when(i == 0)
    def _():
        zbuf[...] = jnp.zeros_like(zbuf)
        for e in range(n_fill):
            @pl.when(fill_ref[e] >= 0)
            def _(e=e):
                fill_copy(e).start()
        for e in range(n_fill):
            @pl.when(fill_ref[e] >= 0)
            def _(e=e):
                fill_copy(e).wait()

    def scatter(h_ref):
        for j in range(TOP_K):
            def body(t, carry, j=j):
                pltpu.make_async_copy(h_ref.at[pl.ds(t, 1)],
                                      xs_hbm.at[pl.ds(pos_ref[0, 0, j * COMB_TB + t], 1)], sem).start()
                return carry
            lax.fori_loop(0, COMB_TB, body, 0, unroll=8)

    @pl.when(i < n_p_tiles)
    def _():
        scatter(hp_ref)

    @pl.when(i >= n_p_tiles)
    def _():
        scatter(hs_ref)

    for _ in range(TOP_K):
        pltpu.make_async_copy(hp_ref, xs_hbm.at[pl.ds(0, COMB_TB)], sem).wait()


def _dispatch(pos, fill, h2p, h2s, n_tiles):
    d = D_MODEL
    n_p, n_s = h2p.shape[0], h2s.shape[0]
    n = (n_p + n_s) // COMB_TB
    n_pt = n_p // COMB_TB
    pos_t = pos.reshape(n, COMB_TB, TOP_K).transpose(0, 2, 1).reshape(n, 1, TOP_K * COMB_TB)
    grid_spec = pltpu.PrefetchScalarGridSpec(
        num_scalar_prefetch=1,
        grid=(n,),
        in_specs=[pl.BlockSpec((1, 1, TOP_K * COMB_TB), lambda i, fill: (i, 0, 0), memory_space=pltpu.SMEM),
                  pl.BlockSpec((COMB_TB, d), lambda i, fill: (jnp.minimum(i, n_pt - 1), 0)),
                  pl.BlockSpec((COMB_TB, d), lambda i, fill: (jnp.maximum(i - n_pt, 0), 0))],
        out_specs=pl.BlockSpec(memory_space=pl.ANY),
        scratch_shapes=[pltpu.VMEM((MOE_TM, d), F32),
                        pltpu.SemaphoreType.DMA(()),
                        pltpu.SemaphoreType.DMA(())])
    return pl.pallas_call(
        functools.partial(_dispatch_kernel, n_p_tiles=n_pt, n_fill=fill.shape[0]),
        grid_spec=grid_spec,
        out_shape=jax.ShapeDtypeStruct((n_tiles * MOE_TM, d), F32),
        compiler_params=pltpu.CompilerParams(
            dimension_semantics=("arbitrary",), vmem_limit_bytes=VMEM_LIMIT),
        name="moe_dispatch",
    )(fill, pos_t, h2p, h2s)


def _experts_kernel(tile_ref, te_ref, tv_ref, tf_ref, x_ref,
                    wgu_ref, bgu_ref, wd_ref, bd_ref, ys_ref, wgu_scr, wd_scr):
    i = pl.program_id(0)

    @pl.when(tf_ref[i] == 1)
    def _():
        wgu_scr[...] = wgu_ref[0].astype(BF16)
        wd_scr[...] = wd_ref[0].astype(BF16)

    @pl.when(tv_ref[i] == 0)
    def _():
        ys_ref[...] = jnp.zeros_like(ys_ref)

    @pl.when(tv_ref[i] == 1)
    def _():
        gu = jnp.dot(x_ref[...].astype(BF16), wgu_scr[...], preferred_element_type=F32) + bgu_ref[0]
        gate = jnp.minimum(gu[:, :D_MODEL], SWIGLU_LIMIT)
        up = jnp.clip(gu[:, D_MODEL:], -SWIGLU_LIMIT, SWIGLU_LIMIT)
        act = (up + 1.0) * gate * _sigmoid(SWIGLU_ALPHA * gate)
        ys_ref[...] = jnp.dot(act.astype(BF16), wd_scr[...], preferred_element_type=F32) + bd_ref[0]


def _experts(xs, tile, te, tv, tf, p, n_tiles):
    d = D_MODEL
    grid_spec = pltpu.PrefetchScalarGridSpec(
        num_scalar_prefetch=4,
        grid=(n_tiles,),
        in_specs=[pl.BlockSpec((MOE_TM, d), lambda i, tile, te, tv, tf: (tile[i], 0)),
                  pl.BlockSpec((1, d, 2 * d), lambda i, tile, te, tv, tf: (te[i], 0, 0)),
                  pl.BlockSpec((1, 1, 2 * d), lambda i, tile, te, tv, tf: (te[i], 0, 0)),
                  pl.BlockSpec((1, d, d), lambda i, tile, te, tv, tf: (te[i], 0, 0)),
                  pl.BlockSpec((1, 1, d), lambda i, tile, te, tv, tf: (te[i], 0, 0))],
        out_specs=pl.BlockSpec((MOE_TM, d), lambda i, tile, te, tv, tf: (i, 0)),
        scratch_shapes=[pltpu.VMEM((d, 2 * d), BF16),
                        pltpu.VMEM((d, d), BF16)])
    return pl.pallas_call(
        _experts_kernel,
        grid_spec=grid_spec,
        out_shape=jax.ShapeDtypeStruct((n_tiles * MOE_TM, d), F32),
        compiler_params=pltpu.CompilerParams(
            dimension_semantics=("arbitrary",), vmem_limit_bytes=VMEM_LIMIT),
        name="moe_experts",
    )(tile, te, tv, tf, xs, p['w_gate_up'], p['b_gate_up'], p['w_down'], p['b_down'])


def _combine_kernel(pos_ref, posn_ref, ys_hbm, ew_ref, x1_ref, mg_ref, gpost_ref, o_ref, buf, sem, *, n):
    i = pl.program_id(0)
    slot = i % 2

    def row_copy(row, j, t, slot_):
        return pltpu.make_async_copy(ys_hbm.at[pl.ds(row, 1)], buf.at[slot_, j, pl.ds(t, 1)], sem.at[slot_])

    def gather(src_ref, slot_):
        for j in range(TOP_K):
            def body(t, carry, j=j):
                row_copy(src_ref[0, 0, j * COMB_TB + t], j, t, slot_).start()
                return carry
            lax.fori_loop(0, COMB_TB, body, 0, unroll=8)

    @pl.when(i == 0)
    def _():
        gather(pos_ref, 0)

    @pl.when(i + 1 < n)
    def _():
        gather(posn_ref, 1 - slot)

    for j in range(TOP_K):
        pltpu.make_async_copy(ys_hbm.at[pl.ds(0, COMB_TB)], buf.at[slot, j], sem.at[slot]).wait()
    ew = ew_ref[...]
    f = sum(ew[:, j:j + 1] * buf[slot, j] for j in range(TOP_K))
    o_ref[...] = x1_ref[...] + mg_ref[0] * _rms(f, gpost_ref[...])


def _combine(ys, pos, ew, x1, mod_gate, p):
    b, s, d = x1.shape
    n_tok = b * s
    n_tiles = n_tok // COMB_TB
    per_batch = mod_gate.shape[0] > 1
    if per_batch:
        mod_gate = jnp.repeat(mod_gate, s // COMB_TB, axis=0)
    mod_map = (lambda i: (i, 0, 0)) if per_batch else (lambda i: (0, 0, 0))
    pos_t = pos.reshape(n_tiles, COMB_TB, TOP_K).transpose(0, 2, 1).reshape(n_tiles, 1, TOP_K * COMB_TB)
    smem_tile = lambda f: pl.BlockSpec((1, 1, TOP_K * COMB_TB), f, memory_space=pltpu.SMEM)
    out = pl.pallas_call(
        functools.partial(_combine_kernel, n=n_tiles),
        grid=(n_tiles,),
        in_specs=[smem_tile(lambda i: (i, 0, 0)),
                  smem_tile(lambda i: (jnp.minimum(i + 1, n_tiles - 1), 0, 0)),
                  pl.BlockSpec(memory_space=pl.ANY),
                  pl.BlockSpec((COMB_TB, LANES), lambda i: (i, 0)),
                  pl.BlockSpec((COMB_TB, d), lambda i: (i, 0)),
                  pl.BlockSpec((1, 1, d), mod_map),
                  pl.BlockSpec((1, d), lambda i: (0, 0))],
        out_specs=pl.BlockSpec((COMB_TB, d), lambda i: (i, 0)),
        out_shape=jax.ShapeDtypeStruct((n_tok, d), F32),
        scratch_shapes=[pltpu.VMEM((2, TOP_K, COMB_TB, d), F32),
                        pltpu.SemaphoreType.DMA((2,))],
        compiler_params=pltpu.CompilerParams(
            dimension_semantics=("arbitrary",), vmem_limit_bytes=VMEM_LIMIT),
        name="moe_combine",
    )(pos_t, pos_t, ys, ew.reshape(n_tok, LANES), x1.reshape(n_tok, d), mod_gate, p['g_post_ffn'])
    return out.reshape(b, s, d)


def _rope_tables(n_tokens):
    rows = n_tokens // GRID_W
    row = jnp.repeat(jnp.arange(rows, dtype=F32), GRID_W)
    col = jnp.tile(jnp.arange(GRID_W, dtype=F32), rows)
    inv = ROPE_THETA ** (-jnp.arange(ROPE_FREQS, dtype=F32) / ROPE_FREQS)
    ang_r = row[:, None] * inv[None, :]
    ang_c = col[:, None] * inv[None, :]
    ang = jnp.concatenate([ang_r, ang_r, ang_c, ang_c] * 2, axis=-1)
    return jnp.cos(ang), jnp.sin(ang)


def _stream(x, mods, proj_w, lam, p, rope, ctx, s0, cnt0, tm_proj, tm_tok):
    b, s, d = x.shape
    xf = x if mods[0].shape[0] > 1 else x.reshape(1, b * s, d)
    proj = _norm_proj(xf, p['g_pre_mix'], mods[1], mods[0], proj_w, tm_proj, 768)
    proj = proj.reshape(b, s, C_IN)
    oa = _diff_attention(proj, lam, p['g_subln'], rope, ctx)
    r, v, kkn, kd, ba, lw, bonus, gate = _rwkv_prep(proj, p, min(s, 256))
    y0, sf0 = _rwkv_scan(r, v, kkn, kd, ba, lw, s0, 0)
    y1, sf1 = _rwkv_scan(r, v, kkn, kd, ba, lw, s0, 1)
    tb, tsq = xf.shape[0], xf.shape[1]
    x1, h2, route, ew, cnt = _merge_out(
        *(a.reshape(tb, tsq, a.shape[-1]) for a in (x, oa, y0, y1, bonus, gate, proj)),
        (mods[2], mods[4], mods[3]), p, cnt0, tm_tok)
    n = b * s
    return (x1.reshape(b, s, d), h2.reshape(n, d), route.reshape(n, LANES), ew.reshape(n, LANES), cnt,
            proj, jnp.stack([sf0, sf1], axis=1))


def kernel(x_prompt, x_sample, cache_k, cache_v, state_rwkv, c, c_ctx, w_ada, b_ada, g_pre_mix, g_post_mix, g_pre_ffn, g_post_ffn, w_in, mu_prev, mu_next, lam, g_subln, k_k, k_a, r_k, w0, w_up, a0, a_up, g_up, gn_w, gn_b, w_out, w_router, b_router, w_gate_up, b_gate_up, w_down, b_down):
    l = 0
    d = D_MODEL
    bp, sp, _ = x_prompt.shape
    bs, ss, _ = x_sample.shape

    n_cond = 1 + bs
    rows = -(-n_cond // SUBLANES) * SUBLANES
    cond = jnp.concatenate([c_ctx[None, :], c, jnp.zeros((rows - n_cond, d), F32)], axis=0)
    mod = _modulation(cond, w_ada[l], b_ada[l][None, :])
    mods_p = [mod[0:1, i * d:(i + 1) * d].reshape(1, 1, d) for i in range(N_MOD)]
    mods_s = [mod[1:1 + bs, i * d:(i + 1) * d].reshape(bs, 1, d) for i in range(N_MOD)]

    w = w_in[l]
    o_rwkv = 3 * d
    o_gate = o_rwkv + 3 * d + C_LORA
    w_perm = jnp.concatenate([w[:, :3 * d], w[:, o_gate:o_gate + 2 * d],
                              w[:, o_rwkv:o_rwkv + 3 * d], w[:, o_rwkv + 3 * d:o_gate]],
                             axis=1).astype(BF16)

    lq = lam[l]
    lam_val = (jnp.exp(jnp.sum(lq[0] * lq[1])) - jnp.exp(jnp.sum(lq[2] * lq[3])) + LAM_INIT).reshape(1)

    head = jnp.arange(d) // N_B
    wr = jnp.concatenate([w_router[l], jnp.zeros((d, LANES - N_EXPERTS), F32)], axis=1)
    br = jnp.concatenate([b_router[l], jnp.full((LANES - N_EXPERTS,), -jnp.inf, F32)])[None, :]
    mup, mun = mu_prev[l][None, :], mu_next[l][None, :]
    p = {
        'g_pre_mix': g_pre_mix[l][None, :], 'g_post_mix': g_post_mix[l][None, :],
        'g_pre_ffn': g_pre_ffn[l][None, :], 'g_post_ffn': g_post_ffn[l][None, :],
        'g_subln': g_subln[l][None, :],
        'mu_prev_main': mup[:, :3 * d], 'mu_next_main': mun[:, :3 * d],
        'mu_prev_lora': mup[:, 3 * d:], 'mu_next_lora': mun[:, 3 * d:],
        'k_k': k_k[l][None, :], 'k_a': k_a[l][None, :], 'r_k': r_k[l].reshape(1, d),
        'w0': w0[l].reshape(1, 2 * d),
        'w_up': jnp.concatenate([w_up[l, 0], w_up[l, 1]], axis=1).astype(BF16),
        'a0': a0[l].reshape(1, 2 * d),
        'a_up': jnp.concatenate([a_up[l, 0], a_up[l, 1]], axis=1).astype(BF16),
        'g_up': g_up[l].astype(BF16),
        'head_ones': (head[:, None] == head[None, :]).astype(BF16),
        'gn_w': gn_w[l][None, :], 'gn_b': gn_b[l][None, :],
        'w_out': w_out[l].astype(BF16),
        'w_router': wr, 'b_router': br,
        'w_gate_up': w_gate_up[l], 'b_gate_up': b_gate_up[l][:, None, :],
        'w_down': w_down[l], 'b_down': b_down[l][:, None, :],
    }

    rope = _rope_tables(ss)
    ctx = (cache_k[:, l].reshape(bs, -1, d), cache_v[:, l].reshape(bs, -1, d))

    cnt0 = jnp.zeros((1, LANES), F32)
    x1p, h2p, route_p, ewp, cnt_p, proj_p, sfin = _stream(
        x_prompt, mods_p, w_perm, lam_val, p, None, None, None, cnt0, 1024, 512)
    x1s, h2s, route_s, ews, cnt_s, _, _ = _stream(
        x_sample, mods_s, w_perm, lam_val, p, rope, ctx, state_rwkv[:, l], cnt_p, 1024, 512)

    n_p, n_s = bp * sp, bs * ss
    n_tok = n_p + n_s
    route = jnp.concatenate([route_p[:, :2 * TOP_K], route_s[:, :2 * TOP_K]], axis=0)
    counts = cnt_s[0, :N_EXPERTS].astype(jnp.int32)
    n_tiles = -(-(n_tok * TOP_K + N_EXPERTS * (MOE_TM - 1)) // MOE_TM)
    pos, tile, te, tv, tf, fill = _route_metadata(route[:, :TOP_K], route[:, TOP_K:], counts, n_tiles)
    xs = _dispatch(pos, fill, h2p, h2s, n_tiles)
    rows = _experts(xs, tile, te, tv, tf, p, n_tiles)
    yp = _combine(rows, pos[:n_p], ewp, x1p, mods_p[5], p)
    ys = _combine(rows, pos[n_p:], ews, x1s, mods_s[5], p)

    new_k = proj_p[:, :, COL_K:COL_K + d].reshape(bp, 1, sp, H_A, 2, HD_A)
    new_v = proj_p[:, :, COL_V:COL_V + d].reshape(bp, 1, sp, H_A, DV_A)
    return (yp, ys, new_k, new_v, sfin[:, None])
```

```python
import functools
import math

import jax
import jax.numpy as jnp
from jax import lax
from jax.experimental import pallas as pl
from jax.experimental.pallas import tpu as pltpu

F32 = jnp.float32
BF16 = jnp.bfloat16

D_MODEL = 1024
GRID_W = 64
HD_A = 64
DV_A = 2 * HD_A
H_A = D_MODEL // DV_A
N_B = 64
H_B = D_MODEL // N_B
LORA_W = 64
LORA_A = 64
LORA_G = 128
N_EXPERTS = 32
TOP_K = 4
SWIGLU_LIMIT = 7.0
SWIGLU_ALPHA = 1.702
ROPE_THETA = 10000.0
ROPE_FREQS = HD_A // 4
NORM_EPS = 1e-6
GN_EPS = 64e-5
L2_EPS = 1e-12
ATTN_SCALE = HD_A ** -0.5
N_MOD = 6
LAM_INIT = 0.8 - 0.6 * math.exp(-0.3 * 0)
DECAY_SCALE = -math.exp(-0.5)

LANES = 128
SUBLANES = 8
CHUNK = 64
MOE_TM = 256
COMB_TB = 256
ATT_SUB = 256
LOG2_E = math.log2(math.e)
VMEM_LIMIT = 56 * 1024 * 1024

COL_Q, COL_K, COL_V, COL_G0, COL_G1, COL_R, COL_KR, COL_VR, COL_LORA = (
    0, 1024, 2048, 3072, 4096, 5120, 6144, 7168, 8192)
C_LORA = LORA_W + LORA_A + LORA_G
C_IN = COL_LORA + C_LORA


def _sigmoid(x):
    return 1.0 / (1.0 + jnp.exp(-x))


def _dot(a, b):
    return jnp.dot(a.astype(BF16), b.astype(BF16), preferred_element_type=F32)


def _dot_nt(a, b):
    return lax.dot_general(a.astype(BF16), b.astype(BF16), (((1,), (1,)), ((), ())),
                           preferred_element_type=F32)


def _dot_tn(a, b):
    return lax.dot_general(a.astype(BF16), b.astype(BF16), (((0,), (0,)), ((), ())),
                           preferred_element_type=F32)


def _split3(x):
    hi = x.astype(BF16)
    r1 = x - hi.astype(F32)
    mid = r1.astype(BF16)
    lo = (r1 - mid.astype(F32)).astype(BF16)
    return hi, mid, lo


def _headsum(x, ind):
    e, et = ind
    hi = x.astype(BF16)
    lo = (x - hi.astype(F32)).astype(BF16)
    s = jnp.dot(hi, e, preferred_element_type=F32) + jnp.dot(lo, e, preferred_element_type=F32)
    s_hi = s.astype(BF16)
    s_lo = (s - s_hi.astype(F32)).astype(BF16)
    return jnp.dot(s_hi, et, preferred_element_type=F32) + jnp.dot(s_lo, et, preferred_element_type=F32)


def _rms(x, g):
    return x * lax.rsqrt(jnp.mean(x * x, axis=-1, keepdims=True) + NORM_EPS) * g


def _mod_kernel(c_ref, w_ref, b_ref, o_ref):
    c = c_ref[...]
    s = c * _sigmoid(c)
    o_ref[...] = _dot(s, w_ref[...]) + b_ref[...]


def _modulation(cond, w_ada, b_ada):
    rows, d = cond.shape
    n = w_ada.shape[1]
    tn = 768
    return pl.pallas_call(
        _mod_kernel,
        grid=(n // tn,),
        in_specs=[pl.BlockSpec((rows, d), lambda j: (0, 0)),
                  pl.BlockSpec((d, tn), lambda j: (0, j)),
                  pl.BlockSpec((1, tn), lambda j: (0, j))],
        out_specs=pl.BlockSpec((rows, tn), lambda j: (0, j)),
        out_shape=jax.ShapeDtypeStruct((rows, n), F32),
        name="modulation",
    )(cond, w_ada, b_ada)


def _proj_kernel(x_ref, g_ref, sc_ref, sh_ref, w_ref, o_ref, h_scr):
    @pl.when(pl.program_id(2) == 0)
    def _():
        h = _rms(x_ref[0], g_ref[...]) * (1.0 + sc_ref[0]) + sh_ref[0]
        h_scr[...] = h.astype(BF16)

    o_ref[0] = jnp.dot(h_scr[...], w_ref[...], preferred_element_type=F32)


def _norm_proj(x, g, scale, shift, w, tm, tn):
    b, s, d = x.shape
    n = w.shape[1]
    per_batch = scale.shape[0] > 1
    mod_map = (lambda bi, i, j: (bi, 0, 0)) if per_batch else (lambda bi, i, j: (0, 0, 0))
    return pl.pallas_call(
        _proj_kernel,
        grid=(b, s // tm, n // tn),
        in_specs=[pl.BlockSpec((1, tm, d), lambda bi, i, j: (bi, i, 0)),
                  pl.BlockSpec((1, d), lambda bi, i, j: (0, 0)),
                  pl.BlockSpec((1, 1, d), mod_map),
                  pl.BlockSpec((1, 1, d), mod_map),
                  pl.BlockSpec((d, tn), lambda bi, i, j: (0, j))],
        out_specs=pl.BlockSpec((1, tm, tn), lambda bi, i, j: (bi, i, j)),
        out_shape=jax.ShapeDtypeStruct((b, s, n), F32),
        scratch_shapes=[pltpu.VMEM((tm, d), BF16)],
        compiler_params=pltpu.CompilerParams(
            dimension_semantics=("arbitrary", "arbitrary", "arbitrary"),
            vmem_limit_bytes=VMEM_LIMIT),
        name="norm_proj",
    )(x, g, scale, shift, w)


def _rope(x, cos, sin):
    lane = lax.broadcasted_iota(jnp.int32, x.shape, 1)
    even = (lane // ROPE_FREQS) % 2 == 0
    rot = jnp.where(even, -pltpu.roll(x, LANES - ROPE_FREQS, 1), pltpu.roll(x, ROPE_FREQS, 1))
    return x * cos + rot * sin


def _attn_kernel(*refs, use_rope, use_ctx, s_new):
    it = iter(refs)
    lam_ref = next(it)
    q_ref, k_ref, v_ref = next(it), next(it), next(it)
    if use_rope:
        cq_ref, sq_ref, ck_ref, sk_ref = next(it), next(it), next(it), next(it)
    if use_ctx:
        ctxk_ref, ctxv_ref = next(it), next(it)
    g_ref = next(it)
    o_ref = next(it)
    k_scr, vt_scr = next(it), next(it)

    @pl.when(pl.program_id(2) == 0)
    def _():
        k = k_ref[0]
        if use_rope:
            k = _rope(k, ck_ref[...], sk_ref[...])
        k_scr[0:s_new, :] = k.astype(BF16)
        vt_scr[:, 0:s_new] = v_ref[0].T.astype(BF16)
        if use_ctx:
            k_scr[s_new:, :] = ctxk_ref[0].astype(BF16)
            vt_scr[:, s_new:] = ctxv_ref[0].T.astype(BF16)

    lam = lam_ref[0]
    q = q_ref[0]
    if use_rope:
        q = _rope(q, cq_ref[...], sq_ref[...])
    q = q * ATTN_SCALE
    n_sub = q.shape[0] // ATT_SUB
    sts = []
    lane = lax.broadcasted_iota(jnp.int32, (ATT_SUB, DV_A), 1)
    for t in range(n_sub):
        rows = slice(t * ATT_SUB, (t + 1) * ATT_SUB)
        for m in range(2):
            in_map = (lane >= HD_A) if m else (lane < HD_A)
            qm = jnp.where(in_map, q[rows], 0.0) * LOG2_E
            sts.append(_dot_nt(k_scr[...], qm))
    outs = []
    for st in sts:
        et = jnp.exp2(st - jnp.max(st, axis=0, keepdims=True))
        l = jnp.sum(et, axis=0, keepdims=True)
        ot = jnp.dot(vt_scr[...], et.astype(BF16), preferred_element_type=F32)
        outs.append(ot / l)
    for t in range(n_sub):
        o = (outs[2 * t] - lam * outs[2 * t + 1]).T
        o = o * lax.rsqrt(jnp.mean(o * o, axis=-1, keepdims=True) + NORM_EPS)
        o_ref[0, t * ATT_SUB:(t + 1) * ATT_SUB, :] = o * g_ref[...] * (1.0 - LAM_INIT)


def _diff_attention(proj, lam, g_subln, rope=None, ctx=None, tq=256):
    b, s, _ = proj.shape
    use_rope, use_ctx = rope is not None, ctx is not None
    s_tot = s + (ctx[0].shape[1] if use_ctx else 0)
    qmap = lambda bi, h, i: (bi, i, COL_Q // LANES + h)
    in_specs = [pl.BlockSpec(memory_space=pltpu.SMEM),
                pl.BlockSpec((1, tq, DV_A), qmap),
                pl.BlockSpec((1, s, DV_A), lambda bi, h, i: (bi, 0, COL_K // LANES + h)),
                pl.BlockSpec((1, s, DV_A), lambda bi, h, i: (bi, 0, COL_V // LANES + h))]
    args = [lam, proj, proj, proj]
    if use_rope:
        cos, sin = rope
        in_specs += [pl.BlockSpec((tq, DV_A), lambda bi, h, i: (i, 0)),
                     pl.BlockSpec((tq, DV_A), lambda bi, h, i: (i, 0)),
                     pl.BlockSpec((s, DV_A), lambda bi, h, i: (0, 0)),
                     pl.BlockSpec((s, DV_A), lambda bi, h, i: (0, 0))]
        args += [cos, sin, cos, sin]
    if use_ctx:
        p = ctx[0].shape[1]
        in_specs += [pl.BlockSpec((1, p, DV_A), lambda bi, h, i: (bi, 0, h)),
                     pl.BlockSpec((1, p, DV_A), lambda bi, h, i: (bi, 0, h))]
        args += [ctx[0], ctx[1]]
    in_specs.append(pl.BlockSpec((1, DV_A), lambda bi, h, i: (0, h)))
    args.append(g_subln)
    return pl.pallas_call(
        functools.partial(_attn_kernel, use_rope=use_rope, use_ctx=use_ctx, s_new=s),
        grid=(b, H_A, s // tq),
        in_specs=in_specs,
        out_specs=pl.BlockSpec((1, tq, DV_A), lambda bi, h, i: (bi, i, h)),
        out_shape=jax.ShapeDtypeStruct((b, s, D_MODEL), F32),
        scratch_shapes=[pltpu.VMEM((s_tot, DV_A), BF16), pltpu.VMEM((DV_A, s_tot), BF16)],
        compiler_params=pltpu.CompilerParams(
            dimension_semantics=("arbitrary", "arbitrary", "arbitrary"),
            vmem_limit_bytes=VMEM_LIMIT),
        name="diff_attention",
    )(*args)


def _shifted(x_ref, p_ref, n_ref, mup, mun, first, last):
    x = x_ref[0]
    ts = x.shape[0]
    row = lax.broadcasted_iota(jnp.int32, x.shape, 0)
    prev_row = p_ref[0][SUBLANES - 1:SUBLANES, :] * first
    next_row = n_ref[0][0:1, :] * last
    prev = jnp.where(row == 0, prev_row, pltpu.roll(x, 1, 0))
    nxt = jnp.where(row == ts - 1, next_row, pltpu.roll(x, ts - 1, 0))
    return x + mup * (prev - x) + mun * (nxt - x)


def _prep_kernel(r_ref, rp_ref, rn_ref, k_ref, kp_ref, kn_ref, v_ref, vp_ref, vn_ref,
                 l_ref, lp_ref, ln_ref, mup_ref, mun_ref, mupl_ref, munl_ref,
                 kk_ref, ka_ref, rk_ref, w0_ref, wup_ref, a0_ref, aup_ref, gup_ref, e_ref, et_ref,
                 ro_ref, vo_ref, kko_ref, kd_ref, ba_ref, lw_ref, bonus_ref, gate_ref, *, n_tiles):
    i = pl.program_id(1)
    first = (i > 0).astype(F32)
    last = (i < n_tiles - 1).astype(F32)
    mup, mun = mup_ref[...], mun_ref[...]
    d = D_MODEL
    r = _shifted(r_ref, rp_ref, rn_ref, mup[:, 0:d], mun[:, 0:d], first, last)
    k = _shifted(k_ref, kp_ref, kn_ref, mup[:, d:2 * d], mun[:, d:2 * d], first, last)
    v = _shifted(v_ref, vp_ref, vn_ref, mup[:, 2 * d:3 * d], mun[:, 2 * d:3 * d], first, last)
    lo = _shifted(l_ref, lp_ref, ln_ref, mupl_ref[...], munl_ref[...], first, last)
    xw = lo[:, 0:LORA_W]
    xa = lo[:, LORA_W:LORA_W + LORA_A]
    xg = lo[:, LORA_W + LORA_A:]
    g = (e_ref[...], et_ref[...])

    kk = k * kk_ref[...]
    nrm = jnp.sqrt(_headsum(kk * kk, g))
    kkn = kk / jnp.maximum(nrm, L2_EPS)
    ro_ref[0] = r
    vo_ref[0] = v
    kko_ref[0] = kkn
    gate_ref[0] = _dot(_sigmoid(xg), gup_ref[...])

    wlog = w0_ref[...] + _dot(jnp.tanh(xw), wup_ref[...])
    alog = a0_ref[...] + _dot(xa, aup_ref[...])
    ka = ka_ref[...]
    rrk = r * rk_ref[...]
    dots = None
    for dr in range(2):
        sl = slice(dr * d, (dr + 1) * d)
        lw_ref[dr, 0] = DECAY_SCALE * _sigmoid(wlog[:, sl])
        a = _sigmoid(alog[:, sl])
        kd = k * (1.0 + (a - 1.0) * ka)
        kd_ref[dr, 0] = kd
        ba_ref[dr, 0] = kkn * a
        t = rrk * kd
        dots = t if dots is None else dots + t
    bonus_ref[0] = _headsum(dots, g) * v


def _rwkv_prep(proj, p, ts):
    b, s, _ = proj.shape
    d = D_MODEL
    nt = s // ts
    hb = ts // SUBLANES
    nhb = s // SUBLANES

    def main(col, w):
        return pl.BlockSpec((1, ts, w), lambda bi, i: (bi, i, col // w))

    def prev(col, w):
        return pl.BlockSpec((1, SUBLANES, w), lambda bi, i: (bi, jnp.maximum(i * hb - 1, 0), col // w))

    def nxt(col, w):
        return pl.BlockSpec((1, SUBLANES, w),
                            lambda bi, i: (bi, jnp.minimum((i + 1) * hb, nhb - 1), col // w))

    def full(a):
        return pl.BlockSpec(a.shape, lambda bi, i: (0,) * a.ndim)

    in_specs, args = [], []
    for col, w in ((COL_R, d), (COL_KR, d), (COL_VR, d), (COL_LORA, C_LORA)):
        in_specs += [main(col, w), prev(col, w), nxt(col, w)]
        args += [proj, proj, proj]
    consts = [p['mu_prev_main'], p['mu_next_main'], p['mu_prev_lora'], p['mu_next_lora'],
              p['k_k'], p['k_a'], p['r_k'], p['w0'], p['w_up'], p['a0'], p['a_up'], p['g_up'],
              p['head_ind'], p['head_ind_t']]
    in_specs += [full(a) for a in consts]
    args += consts
    tok = pl.BlockSpec((1, ts, d), lambda bi, i: (bi, i, 0))
    tok2 = pl.BlockSpec((2, 1, ts, d), lambda bi, i: (0, bi, i, 0))
    one = jax.ShapeDtypeStruct((b, s, d), F32)
    two = jax.ShapeDtypeStruct((2, b, s, d), F32)
    return pl.pallas_call(
        functools.partial(_prep_kernel, n_tiles=nt),
        grid=(b, nt),
        in_specs=in_specs,
        out_specs=[tok, tok, tok, tok2, tok2, tok2, tok, tok],
        out_shape=[one, one, one, two, two, two, one, one],
        compiler_params=pltpu.CompilerParams(
            dimension_semantics=("arbitrary", "arbitrary"),
            vmem_limit_bytes=VMEM_LIMIT),
        name="rwkv_prep",
    )(*args)


def _scan_kernel(*refs, reverse, has_s0, n_chunks):
    it = iter(refs)
    r_ref, v_ref, kk_ref, kd_ref, ba_ref, lw_ref = (next(it) for _ in range(6))
    s0_ref = next(it) if has_s0 else None
    y_ref, sf_ref = next(it), next(it)
    s_scr = next(it)
    c = pl.program_id(1)

    @pl.when(c == 0)
    def _():
        if has_s0:
            s_scr[...] = s0_ref[0]
        else:
            s_scr[...] = jnp.zeros_like(s_scr)

    n = CHUNK
    tt = lax.broadcasted_iota(jnp.int32, (n, n), 0)
    ss = lax.broadcasted_iota(jnp.int32, (n, n), 1)
    hi, lo = (ss, tt) if reverse else (tt, ss)
    strict = hi > lo
    incl = hi >= lo
    eye = (tt == ss).astype(F32)

    lw = lw_ref[0, 0]
    tri = jnp.where(incl, 1.0, 0.0).astype(BF16)
    cum = sum(jnp.dot(tri, part, preferred_element_type=F32) for part in _split3(lw))
    tot = cum[0:1, :] if reverse else cum[n - 1:n, :]
    r, v, kkn = r_ref[0], v_ref[0], kk_ref[0]
    kd, ba = kd_ref[0, 0], ba_ref[0, 0]
    g_in = jnp.exp(cum)
    g_inv = jnp.exp(-cum)
    g_rest = jnp.exp(tot - cum)
    a_t = (kkn * jnp.exp(cum - lw)).astype(BF16)
    r_t = (r * g_in).astype(BF16)
    b_t = (ba * g_inv).astype(BF16)
    k_t = (kd * g_inv).astype(BF16)
    b_h = (ba * g_rest).astype(BF16)
    k_h = (kd * g_rest).astype(BF16)
    g_tot = jnp.exp(tot)
    vb = v.astype(BF16)

    heads = range(H_B)
    sls = [slice(h * N_B, (h + 1) * N_B) for h in heads]
    s_old = [s_scr[h] for h in heads]
    ar = [jnp.concatenate([a_t[:, sl], r_t[:, sl]], axis=0) for sl in sls]
    bk = [jnp.concatenate([b_t[:, sl], k_t[:, sl]], axis=0) for sl in sls]
    bkh = [jnp.concatenate([b_h[:, sl], k_h[:, sl]], axis=0) for sl in sls]
    vh = [vb[:, sl] for sl in sls]
    tt2 = lax.broadcasted_iota(jnp.int32, (n, 2 * n), 0)
    col2 = lax.broadcasted_iota(jnp.int32, (n, 2 * n), 1)
    ss2 = col2 % n
    hi2, lo2 = (ss2, tt2) if reverse else (tt2, ss2)
    strict_r = (hi2 > lo2) & (col2 >= n)
    incl2 = hi2 >= lo2

    gram = [_dot_nt(ar[h], bk[h]) for h in heads]
    ars = [_dot_nt(ar[h], s_old[h]) for h in heads]
    nmat = [jnp.where(strict, gram[h][0:n, 0:n], 0.0) for h in heads]
    mak = [jnp.where(strict_r, gram[h][0:n, :], 0.0) for h in heads]
    pr = [jnp.where(incl2, gram[h][n:, :], 0.0) for h in heads]
    z = [ars[h][0:n] + _dot(mak[h], jnp.concatenate([vh[h], vh[h]], axis=0)) for h in heads]
    x = [eye - jnp.where((hi // 2 == lo // 2) & strict, nmat[h], 0.0) for h in heads]
    blk = 2
    while blk < n:
        m = (hi // (2 * blk) == lo // (2 * blk)) & ((hi // blk) % 2 == 1) & ((lo // blk) % 2 == 0)
        xl = [_dot(x[h], jnp.where(m, nmat[h], 0.0)) for h in heads]
        x = [x[h] - _dot(xl[h], x[h]) for h in heads]
        blk *= 2
    u = [-_dot(x[h], z[h]) for h in heads]
    uv = [jnp.concatenate([u[h].astype(BF16), vh[h]], axis=0) for h in heads]
    y = [ars[h][n:] + _dot(pr[h], uv[h]) for h in heads]
    s_new = [s_old[h] * g_tot[:, sls[h]] + _dot_tn(uv[h], bkh[h]) for h in heads]
    for h in heads:
        y_ref[0, :, sls[h]] = y[h]
    for h in heads:
        s_scr[h] = s_new[h]

    @pl.when(c == n_chunks - 1)
    def _():
        sf_ref[0] = s_scr[...]


def _rwkv_scan(r, v, kkn, kd, ba, lw, s0, dr):
    b, s, d = r.shape
    nc = s // CHUNK
    reverse = dr == 1
    tmap = (lambda bi, c: (bi, nc - 1 - c, 0)) if reverse else (lambda bi, c: (bi, c, 0))
    dmap = (lambda bi, c: (dr, bi, nc - 1 - c, 0)) if reverse else (lambda bi, c: (dr, bi, c, 0))
    tok = pl.BlockSpec((1, CHUNK, d), tmap)
    tok2 = pl.BlockSpec((1, 1, CHUNK, d), dmap)
    in_specs = [tok, tok, tok, tok2, tok2, tok2]
    args = [r, v, kkn, kd, ba, lw]
    if s0 is not None:
        in_specs.append(pl.BlockSpec((1, None, H_B, N_B, N_B), lambda bi, c: (bi, dr, 0, 0, 0)))
        args.append(s0)
    return pl.pallas_call(
        functools.partial(_scan_kernel, reverse=reverse, has_s0=s0 is not None, n_chunks=nc),
        grid=(b, nc),
        in_specs=in_specs,
        out_specs=[tok, pl.BlockSpec((1, H_B, N_B, N_B), lambda bi, c: (bi, 0, 0, 0))],
        out_shape=[jax.ShapeDtypeStruct((b, s, d), F32),
                   jax.ShapeDtypeStruct((b, H_B, N_B, N_B), F32)],
        scratch_shapes=[pltpu.VMEM((H_B, N_B, N_B), F32)],
        compiler_params=pltpu.CompilerParams(
            dimension_semantics=("arbitrary", "arbitrary"),
            vmem_limit_bytes=VMEM_LIMIT),
        name="rwkv_scan",
    )(*args)


def _merge_kernel(x_ref, oa_ref, y0_ref, y1_ref, bonus_ref, gate_ref, g0_ref, g1_ref,
                  mg_ref, sc_ref, sh_ref, gnw_ref, gnb_ref, e_ref, et_ref, wout_ref,
                  gpost_ref, gpre_ref, wr_ref, br_ref, cnt0_ref,
                  x1_ref, h2_ref, route_ref, rank_ref, ew_ref, cnt_ref, cnt_scr):
    @pl.when((pl.program_id(0) == 0) & (pl.program_id(1) == 0))
    def _():
        cnt_scr[...] = cnt0_ref[...]

    g = (e_ref[...], et_ref[...])
    y = y0_ref[0] + y1_ref[0]
    mu = _headsum(y, g) * (1.0 / N_B)
    yc = y - mu
    var = _headsum(yc * yc, g) * (1.0 / N_B)
    yn = yc * lax.rsqrt(var + GN_EPS) * gnw_ref[...] + gnb_ref[...]
    ob = (yn + bonus_ref[0]) * gate_ref[0]
    merged = _sigmoid(g0_ref[0]) * oa_ref[0] + _sigmoid(g1_ref[0]) * ob
    out = _dot(merged, wout_ref[...])
    x1 = x_ref[0] + mg_ref[0] * _rms(out, gpost_ref[...])
    x1_ref[0] = x1
    h2 = _rms(x1, gpre_ref[...]) * (1.0 + sc_ref[0]) + sh_ref[0]
    h2_ref[0] = h2

    h_hi = h2.astype(BF16)
    h_lo = (h2 - h_hi.astype(F32)).astype(BF16)
    wr = wr_ref[...]
    w_hi = wr.astype(BF16)
    w_lo = (wr - w_hi.astype(F32)).astype(BF16)
    logits = (jnp.dot(h_hi, w_hi, preferred_element_type=F32) + jnp.dot(h_hi, w_lo, preferred_element_type=F32)
              + jnp.dot(h_lo, w_hi, preferred_element_type=F32)) + br_ref[...]
    lane = lax.broadcasted_iota(jnp.int32, logits.shape, 1)
    work = logits
    top = None
    picks = []
    for _ in range(TOP_K):
        mx = jnp.max(work, axis=-1, keepdims=True)
        idx = jnp.min(jnp.where(work == mx, lane, LANES), axis=-1, keepdims=True)
        if top is None:
            top = mx
        picks.append((idx, jnp.exp(mx - top)))
        work = jnp.where(lane == idx, -jnp.inf, work)
    denom = sum(e for _, e in picks)
    tm = logits.shape[0]
    onehot = jnp.zeros_like(logits)
    for idx, _ in picks:
        onehot = onehot + jnp.where(lane == idx, 1.0, 0.0)
    rr = lax.broadcasted_iota(jnp.int32, (tm, tm), 0)
    cc = lax.broadcasted_iota(jnp.int32, (tm, tm), 1)
    before = cnt_scr[...] + _dot(jnp.where(rr > cc, 1.0, 0.0), onehot)
    route = jnp.zeros(logits.shape, jnp.int32)
    ew = jnp.zeros_like(logits)
    for j, (idx, e) in enumerate(picks):
        route = jnp.where(lane == j, idx, route)
        ew = jnp.where(lane == j, e / denom, ew)
    route_ref[0] = route
    rank_ref[0] = jnp.where(onehot > 0.0, before, 0.0).astype(jnp.int32)
    ew_ref[0] = ew
    cnt_scr[...] = cnt_scr[...] + jnp.sum(onehot, axis=0, keepdims=True)
    cnt_ref[...] = cnt_scr[...]


def _merge_out(x, oa, y0, y1, bonus, gate, proj, mods, p, cnt0, tm):
    b, s, d = x.shape
    per_batch = mods[0].shape[0] > 1
    mod_map = (lambda bi, i: (bi, 0, 0)) if per_batch else (lambda bi, i: (0, 0, 0))
    tok = pl.BlockSpec((1, tm, d), lambda bi, i: (bi, i, 0))

    def full(a):
        return pl.BlockSpec(a.shape, lambda bi, i: (0,) * a.ndim)

    consts = [p['gn_w'], p['gn_b'], p['head_ind'], p['head_ind_t'], p['w_out'], p['g_post_mix'], p['g_pre_ffn'],
              p['w_router'], p['b_router']]
    in_specs = ([tok] * 6
                + [pl.BlockSpec((1, tm, d), lambda bi, i: (bi, i, COL_G0 // d)),
                   pl.BlockSpec((1, tm, d), lambda bi, i: (bi, i, COL_G1 // d))]
                + [pl.BlockSpec((1, 1, d), mod_map)] * 3
                + [full(a) for a in consts] + [full(cnt0)])
    return pl.pallas_call(
        _merge_kernel,
        grid=(b, s // tm),
        in_specs=in_specs,
        out_specs=[tok, tok] + [pl.BlockSpec((1, tm, LANES), lambda bi, i: (bi, i, 0))] * 3
                  + [pl.BlockSpec((1, LANES), lambda bi, i: (0, 0))],
        out_shape=[jax.ShapeDtypeStruct((b, s, d), F32),
                   jax.ShapeDtypeStruct((b, s, d), F32),
                   jax.ShapeDtypeStruct((b, s, LANES), jnp.int32),
                   jax.ShapeDtypeStruct((b, s, LANES), jnp.int32),
                   jax.ShapeDtypeStruct((b, s, LANES), F32),
                   jax.ShapeDtypeStruct((1, LANES), F32)],
        scratch_shapes=[pltpu.VMEM((1, LANES), F32)],
        compiler_params=pltpu.CompilerParams(
            dimension_semantics=("arbitrary", "arbitrary"),
            vmem_limit_bytes=VMEM_LIMIT),
        name="merge_out",
    )(x, oa, y0, y1, bonus, gate, proj, proj, *mods, *consts, cnt0)


def _route_metadata(eid, rank, counts, n_tiles):
    padded = (counts + MOE_TM - 1) // MOE_TM * MOE_TM
    ends = jnp.cumsum(padded)
    offs = ends - padded
    experts = jnp.arange(N_EXPERTS, dtype=jnp.int32)
    pos = jnp.sum(jnp.where(eid[:, :, None] == experts, (offs + rank)[:, None, :], 0), axis=-1)
    idx = jnp.arange(n_tiles, dtype=jnp.int32)
    valid = idx * MOE_TM < ends[-1]
    tile = jnp.where(valid, idx, ends[-1] // MOE_TM - 1)
    te = jnp.sum((ends[None, :] <= (tile * MOE_TM)[:, None]).astype(jnp.int32), axis=1)
    first = valid & jnp.concatenate([jnp.ones((1,), bool), te[1:] != te[:-1]])
    used = counts > 0
    slot_of = (jnp.cumsum(used.astype(jnp.int32)) - 1) % 2
    later = (experts[None, :] > experts[:, None]) & used[None, :]
    next_of = jnp.min(jnp.where(later, experts[None, :], N_EXPERTS), axis=1)
    next_of = jnp.where(next_of == N_EXPERTS, -1, next_of)
    sched = (tile, te, valid.astype(jnp.int32), first.astype(jnp.int32), slot_of[te], next_of[te])
    fill = jnp.concatenate([jnp.where(padded > counts, ends - MOE_TM, -1),
                            jnp.where(valid, -1, idx * MOE_TM)[eid.size // MOE_TM:]])
    return pos, tuple(a.astype(jnp.int32) for a in sched), fill.astype(jnp.int32)


def _dispatch_kernel(fill_ref, pos_ref, hp_ref, hs_ref, xs_hbm, zbuf, sem, fsem, *, n_p_tiles, n_fill):
    i = pl.program_id(0)

    def fill_copy(e):
        start = pl.multiple_of(fill_ref[e], MOE_TM)
        return pltpu.make_async_copy(zbuf, xs_hbm.at[pl.ds(start, MOE_TM)], fsem)

    @pl.when(i == 0)
    def _():
        zbuf[...] = jnp.zeros_like(zbuf)
        for e in range(n_fill):
            @pl.when(fill_ref[e] >= 0)
            def _(e=e):
                fill_copy(e).start()
        for e in range(n_fill):
            @pl.when(fill_ref[e] >= 0)
            def _(e=e):
                fill_copy(e).wait()

    def scatter(h_ref):
        for j in range(TOP_K):
            def body(t, carry, j=j):
                pltpu.make_async_copy(h_ref.at[pl.ds(t, 1)],
                                      xs_hbm.at[pl.ds(pos_ref[0, 0, j * COMB_TB + t], 1)], sem).start()
                return carry
            lax.fori_loop(0, COMB_TB, body, 0, unroll=8)

    @pl.when(i < n_p_tiles)
    def _():
        scatter(hp_ref)

    @pl.when(i >= n_p_tiles)
    def _():
        scatter(hs_ref)

    for _ in range(TOP_K):
        pltpu.make_async_copy(hp_ref, xs_hbm.at[pl.ds(0, COMB_TB)], sem).wait()


def _dispatch(pos, fill, h2p, h2s, n_tiles):
    d = D_MODEL
    n_p, n_s = h2p.shape[0], h2s.shape[0]
    n = (n_p + n_s) // COMB_TB
    n_pt = n_p // COMB_TB
    pos_t = pos.reshape(n, COMB_TB, TOP_K).transpose(0, 2, 1).reshape(n, 1, TOP_K * COMB_TB)
    grid_spec = pltpu.PrefetchScalarGridSpec(
        num_scalar_prefetch=1,
        grid=(n,),
        in_specs=[pl.BlockSpec((1, 1, TOP_K * COMB_TB), lambda i, fill: (i, 0, 0), memory_space=pltpu.SMEM),
                  pl.BlockSpec((COMB_TB, d), lambda i, fill: (jnp.minimum(i, n_pt - 1), 0)),
                  pl.BlockSpec((COMB_TB, d), lambda i, fill: (jnp.maximum(i - n_pt, 0), 0))],
        out_specs=pl.BlockSpec(memory_space=pl.ANY),
        scratch_shapes=[pltpu.VMEM((MOE_TM, d), F32),
                        pltpu.SemaphoreType.DMA(()),
                        pltpu.SemaphoreType.DMA(())])
    return pl.pallas_call(
        functools.partial(_dispatch_kernel, n_p_tiles=n_pt, n_fill=fill.shape[0]),
        grid_spec=grid_spec,
        out_shape=jax.ShapeDtypeStruct((n_tiles * MOE_TM, d), F32),
        compiler_params=pltpu.CompilerParams(
            dimension_semantics=("arbitrary",), vmem_limit_bytes=VMEM_LIMIT),
        name="moe_dispatch",
    )(fill, pos_t, h2p, h2s)


def _experts_kernel(tile_ref, te_ref, tv_ref, tf_ref, ws_ref, nx_ref, x_ref,
                    wgu_hbm, bgu_ref, wd_hbm, bd_ref, ys_ref, wgu_buf, wd_buf, wgu_scr, wd_scr, sem):
    i = pl.program_id(0)

    def weight_copies(e, slot):
        return (pltpu.make_async_copy(wgu_hbm.at[e], wgu_buf.at[slot], sem.at[0, slot]),
                pltpu.make_async_copy(wd_hbm.at[e], wd_buf.at[slot], sem.at[1, slot]))

    @pl.when(i == 0)
    def _():
        for cp in weight_copies(te_ref[0], 0):
            cp.start()

    @pl.when(tf_ref[i] == 1)
    def _():
        slot = ws_ref[i]

        @pl.when(nx_ref[i] >= 0)
        def _():
            for cp in weight_copies(nx_ref[i], 1 - slot):
                cp.start()

        for cp in weight_copies(te_ref[i], slot):
            cp.wait()
        wgu_scr[...] = wgu_buf[slot].astype(BF16)
        wd_scr[...] = wd_buf[slot].astype(BF16)

    @pl.when(tv_ref[i] == 0)
    def _():
        ys_ref[...] = jnp.zeros_like(ys_ref)

    @pl.when(tv_ref[i] == 1)
    def _():
        gu = jnp.dot(x_ref[...].astype(BF16), wgu_scr[...], preferred_element_type=F32) + bgu_ref[0]
        gate = jnp.minimum(gu[:, :D_MODEL], SWIGLU_LIMIT)
        up = jnp.clip(gu[:, D_MODEL:], -SWIGLU_LIMIT, SWIGLU_LIMIT)
        act = (up + 1.0) * gate * _sigmoid(SWIGLU_ALPHA * gate)
        ys_ref[...] = jnp.dot(act.astype(BF16), wd_scr[...], preferred_element_type=F32) + bd_ref[0]


def _experts(xs, sched, p, n_tiles):
    d = D_MODEL
    bias = lambda w: pl.BlockSpec((1, 1, w), lambda i, tile, te, *_: (te[i], 0, 0))
    grid_spec = pltpu.PrefetchScalarGridSpec(
        num_scalar_prefetch=len(sched),
        grid=(n_tiles,),
        in_specs=[pl.BlockSpec((MOE_TM, d), lambda i, tile, *_: (tile[i], 0)),
                  pl.BlockSpec(memory_space=pl.ANY), bias(2 * d),
                  pl.BlockSpec(memory_space=pl.ANY), bias(d)],
        out_specs=pl.BlockSpec((MOE_TM, d), lambda i, *_: (i, 0)),
        scratch_shapes=[pltpu.VMEM((2, d, 2 * d), F32),
                        pltpu.VMEM((2, d, d), F32),
                        pltpu.VMEM((d, 2 * d), BF16),
                        pltpu.VMEM((d, d), BF16),
                        pltpu.SemaphoreType.DMA((2, 2))])
    return pl.pallas_call(
        _experts_kernel,
        grid_spec=grid_spec,
        out_shape=jax.ShapeDtypeStruct((n_tiles * MOE_TM, d), F32),
        compiler_params=pltpu.CompilerParams(
            dimension_semantics=("arbitrary",), vmem_limit_bytes=VMEM_LIMIT),
        name="moe_experts",
    )(*sched, xs, p['w_gate_up'], p['b_gate_up'], p['w_down'], p['b_down'])


def _combine_kernel(pos_ref, posn_ref, ys_hbm, ew_ref, x1_ref, mg_ref, gpost_ref, o_ref, buf, sem, *, n):
    i = pl.program_id(0)
    slot = i % 2

    def row_copy(row, j, t, slot_):
        return pltpu.make_async_copy(ys_hbm.at[pl.ds(row, 1)], buf.at[slot_, j, pl.ds(t, 1)], sem.at[slot_])

    def gather(src_ref, slot_):
        for j in range(TOP_K):
            def body(t, carry, j=j):
                row_copy(src_ref[0, 0, j * COMB_TB + t], j, t, slot_).start()
                return carry
            lax.fori_loop(0, COMB_TB, body, 0, unroll=8)

    @pl.when(i == 0)
    def _():
        gather(pos_ref, 0)

    @pl.when(i + 1 < n)
    def _():
        gather(posn_ref, 1 - slot)

    for j in range(TOP_K):
        pltpu.make_async_copy(ys_hbm.at[pl.ds(0, COMB_TB)], buf.at[slot, j], sem.at[slot]).wait()
    ew = ew_ref[...]
    f = sum(ew[:, j:j + 1] * buf[slot, j] for j in range(TOP_K))
    o_ref[...] = x1_ref[...] + mg_ref[0] * _rms(f, gpost_ref[...])


def _combine(ys, pos, ew, x1, mod_gate, p):
    b, s, d = x1.shape
    n_tok = b * s
    n_tiles = n_tok // COMB_TB
    per_batch = mod_gate.shape[0] > 1
    if per_batch:
        mod_gate = jnp.repeat(mod_gate, s // COMB_TB, axis=0)
    mod_map = (lambda i: (i, 0, 0)) if per_batch else (lambda i: (0, 0, 0))
    pos_t = pos.reshape(n_tiles, COMB_TB, TOP_K).transpose(0, 2, 1).reshape(n_tiles, 1, TOP_K * COMB_TB)
    smem_tile = lambda f: pl.BlockSpec((1, 1, TOP_K * COMB_TB), f, memory_space=pltpu.SMEM)
    out = pl.pallas_call(
        functools.partial(_combine_kernel, n=n_tiles),
        grid=(n_tiles,),
        in_specs=[smem_tile(lambda i: (i, 0, 0)),
                  smem_tile(lambda i: (jnp.minimum(i + 1, n_tiles - 1), 0, 0)),
                  pl.BlockSpec(memory_space=pl.ANY),
                  pl.BlockSpec((COMB_TB, LANES), lambda i: (i, 0)),
                  pl.BlockSpec((COMB_TB, d), lambda i: (i, 0)),
                  pl.BlockSpec((1, 1, d), mod_map),
                  pl.BlockSpec((1, d), lambda i: (0, 0))],
        out_specs=pl.BlockSpec((COMB_TB, d), lambda i: (i, 0)),
        out_shape=jax.ShapeDtypeStruct((n_tok, d), F32),
        scratch_shapes=[pltpu.VMEM((2, TOP_K, COMB_TB, d), F32),
                        pltpu.SemaphoreType.DMA((2,))],
        compiler_params=pltpu.CompilerParams(
            dimension_semantics=("arbitrary",), vmem_limit_bytes=VMEM_LIMIT),
        name="moe_combine",
    )(pos_t, pos_t, ys, ew.reshape(n_tok, LANES), x1.reshape(n_tok, d), mod_gate, p['g_post_ffn'])
    return out.reshape(b, s, d)


def _rope_tables(n_tokens):
    rows = n_tokens // GRID_W
    row = jnp.repeat(jnp.arange(rows, dtype=F32), GRID_W)
    col = jnp.tile(jnp.arange(GRID_W, dtype=F32), rows)
    inv = ROPE_THETA ** (-jnp.arange(ROPE_FREQS, dtype=F32) / ROPE_FREQS)
    ang_r = row[:, None] * inv[None, :]
    ang_c = col[:, None] * inv[None, :]
    ang = jnp.concatenate([ang_r, ang_r, ang_c, ang_c] * 2, axis=-1)
    return jnp.cos(ang), jnp.sin(ang)


def _stream(x, mods, proj_w, lam, p, rope, ctx, s0, cnt0, tm_proj, tm_tok):
    b, s, d = x.shape
    xf = x if mods[0].shape[0] > 1 else x.reshape(1, b * s, d)
    proj = _norm_proj(xf, p['g_pre_mix'], mods[1], mods[0], proj_w, tm_proj, 768)
    proj = proj.reshape(b, s, C_IN)
    oa = _diff_attention(proj, lam, p['g_subln'], rope, ctx, tq=min(s, 2 * ATT_SUB))
    r, v, kkn, kd, ba, lw, bonus, gate = _rwkv_prep(proj, p, min(s, 256))
    y0, sf0 = _rwkv_scan(r, v, kkn, kd, ba, lw, s0, 0)
    y1, sf1 = _rwkv_scan(r, v, kkn, kd, ba, lw, s0, 1)
    tb, tsq = xf.shape[0], xf.shape[1]
    x1, h2, route, rank, ew, cnt = _merge_out(
        *(a.reshape(tb, tsq, a.shape[-1]) for a in (x, oa, y0, y1, bonus, gate, proj)),
        (mods[2], mods[4], mods[3]), p, cnt0, tm_tok)
    n = b * s
    route = (route.reshape(n, LANES)[:, :TOP_K], rank.reshape(n, LANES)[:, :N_EXPERTS])
    return (x1.reshape(b, s, d), h2.reshape(n, d), route, ew.reshape(n, LANES), cnt,
            proj, jnp.stack([sf0, sf1], axis=1))


def kernel(x_prompt, x_sample, cache_k, cache_v, state_rwkv, c, c_ctx, w_ada, b_ada, g_pre_mix, g_post_mix, g_pre_ffn, g_post_ffn, w_in, mu_prev, mu_next, lam, g_subln, k_k, k_a, r_k, w0, w_up, a0, a_up, g_up, gn_w, gn_b, w_out, w_router, b_router, w_gate_up, b_gate_up, w_down, b_down):
    l = 0
    d = D_MODEL
    bp, sp, _ = x_prompt.shape
    bs, ss, _ = x_sample.shape

    n_cond = 1 + bs
    rows = -(-n_cond // SUBLANES) * SUBLANES
    cond = jnp.concatenate([c_ctx[None, :], c, jnp.zeros((rows - n_cond, d), F32)], axis=0)
    mod = _modulation(cond, w_ada[l], b_ada[l][None, :])
    mods_p = [mod[0:1, i * d:(i + 1) * d].reshape(1, 1, d) for i in range(N_MOD)]
    mods_s = [mod[1:1 + bs, i * d:(i + 1) * d].reshape(bs, 1, d) for i in range(N_MOD)]

    w = w_in[l]
    o_rwkv = 3 * d
    o_gate = o_rwkv + 3 * d + C_LORA
    w_perm = jnp.concatenate([w[:, :3 * d], w[:, o_gate:o_gate + 2 * d],
                              w[:, o_rwkv:o_rwkv + 3 * d], w[:, o_rwkv + 3 * d:o_gate]],
                             axis=1).astype(BF16)

    lq = lam[l]
    lam_val = (jnp.exp(jnp.sum(lq[0] * lq[1])) - jnp.exp(jnp.sum(lq[2] * lq[3])) + LAM_INIT).reshape(1)

    head = jnp.arange(d) // N_B
    wr = jnp.concatenate([w_router[l], jnp.zeros((d, LANES - N_EXPERTS), F32)], axis=1)
    br = jnp.concatenate([b_router[l], jnp.full((LANES - N_EXPERTS,), -jnp.inf, F32)])[None, :]
    mup, mun = mu_prev[l][None, :], mu_next[l][None, :]
    p = {
        'g_pre_mix': g_pre_mix[l][None, :], 'g_post_mix': g_post_mix[l][None, :],
        'g_pre_ffn': g_pre_ffn[l][None, :], 'g_post_ffn': g_post_ffn[l][None, :],
        'g_subln': g_subln[l][None, :],
        'mu_prev_main': mup[:, :3 * d], 'mu_next_main': mun[:, :3 * d],
        'mu_prev_lora': mup[:, 3 * d:], 'mu_next_lora': mun[:, 3 * d:],
        'k_k': k_k[l][None, :], 'k_a': k_a[l][None, :], 'r_k': r_k[l].reshape(1, d),
        'w0': w0[l].reshape(1, 2 * d),
        'w_up': jnp.concatenate([w_up[l, 0], w_up[l, 1]], axis=1).astype(BF16),
        'a0': a0[l].reshape(1, 2 * d),
        'a_up': jnp.concatenate([a_up[l, 0], a_up[l, 1]], axis=1).astype(BF16),
        'g_up': g_up[l].astype(BF16),
        'head_ind': (head[:, None] == jnp.arange(LANES)[None, :]).astype(BF16),
        'head_ind_t': (jnp.arange(LANES)[:, None] == head[None, :]).astype(BF16),
        'gn_w': gn_w[l][None, :], 'gn_b': gn_b[l][None, :],
        'w_out': w_out[l].astype(BF16),
        'w_router': wr, 'b_router': br,
        'w_gate_up': w_gate_up[l], 'b_gate_up': b_gate_up[l][:, None, :],
        'w_down': w_down[l], 'b_down': b_down[l][:, None, :],
    }

    rope = _rope_tables(ss)
    ctx = (cache_k[:, l].reshape(bs, -1, d), cache_v[:, l].reshape(bs, -1, d))

    cnt0 = jnp.zeros((1, LANES), F32)
    x1p, h2p, route_p, ewp, cnt_p, proj_p, sfin = _stream(
        x_prompt, mods_p, w_perm, lam_val, p, None, None, None, cnt0, 1024, 512)
    x1s, h2s, route_s, ews, cnt_s, _, _ = _stream(
        x_sample, mods_s, w_perm, lam_val, p, rope, ctx, state_rwkv[:, l], cnt_p, 1024, 512)

    n_p, n_s = bp * sp, bs * ss
    n_tok = n_p + n_s
    eid, rank = (jnp.concatenate([a, b_], axis=0) for a, b_ in zip(route_p, route_s))
    counts = cnt_s[0, :N_EXPERTS].astype(jnp.int32)
    n_tiles = -(-(n_tok * TOP_K + N_EXPERTS * (MOE_TM - 1)) // MOE_TM)
    pos, sched, fill = _route_metadata(eid, rank, counts, n_tiles)
    xs = _dispatch(pos, fill, h2p, h2s, n_tiles)
    rows = _experts(xs, sched, p, n_tiles)
    yp = _combine(rows, pos[:n_p], ewp, x1p, mods_p[5], p)
    ys = _combine(rows, pos[n_p:], ews, x1s, mods_s[5], p)

    new_k = proj_p[:, :, COL_K:COL_K + d].reshape(bp, 1, sp, H_A, 2, HD_A)
    new_v = proj_p[:, :, COL_V:COL_V + d].reshape(bp, 1, sp, H_A, DV_A)
    return (yp, ys, new_k, new_v, sfin[:, None])
```

```python
import functools
import math

import jax
import jax.numpy as jnp
from jax import lax
from jax.experimental import pallas as pl
from jax.experimental.pallas import tpu as pltpu

F32 = jnp.float32
BF16 = jnp.bfloat16

D_MODEL = 1024
GRID_W = 64
HD_A = 64
DV_A = 2 * HD_A
H_A = D_MODEL // DV_A
N_B = 64
H_B = D_MODEL // N_B
LORA_W = 64
LORA_A = 64
LORA_G = 128
N_EXPERTS = 32
TOP_K = 4
SWIGLU_LIMIT = 7.0
SWIGLU_ALPHA = 1.702
ROPE_THETA = 10000.0
ROPE_FREQS = HD_A // 4
NORM_EPS = 1e-6
GN_EPS = 64e-5
L2_EPS = 1e-12
ATTN_SCALE = HD_A ** -0.5
N_MOD = 6
LAM_INIT = 0.8 - 0.6 * math.exp(-0.3 * 0)
DECAY_SCALE = -math.exp(-0.5)

LANES = 128
SUBLANES = 8
CHUNK = 64
MOE_TM = 256
COMB_TB = 256
ATT_SUB = 256
LOG2_E = math.log2(math.e)
VMEM_LIMIT = 56 * 1024 * 1024

COL_Q, COL_K, COL_V, COL_G0, COL_G1, COL_R, COL_KR, COL_VR, COL_LORA = (
    0, 1024, 2048, 3072, 4096, 5120, 6144, 7168, 8192)
C_LORA = LORA_W + LORA_A + LORA_G
C_IN = COL_LORA + C_LORA


def _sigmoid(x):
    return 1.0 / (1.0 + jnp.exp(-x))


def _dot(a, b):
    return jnp.dot(a.astype(BF16), b.astype(BF16), preferred_element_type=F32)


def _dot_nt(a, b):
    return lax.dot_general(a.astype(BF16), b.astype(BF16), (((1,), (1,)), ((), ())),
                           preferred_element_type=F32)


def _dot_tn(a, b):
    return lax.dot_general(a.astype(BF16), b.astype(BF16), (((0,), (0,)), ((), ())),
                           preferred_element_type=F32)


def _split3(x):
    hi = x.astype(BF16)
    r1 = x - hi.astype(F32)
    mid = r1.astype(BF16)
    lo = (r1 - mid.astype(F32)).astype(BF16)
    return hi, mid, lo


def _headsum(x, ind):
    e, et = ind
    hi = x.astype(BF16)
    lo = (x - hi.astype(F32)).astype(BF16)
    s = jnp.dot(hi, e, preferred_element_type=F32) + jnp.dot(lo, e, preferred_element_type=F32)
    s_hi = s.astype(BF16)
    s_lo = (s - s_hi.astype(F32)).astype(BF16)
    return jnp.dot(s_hi, et, preferred_element_type=F32) + jnp.dot(s_lo, et, preferred_element_type=F32)


def _rms(x, g):
    return x * lax.rsqrt(jnp.mean(x * x, axis=-1, keepdims=True) + NORM_EPS) * g


def _mod_kernel(c_ref, w_ref, b_ref, o_ref):
    c = c_ref[...]
    s = c * _sigmoid(c)
    o_ref[...] = _dot(s, w_ref[...]) + b_ref[...]


def _modulation(cond, w_ada, b_ada):
    rows, d = cond.shape
    n = w_ada.shape[1]
    tn = 768
    return pl.pallas_call(
        _mod_kernel,
        grid=(n // tn,),
        in_specs=[pl.BlockSpec((rows, d), lambda j: (0, 0)),
                  pl.BlockSpec((d, tn), lambda j: (0, j)),
                  pl.BlockSpec((1, tn), lambda j: (0, j))],
        out_specs=pl.BlockSpec((rows, tn), lambda j: (0, j)),
        out_shape=jax.ShapeDtypeStruct((rows, n), F32),
        name="modulation",
    )(cond, w_ada, b_ada)


def _proj_kernel(x_ref, g_ref, sc_ref, sh_ref, w_ref, o_ref, h_scr):
    @pl.when(pl.program_id(2) == 0)
    def _():
        h = _rms(x_ref[0], g_ref[...]) * (1.0 + sc_ref[0]) + sh_ref[0]
        h_scr[...] = h.astype(BF16)

    o_ref[0] = jnp.dot(h_scr[...], w_ref[...], preferred_element_type=F32)


def _norm_proj(x, g, scale, shift, w, tm, tn):
    b, s, d = x.shape
    n = w.shape[1]
    per_batch = scale.shape[0] > 1
    mod_map = (lambda bi, i, j: (bi, 0, 0)) if per_batch else (lambda bi, i, j: (0, 0, 0))
    return pl.pallas_call(
        _proj_kernel,
        grid=(b, s // tm, n // tn),
        in_specs=[pl.BlockSpec((1, tm, d), lambda bi, i, j: (bi, i, 0)),
                  pl.BlockSpec((1, d), lambda bi, i, j: (0, 0)),
                  pl.BlockSpec((1, 1, d), mod_map),
                  pl.BlockSpec((1, 1, d), mod_map),
                  pl.BlockSpec((d, tn), lambda bi, i, j: (0, j))],
        out_specs=pl.BlockSpec((1, tm, tn), lambda bi, i, j: (bi, i, j)),
        out_shape=jax.ShapeDtypeStruct((b, s, n), F32),
        scratch_shapes=[pltpu.VMEM((tm, d), BF16)],
        compiler_params=pltpu.CompilerParams(
            dimension_semantics=("arbitrary", "arbitrary", "arbitrary"),
            vmem_limit_bytes=VMEM_LIMIT),
        name="norm_proj",
    )(x, g, scale, shift, w)


def _rope(x, cos, sin):
    lane = lax.broadcasted_iota(jnp.int32, x.shape, 1)
    even = (lane // ROPE_FREQS) % 2 == 0
    rot = jnp.where(even, -pltpu.roll(x, LANES - ROPE_FREQS, 1), pltpu.roll(x, ROPE_FREQS, 1))
    return x * cos + rot * sin


def _attn_kernel(*refs, use_rope, use_ctx, s_new):
    it = iter(refs)
    lam_ref = next(it)
    q_ref, k_ref, v_ref = next(it), next(it), next(it)
    if use_rope:
        cq_ref, sq_ref, ck_ref, sk_ref = next(it), next(it), next(it), next(it)
    if use_ctx:
        ctxk_ref, ctxv_ref = next(it), next(it)
    g_ref = next(it)
    o_ref = next(it)
    k_scr, vt_scr = next(it), next(it)

    @pl.when(pl.program_id(2) == 0)
    def _():
        k = k_ref[0]
        if use_rope:
            k = _rope(k, ck_ref[...], sk_ref[...])
        k_scr[0:s_new, :] = k.astype(BF16)
        vt_scr[:, 0:s_new] = v_ref[0].T.astype(BF16)
        if use_ctx:
            k_scr[s_new:, :] = ctxk_ref[0].astype(BF16)
            vt_scr[:, s_new:] = ctxv_ref[0].T.astype(BF16)

    lam = lam_ref[0]
    q = q_ref[0]
    if use_rope:
        q = _rope(q, cq_ref[...], sq_ref[...])
    q = q * ATTN_SCALE
    n_sub = q.shape[0] // ATT_SUB
    sts = []
    lane = lax.broadcasted_iota(jnp.int32, (ATT_SUB, DV_A), 1)
    for t in range(n_sub):
        rows = slice(t * ATT_SUB, (t + 1) * ATT_SUB)
        for m in range(2):
            in_map = (lane >= HD_A) if m else (lane < HD_A)
            qm = jnp.where(in_map, q[rows], 0.0) * LOG2_E
            sts.append(_dot_nt(k_scr[...], qm))
    outs = []
    for st in sts:
        et = jnp.exp2(st - jnp.max(st, axis=0, keepdims=True))
        l = jnp.sum(et, axis=0, keepdims=True)
        ot = jnp.dot(vt_scr[...], et.astype(BF16), preferred_element_type=F32)
        outs.append(ot / l)
    for t in range(n_sub):
        o = (outs[2 * t] - lam * outs[2 * t + 1]).T
        o = o * lax.rsqrt(jnp.mean(o * o, axis=-1, keepdims=True) + NORM_EPS)
        o_ref[0, t * ATT_SUB:(t + 1) * ATT_SUB, :] = o * g_ref[...] * (1.0 - LAM_INIT)


def _diff_attention(proj, lam, g_subln, rope=None, ctx=None, tq=256):
    b, s, _ = proj.shape
    use_rope, use_ctx = rope is not None, ctx is not None
    s_tot = s + (ctx[0].shape[1] if use_ctx else 0)
    qmap = lambda bi, h, i: (bi, i, COL_Q // LANES + h)
    in_specs = [pl.BlockSpec(memory_space=pltpu.SMEM),
                pl.BlockSpec((1, tq, DV_A), qmap),
                pl.BlockSpec((1, s, DV_A), lambda bi, h, i: (bi, 0, COL_K // LANES + h)),
                pl.BlockSpec((1, s, DV_A), lambda bi, h, i: (bi, 0, COL_V // LANES + h))]
    args = [lam, proj, proj, proj]
    if use_rope:
        cos, sin = rope
        in_specs += [pl.BlockSpec((tq, DV_A), lambda bi, h, i: (i, 0)),
                     pl.BlockSpec((tq, DV_A), lambda bi, h, i: (i, 0)),
                     pl.BlockSpec((s, DV_A), lambda bi, h, i: (0, 0)),
                     pl.BlockSpec((s, DV_A), lambda bi, h, i: (0, 0))]
        args += [cos, sin, cos, sin]
    if use_ctx:
        p = ctx[0].shape[1]
        in_specs += [pl.BlockSpec((1, p, DV_A), lambda bi, h, i: (bi, 0, h)),
                     pl.BlockSpec((1, p, DV_A), lambda bi, h, i: (bi, 0, h))]
        args += [ctx[0], ctx[1]]
    in_specs.append(pl.BlockSpec((1, DV_A), lambda bi, h, i: (0, h)))
    args.append(g_subln)
    return pl.pallas_call(
        functools.partial(_attn_kernel, use_rope=use_rope, use_ctx=use_ctx, s_new=s),
        grid=(b, H_A, s // tq),
        in_specs=in_specs,
        out_specs=pl.BlockSpec((1, tq, DV_A), lambda bi, h, i: (bi, i, h)),
        out_shape=jax.ShapeDtypeStruct((b, s, D_MODEL), F32),
        scratch_shapes=[pltpu.VMEM((s_tot, DV_A), BF16), pltpu.VMEM((DV_A, s_tot), BF16)],
        compiler_params=pltpu.CompilerParams(
            dimension_semantics=("arbitrary", "arbitrary", "arbitrary"),
            vmem_limit_bytes=VMEM_LIMIT),
        name="diff_attention",
    )(*args)


def _shifted(x_ref, p_ref, n_ref, mup, mun, first, last):
    x = x_ref[0]
    ts = x.shape[0]
    row = lax.broadcasted_iota(jnp.int32, x.shape, 0)
    prev_row = p_ref[0][SUBLANES - 1:SUBLANES, :] * first
    next_row = n_ref[0][0:1, :] * last
    prev = jnp.where(row == 0, prev_row, pltpu.roll(x, 1, 0))
    nxt = jnp.where(row == ts - 1, next_row, pltpu.roll(x, ts - 1, 0))
    return x + mup * (prev - x) + mun * (nxt - x)


def _prep_kernel(r_ref, rp_ref, rn_ref, k_ref, kp_ref, kn_ref, v_ref, vp_ref, vn_ref,
                 l_ref, lp_ref, ln_ref, mup_ref, mun_ref, mupl_ref, munl_ref,
                 kk_ref, ka_ref, rk_ref, w0_ref, wup_ref, a0_ref, aup_ref, gup_ref, e_ref, et_ref,
                 ro_ref, vo_ref, kko_ref, kd_ref, ba_ref, lw_ref, bonus_ref, gate_ref, *, n_tiles):
    i = pl.program_id(1)
    first = (i > 0).astype(F32)
    last = (i < n_tiles - 1).astype(F32)
    mup, mun = mup_ref[...], mun_ref[...]
    d = D_MODEL
    r = _shifted(r_ref, rp_ref, rn_ref, mup[:, 0:d], mun[:, 0:d], first, last)
    k = _shifted(k_ref, kp_ref, kn_ref, mup[:, d:2 * d], mun[:, d:2 * d], first, last)
    v = _shifted(v_ref, vp_ref, vn_ref, mup[:, 2 * d:3 * d], mun[:, 2 * d:3 * d], first, last)
    lo = _shifted(l_ref, lp_ref, ln_ref, mupl_ref[...], munl_ref[...], first, last)
    xw = lo[:, 0:LORA_W]
    xa = lo[:, LORA_W:LORA_W + LORA_A]
    xg = lo[:, LORA_W + LORA_A:]
    g = (e_ref[...], et_ref[...])

    kk = k * kk_ref[...]
    nrm = jnp.sqrt(_headsum(kk * kk, g))
    kkn = kk / jnp.maximum(nrm, L2_EPS)
    ro_ref[0] = r
    vo_ref[0] = v
    kko_ref[0] = kkn
    gate_ref[0] = _dot(_sigmoid(xg), gup_ref[...])

    wlog = w0_ref[...] + _dot(jnp.tanh(xw), wup_ref[...])
    alog = a0_ref[...] + _dot(xa, aup_ref[...])
    ka = ka_ref[...]
    rrk = r * rk_ref[...]
    dots = None
    for dr in range(2):
        sl = slice(dr * d, (dr + 1) * d)
        lw_ref[dr, 0] = DECAY_SCALE * _sigmoid(wlog[:, sl])
        a = _sigmoid(alog[:, sl])
        kd = k * (1.0 + (a - 1.0) * ka)
        kd_ref[dr, 0] = kd
        ba_ref[dr, 0] = kkn * a
        t = rrk * kd
        dots = t if dots is None else dots + t
    bonus_ref[0] = _headsum(dots, g) * v


def _rwkv_prep(proj, p, ts):
    b, s, _ = proj.shape
    d = D_MODEL
    nt = s // ts
    hb = ts // SUBLANES
    nhb = s // SUBLANES

    def main(col, w):
        return pl.BlockSpec((1, ts, w), lambda bi, i: (bi, i, col // w))

    def prev(col, w):
        return pl.BlockSpec((1, SUBLANES, w), lambda bi, i: (bi, jnp.maximum(i * hb - 1, 0), col // w))

    def nxt(col, w):
        return pl.BlockSpec((1, SUBLANES, w),
                            lambda bi, i: (bi, jnp.minimum((i + 1) * hb, nhb - 1), col // w))

    def full(a):
        return pl.BlockSpec(a.shape, lambda bi, i: (0,) * a.ndim)

    in_specs, args = [], []
    for col, w in ((COL_R, d), (COL_KR, d), (COL_VR, d), (COL_LORA, C_LORA)):
        in_specs += [main(col, w), prev(col, w), nxt(col, w)]
        args += [proj, proj, proj]
    consts = [p['mu_prev_main'], p['mu_next_main'], p['mu_prev_lora'], p['mu_next_lora'],
              p['k_k'], p['k_a'], p['r_k'], p['w0'], p['w_up'], p['a0'], p['a_up'], p['g_up'],
              p['head_ind'], p['head_ind_t']]
    in_specs += [full(a) for a in consts]
    args += consts
    tok = pl.BlockSpec((1, ts, d), lambda bi, i: (bi, i, 0))
    tok2 = pl.BlockSpec((2, 1, ts, d), lambda bi, i: (0, bi, i, 0))
    one = jax.ShapeDtypeStruct((b, s, d), F32)
    two = jax.ShapeDtypeStruct((2, b, s, d), F32)
    return pl.pallas_call(
        functools.partial(_prep_kernel, n_tiles=nt),
        grid=(b, nt),
        in_specs=in_specs,
        out_specs=[tok, tok, tok, tok2, tok2, tok2, tok, tok],
        out_shape=[one, one, one, two, two, two, one, one],
        compiler_params=pltpu.CompilerParams(
            dimension_semantics=("arbitrary", "arbitrary"),
            vmem_limit_bytes=VMEM_LIMIT),
        name="rwkv_prep",
    )(*args)


def _scan_masks(reverse):
    n = CHUNK
    tt = lax.broadcasted_iota(jnp.int32, (n, n), 0)
    ss = lax.broadcasted_iota(jnp.int32, (n, n), 1)
    hi, lo = (ss, tt) if reverse else (tt, ss)
    tt2 = lax.broadcasted_iota(jnp.int32, (n, 2 * n), 0)
    col2 = lax.broadcasted_iota(jnp.int32, (n, 2 * n), 1)
    ss2 = col2 % n
    hi2, lo2 = (ss2, tt2) if reverse else (tt2, ss2)
    levels = []
    blk = 2
    while blk < n:
        levels.append((hi // (2 * blk) == lo // (2 * blk)) & ((hi // blk) % 2 == 1) & ((lo // blk) % 2 == 0))
        blk *= 2
    return dict(strict=hi > lo, incl=hi >= lo, eye=(tt == ss).astype(F32),
                first=(hi // 2 == lo // 2) & (hi > lo), levels=levels,
                strict_r=(hi2 > lo2) & (col2 >= n),
                incl2=hi2 >= lo2)


def _scan_operands(reverse, incl, lw, r, v, kkn, kd, ba):
    n = CHUNK
    tri = jnp.where(incl, 1.0, 0.0).astype(BF16)
    cum = sum(jnp.dot(tri, part, preferred_element_type=F32) for part in _split3(lw))
    tot = cum[0:1, :] if reverse else cum[n - 1:n, :]
    g_inv = jnp.exp(-cum)
    g_rest = jnp.exp(tot - cum)
    return dict(a=(kkn * jnp.exp(cum - lw)).astype(BF16), r=(r * jnp.exp(cum)).astype(BF16),
                b=(ba * g_inv).astype(BF16), k=(kd * g_inv).astype(BF16),
                bh=(ba * g_rest).astype(BF16), kh=(kd * g_rest).astype(BF16),
                v=v.astype(BF16), g_tot=jnp.exp(tot))


def _scan_kernel(*refs, has_s0, n_chunks):
    it = iter(refs)
    tok_refs = [[next(it) for _ in range(3)] for _ in range(2)]
    dir_refs = [[next(it) for _ in range(3)] for _ in range(2)]
    s0_ref = next(it) if has_s0 else None
    y_refs = [next(it), next(it)]
    sf_ref = next(it)
    s_scr = next(it)
    c = pl.program_id(1)

    @pl.when(c == 0)
    def _():
        if has_s0:
            s_scr[...] = s0_ref[0]
        else:
            s_scr[...] = jnp.zeros_like(s_scr)

    n = CHUNK
    masks = [_scan_masks(dr == 1) for dr in range(2)]
    ops = []
    for dr in range(2):
        r_ref, v_ref, kk_ref = tok_refs[dr]
        kd_ref, ba_ref, lw_ref = dir_refs[dr]
        ops.append(_scan_operands(dr == 1, masks[dr]['incl'], lw_ref[0, 0], r_ref[0], v_ref[0], kk_ref[0],
                                  kd_ref[0, 0], ba_ref[0, 0]))

    units = [(dr, h) for dr in range(2) for h in range(H_B)]
    idx = range(len(units))
    sls = [slice(h * N_B, (h + 1) * N_B) for _, h in units]
    op = lambda u, name: ops[units[u][0]][name][:, sls[u]]
    mk = lambda u, name: masks[units[u][0]][name]
    s_old = [s_scr[dr, h] for dr, h in units]
    ar = [jnp.concatenate([op(u, 'a'), op(u, 'r')], axis=0) for u in idx]
    bk = [jnp.concatenate([op(u, 'b'), op(u, 'k')], axis=0) for u in idx]
    bkh = [jnp.concatenate([op(u, 'bh'), op(u, 'kh')], axis=0) for u in idx]
    vh = [op(u, 'v') for u in idx]

    gram = [_dot_nt(ar[u], bk[u]) for u in idx]
    ars = [_dot_nt(ar[u], s_old[u]) for u in idx]
    nmat = [jnp.where(mk(u, 'strict'), gram[u][0:n, 0:n], 0.0) for u in idx]
    mak = [jnp.where(mk(u, 'strict_r'), gram[u][0:n, :], 0.0) for u in idx]
    pr = [jnp.where(mk(u, 'incl2'), gram[u][n:, :], 0.0) for u in idx]
    z = [ars[u][0:n] + _dot(mak[u], jnp.concatenate([vh[u], vh[u]], axis=0)) for u in idx]
    x = [mk(u, 'eye') - jnp.where(mk(u, 'first'), nmat[u], 0.0) for u in idx]
    for lvl in range(len(masks[0]['levels'])):
        xl = [_dot(x[u], jnp.where(mk(u, 'levels')[lvl], nmat[u], 0.0)) for u in idx]
        x = [x[u] - _dot(xl[u], x[u]) for u in idx]
    uu = [-_dot(x[u], z[u]) for u in idx]
    uv = [jnp.concatenate([uu[u].astype(BF16), vh[u]], axis=0) for u in idx]
    y = [ars[u][n:] + _dot(pr[u], uv[u]) for u in idx]
    s_new = [s_old[u] * ops[units[u][0]]['g_tot'][:, sls[u]] + _dot_tn(uv[u], bkh[u]) for u in idx]
    for u, (dr, h) in enumerate(units):
        y_refs[dr][0, :, sls[u]] = y[u]
    for u, (dr, h) in enumerate(units):
        s_scr[dr, h] = s_new[u]

    @pl.when(c == n_chunks - 1)
    def _():
        sf_ref[0] = s_scr[...]


def _rwkv_scan(r, v, kkn, kd, ba, lw, s0):
    b, s, d = r.shape
    nc = s // CHUNK
    fwd = pl.BlockSpec((1, CHUNK, d), lambda bi, c: (bi, c, 0))
    bwd = pl.BlockSpec((1, CHUNK, d), lambda bi, c: (bi, nc - 1 - c, 0))
    fwd2 = pl.BlockSpec((1, 1, CHUNK, d), lambda bi, c: (0, bi, c, 0))
    bwd2 = pl.BlockSpec((1, 1, CHUNK, d), lambda bi, c: (1, bi, nc - 1 - c, 0))
    state = pl.BlockSpec((1, 2, H_B, N_B, N_B), lambda bi, c: (bi, 0, 0, 0, 0))
    in_specs = [fwd] * 3 + [bwd] * 3 + [fwd2] * 3 + [bwd2] * 3
    args = [r, v, kkn, r, v, kkn, kd, ba, lw, kd, ba, lw]
    if s0 is not None:
        in_specs.append(state)
        args.append(s0)
    return pl.pallas_call(
        functools.partial(_scan_kernel, has_s0=s0 is not None, n_chunks=nc),
        grid=(b, nc),
        in_specs=in_specs,
        out_specs=[fwd, bwd, state],
        out_shape=[jax.ShapeDtypeStruct((b, s, d), F32),
                   jax.ShapeDtypeStruct((b, s, d), F32),
                   jax.ShapeDtypeStruct((b, 2, H_B, N_B, N_B), F32)],
        scratch_shapes=[pltpu.VMEM((2, H_B, N_B, N_B), F32)],
        compiler_params=pltpu.CompilerParams(
            dimension_semantics=("arbitrary", "arbitrary"),
            vmem_limit_bytes=VMEM_LIMIT),
        name="rwkv_scan",
    )(*args)


def _merge_kernel(x_ref, oa_ref, y0_ref, y1_ref, bonus_ref, gate_ref, g0_ref, g1_ref,
                  mg_ref, sc_ref, sh_ref, gnw_ref, gnb_ref, e_ref, et_ref, wout_ref,
                  gpost_ref, gpre_ref, wr_ref, br_ref, cnt0_ref,
                  x1_ref, h2_ref, route_ref, rank_ref, ew_ref, cnt_ref, cnt_scr):
    @pl.when((pl.program_id(0) == 0) & (pl.program_id(1) == 0))
    def _():
        cnt_scr[...] = cnt0_ref[...]

    g = (e_ref[...], et_ref[...])
    y = y0_ref[0] + y1_ref[0]
    mu = _headsum(y, g) * (1.0 / N_B)
    yc = y - mu
    var = _headsum(yc * yc, g) * (1.0 / N_B)
    yn = yc * lax.rsqrt(var + GN_EPS) * gnw_ref[...] + gnb_ref[...]
    ob = (yn + bonus_ref[0]) * gate_ref[0]
    merged = _sigmoid(g0_ref[0]) * oa_ref[0] + _sigmoid(g1_ref[0]) * ob
    out = _dot(merged, wout_ref[...])
    x1 = x_ref[0] + mg_ref[0] * _rms(out, gpost_ref[...])
    x1_ref[0] = x1
    h2 = _rms(x1, gpre_ref[...]) * (1.0 + sc_ref[0]) + sh_ref[0]
    h2_ref[0] = h2

    h_hi = h2.astype(BF16)
    h_lo = (h2 - h_hi.astype(F32)).astype(BF16)
    wr = wr_ref[...]
    w_hi = wr.astype(BF16)
    w_lo = (wr - w_hi.astype(F32)).astype(BF16)
    logits = (jnp.dot(h_hi, w_hi, preferred_element_type=F32) + jnp.dot(h_hi, w_lo, preferred_element_type=F32)
              + jnp.dot(h_lo, w_hi, preferred_element_type=F32)) + br_ref[...]
    lane = lax.broadcasted_iota(jnp.int32, logits.shape, 1)
    work = logits
    top = None
    picks = []
    for _ in range(TOP_K):
        mx = jnp.max(work, axis=-1, keepdims=True)
        idx = jnp.min(jnp.where(work == mx, lane, LANES), axis=-1, keepdims=True)
        if top is None:
            top = mx
        picks.append((idx, jnp.exp(mx - top)))
        work = jnp.where(lane == idx, -jnp.inf, work)
    denom = sum(e for _, e in picks)
    tm = logits.shape[0]
    onehot = jnp.zeros_like(logits)
    for idx, _ in picks:
        onehot = onehot + jnp.where(lane == idx, 1.0, 0.0)
    rr = lax.broadcasted_iota(jnp.int32, (tm, tm), 0)
    cc = lax.broadcasted_iota(jnp.int32, (tm, tm), 1)
    before = cnt_scr[...] + _dot(jnp.where(rr > cc, 1.0, 0.0), onehot)
    route = jnp.zeros(logits.shape, jnp.int32)
    ew = jnp.zeros_like(logits)
    for j, (idx, e) in enumerate(picks):
        route = jnp.where(lane == j, idx, route)
        ew = jnp.where(lane == j, e / denom, ew)
    route_ref[0] = route
    rank_ref[0] = jnp.where(onehot > 0.0, before, 0.0).astype(jnp.int32)
    ew_ref[0] = ew
    cnt_scr[...] = cnt_scr[...] + jnp.sum(onehot, axis=0, keepdims=True)
    cnt_ref[...] = cnt_scr[...]


def _merge_out(x, oa, y0, y1, bonus, gate, proj, mods, p, cnt0, tm):
    b, s, d = x.shape
    per_batch = mods[0].shape[0] > 1
    mod_map = (lambda bi, i: (bi, 0, 0)) if per_batch else (lambda bi, i: (0, 0, 0))
    tok = pl.BlockSpec((1, tm, d), lambda bi, i: (bi, i, 0))

    def full(a):
        return pl.BlockSpec(a.shape, lambda bi, i: (0,) * a.ndim)

    consts = [p['gn_w'], p['gn_b'], p['head_ind'], p['head_ind_t'], p['w_out'], p['g_post_mix'], p['g_pre_ffn'],
              p['w_router'], p['b_router']]
    in_specs = ([tok] * 6
                + [pl.BlockSpec((1, tm, d), lambda bi, i: (bi, i, COL_G0 // d)),
                   pl.BlockSpec((1, tm, d), lambda bi, i: (bi, i, COL_G1 // d))]
                + [pl.BlockSpec((1, 1, d), mod_map)] * 3
                + [full(a) for a in consts] + [full(cnt0)])
    return pl.pallas_call(
        _merge_kernel,
        grid=(b, s // tm),
        in_specs=in_specs,
        out_specs=[tok, tok] + [pl.BlockSpec((1, tm, LANES), lambda bi, i: (bi, i, 0))] * 3
                  + [pl.BlockSpec((1, LANES), lambda bi, i: (0, 0))],
        out_shape=[jax.ShapeDtypeStruct((b, s, d), F32),
                   jax.ShapeDtypeStruct((b, s, d), F32),
                   jax.ShapeDtypeStruct((b, s, LANES), jnp.int32),
                   jax.ShapeDtypeStruct((b, s, LANES), jnp.int32),
                   jax.ShapeDtypeStruct((b, s, LANES), F32),
                   jax.ShapeDtypeStruct((1, LANES), F32)],
        scratch_shapes=[pltpu.VMEM((1, LANES), F32)],
        compiler_params=pltpu.CompilerParams(
            dimension_semantics=("arbitrary", "arbitrary"),
            vmem_limit_bytes=VMEM_LIMIT),
        name="merge_out",
    )(x, oa, y0, y1, bonus, gate, proj, proj, *mods, *consts, cnt0)


def _route_metadata(eid, rank, counts, n_tiles):
    padded = (counts + MOE_TM - 1) // MOE_TM * MOE_TM
    ends = jnp.cumsum(padded)
    offs = ends - padded
    experts = jnp.arange(N_EXPERTS, dtype=jnp.int32)
    pos = jnp.sum(jnp.where(eid[:, :, None] == experts, (offs + rank)[:, None, :], 0), axis=-1)
    idx = jnp.arange(n_tiles, dtype=jnp.int32)
    valid = idx * MOE_TM < ends[-1]
    tile = jnp.where(valid, idx, ends[-1] // MOE_TM - 1)
    te = jnp.sum((ends[None, :] <= (tile * MOE_TM)[:, None]).astype(jnp.int32), axis=1)
    first = valid & jnp.concatenate([jnp.ones((1,), bool), te[1:] != te[:-1]])
    used = counts > 0
    slot_of = (jnp.cumsum(used.astype(jnp.int32)) - 1) % 2
    later = (experts[None, :] > experts[:, None]) & used[None, :]
    next_of = jnp.min(jnp.where(later, experts[None, :], N_EXPERTS), axis=1)
    next_of = jnp.where(next_of == N_EXPERTS, -1, next_of)
    sched = (tile, te, valid.astype(jnp.int32), first.astype(jnp.int32), slot_of[te], next_of[te])
    fill = jnp.concatenate([jnp.where(padded > counts, ends - MOE_TM, -1),
                            jnp.where(valid, -1, idx * MOE_TM)[eid.size // MOE_TM:]])
    return pos, tuple(a.astype(jnp.int32) for a in sched), fill.astype(jnp.int32)


def _dispatch_kernel(fill_ref, pos_ref, hp_ref, hs_ref, xs_hbm, zbuf, sem, fsem, *, n_p_tiles, n_fill):
    i = pl.program_id(0)

    def fill_copy(e):
        start = pl.multiple_of(fill_ref[e], MOE_TM)
        return pltpu.make_async_copy(zbuf, xs_hbm.at[pl.ds(start, MOE_TM)], fsem)

    @pl.when(i == 0)
    def _():
        zbuf[...] = jnp.zeros_like(zbuf)
        for e in range(n_fill):
            @pl.when(fill_ref[e] >= 0)
            def _(e=e):
                fill_copy(e).start()
        for e in range(n_fill):
            @pl.when(fill_ref[e] >= 0)
            def _(e=e):
                fill_copy(e).wait()

    def scatter(h_ref):
        for t in range(COMB_TB):
            for j in range(TOP_K):
                pltpu.make_async_copy(h_ref.at[pl.ds(t, 1)],
                                      xs_hbm.at[pl.ds(pos_ref[0, 0, j * COMB_TB + t], 1)], sem).start()

    @pl.when(i < n_p_tiles)
    def _():
        scatter(hp_ref)

    @pl.when(i >= n_p_tiles)
    def _():
        scatter(hs_ref)

    for _ in range(TOP_K):
        pltpu.make_async_copy(hp_ref, xs_hbm.at[pl.ds(0, COMB_TB)], sem).wait()


def _dispatch(pos, fill, h2p, h2s, n_tiles):
    d = D_MODEL
    n_p, n_s = h2p.shape[0], h2s.shape[0]
    n = (n_p + n_s) // COMB_TB
    n_pt = n_p // COMB_TB
    pos_t = pos.reshape(n, COMB_TB, TOP_K).transpose(0, 2, 1).reshape(n, 1, TOP_K * COMB_TB)
    grid_spec = pltpu.PrefetchScalarGridSpec(
        num_scalar_prefetch=1,
        grid=(n,),
        in_specs=[pl.BlockSpec((1, 1, TOP_K * COMB_TB), lambda i, fill: (i, 0, 0), memory_space=pltpu.SMEM),
                  pl.BlockSpec((COMB_TB, d), lambda i, fill: (jnp.minimum(i, n_pt - 1), 0)),
                  pl.BlockSpec((COMB_TB, d), lambda i, fill: (jnp.maximum(i - n_pt, 0), 0))],
        out_specs=pl.BlockSpec(memory_space=pl.ANY),
        scratch_shapes=[pltpu.VMEM((MOE_TM, d), F32),
                        pltpu.SemaphoreType.DMA(()),
                        pltpu.SemaphoreType.DMA(())])
    return pl.pallas_call(
        functools.partial(_dispatch_kernel, n_p_tiles=n_pt, n_fill=fill.shape[0]),
        grid_spec=grid_spec,
        out_shape=jax.ShapeDtypeStruct((n_tiles * MOE_TM, d), F32),
        compiler_params=pltpu.CompilerParams(
            dimension_semantics=("arbitrary",), vmem_limit_bytes=VMEM_LIMIT),
        name="moe_dispatch",
    )(fill, pos_t, h2p, h2s)


def _experts_kernel(tile_ref, te_ref, tv_ref, tf_ref, ws_ref, nx_ref, x_ref,
                    wgu_hbm, bgu_ref, wd_hbm, bd_ref, ys_ref, wgu_buf, wd_buf, wgu_scr, wd_scr, sem):
    i = pl.program_id(0)

    def weight_copies(e, slot):
        return (pltpu.make_async_copy(wgu_hbm.at[e], wgu_buf.at[slot], sem.at[0, slot]),
                pltpu.make_async_copy(wd_hbm.at[e], wd_buf.at[slot], sem.at[1, slot]))

    @pl.when(i == 0)
    def _():
        for cp in weight_copies(te_ref[0], 0):
            cp.start()

    @pl.when(tf_ref[i] == 1)
    def _():
        slot = ws_ref[i]

        @pl.when(nx_ref[i] >= 0)
        def _():
            for cp in weight_copies(nx_ref[i], 1 - slot):
                cp.start()

        for cp in weight_copies(te_ref[i], slot):
            cp.wait()
        wgu_scr[...] = wgu_buf[slot].astype(BF16)
        wd_scr[...] = wd_buf[slot].astype(BF16)

    @pl.when(tv_ref[i] == 0)
    def _():
        ys_ref[...] = jnp.zeros_like(ys_ref)

    @pl.when(tv_ref[i] == 1)
    def _():
        gu = jnp.dot(x_ref[...].astype(BF16), wgu_scr[...], preferred_element_type=F32) + bgu_ref[0]
        gate = jnp.minimum(gu[:, :D_MODEL], SWIGLU_LIMIT)
        up = jnp.clip(gu[:, D_MODEL:], -SWIGLU_LIMIT, SWIGLU_LIMIT)
        act = (up + 1.0) * gate * _sigmoid(SWIGLU_ALPHA * gate)
        ys_ref[...] = jnp.dot(act.astype(BF16), wd_scr[...], preferred_element_type=F32) + bd_ref[0]


def _experts(xs, sched, p, n_tiles):
    d = D_MODEL
    bias = lambda w: pl.BlockSpec((1, 1, w), lambda i, tile, te, *_: (te[i], 0, 0))
    grid_spec = pltpu.PrefetchScalarGridSpec(
        num_scalar_prefetch=len(sched),
        grid=(n_tiles,),
        in_specs=[pl.BlockSpec((MOE_TM, d), lambda i, tile, *_: (tile[i], 0)),
                  pl.BlockSpec(memory_space=pl.ANY), bias(2 * d),
                  pl.BlockSpec(memory_space=pl.ANY), bias(d)],
        out_specs=pl.BlockSpec((MOE_TM, d), lambda i, *_: (i, 0)),
        scratch_shapes=[pltpu.VMEM((2, d, 2 * d), F32),
                        pltpu.VMEM((2, d, d), F32),
                        pltpu.VMEM((d, 2 * d), BF16),
                        pltpu.VMEM((d, d), BF16),
                        pltpu.SemaphoreType.DMA((2, 2))])
    return pl.pallas_call(
        _experts_kernel,
        grid_spec=grid_spec,
        out_shape=jax.ShapeDtypeStruct((n_tiles * MOE_TM, d), F32),
        compiler_params=pltpu.CompilerParams(
            dimension_semantics=("arbitrary",), vmem_limit_bytes=VMEM_LIMIT),
        name="moe_experts",
    )(*sched, xs, p['w_gate_up'], p['b_gate_up'], p['w_down'], p['b_down'])


def _combine_kernel(pos_ref, posn_ref, ys_hbm, ew_ref, x1_ref, mg_ref, gpost_ref, o_ref, buf, sem, *, n):
    i = pl.program_id(0)
    slot = i % 2

    def row_copy(row, j, t, slot_):
        return pltpu.make_async_copy(ys_hbm.at[pl.ds(row, 1)], buf.at[slot_, j, pl.ds(t, 1)], sem.at[slot_])

    def gather(src_ref, slot_):
        for t in range(COMB_TB):
            for j in range(TOP_K):
                row_copy(src_ref[0, 0, j * COMB_TB + t], j, t, slot_).start()

    @pl.when(i == 0)
    def _():
        gather(pos_ref, 0)

    @pl.when(i + 1 < n)
    def _():
        gather(posn_ref, 1 - slot)

    for j in range(TOP_K):
        pltpu.make_async_copy(ys_hbm.at[pl.ds(0, COMB_TB)], buf.at[slot, j], sem.at[slot]).wait()
    ew = ew_ref[...]
    f = sum(ew[:, j:j + 1] * buf[slot, j] for j in range(TOP_K))
    o_ref[...] = x1_ref[...] + mg_ref[0] * _rms(f, gpost_ref[...])


def _combine(ys, pos, ew, x1, mod_gate, p):
    b, s, d = x1.shape
    n_tok = b * s
    n_tiles = n_tok // COMB_TB
    per_batch = mod_gate.shape[0] > 1
    if per_batch:
        mod_gate = jnp.repeat(mod_gate, s // COMB_TB, axis=0)
    mod_map = (lambda i: (i, 0, 0)) if per_batch else (lambda i: (0, 0, 0))
    pos_t = pos.reshape(n_tiles, COMB_TB, TOP_K).transpose(0, 2, 1).reshape(n_tiles, 1, TOP_K * COMB_TB)
    smem_tile = lambda f: pl.BlockSpec((1, 1, TOP_K * COMB_TB), f, memory_space=pltpu.SMEM)
    out = pl.pallas_call(
        functools.partial(_combine_kernel, n=n_tiles),
        grid=(n_tiles,),
        in_specs=[smem_tile(lambda i: (i, 0, 0)),
                  smem_tile(lambda i: (jnp.minimum(i + 1, n_tiles - 1), 0, 0)),
                  pl.BlockSpec(memory_space=pl.ANY),
                  pl.BlockSpec((COMB_TB, LANES), lambda i: (i, 0)),
                  pl.BlockSpec((COMB_TB, d), lambda i: (i, 0)),
                  pl.BlockSpec((1, 1, d), mod_map),
                  pl.BlockSpec((1, d), lambda i: (0, 0))],
        out_specs=pl.BlockSpec((COMB_TB, d), lambda i: (i, 0)),
        out_shape=jax.ShapeDtypeStruct((n_tok, d), F32),
        scratch_shapes=[pltpu.VMEM((2, TOP_K, COMB_TB, d), F32),
                        pltpu.SemaphoreType.DMA((2,))],
        compiler_params=pltpu.CompilerParams(
            dimension_semantics=("arbitrary",), vmem_limit_bytes=VMEM_LIMIT),
        name="moe_combine",
    )(pos_t, pos_t, ys, ew.reshape(n_tok, LANES), x1.reshape(n_tok, d), mod_gate, p['g_post_ffn'])
    return out.reshape(b, s, d)


def _rope_tables(n_tokens):
    rows = n_tokens // GRID_W
    row = jnp.repeat(jnp.arange(rows, dtype=F32), GRID_W)
    col = jnp.tile(jnp.arange(GRID_W, dtype=F32), rows)
    inv = ROPE_THETA ** (-jnp.arange(ROPE_FREQS, dtype=F32) / ROPE_FREQS)
    ang_r = row[:, None] * inv[None, :]
    ang_c = col[:, None] * inv[None, :]
    ang = jnp.concatenate([ang_r, ang_r, ang_c, ang_c] * 2, axis=-1)
    return jnp.cos(ang), jnp.sin(ang)


def _stream(x, mods, proj_w, lam, p, rope, ctx, s0, cnt0, tm_proj, tm_tok):
    b, s, d = x.shape
    xf = x if mods[0].shape[0] > 1 else x.reshape(1, b * s, d)
    proj = _norm_proj(xf, p['g_pre_mix'], mods[1], mods[0], proj_w, tm_proj, 768)
    proj = proj.reshape(b, s, C_IN)
    oa = _diff_attention(proj, lam, p['g_subln'], rope, ctx, tq=min(s, 2 * ATT_SUB))
    r, v, kkn, kd, ba, lw, bonus, gate = _rwkv_prep(proj, p, min(s, 256))
    y0, y1, sfin = _rwkv_scan(r, v, kkn, kd, ba, lw, s0)
    tb, tsq = xf.shape[0], xf.shape[1]
    x1, h2, route, rank, ew, cnt = _merge_out(
        *(a.reshape(tb, tsq, a.shape[-1]) for a in (x, oa, y0, y1, bonus, gate, proj)),
        (mods[2], mods[4], mods[3]), p, cnt0, tm_tok)
    n = b * s
    route = (route.reshape(n, LANES)[:, :TOP_K], rank.reshape(n, LANES)[:, :N_EXPERTS])
    return (x1.reshape(b, s, d), h2.reshape(n, d), route, ew.reshape(n, LANES), cnt,
            proj, sfin)


def kernel(x_prompt, x_sample, cache_k, cache_v, state_rwkv, c, c_ctx, w_ada, b_ada, g_pre_mix, g_post_mix, g_pre_ffn, g_post_ffn, w_in, mu_prev, mu_next, lam, g_subln, k_k, k_a, r_k, w0, w_up, a0, a_up, g_up, gn_w, gn_b, w_out, w_router, b_router, w_gate_up, b_gate_up, w_down, b_down):
    l = 0
    d = D_MODEL
    bp, sp, _ = x_prompt.shape
    bs, ss, _ = x_sample.shape

    n_cond = 1 + bs
    rows = -(-n_cond // SUBLANES) * SUBLANES
    cond = jnp.concatenate([c_ctx[None, :], c, jnp.zeros((rows - n_cond, d), F32)], axis=0)
    mod = _modulation(cond, w_ada[l], b_ada[l][None, :])
    mods_p = [mod[0:1, i * d:(i + 1) * d].reshape(1, 1, d) for i in range(N_MOD)]
    mods_s = [mod[1:1 + bs, i * d:(i + 1) * d].reshape(bs, 1, d) for i in range(N_MOD)]

    w = w_in[l]
    o_rwkv = 3 * d
    o_gate = o_rwkv + 3 * d + C_LORA
    w_perm = jnp.concatenate([w[:, :3 * d], w[:, o_gate:o_gate + 2 * d],
                              w[:, o_rwkv:o_rwkv + 3 * d], w[:, o_rwkv + 3 * d:o_gate]],
                             axis=1).astype(BF16)

    lq = lam[l]
    lam_val = (jnp.exp(jnp.sum(lq[0] * lq[1])) - jnp.exp(jnp.sum(lq[2] * lq[3])) + LAM_INIT).reshape(1)

    head = jnp.arange(d) // N_B
    wr = jnp.concatenate([w_router[l], jnp.zeros((d, LANES - N_EXPERTS), F32)], axis=1)
    br = jnp.concatenate([b_router[l], jnp.full((LANES - N_EXPERTS,), -jnp.inf, F32)])[None, :]
    mup, mun = mu_prev[l][None, :], mu_next[l][None, :]
    p = {
        'g_pre_mix': g_pre_mix[l][None, :], 'g_post_mix': g_post_mix[l][None, :],
        'g_pre_ffn': g_pre_ffn[l][None, :], 'g_post_ffn': g_post_ffn[l][None, :],
        'g_subln': g_subln[l][None, :],
        'mu_prev_main': mup[:, :3 * d], 'mu_next_main': mun[:, :3 * d],
        'mu_prev_lora': mup[:, 3 * d:], 'mu_next_lora': mun[:, 3 * d:],
        'k_k': k_k[l][None, :], 'k_a': k_a[l][None, :], 'r_k': r_k[l].reshape(1, d),
        'w0': w0[l].reshape(1, 2 * d),
        'w_up': jnp.concatenate([w_up[l, 0], w_up[l, 1]], axis=1).astype(BF16),
        'a0': a0[l].reshape(1, 2 * d),
        'a_up': jnp.concatenate([a_up[l, 0], a_up[l, 1]], axis=1).astype(BF16),
        'g_up': g_up[l].astype(BF16),
        'head_ind': (head[:, None] == jnp.arange(LANES)[None, :]).astype(BF16),
        'head_ind_t': (jnp.arange(LANES)[:, None] == head[None, :]).astype(BF16),
        'gn_w': gn_w[l][None, :], 'gn_b': gn_b[l][None, :],
        'w_out': w_out[l].astype(BF16),
        'w_router': wr, 'b_router': br,
        'w_gate_up': w_gate_up[l], 'b_gate_up': b_gate_up[l][:, None, :],
        'w_down': w_down[l], 'b_down': b_down[l][:, None, :],
    }

    rope = _rope_tables(ss)
    ctx = (cache_k[:, l].reshape(bs, -1, d), cache_v[:, l].reshape(bs, -1, d))

    cnt0 = jnp.zeros((1, LANES), F32)
    x1p, h2p, route_p, ewp, cnt_p, proj_p, sfin = _stream(
        x_prompt, mods_p, w_perm, lam_val, p, None, None, None, cnt0, 1024, 512)
    x1s, h2s, route_s, ews, cnt_s, _, _ = _stream(
        x_sample, mods_s, w_perm, lam_val, p, rope, ctx, state_rwkv[:, l], cnt_p, 1024, 512)

    n_p, n_s = bp * sp, bs * ss
    n_tok = n_p + n_s
    eid, rank = (jnp.concatenate([a, b_], axis=0) for a, b_ in zip(route_p, route_s))
    counts = cnt_s[0, :N_EXPERTS].astype(jnp.int32)
    n_tiles = -(-(n_tok * TOP_K + N_EXPERTS * (MOE_TM - 1)) // MOE_TM)
    pos, sched, fill = _route_metadata(eid, rank, counts, n_tiles)
    xs = _dispatch(pos, fill, h2p, h2s, n_tiles)
    rows = _experts(xs, sched, p, n_tiles)
    yp = _combine(rows, pos[:n_p], ewp, x1p, mods_p[5], p)
    ys = _combine(rows, pos[n_p:], ews, x1s, mods_s[5], p)

    new_k = proj_p[:, :, COL_K:COL_K + d].reshape(bp, 1, sp, H_A, 2, HD_A)
    new_v = proj_p[:, :, COL_V:COL_V + d].reshape(bp, 1, sp, H_A, DV_A)
    return (yp, ys, new_k, new_v, sfin[:, None])
```

```python
import functools
import math

import jax
import jax.numpy as jnp
from jax import lax
from jax.experimental import pallas as pl
from jax.experimental.pallas import tpu as pltpu

F32 = jnp.float32
BF16 = jnp.bfloat16

D_MODEL = 1024
GRID_W = 64
HD_A = 64
DV_A = 2 * HD_A
H_A = D_MODEL // DV_A
N_B = 64
H_B = D_MODEL // N_B
LORA_W = 64
LORA_A = 64
LORA_G = 128
N_EXPERTS = 32
TOP_K = 4
SWIGLU_LIMIT = 7.0
SWIGLU_ALPHA = 1.702
ROPE_THETA = 10000.0
ROPE_FREQS = HD_A // 4
NORM_EPS = 1e-6
GN_EPS = 64e-5
L2_EPS = 1e-12
ATTN_SCALE = HD_A ** -0.5
N_MOD = 6
LAM_INIT = 0.8 - 0.6 * math.exp(-0.3 * 0)
DECAY_SCALE = -math.exp(-0.5)

LANES = 128
SUBLANES = 8
CHUNK = 64
MOE_TM = 256
COMB_TB = 256
ATT_SUB = 256
LOG2_E = math.log2(math.e)
VMEM_LIMIT = 56 * 1024 * 1024

COL_Q, COL_K, COL_V, COL_G0, COL_G1, COL_R, COL_KR, COL_VR, COL_LORA = (
    0, 1024, 2048, 3072, 4096, 5120, 6144, 7168, 8192)
C_LORA = LORA_W + LORA_A + LORA_G
C_IN = COL_LORA + C_LORA


def _sigmoid(x):
    return 1.0 / (1.0 + jnp.exp(-x))


def _dot(a, b):
    return jnp.dot(a.astype(BF16), b.astype(BF16), preferred_element_type=F32)


def _dot_nt(a, b):
    return lax.dot_general(a.astype(BF16), b.astype(BF16), (((1,), (1,)), ((), ())),
                           preferred_element_type=F32)


def _dot_tn(a, b):
    return lax.dot_general(a.astype(BF16), b.astype(BF16), (((0,), (0,)), ((), ())),
                           preferred_element_type=F32)


def _split3(x):
    hi = x.astype(BF16)
    r1 = x - hi.astype(F32)
    mid = r1.astype(BF16)
    lo = (r1 - mid.astype(F32)).astype(BF16)
    return hi, mid, lo


def _headsum(x, ind):
    e, et = ind
    hi = x.astype(BF16)
    lo = (x - hi.astype(F32)).astype(BF16)
    s = jnp.dot(hi, e, preferred_element_type=F32) + jnp.dot(lo, e, preferred_element_type=F32)
    s_hi = s.astype(BF16)
    s_lo = (s - s_hi.astype(F32)).astype(BF16)
    return jnp.dot(s_hi, et, preferred_element_type=F32) + jnp.dot(s_lo, et, preferred_element_type=F32)


def _rms(x, g):
    return x * lax.rsqrt(jnp.mean(x * x, axis=-1, keepdims=True) + NORM_EPS) * g


def _mod_kernel(c_ref, w_ref, b_ref, o_ref):
    c = c_ref[...]
    s = c * _sigmoid(c)
    o_ref[...] = _dot(s, w_ref[...]) + b_ref[...]


def _modulation(cond, w_ada, b_ada):
    rows, d = cond.shape
    n = w_ada.shape[1]
    tn = 768
    return pl.pallas_call(
        _mod_kernel,
        grid=(n // tn,),
        in_specs=[pl.BlockSpec((rows, d), lambda j: (0, 0)),
                  pl.BlockSpec((d, tn), lambda j: (0, j)),
                  pl.BlockSpec((1, tn), lambda j: (0, j))],
        out_specs=pl.BlockSpec((rows, tn), lambda j: (0, j)),
        out_shape=jax.ShapeDtypeStruct((rows, n), F32),
        name="modulation",
    )(cond, w_ada, b_ada)


def _proj_kernel(x_ref, g_ref, sc_ref, sh_ref, w_ref, o_ref, h_scr):
    @pl.when(pl.program_id(2) == 0)
    def _():
        h = _rms(x_ref[0], g_ref[...]) * (1.0 + sc_ref[0]) + sh_ref[0]
        h_scr[...] = h.astype(BF16)

    o_ref[0] = jnp.dot(h_scr[...], w_ref[...], preferred_element_type=F32).astype(o_ref.dtype)


def _norm_proj(x, g, scale, shift, w, tm, tn, out_dtype):
    b, s, d = x.shape
    n = w.shape[1]
    per_batch = scale.shape[0] > 1
    mod_map = (lambda bi, i, j: (bi, 0, 0)) if per_batch else (lambda bi, i, j: (0, 0, 0))
    return pl.pallas_call(
        _proj_kernel,
        grid=(b, s // tm, n // tn),
        in_specs=[pl.BlockSpec((1, tm, d), lambda bi, i, j: (bi, i, 0)),
                  pl.BlockSpec((1, d), lambda bi, i, j: (0, 0)),
                  pl.BlockSpec((1, 1, d), mod_map),
                  pl.BlockSpec((1, 1, d), mod_map),
                  pl.BlockSpec((d, tn), lambda bi, i, j: (0, j))],
        out_specs=pl.BlockSpec((1, tm, tn), lambda bi, i, j: (bi, i, j)),
        out_shape=jax.ShapeDtypeStruct((b, s, n), out_dtype),
        scratch_shapes=[pltpu.VMEM((tm, d), BF16)],
        compiler_params=pltpu.CompilerParams(
            dimension_semantics=("arbitrary", "arbitrary", "arbitrary"),
            vmem_limit_bytes=VMEM_LIMIT),
        name="norm_proj",
    )(x, g, scale, shift, w)


def _rope(x, cos, sin):
    lane = lax.broadcasted_iota(jnp.int32, x.shape, 1)
    even = (lane // ROPE_FREQS) % 2 == 0
    rot = jnp.where(even, -pltpu.roll(x, LANES - ROPE_FREQS, 1), pltpu.roll(x, ROPE_FREQS, 1))
    return x * cos + rot * sin


def _attn_kernel(*refs, use_rope, use_ctx, s_new):
    it = iter(refs)
    lam_ref = next(it)
    q_ref, k_ref, v_ref = next(it), next(it), next(it)
    if use_rope:
        cq_ref, sq_ref, ck_ref, sk_ref = next(it), next(it), next(it), next(it)
    if use_ctx:
        ctxk_ref, ctxv_ref = next(it), next(it)
    g_ref = next(it)
    o_ref = next(it)
    k_scr, vt_scr = next(it), next(it)

    hps = q_ref.shape[-1] // DV_A
    head_lanes = [slice(hh * DV_A, (hh + 1) * DV_A) for hh in range(hps)]

    @pl.when(pl.program_id(2) == 0)
    def _():
        for hh, hl in enumerate(head_lanes):
            k = k_ref[0][:, hl].astype(F32)
            if use_rope:
                k = _rope(k, ck_ref[...], sk_ref[...])
            k_scr[hh, 0:s_new, :] = k.astype(BF16)
            vt_scr[hh, :, 0:s_new] = v_ref[0][:, hl].astype(F32).T.astype(BF16)
            if use_ctx:
                k_scr[hh, s_new:, :] = ctxk_ref[0][:, hl].astype(BF16)
                vt_scr[hh, :, s_new:] = ctxv_ref[0][:, hl].T.astype(BF16)

    lam = lam_ref[0]
    n_sub = q_ref.shape[1] // ATT_SUB
    lane = lax.broadcasted_iota(jnp.int32, (ATT_SUB, DV_A), 1)
    sts = []
    for hh, hl in enumerate(head_lanes):
        q = q_ref[0][:, hl].astype(F32)
        if use_rope:
            q = _rope(q, cq_ref[...], sq_ref[...])
        q = q * (ATTN_SCALE * LOG2_E)
        for t in range(n_sub):
            for m in range(2):
                in_map = (lane >= HD_A) if m else (lane < HD_A)
                qm = jnp.where(in_map, q[t * ATT_SUB:(t + 1) * ATT_SUB], 0.0)
                sts.append((hh, _dot_nt(k_scr[hh], qm)))
    outs = []
    for hh, st in sts:
        et = jnp.exp2(st - jnp.max(st, axis=0, keepdims=True))
        l = jnp.sum(et, axis=0, keepdims=True)
        ot = jnp.dot(vt_scr[hh], et.astype(BF16), preferred_element_type=F32)
        outs.append(ot / l)
    for hh, hl in enumerate(head_lanes):
        for t in range(n_sub):
            u = 2 * (hh * n_sub + t)
            o = (outs[u] - lam * outs[u + 1]).T
            o = o * lax.rsqrt(jnp.mean(o * o, axis=-1, keepdims=True) + NORM_EPS)
            o_ref[0, t * ATT_SUB:(t + 1) * ATT_SUB, hl] = o * g_ref[:, hl] * (1.0 - LAM_INIT)


def _diff_attention(proj, lam, g_subln, rope=None, ctx=None, tq=256, hps=1):
    b, s, _ = proj.shape
    use_rope, use_ctx = rope is not None, ctx is not None
    s_tot = s + (ctx[0].shape[1] if use_ctx else 0)
    w = hps * DV_A
    in_specs = [pl.BlockSpec(memory_space=pltpu.SMEM),
                pl.BlockSpec((1, tq, w), lambda bi, h, i: (bi, i, COL_Q // w + h)),
                pl.BlockSpec((1, s, w), lambda bi, h, i: (bi, 0, COL_K // w + h)),
                pl.BlockSpec((1, s, w), lambda bi, h, i: (bi, 0, COL_V // w + h))]
    args = [lam, proj, proj, proj]
    if use_rope:
        cos, sin = rope
        in_specs += [pl.BlockSpec((tq, DV_A), lambda bi, h, i: (i, 0)),
                     pl.BlockSpec((tq, DV_A), lambda bi, h, i: (i, 0)),
                     pl.BlockSpec((s, DV_A), lambda bi, h, i: (0, 0)),
                     pl.BlockSpec((s, DV_A), lambda bi, h, i: (0, 0))]
        args += [cos, sin, cos, sin]
    if use_ctx:
        p = ctx[0].shape[1]
        in_specs += [pl.BlockSpec((1, p, w), lambda bi, h, i: (bi, 0, h)),
                     pl.BlockSpec((1, p, w), lambda bi, h, i: (bi, 0, h))]
        args += [ctx[0], ctx[1]]
    in_specs.append(pl.BlockSpec((1, w), lambda bi, h, i: (0, h)))
    args.append(g_subln)
    return pl.pallas_call(
        functools.partial(_attn_kernel, use_rope=use_rope, use_ctx=use_ctx, s_new=s),
        grid=(b, H_A // hps, s // tq),
        in_specs=in_specs,
        out_specs=pl.BlockSpec((1, tq, w), lambda bi, h, i: (bi, i, h)),
        out_shape=jax.ShapeDtypeStruct((b, s, D_MODEL), F32),
        scratch_shapes=[pltpu.VMEM((hps, s_tot, DV_A), BF16), pltpu.VMEM((hps, DV_A, s_tot), BF16)],
        compiler_params=pltpu.CompilerParams(
            dimension_semantics=("arbitrary", "arbitrary", "arbitrary"),
            vmem_limit_bytes=VMEM_LIMIT),
        name="diff_attention",
    )(*args)


def _shifted(x_ref, p_ref, n_ref, mup, mun, first, last):
    x = x_ref[0].astype(F32)
    ts = x.shape[0]
    halo = p_ref.shape[1]
    row = lax.broadcasted_iota(jnp.int32, x.shape, 0)
    prev_row = p_ref[0][halo - 1:halo, :].astype(F32) * first
    next_row = n_ref[0][0:1, :].astype(F32) * last
    prev = jnp.where(row == 0, prev_row, pltpu.roll(x, 1, 0))
    nxt = jnp.where(row == ts - 1, next_row, pltpu.roll(x, ts - 1, 0))
    return x + mup * (prev - x) + mun * (nxt - x)


def _prep_kernel(r_ref, rp_ref, rn_ref, k_ref, kp_ref, kn_ref, v_ref, vp_ref, vn_ref,
                 l_ref, lp_ref, ln_ref, mup_ref, mun_ref, mupl_ref, munl_ref,
                 kk_ref, ka_ref, rk_ref, w0_ref, wup_ref, a0_ref, aup_ref, gup_ref, e_ref, et_ref,
                 ro_ref, vo_ref, kko_ref, kd_ref, ba_ref, lw_ref, bonus_ref, gate_ref, *, n_tiles):
    i = pl.program_id(1)
    first = (i > 0).astype(F32)
    last = (i < n_tiles - 1).astype(F32)
    mup, mun = mup_ref[...], mun_ref[...]
    d = D_MODEL
    r = _shifted(r_ref, rp_ref, rn_ref, mup[:, 0:d], mun[:, 0:d], first, last)
    k = _shifted(k_ref, kp_ref, kn_ref, mup[:, d:2 * d], mun[:, d:2 * d], first, last)
    v = _shifted(v_ref, vp_ref, vn_ref, mup[:, 2 * d:3 * d], mun[:, 2 * d:3 * d], first, last)
    lo = _shifted(l_ref, lp_ref, ln_ref, mupl_ref[...], munl_ref[...], first, last)
    xw = lo[:, 0:LORA_W]
    xa = lo[:, LORA_W:LORA_W + LORA_A]
    xg = lo[:, LORA_W + LORA_A:]
    g = (e_ref[...], et_ref[...])

    kk = k * kk_ref[...]
    nrm = jnp.sqrt(_headsum(kk * kk, g))
    kkn = kk / jnp.maximum(nrm, L2_EPS)
    ro_ref[0] = r
    vo_ref[0] = v
    kko_ref[0] = kkn
    gate_ref[0] = _dot(_sigmoid(xg), gup_ref[...])

    wlog = w0_ref[...] + _dot(jnp.tanh(xw), wup_ref[...])
    alog = a0_ref[...] + _dot(xa, aup_ref[...])
    ka = ka_ref[...]
    rrk = r * rk_ref[...]
    dots = None
    for dr in range(2):
        sl = slice(dr * d, (dr + 1) * d)
        lw_ref[dr, 0] = DECAY_SCALE * _sigmoid(wlog[:, sl])
        a = _sigmoid(alog[:, sl])
        kd = k * (1.0 + (a - 1.0) * ka)
        kd_ref[dr, 0] = kd
        ba_ref[dr, 0] = kkn * a
        t = rrk * kd
        dots = t if dots is None else dots + t
    bonus_ref[0] = _headsum(dots, g) * v


def _rwkv_prep(proj, p, ts):
    b, s, _ = proj.shape
    d = D_MODEL
    nt = s // ts
    halo = SUBLANES * (4 // proj.dtype.itemsize)
    hb = ts // halo
    nhb = s // halo

    def main(col, w):
        return pl.BlockSpec((1, ts, w), lambda bi, i: (bi, i, col // w))

    def prev(col, w):
        return pl.BlockSpec((1, halo, w), lambda bi, i: (bi, jnp.maximum(i * hb - 1, 0), col // w))

    def nxt(col, w):
        return pl.BlockSpec((1, halo, w),
                            lambda bi, i: (bi, jnp.minimum((i + 1) * hb, nhb - 1), col // w))

    def full(a):
        return pl.BlockSpec(a.shape, lambda bi, i: (0,) * a.ndim)

    in_specs, args = [], []
    for col, w in ((COL_R, d), (COL_KR, d), (COL_VR, d), (COL_LORA, C_LORA)):
        in_specs += [main(col, w), prev(col, w), nxt(col, w)]
        args += [proj, proj, proj]
    consts = [p['mu_prev_main'], p['mu_next_main'], p['mu_prev_lora'], p['mu_next_lora'],
              p['k_k'], p['k_a'], p['r_k'], p['w0'], p['w_up'], p['a0'], p['a_up'], p['g_up'],
              p['head_ind'], p['head_ind_t']]
    in_specs += [full(a) for a in consts]
    args += consts
    tok = pl.BlockSpec((1, ts, d), lambda bi, i: (bi, i, 0))
    tok2 = pl.BlockSpec((2, 1, ts, d), lambda bi, i: (0, bi, i, 0))
    one = jax.ShapeDtypeStruct((b, s, d), F32)
    two = jax.ShapeDtypeStruct((2, b, s, d), F32)
    return pl.pallas_call(
        functools.partial(_prep_kernel, n_tiles=nt),
        grid=(b, nt),
        in_specs=in_specs,
        out_specs=[tok, tok, tok, tok2, tok2, tok2, tok, tok],
        out_shape=[one, one, one, two, two, two, one, one],
        compiler_params=pltpu.CompilerParams(
            dimension_semantics=("arbitrary", "arbitrary"),
            vmem_limit_bytes=VMEM_LIMIT),
        name="rwkv_prep",
    )(*args)


def _scan_masks(reverse):
    n = CHUNK
    tt = lax.broadcasted_iota(jnp.int32, (n, n), 0)
    ss = lax.broadcasted_iota(jnp.int32, (n, n), 1)
    hi, lo = (ss, tt) if reverse else (tt, ss)
    tt2 = lax.broadcasted_iota(jnp.int32, (n, 2 * n), 0)
    col2 = lax.broadcasted_iota(jnp.int32, (n, 2 * n), 1)
    ss2 = col2 % n
    hi2, lo2 = (ss2, tt2) if reverse else (tt2, ss2)
    levels = []
    blk = 2
    while blk < n:
        levels.append((hi // (2 * blk) == lo // (2 * blk)) & ((hi // blk) % 2 == 1) & ((lo // blk) % 2 == 0))
        blk *= 2
    return dict(strict=hi > lo, incl=hi >= lo, eye=(tt == ss).astype(F32),
                first=(hi // 2 == lo // 2) & (hi > lo), levels=levels,
                strict_r=(hi2 > lo2) & (col2 >= n),
                incl2=hi2 >= lo2)


def _scan_operands(reverse, incl, lw, r, v, kkn, kd, ba):
    n = CHUNK
    tri = jnp.where(incl, 1.0, 0.0).astype(BF16)
    cum = sum(jnp.dot(tri, part, preferred_element_type=F32) for part in _split3(lw))
    tot = cum[0:1, :] if reverse else cum[n - 1:n, :]
    g_inv = jnp.exp(-cum)
    g_rest = jnp.exp(tot - cum)
    return dict(a=(kkn * jnp.exp(cum - lw)).astype(BF16), r=(r * jnp.exp(cum)).astype(BF16),
                b=(ba * g_inv).astype(BF16), k=(kd * g_inv).astype(BF16),
                bh=(ba * g_rest).astype(BF16), kh=(kd * g_rest).astype(BF16),
                v=v.astype(BF16), g_tot=jnp.exp(tot))


def _scan_kernel(*refs, has_s0, n_chunks):
    it = iter(refs)
    tok_refs = [[next(it) for _ in range(3)] for _ in range(2)]
    dir_refs = [[next(it) for _ in range(3)] for _ in range(2)]
    s0_ref = next(it) if has_s0 else None
    y_refs = [next(it), next(it)]
    sf_ref = next(it)
    s_scr = next(it)
    c = pl.program_id(1)

    @pl.when(c == 0)
    def _():
        if has_s0:
            s_scr[...] = s0_ref[0]
        else:
            s_scr[...] = jnp.zeros_like(s_scr)

    n = CHUNK
    masks = [_scan_masks(dr == 1) for dr in range(2)]
    ops = []
    for dr in range(2):
        r_ref, v_ref, kk_ref = tok_refs[dr]
        kd_ref, ba_ref, lw_ref = dir_refs[dr]
        ops.append(_scan_operands(dr == 1, masks[dr]['incl'], lw_ref[0, 0], r_ref[0], v_ref[0], kk_ref[0],
                                  kd_ref[0, 0], ba_ref[0, 0]))

    units = [(dr, h) for dr in range(2) for h in range(H_B)]
    idx = range(len(units))
    sls = [slice(h * N_B, (h + 1) * N_B) for _, h in units]
    op = lambda u, name: ops[units[u][0]][name][:, sls[u]]
    mk = lambda u, name: masks[units[u][0]][name]
    s_old = [s_scr[dr, h] for dr, h in units]
    ar = [jnp.concatenate([op(u, 'a'), op(u, 'r')], axis=0) for u in idx]
    bk = [jnp.concatenate([op(u, 'b'), op(u, 'k')], axis=0) for u in idx]
    bkh = [jnp.concatenate([op(u, 'bh'), op(u, 'kh')], axis=0) for u in idx]
    vh = [op(u, 'v') for u in idx]

    gram = [_dot_nt(ar[u], bk[u]) for u in idx]
    ars = [_dot_nt(ar[u], s_old[u]) for u in idx]
    nmat = [jnp.where(mk(u, 'strict'), gram[u][0:n, 0:n], 0.0) for u in idx]
    mak = [jnp.where(mk(u, 'strict_r'), gram[u][0:n, :], 0.0) for u in idx]
    pr = [jnp.where(mk(u, 'incl2'), gram[u][n:, :], 0.0) for u in idx]
    z = [ars[u][0:n] + _dot(mak[u], jnp.concatenate([vh[u], vh[u]], axis=0)) for u in idx]
    x = [mk(u, 'eye') - jnp.where(mk(u, 'first'), nmat[u], 0.0) for u in idx]
    for lvl in range(len(masks[0]['levels'])):
        xl = [_dot(x[u], jnp.where(mk(u, 'levels')[lvl], nmat[u], 0.0)) for u in idx]
        x = [x[u] - _dot(xl[u], x[u]) for u in idx]
    uu = [-_dot(x[u], z[u]) for u in idx]
    uv = [jnp.concatenate([uu[u].astype(BF16), vh[u]], axis=0) for u in idx]
    y = [ars[u][n:] + _dot(pr[u], uv[u]) for u in idx]
    s_new = [s_old[u] * ops[units[u][0]]['g_tot'][:, sls[u]] + _dot_tn(uv[u], bkh[u]) for u in idx]
    for u, (dr, h) in enumerate(units):
        y_refs[dr][0, :, sls[u]] = y[u]
    for u, (dr, h) in enumerate(units):
        s_scr[dr, h] = s_new[u]

    @pl.when(c == n_chunks - 1)
    def _():
        sf_ref[0] = s_scr[...]


def _rwkv_scan(r, v, kkn, kd, ba, lw, s0):
    b, s, d = r.shape
    nc = s // CHUNK
    fwd = pl.BlockSpec((1, CHUNK, d), lambda bi, c: (bi, c, 0))
    bwd = pl.BlockSpec((1, CHUNK, d), lambda bi, c: (bi, nc - 1 - c, 0))
    fwd2 = pl.BlockSpec((1, 1, CHUNK, d), lambda bi, c: (0, bi, c, 0))
    bwd2 = pl.BlockSpec((1, 1, CHUNK, d), lambda bi, c: (1, bi, nc - 1 - c, 0))
    state = pl.BlockSpec((1, 2, H_B, N_B, N_B), lambda bi, c: (bi, 0, 0, 0, 0))
    in_specs = [fwd] * 3 + [bwd] * 3 + [fwd2] * 3 + [bwd2] * 3
    args = [r, v, kkn, r, v, kkn, kd, ba, lw, kd, ba, lw]
    if s0 is not None:
        in_specs.append(state)
        args.append(s0)
    return pl.pallas_call(
        functools.partial(_scan_kernel, has_s0=s0 is not None, n_chunks=nc),
        grid=(b, nc),
        in_specs=in_specs,
        out_specs=[fwd, bwd, state],
        out_shape=[jax.ShapeDtypeStruct((b, s, d), F32),
                   jax.ShapeDtypeStruct((b, s, d), F32),
                   jax.ShapeDtypeStruct((b, 2, H_B, N_B, N_B), F32)],
        scratch_shapes=[pltpu.VMEM((2, H_B, N_B, N_B), F32)],
        compiler_params=pltpu.CompilerParams(
            dimension_semantics=("arbitrary", "arbitrary"),
            vmem_limit_bytes=VMEM_LIMIT),
        name="rwkv_scan",
    )(*args)


def _merge_kernel(x_ref, oa_ref, y0_ref, y1_ref, bonus_ref, gate_ref, g0_ref, g1_ref,
                  mg_ref, sc_ref, sh_ref, gnw_ref, gnb_ref, e_ref, et_ref, wout_ref,
                  gpost_ref, gpre_ref, wr_ref, br_ref, cnt0_ref,
                  x1_ref, h2_ref, route_ref, rank_ref, ew_ref, cnt_ref, cnt_scr):
    @pl.when((pl.program_id(0) == 0) & (pl.program_id(1) == 0))
    def _():
        cnt_scr[...] = cnt0_ref[...]

    g = (e_ref[...], et_ref[...])
    y = y0_ref[0] + y1_ref[0]
    mu = _headsum(y, g) * (1.0 / N_B)
    yc = y - mu
    var = _headsum(yc * yc, g) * (1.0 / N_B)
    yn = yc * lax.rsqrt(var + GN_EPS) * gnw_ref[...] + gnb_ref[...]
    ob = (yn + bonus_ref[0]) * gate_ref[0]
    merged = _sigmoid(g0_ref[0].astype(F32)) * oa_ref[0] + _sigmoid(g1_ref[0].astype(F32)) * ob
    out = _dot(merged, wout_ref[...])
    x1 = x_ref[0] + mg_ref[0] * _rms(out, gpost_ref[...])
    x1_ref[0] = x1
    h2 = _rms(x1, gpre_ref[...]) * (1.0 + sc_ref[0]) + sh_ref[0]
    h2_ref[0] = h2

    h_hi = h2.astype(BF16)
    h_lo = (h2 - h_hi.astype(F32)).astype(BF16)
    wr = wr_ref[...]
    w_hi = wr.astype(BF16)
    w_lo = (wr - w_hi.astype(F32)).astype(BF16)
    logits = (jnp.dot(h_hi, w_hi, preferred_element_type=F32) + jnp.dot(h_hi, w_lo, preferred_element_type=F32)
              + jnp.dot(h_lo, w_hi, preferred_element_type=F32)) + br_ref[...]
    lane = lax.broadcasted_iota(jnp.int32, logits.shape, 1)
    work = logits
    top = None
    picks = []
    for _ in range(TOP_K):
        mx = jnp.max(work, axis=-1, keepdims=True)
        idx = jnp.min(jnp.where(work == mx, lane, LANES), axis=-1, keepdims=True)
        if top is None:
            top = mx
        picks.append((idx, jnp.exp(mx - top)))
        work = jnp.where(lane == idx, -jnp.inf, work)
    denom = sum(e for _, e in picks)
    tm = logits.shape[0]
    onehot = jnp.zeros_like(logits)
    for idx, _ in picks:
        onehot = onehot + jnp.where(lane == idx, 1.0, 0.0)
    rr = lax.broadcasted_iota(jnp.int32, (tm, tm), 0)
    cc = lax.broadcasted_iota(jnp.int32, (tm, tm), 1)
    before = cnt_scr[...] + _dot(jnp.where(rr > cc, 1.0, 0.0), onehot)
    route = jnp.zeros(logits.shape, jnp.int32)
    ew = jnp.zeros_like(logits)
    for j, (idx, e) in enumerate(picks):
        route = jnp.where(lane == j, idx, route)
        ew = jnp.where(lane == j, e / denom, ew)
    route_ref[0] = route
    rank_ref[0] = jnp.where(onehot > 0.0, before, 0.0).astype(jnp.int32)
    ew_ref[0] = ew
    cnt_scr[...] = cnt_scr[...] + jnp.sum(onehot, axis=0, keepdims=True)
    cnt_ref[...] = cnt_scr[...]


def _merge_out(x, oa, y0, y1, bonus, gate, proj, mods, p, cnt0, tm):
    b, s, d = x.shape
    per_batch = mods[0].shape[0] > 1
    mod_map = (lambda bi, i: (bi, 0, 0)) if per_batch else (lambda bi, i: (0, 0, 0))
    tok = pl.BlockSpec((1, tm, d), lambda bi, i: (bi, i, 0))

    def full(a):
        return pl.BlockSpec(a.shape, lambda bi, i: (0,) * a.ndim)

    consts = [p['gn_w'], p['gn_b'], p['head_ind'], p['head_ind_t'], p['w_out'], p['g_post_mix'], p['g_pre_ffn'],
              p['w_router'], p['b_router']]
    in_specs = ([tok] * 6
                + [pl.BlockSpec((1, tm, d), lambda bi, i: (bi, i, COL_G0 // d)),
                   pl.BlockSpec((1, tm, d), lambda bi, i: (bi, i, COL_G1 // d))]
                + [pl.BlockSpec((1, 1, d), mod_map)] * 3
                + [full(a) for a in consts] + [full(cnt0)])
    return pl.pallas_call(
        _merge_kernel,
        grid=(b, s // tm),
        in_specs=in_specs,
        out_specs=[tok, tok] + [pl.BlockSpec((1, tm, LANES), lambda bi, i: (bi, i, 0))] * 3
                  + [pl.BlockSpec((1, LANES), lambda bi, i: (0, 0))],
        out_shape=[jax.ShapeDtypeStruct((b, s, d), F32),
                   jax.ShapeDtypeStruct((b, s, d), F32),
                   jax.ShapeDtypeStruct((b, s, LANES), jnp.int32),
                   jax.ShapeDtypeStruct((b, s, LANES), jnp.int32),
                   jax.ShapeDtypeStruct((b, s, LANES), F32),
                   jax.ShapeDtypeStruct((1, LANES), F32)],
        scratch_shapes=[pltpu.VMEM((1, LANES), F32)],
        compiler_params=pltpu.CompilerParams(
            dimension_semantics=("arbitrary", "arbitrary"),
            vmem_limit_bytes=VMEM_LIMIT),
        name="merge_out",
    )(x, oa, y0, y1, bonus, gate, proj, proj, *mods, *consts, cnt0)


def _route_metadata(eid, rank, counts, n_tiles):
    padded = (counts + MOE_TM - 1) // MOE_TM * MOE_TM
    ends = jnp.cumsum(padded)
    offs = ends - padded
    experts = jnp.arange(N_EXPERTS, dtype=jnp.int32)
    pos = jnp.sum(jnp.where(eid[:, :, None] == experts, (offs + rank)[:, None, :], 0), axis=-1)
    idx = jnp.arange(n_tiles, dtype=jnp.int32)
    valid = idx * MOE_TM < ends[-1]
    tile = jnp.where(valid, idx, ends[-1] // MOE_TM - 1)
    te = jnp.sum((ends[None, :] <= (tile * MOE_TM)[:, None]).astype(jnp.int32), axis=1)
    first = valid & jnp.concatenate([jnp.ones((1,), bool), te[1:] != te[:-1]])
    used = counts > 0
    slot_of = (jnp.cumsum(used.astype(jnp.int32)) - 1) % 2
    later = (experts[None, :] > experts[:, None]) & used[None, :]
    next_of = jnp.min(jnp.where(later, experts[None, :], N_EXPERTS), axis=1)
    next_of = jnp.where(next_of == N_EXPERTS, -1, next_of)
    sched = (tile, te, valid.astype(jnp.int32), first.astype(jnp.int32), slot_of[te], next_of[te])
    fill = jnp.concatenate([jnp.where(padded > counts, ends - MOE_TM, -1),
                            jnp.where(valid, -1, idx * MOE_TM)[eid.size // MOE_TM:]])
    return pos, tuple(a.astype(jnp.int32) for a in sched), fill.astype(jnp.int32)


def _dispatch_kernel(fill_ref, pos_ref, hp_ref, hs_ref, xs_hbm, zbuf, sem, fsem, *, n_p_tiles, n_fill):
    i = pl.program_id(0)

    def fill_copy(e):
        start = pl.multiple_of(fill_ref[e], MOE_TM)
        return pltpu.make_async_copy(zbuf, xs_hbm.at[pl.ds(start, MOE_TM)], fsem)

    @pl.when(i == 0)
    def _():
        zbuf[...] = jnp.zeros_like(zbuf)
        for e in range(n_fill):
            @pl.when(fill_ref[e] >= 0)
            def _(e=e):
                fill_copy(e).start()
        for e in range(n_fill):
            @pl.when(fill_ref[e] >= 0)
            def _(e=e):
                fill_copy(e).wait()

    def scatter(h_ref):
        for t in range(COMB_TB):
            for j in range(TOP_K):
                pltpu.make_async_copy(h_ref.at[pl.ds(t, 1)],
                                      xs_hbm.at[pl.ds(pos_ref[0, 0, j * COMB_TB + t], 1)], sem).start()

    @pl.when(i < n_p_tiles)
    def _():
        scatter(hp_ref)

    @pl.when(i >= n_p_tiles)
    def _():
        scatter(hs_ref)

    for _ in range(TOP_K):
        pltpu.make_async_copy(hp_ref, xs_hbm.at[pl.ds(0, COMB_TB)], sem).wait()


def _dispatch(pos, fill, h2p, h2s, n_tiles):
    d = D_MODEL
    n_p, n_s = h2p.shape[0], h2s.shape[0]
    n = (n_p + n_s) // COMB_TB
    n_pt = n_p // COMB_TB
    pos_t = pos.reshape(n, COMB_TB, TOP_K).transpose(0, 2, 1).reshape(n, 1, TOP_K * COMB_TB)
    grid_spec = pltpu.PrefetchScalarGridSpec(
        num_scalar_prefetch=1,
        grid=(n,),
        in_specs=[pl.BlockSpec((1, 1, TOP_K * COMB_TB), lambda i, fill: (i, 0, 0), memory_space=pltpu.SMEM),
                  pl.BlockSpec((COMB_TB, d), lambda i, fill: (jnp.minimum(i, n_pt - 1), 0)),
                  pl.BlockSpec((COMB_TB, d), lambda i, fill: (jnp.maximum(i - n_pt, 0), 0))],
        out_specs=pl.BlockSpec(memory_space=pl.ANY),
        scratch_shapes=[pltpu.VMEM((MOE_TM, d), F32),
                        pltpu.SemaphoreType.DMA(()),
                        pltpu.SemaphoreType.DMA(())])
    return pl.pallas_call(
        functools.partial(_dispatch_kernel, n_p_tiles=n_pt, n_fill=fill.shape[0]),
        grid_spec=grid_spec,
        out_shape=jax.ShapeDtypeStruct((n_tiles * MOE_TM, d), F32),
        compiler_params=pltpu.CompilerParams(
            dimension_semantics=("arbitrary",), vmem_limit_bytes=VMEM_LIMIT),
        name="moe_dispatch",
    )(fill, pos_t, h2p, h2s)


def _experts_kernel(tile_ref, te_ref, tv_ref, tf_ref, ws_ref, nx_ref, x_ref,
                    wgu_hbm, bgu_ref, wd_hbm, bd_ref, ys_ref, wgu_buf, wd_buf, wgu_scr, wd_scr, sem):
    i = pl.program_id(0)

    def weight_copies(e, slot):
        return (pltpu.make_async_copy(wgu_hbm.at[e], wgu_buf.at[slot], sem.at[0, slot]),
                pltpu.make_async_copy(wd_hbm.at[e], wd_buf.at[slot], sem.at[1, slot]))

    @pl.when(i == 0)
    def _():
        for cp in weight_copies(te_ref[0], 0):
            cp.start()

    @pl.when(tf_ref[i] == 1)
    def _():
        slot = ws_ref[i]

        @pl.when(nx_ref[i] >= 0)
        def _():
            for cp in weight_copies(nx_ref[i], 1 - slot):
                cp.start()

        for cp in weight_copies(te_ref[i], slot):
            cp.wait()
        wgu_scr[...] = wgu_buf[slot].astype(BF16)
        wd_scr[...] = wd_buf[slot].astype(BF16)

    @pl.when(tv_ref[i] == 0)
    def _():
        ys_ref[...] = jnp.zeros_like(ys_ref)

    @pl.when(tv_ref[i] == 1)
    def _():
        gu = jnp.dot(x_ref[...].astype(BF16), wgu_scr[...], preferred_element_type=F32) + bgu_ref[0]
        gate = jnp.minimum(gu[:, :D_MODEL], SWIGLU_LIMIT)
        up = jnp.clip(gu[:, D_MODEL:], -SWIGLU_LIMIT, SWIGLU_LIMIT)
        act = (up + 1.0) * gate * _sigmoid(SWIGLU_ALPHA * gate)
        ys_ref[...] = jnp.dot(act.astype(BF16), wd_scr[...], preferred_element_type=F32) + bd_ref[0]


def _experts(xs, sched, p, n_tiles):
    d = D_MODEL
    bias = lambda w: pl.BlockSpec((1, 1, w), lambda i, tile, te, *_: (te[i], 0, 0))
    grid_spec = pltpu.PrefetchScalarGridSpec(
        num_scalar_prefetch=len(sched),
        grid=(n_tiles,),
        in_specs=[pl.BlockSpec((MOE_TM, d), lambda i, tile, *_: (tile[i], 0)),
                  pl.BlockSpec(memory_space=pl.ANY), bias(2 * d),
                  pl.BlockSpec(memory_space=pl.ANY), bias(d)],
        out_specs=pl.BlockSpec((MOE_TM, d), lambda i, *_: (i, 0)),
        scratch_shapes=[pltpu.VMEM((2, d, 2 * d), F32),
                        pltpu.VMEM((2, d, d), F32),
                        pltpu.VMEM((d, 2 * d), BF16),
                        pltpu.VMEM((d, d), BF16),
                        pltpu.SemaphoreType.DMA((2, 2))])
    return pl.pallas_call(
        _experts_kernel,
        grid_spec=grid_spec,
        out_shape=jax.ShapeDtypeStruct((n_tiles * MOE_TM, d), F32),
        compiler_params=pltpu.CompilerParams(
            dimension_semantics=("arbitrary",), vmem_limit_bytes=VMEM_LIMIT),
        name="moe_experts",
    )(*sched, xs, p['w_gate_up'], p['b_gate_up'], p['w_down'], p['b_down'])


def _combine_kernel(pos_ref, posn_ref, ys_hbm, ew_ref, x1_ref, mg_ref, gpost_ref, o_ref, buf, sem, *, n):
    i = pl.program_id(0)
    slot = i % 2

    def row_copy(row, j, t, slot_):
        return pltpu.make_async_copy(ys_hbm.at[pl.ds(row, 1)], buf.at[slot_, j, pl.ds(t, 1)], sem.at[slot_])

    def gather(src_ref, slot_):
        for t in range(COMB_TB):
            for j in range(TOP_K):
                row_copy(src_ref[0, 0, j * COMB_TB + t], j, t, slot_).start()

    @pl.when(i == 0)
    def _():
        gather(pos_ref, 0)

    @pl.when(i + 1 < n)
    def _():
        gather(posn_ref, 1 - slot)

    for j in range(TOP_K):
        pltpu.make_async_copy(ys_hbm.at[pl.ds(0, COMB_TB)], buf.at[slot, j], sem.at[slot]).wait()
    ew = ew_ref[...]
    f = sum(ew[:, j:j + 1] * buf[slot, j] for j in range(TOP_K))
    o_ref[...] = x1_ref[...] + mg_ref[0] * _rms(f, gpost_ref[...])


def _combine(ys, pos, ew, x1, mod_gate, p):
    b, s, d = x1.shape
    n_tok = b * s
    n_tiles = n_tok // COMB_TB
    per_batch = mod_gate.shape[0] > 1
    if per_batch:
        mod_gate = jnp.repeat(mod_gate, s // COMB_TB, axis=0)
    mod_map = (lambda i: (i, 0, 0)) if per_batch else (lambda i: (0, 0, 0))
    pos_t = pos.reshape(n_tiles, COMB_TB, TOP_K).transpose(0, 2, 1).reshape(n_tiles, 1, TOP_K * COMB_TB)
    smem_tile = lambda f: pl.BlockSpec((1, 1, TOP_K * COMB_TB), f, memory_space=pltpu.SMEM)
    out = pl.pallas_call(
        functools.partial(_combine_kernel, n=n_tiles),
        grid=(n_tiles,),
        in_specs=[smem_tile(lambda i: (i, 0, 0)),
                  smem_tile(lambda i: (jnp.minimum(i + 1, n_tiles - 1), 0, 0)),
                  pl.BlockSpec(memory_space=pl.ANY),
                  pl.BlockSpec((COMB_TB, LANES), lambda i: (i, 0)),
                  pl.BlockSpec((COMB_TB, d), lambda i: (i, 0)),
                  pl.BlockSpec((1, 1, d), mod_map),
                  pl.BlockSpec((1, d), lambda i: (0, 0))],
        out_specs=pl.BlockSpec((COMB_TB, d), lambda i: (i, 0)),
        out_shape=jax.ShapeDtypeStruct((n_tok, d), F32),
        scratch_shapes=[pltpu.VMEM((2, TOP_K, COMB_TB, d), F32),
                        pltpu.SemaphoreType.DMA((2,))],
        compiler_params=pltpu.CompilerParams(
            dimension_semantics=("arbitrary",), vmem_limit_bytes=VMEM_LIMIT),
        name="moe_combine",
    )(pos_t, pos_t, ys, ew.reshape(n_tok, LANES), x1.reshape(n_tok, d), mod_gate, p['g_post_ffn'])
    return out.reshape(b, s, d)


def _rope_tables(n_tokens):
    rows = n_tokens // GRID_W
    row = jnp.repeat(jnp.arange(rows, dtype=F32), GRID_W)
    col = jnp.tile(jnp.arange(GRID_W, dtype=F32), rows)
    inv = ROPE_THETA ** (-jnp.arange(ROPE_FREQS, dtype=F32) / ROPE_FREQS)
    ang_r = row[:, None] * inv[None, :]
    ang_c = col[:, None] * inv[None, :]
    ang = jnp.concatenate([ang_r, ang_r, ang_c, ang_c] * 2, axis=-1)
    return jnp.cos(ang), jnp.sin(ang)


def _stream(x, mods, proj_w, lam, p, rope, ctx, s0, cnt0, proj_dtype, tm_proj, tm_tok):
    b, s, d = x.shape
    xf = x if mods[0].shape[0] > 1 else x.reshape(1, b * s, d)
    proj = _norm_proj(xf, p['g_pre_mix'], mods[1], mods[0], proj_w, tm_proj, 768, proj_dtype)
    proj = proj.reshape(b, s, C_IN)
    oa = _diff_attention(proj, lam, p['g_subln'], rope, ctx, tq=min(s, 4 * ATT_SUB),
                         hps=H_A if s <= ATT_SUB else 1)
    r, v, kkn, kd, ba, lw, bonus, gate = _rwkv_prep(proj, p, min(s, 256))
    y0, y1, sfin = _rwkv_scan(r, v, kkn, kd, ba, lw, s0)
    tb, tsq = xf.shape[0], xf.shape[1]
    x1, h2, route, rank, ew, cnt = _merge_out(
        *(a.reshape(tb, tsq, a.shape[-1]) for a in (x, oa, y0, y1, bonus, gate, proj)),
        (mods[2], mods[4], mods[3]), p, cnt0, tm_tok)
    n = b * s
    route = (route.reshape(n, LANES)[:, :TOP_K], rank.reshape(n, LANES)[:, :N_EXPERTS])
    return (x1.reshape(b, s, d), h2.reshape(n, d), route, ew.reshape(n, LANES), cnt,
            proj, sfin)


def kernel(x_prompt, x_sample, cache_k, cache_v, state_rwkv, c, c_ctx, w_ada, b_ada, g_pre_mix, g_post_mix, g_pre_ffn, g_post_ffn, w_in, mu_prev, mu_next, lam, g_subln, k_k, k_a, r_k, w0, w_up, a0, a_up, g_up, gn_w, gn_b, w_out, w_router, b_router, w_gate_up, b_gate_up, w_down, b_down):
    l = 0
    d = D_MODEL
    bp, sp, _ = x_prompt.shape
    bs, ss, _ = x_sample.shape

    n_cond = 1 + bs
    rows = -(-n_cond // SUBLANES) * SUBLANES
    cond = jnp.concatenate([c_ctx[None, :], c, jnp.zeros((rows - n_cond, d), F32)], axis=0)
    mod = _modulation(cond, w_ada[l], b_ada[l][None, :])
    mods_p = [mod[0:1, i * d:(i + 1) * d].reshape(1, 1, d) for i in range(N_MOD)]
    mods_s = [mod[1:1 + bs, i * d:(i + 1) * d].reshape(bs, 1, d) for i in range(N_MOD)]

    w = w_in[l]
    o_rwkv = 3 * d
    o_gate = o_rwkv + 3 * d + C_LORA
    w_perm = jnp.concatenate([w[:, :3 * d], w[:, o_gate:o_gate + 2 * d],
                              w[:, o_rwkv:o_rwkv + 3 * d], w[:, o_rwkv + 3 * d:o_gate]],
                             axis=1).astype(BF16)

    lq = lam[l]
    lam_val = (jnp.exp(jnp.sum(lq[0] * lq[1])) - jnp.exp(jnp.sum(lq[2] * lq[3])) + LAM_INIT).reshape(1)

    head = jnp.arange(d) // N_B
    wr = jnp.concatenate([w_router[l], jnp.zeros((d, LANES - N_EXPERTS), F32)], axis=1)
    br = jnp.concatenate([b_router[l], jnp.full((LANES - N_EXPERTS,), -jnp.inf, F32)])[None, :]
    mup, mun = mu_prev[l][None, :], mu_next[l][None, :]
    p = {
        'g_pre_mix': g_pre_mix[l][None, :], 'g_post_mix': g_post_mix[l][None, :],
        'g_pre_ffn': g_pre_ffn[l][None, :], 'g_post_ffn': g_post_ffn[l][None, :],
        'g_subln': g_subln[l][None, :],
        'mu_prev_main': mup[:, :3 * d], 'mu_next_main': mun[:, :3 * d],
        'mu_prev_lora': mup[:, 3 * d:], 'mu_next_lora': mun[:, 3 * d:],
        'k_k': k_k[l][None, :], 'k_a': k_a[l][None, :], 'r_k': r_k[l].reshape(1, d),
        'w0': w0[l].reshape(1, 2 * d),
        'w_up': jnp.concatenate([w_up[l, 0], w_up[l, 1]], axis=1).astype(BF16),
        'a0': a0[l].reshape(1, 2 * d),
        'a_up': jnp.concatenate([a_up[l, 0], a_up[l, 1]], axis=1).astype(BF16),
        'g_up': g_up[l].astype(BF16),
        'head_ind': (head[:, None] == jnp.arange(LANES)[None, :]).astype(BF16),
        'head_ind_t': (jnp.arange(LANES)[:, None] == head[None, :]).astype(BF16),
        'gn_w': gn_w[l][None, :], 'gn_b': gn_b[l][None, :],
        'w_out': w_out[l].astype(BF16),
        'w_router': wr, 'b_router': br,
        'w_gate_up': w_gate_up[l], 'b_gate_up': b_gate_up[l][:, None, :],
        'w_down': w_down[l], 'b_down': b_down[l][:, None, :],
    }

    rope = _rope_tables(ss)
    ctx = (cache_k[:, l].reshape(bs, -1, d), cache_v[:, l].reshape(bs, -1, d))

    cnt0 = jnp.zeros((1, LANES), F32)
    x1p, h2p, route_p, ewp, cnt_p, proj_p, sfin = _stream(
        x_prompt, mods_p, w_perm, lam_val, p, None, None, None, cnt0, F32, 1024, 512)
    x1s, h2s, route_s, ews, cnt_s, _, _ = _stream(
        x_sample, mods_s, w_perm, lam_val, p, rope, ctx, state_rwkv[:, l], cnt_p, BF16, 1024, 512)

    n_p, n_s = bp * sp, bs * ss
    n_tok = n_p + n_s
    eid, rank = (jnp.concatenate([a, b_], axis=0) for a, b_ in zip(route_p, route_s))
    counts = cnt_s[0, :N_EXPERTS].astype(jnp.int32)
    n_tiles = -(-(n_tok * TOP_K + N_EXPERTS * (MOE_TM - 1)) // MOE_TM)
    pos, sched, fill = _route_metadata(eid, rank, counts, n_tiles)
    xs = _dispatch(pos, fill, h2p, h2s, n_tiles)
    rows = _experts(xs, sched, p, n_tiles)
    yp = _combine(rows, pos[:n_p], ewp, x1p, mods_p[5], p)
    ys = _combine(rows, pos[n_p:], ews, x1s, mods_s[5], p)

    new_k = proj_p[:, :, COL_K:COL_K + d].reshape(bp, 1, sp, H_A, 2, HD_A)
    new_v = proj_p[:, :, COL_V:COL_V + d].reshape(bp, 1, sp, H_A, DV_A)
    return (yp, ys, new_k, new_v, sfin[:, None])
```

```python
import functools
import math

import jax
import jax.numpy as jnp
from jax import lax
from jax.experimental import pallas as pl
from jax.experimental.pallas import tpu as pltpu

F32 = jnp.float32
BF16 = jnp.bfloat16

D_MODEL = 1024
GRID_W = 64
HD_A = 64
DV_A = 2 * HD_A
H_A = D_MODEL // DV_A
N_B = 64
H_B = D_MODEL // N_B
LORA_W = 64
LORA_A = 64
LORA_G = 128
N_EXPERTS = 32
TOP_K = 4
SWIGLU_LIMIT = 7.0
SWIGLU_ALPHA = 1.702
ROPE_THETA = 10000.0
ROPE_FREQS = HD_A // 4
NORM_EPS = 1e-6
GN_EPS = 64e-5
L2_EPS = 1e-12
ATTN_SCALE = HD_A ** -0.5
N_MOD = 6
LAM_INIT = 0.8 - 0.6 * math.exp(-0.3 * 0)
DECAY_SCALE = -math.exp(-0.5)

LANES = 128
SUBLANES = 8
CHUNK = 64
MOE_TM = 512
PROJ_TN = 1408
COMB_TB = 256
ATT_SUB = 256
LOG2_E = math.log2(math.e)
VMEM_LIMIT = 56 * 1024 * 1024

COL_Q, COL_K, COL_V, COL_G0, COL_G1, COL_R, COL_KR, COL_VR, COL_LORA = (
    0, 1024, 2048, 3072, 4096, 5120, 6144, 7168, 8192)
C_LORA = LORA_W + LORA_A + LORA_G
C_IN = COL_LORA + C_LORA


def _sigmoid(x):
    return 1.0 / (1.0 + jnp.exp(-x))


def _dot(a, b):
    return jnp.dot(a.astype(BF16), b.astype(BF16), preferred_element_type=F32)


def _dot_nt(a, b):
    return lax.dot_general(a.astype(BF16), b.astype(BF16), (((1,), (1,)), ((), ())),
                           preferred_element_type=F32)


def _dot_tn(a, b):
    return lax.dot_general(a.astype(BF16), b.astype(BF16), (((0,), (0,)), ((), ())),
                           preferred_element_type=F32)


def _split3(x):
    hi = x.astype(BF16)
    r1 = x - hi.astype(F32)
    mid = r1.astype(BF16)
    lo = (r1 - mid.astype(F32)).astype(BF16)
    return hi, mid, lo


def _headsum(x, ind):
    e, et = ind
    hi = x.astype(BF16)
    lo = (x - hi.astype(F32)).astype(BF16)
    s = jnp.dot(hi, e, preferred_element_type=F32) + jnp.dot(lo, e, preferred_element_type=F32)
    s_hi = s.astype(BF16)
    s_lo = (s - s_hi.astype(F32)).astype(BF16)
    return jnp.dot(s_hi, et, preferred_element_type=F32) + jnp.dot(s_lo, et, preferred_element_type=F32)


def _rms(x, g):
    return x * lax.rsqrt(jnp.mean(x * x, axis=-1, keepdims=True) + NORM_EPS) * g


def _mod_kernel(c_ref, w_ref, b_ref, o_ref):
    c = c_ref[...]
    s = c * _sigmoid(c)
    o_ref[...] = _dot(s, w_ref[...]) + b_ref[...]


def _modulation(cond, w_ada, b_ada):
    rows, d = cond.shape
    n = w_ada.shape[1]
    tn = 768
    return pl.pallas_call(
        _mod_kernel,
        grid=(n // tn,),
        in_specs=[pl.BlockSpec((rows, d), lambda j: (0, 0)),
                  pl.BlockSpec((d, tn), lambda j: (0, j)),
                  pl.BlockSpec((1, tn), lambda j: (0, j))],
        out_specs=pl.BlockSpec((rows, tn), lambda j: (0, j)),
        out_shape=jax.ShapeDtypeStruct((rows, n), F32),
        name="modulation",
    )(cond, w_ada, b_ada)


def _proj_kernel(x_ref, g_ref, sc_ref, sh_ref, w_ref, o_ref, h_scr):
    @pl.when(pl.program_id(2) == 0)
    def _():
        h = _rms(x_ref[0], g_ref[...]) * (1.0 + sc_ref[0]) + sh_ref[0]
        h_scr[...] = h.astype(BF16)

    o_ref[0] = jnp.dot(h_scr[...], w_ref[...], preferred_element_type=F32).astype(o_ref.dtype)


def _norm_proj(x, g, scale, shift, w, tm, tn, out_dtype):
    b, s, d = x.shape
    n = w.shape[1]
    per_batch = scale.shape[0] > 1
    mod_map = (lambda bi, i, j: (bi, 0, 0)) if per_batch else (lambda bi, i, j: (0, 0, 0))
    return pl.pallas_call(
        _proj_kernel,
        grid=(b, s // tm, n // tn),
        in_specs=[pl.BlockSpec((1, tm, d), lambda bi, i, j: (bi, i, 0)),
                  pl.BlockSpec((1, d), lambda bi, i, j: (0, 0)),
                  pl.BlockSpec((1, 1, d), mod_map),
                  pl.BlockSpec((1, 1, d), mod_map),
                  pl.BlockSpec((d, tn), lambda bi, i, j: (0, j))],
        out_specs=pl.BlockSpec((1, tm, tn), lambda bi, i, j: (bi, i, j)),
        out_shape=jax.ShapeDtypeStruct((b, s, n), out_dtype),
        scratch_shapes=[pltpu.VMEM((tm, d), BF16)],
        compiler_params=pltpu.CompilerParams(
            dimension_semantics=("arbitrary", "arbitrary", "arbitrary"),
            vmem_limit_bytes=VMEM_LIMIT),
        name="norm_proj",
    )(x, g, scale, shift, w)


def _rope(x, cos, sin):
    lane = lax.broadcasted_iota(jnp.int32, x.shape, 1)
    even = (lane // ROPE_FREQS) % 2 == 0
    rot = jnp.where(even, -pltpu.roll(x, LANES - ROPE_FREQS, 1), pltpu.roll(x, ROPE_FREQS, 1))
    return x * cos + rot * sin


def _attn_kernel(*refs, use_rope, use_ctx, s_new):
    it = iter(refs)
    lam_ref = next(it)
    q_ref, k_ref, v_ref = next(it), next(it), next(it)
    if use_rope:
        cq_ref, sq_ref, ck_ref, sk_ref = next(it), next(it), next(it), next(it)
    if use_ctx:
        ctxk_ref, ctxv_ref = next(it), next(it)
    g_ref = next(it)
    o_ref = next(it)
    k_scr, vt_scr = next(it), next(it)

    hps = q_ref.shape[-1] // DV_A
    head_lanes = [slice(hh * DV_A, (hh + 1) * DV_A) for hh in range(hps)]

    @pl.when(pl.program_id(2) == 0)
    def _():
        for hh, hl in enumerate(head_lanes):
            k = k_ref[0][:, hl].astype(F32)
            if use_rope:
                k = _rope(k, ck_ref[...], sk_ref[...])
            k_scr[hh, 0:s_new, :] = k.astype(BF16)
            vt_scr[hh, :, 0:s_new] = v_ref[0][:, hl].astype(F32).T.astype(BF16)
            if use_ctx:
                k_scr[hh, s_new:, :] = ctxk_ref[0][:, hl].astype(BF16)
                vt_scr[hh, :, s_new:] = ctxv_ref[0][:, hl].T.astype(BF16)

    lam = lam_ref[0]
    n_sub = q_ref.shape[1] // ATT_SUB
    lane = lax.broadcasted_iota(jnp.int32, (ATT_SUB, DV_A), 1)
    sts = []
    for hh, hl in enumerate(head_lanes):
        q = q_ref[0][:, hl].astype(F32)
        if use_rope:
            q = _rope(q, cq_ref[...], sq_ref[...])
        q = q * (ATTN_SCALE * LOG2_E)
        for t in range(n_sub):
            for m in range(2):
                in_map = (lane >= HD_A) if m else (lane < HD_A)
                qm = jnp.where(in_map, q[t * ATT_SUB:(t + 1) * ATT_SUB], 0.0)
                sts.append((hh, _dot_nt(k_scr[hh], qm)))
    outs = []
    for hh, st in sts:
        et = jnp.exp2(st - jnp.max(st, axis=0, keepdims=True))
        l = jnp.sum(et, axis=0, keepdims=True)
        ot = jnp.dot(vt_scr[hh], et.astype(BF16), preferred_element_type=F32)
        outs.append(ot / l)
    for hh, hl in enumerate(head_lanes):
        for t in range(n_sub):
            u = 2 * (hh * n_sub + t)
            o = (outs[u] - lam * outs[u + 1]).T
            o = o * lax.rsqrt(jnp.mean(o * o, axis=-1, keepdims=True) + NORM_EPS)
            o_ref[0, t * ATT_SUB:(t + 1) * ATT_SUB, hl] = o * g_ref[:, hl] * (1.0 - LAM_INIT)


def _diff_attention(proj, lam, g_subln, rope=None, ctx=None, tq=256, hps=1):
    b, s, _ = proj.shape
    use_rope, use_ctx = rope is not None, ctx is not None
    s_tot = s + (ctx[0].shape[1] if use_ctx else 0)
    w = hps * DV_A
    in_specs = [pl.BlockSpec(memory_space=pltpu.SMEM),
                pl.BlockSpec((1, tq, w), lambda bi, h, i: (bi, i, COL_Q // w + h)),
                pl.BlockSpec((1, s, w), lambda bi, h, i: (bi, 0, COL_K // w + h)),
                pl.BlockSpec((1, s, w), lambda bi, h, i: (bi, 0, COL_V // w + h))]
    args = [lam, proj, proj, proj]
    if use_rope:
        cos, sin = rope
        in_specs += [pl.BlockSpec((tq, DV_A), lambda bi, h, i: (i, 0)),
                     pl.BlockSpec((tq, DV_A), lambda bi, h, i: (i, 0)),
                     pl.BlockSpec((s, DV_A), lambda bi, h, i: (0, 0)),
                     pl.BlockSpec((s, DV_A), lambda bi, h, i: (0, 0))]
        args += [cos, sin, cos, sin]
    if use_ctx:
        p = ctx[0].shape[1]
        in_specs += [pl.BlockSpec((1, p, w), lambda bi, h, i: (bi, 0, h)),
                     pl.BlockSpec((1, p, w), lambda bi, h, i: (bi, 0, h))]
        args += [ctx[0], ctx[1]]
    in_specs.append(pl.BlockSpec((1, w), lambda bi, h, i: (0, h)))
    args.append(g_subln)
    return pl.pallas_call(
        functools.partial(_attn_kernel, use_rope=use_rope, use_ctx=use_ctx, s_new=s),
        grid=(b, H_A // hps, s // tq),
        in_specs=in_specs,
        out_specs=pl.BlockSpec((1, tq, w), lambda bi, h, i: (bi, i, h)),
        out_shape=jax.ShapeDtypeStruct((b, s, D_MODEL), F32),
        scratch_shapes=[pltpu.VMEM((hps, s_tot, DV_A), BF16), pltpu.VMEM((hps, DV_A, s_tot), BF16)],
        compiler_params=pltpu.CompilerParams(
            dimension_semantics=("arbitrary", "arbitrary", "arbitrary"),
            vmem_limit_bytes=VMEM_LIMIT),
        name="diff_attention",
    )(*args)


def _shifted(x_ref, p_ref, n_ref, mup, mun, first, last):
    x = x_ref[0].astype(F32)
    ts = x.shape[0]
    halo = p_ref.shape[1]
    row = lax.broadcasted_iota(jnp.int32, x.shape, 0)
    prev_row = p_ref[0][halo - 1:halo, :].astype(F32) * first
    next_row = n_ref[0][0:1, :].astype(F32) * last
    prev = jnp.where(row == 0, prev_row, pltpu.roll(x, 1, 0))
    nxt = jnp.where(row == ts - 1, next_row, pltpu.roll(x, ts - 1, 0))
    return x + mup * (prev - x) + mun * (nxt - x)


def _prep_kernel(r_ref, rp_ref, rn_ref, k_ref, kp_ref, kn_ref, v_ref, vp_ref, vn_ref,
                 l_ref, lp_ref, ln_ref, mup_ref, mun_ref, mupl_ref, munl_ref,
                 kk_ref, ka_ref, rk_ref, w0_ref, wup_ref, a0_ref, aup_ref, gup_ref, e_ref, et_ref,
                 ro_ref, vo_ref, kko_ref, kd_ref, ba_ref, lw_ref, bonus_ref, gate_ref, *, n_tiles):
    i = pl.program_id(1)
    first = (i > 0).astype(F32)
    last = (i < n_tiles - 1).astype(F32)
    mup, mun = mup_ref[...], mun_ref[...]
    d = D_MODEL
    r = _shifted(r_ref, rp_ref, rn_ref, mup[:, 0:d], mun[:, 0:d], first, last)
    k = _shifted(k_ref, kp_ref, kn_ref, mup[:, d:2 * d], mun[:, d:2 * d], first, last)
    v = _shifted(v_ref, vp_ref, vn_ref, mup[:, 2 * d:3 * d], mun[:, 2 * d:3 * d], first, last)
    lo = _shifted(l_ref, lp_ref, ln_ref, mupl_ref[...], munl_ref[...], first, last)
    xw = lo[:, 0:LORA_W]
    xa = lo[:, LORA_W:LORA_W + LORA_A]
    xg = lo[:, LORA_W + LORA_A:]
    g = (e_ref[...], et_ref[...])

    kk = k * kk_ref[...]
    nrm = jnp.sqrt(_headsum(kk * kk, g))
    kkn = kk / jnp.maximum(nrm, L2_EPS)
    ro_ref[0] = r
    vo_ref[0] = v
    kko_ref[0] = kkn
    gate_ref[0] = _dot(_sigmoid(xg), gup_ref[...])

    wlog = w0_ref[...] + _dot(jnp.tanh(xw), wup_ref[...])
    alog = a0_ref[...] + _dot(xa, aup_ref[...])
    ka = ka_ref[...]
    rrk = r * rk_ref[...]
    dots = None
    for dr in range(2):
        sl = slice(dr * d, (dr + 1) * d)
        lw_ref[dr, 0] = DECAY_SCALE * _sigmoid(wlog[:, sl])
        a = _sigmoid(alog[:, sl])
        kd = k * (1.0 + (a - 1.0) * ka)
        kd_ref[dr, 0] = kd
        ba_ref[dr, 0] = kkn * a
        t = rrk * kd
        dots = t if dots is None else dots + t
    bonus_ref[0] = _headsum(dots, g) * v


def _rwkv_prep(proj, p, ts):
    b, s, _ = proj.shape
    d = D_MODEL
    nt = s // ts
    halo = SUBLANES * (4 // proj.dtype.itemsize)
    hb = ts // halo
    nhb = s // halo

    def main(col, w):
        return pl.BlockSpec((1, ts, w), lambda bi, i: (bi, i, col // w))

    def prev(col, w):
        return pl.BlockSpec((1, halo, w), lambda bi, i: (bi, jnp.maximum(i * hb - 1, 0), col // w))

    def nxt(col, w):
        return pl.BlockSpec((1, halo, w),
                            lambda bi, i: (bi, jnp.minimum((i + 1) * hb, nhb - 1), col // w))

    def full(a):
        return pl.BlockSpec(a.shape, lambda bi, i: (0,) * a.ndim)

    in_specs, args = [], []
    for col, w in ((COL_R, d), (COL_KR, d), (COL_VR, d), (COL_LORA, C_LORA)):
        in_specs += [main(col, w), prev(col, w), nxt(col, w)]
        args += [proj, proj, proj]
    consts = [p['mu_prev_main'], p['mu_next_main'], p['mu_prev_lora'], p['mu_next_lora'],
              p['k_k'], p['k_a'], p['r_k'], p['w0'], p['w_up'], p['a0'], p['a_up'], p['g_up'],
              p['head_ind'], p['head_ind_t']]
    in_specs += [full(a) for a in consts]
    args += consts
    tok = pl.BlockSpec((1, ts, d), lambda bi, i: (bi, i, 0))
    tok2 = pl.BlockSpec((2, 1, ts, d), lambda bi, i: (0, bi, i, 0))
    one = jax.ShapeDtypeStruct((b, s, d), F32)
    two = jax.ShapeDtypeStruct((2, b, s, d), F32)
    return pl.pallas_call(
        functools.partial(_prep_kernel, n_tiles=nt),
        grid=(b, nt),
        in_specs=in_specs,
        out_specs=[tok, tok, tok, tok2, tok2, tok2, tok, tok],
        out_shape=[one, one, one, two, two, two, one, one],
        compiler_params=pltpu.CompilerParams(
            dimension_semantics=("arbitrary", "arbitrary"),
            vmem_limit_bytes=VMEM_LIMIT),
        name="rwkv_prep",
    )(*args)


def _scan_masks(reverse):
    n = CHUNK
    tt = lax.broadcasted_iota(jnp.int32, (n, n), 0)
    ss = lax.broadcasted_iota(jnp.int32, (n, n), 1)
    hi, lo = (ss, tt) if reverse else (tt, ss)
    tt2 = lax.broadcasted_iota(jnp.int32, (n, 2 * n), 0)
    col2 = lax.broadcasted_iota(jnp.int32, (n, 2 * n), 1)
    ss2 = col2 % n
    hi2, lo2 = (ss2, tt2) if reverse else (tt2, ss2)
    levels = []
    blk = 2
    while blk < n:
        levels.append((hi // (2 * blk) == lo // (2 * blk)) & ((hi // blk) % 2 == 1) & ((lo // blk) % 2 == 0))
        blk *= 2
    return dict(strict=hi > lo, incl=hi >= lo, eye=(tt == ss).astype(F32),
                first=(hi // 2 == lo // 2) & (hi > lo), levels=levels,
                strict_r=(hi2 > lo2) & (col2 >= n),
                incl2=hi2 >= lo2)


def _scan_operands(reverse, incl, lw, r, v, kkn, kd, ba):
    n = CHUNK
    tri = jnp.where(incl, 1.0, 0.0).astype(BF16)
    cum = sum(jnp.dot(tri, part, preferred_element_type=F32) for part in _split3(lw))
    tot = cum[0:1, :] if reverse else cum[n - 1:n, :]
    g_inv = jnp.exp(-cum)
    g_rest = jnp.exp(tot - cum)
    return dict(a=(kkn * jnp.exp(cum - lw)).astype(BF16), r=(r * jnp.exp(cum)).astype(BF16),
                b=(ba * g_inv).astype(BF16), k=(kd * g_inv).astype(BF16),
                bh=(ba * g_rest).astype(BF16), kh=(kd * g_rest).astype(BF16),
                v=v.astype(BF16), g_tot=jnp.exp(tot))


def _scan_kernel(*refs, has_s0, n_chunks):
    it = iter(refs)
    tok_refs = [[next(it) for _ in range(3)] for _ in range(2)]
    dir_refs = [[next(it) for _ in range(3)] for _ in range(2)]
    s0_ref = next(it) if has_s0 else None
    y_refs = [next(it), next(it)]
    sf_ref = next(it)
    s_scr = next(it)
    c = pl.program_id(1)

    @pl.when(c == 0)
    def _():
        if has_s0:
            s_scr[...] = s0_ref[0]
        else:
            s_scr[...] = jnp.zeros_like(s_scr)

    n = CHUNK
    masks = [_scan_masks(dr == 1) for dr in range(2)]
    ops = []
    for dr in range(2):
        r_ref, v_ref, kk_ref = tok_refs[dr]
        kd_ref, ba_ref, lw_ref = dir_refs[dr]
        ops.append(_scan_operands(dr == 1, masks[dr]['incl'], lw_ref[0, 0], r_ref[0], v_ref[0], kk_ref[0],
                                  kd_ref[0, 0], ba_ref[0, 0]))

    units = [(dr, h) for dr in range(2) for h in range(H_B)]
    idx = range(len(units))
    sls = [slice(h * N_B, (h + 1) * N_B) for _, h in units]
    op = lambda u, name: ops[units[u][0]][name][:, sls[u]]
    mk = lambda u, name: masks[units[u][0]][name]
    s_old = [s_scr[dr, h] for dr, h in units]
    ar = [jnp.concatenate([op(u, 'a'), op(u, 'r')], axis=0) for u in idx]
    bk = [jnp.concatenate([op(u, 'b'), op(u, 'k')], axis=0) for u in idx]
    bkh = [jnp.concatenate([op(u, 'bh'), op(u, 'kh')], axis=0) for u in idx]
    vh = [op(u, 'v') for u in idx]

    gram = [_dot_nt(ar[u], bk[u]) for u in idx]
    ars = [_dot_nt(ar[u], s_old[u]) for u in idx]
    nmat = [jnp.where(mk(u, 'strict'), gram[u][0:n, 0:n], 0.0) for u in idx]
    mak = [jnp.where(mk(u, 'strict_r'), gram[u][0:n, :], 0.0) for u in idx]
    pr = [jnp.where(mk(u, 'incl2'), gram[u][n:, :], 0.0) for u in idx]
    z = [ars[u][0:n] + _dot(mak[u], jnp.concatenate([vh[u], vh[u]], axis=0)) for u in idx]
    x = [mk(u, 'eye') - jnp.where(mk(u, 'first'), nmat[u], 0.0) for u in idx]
    for lvl in range(len(masks[0]['levels'])):
        xl = [_dot(x[u], jnp.where(mk(u, 'levels')[lvl], nmat[u], 0.0)) for u in idx]
        x = [x[u] - _dot(xl[u], x[u]) for u in idx]
    uu = [-_dot(x[u], z[u]) for u in idx]
    uv = [jnp.concatenate([uu[u].astype(BF16), vh[u]], axis=0) for u in idx]
    y = [ars[u][n:] + _dot(pr[u], uv[u]) for u in idx]
    s_new = [s_old[u] * ops[units[u][0]]['g_tot'][:, sls[u]] + _dot_tn(uv[u], bkh[u]) for u in idx]
    for u, (dr, h) in enumerate(units):
        y_refs[dr][0, :, sls[u]] = y[u]
    for u, (dr, h) in enumerate(units):
        s_scr[dr, h] = s_new[u]

    @pl.when(c == n_chunks - 1)
    def _():
        sf_ref[0] = s_scr[...]


def _rwkv_scan(r, v, kkn, kd, ba, lw, s0):
    b, s, d = r.shape
    nc = s // CHUNK
    fwd = pl.BlockSpec((1, CHUNK, d), lambda bi, c: (bi, c, 0))
    bwd = pl.BlockSpec((1, CHUNK, d), lambda bi, c: (bi, nc - 1 - c, 0))
    fwd2 = pl.BlockSpec((1, 1, CHUNK, d), lambda bi, c: (0, bi, c, 0))
    bwd2 = pl.BlockSpec((1, 1, CHUNK, d), lambda bi, c: (1, bi, nc - 1 - c, 0))
    state = pl.BlockSpec((1, 2, H_B, N_B, N_B), lambda bi, c: (bi, 0, 0, 0, 0))
    in_specs = [fwd] * 3 + [bwd] * 3 + [fwd2] * 3 + [bwd2] * 3
    args = [r, v, kkn, r, v, kkn, kd, ba, lw, kd, ba, lw]
    if s0 is not None:
        in_specs.append(state)
        args.append(s0)
    return pl.pallas_call(
        functools.partial(_scan_kernel, has_s0=s0 is not None, n_chunks=nc),
        grid=(b, nc),
        in_specs=in_specs,
        out_specs=[fwd, bwd, state],
        out_shape=[jax.ShapeDtypeStruct((b, s, d), F32),
                   jax.ShapeDtypeStruct((b, s, d), F32),
                   jax.ShapeDtypeStruct((b, 2, H_B, N_B, N_B), F32)],
        scratch_shapes=[pltpu.VMEM((2, H_B, N_B, N_B), F32)],
        compiler_params=pltpu.CompilerParams(
            dimension_semantics=("arbitrary", "arbitrary"),
            vmem_limit_bytes=VMEM_LIMIT),
        name="rwkv_scan",
    )(*args)


def _merge_kernel(x_ref, oa_ref, y0_ref, y1_ref, bonus_ref, gate_ref, g0_ref, g1_ref,
                  mg_ref, sc_ref, sh_ref, gnw_ref, gnb_ref, e_ref, et_ref, wout_ref,
                  gpost_ref, gpre_ref, wr_ref, br_ref, cnt0_ref,
                  x1_ref, h2_ref, route_ref, rank_ref, ew_ref, cnt_ref, cnt_scr):
    @pl.when((pl.program_id(0) == 0) & (pl.program_id(1) == 0))
    def _():
        cnt_scr[...] = cnt0_ref[...]

    g = (e_ref[...], et_ref[...])
    y = y0_ref[0] + y1_ref[0]
    mu = _headsum(y, g) * (1.0 / N_B)
    yc = y - mu
    var = _headsum(yc * yc, g) * (1.0 / N_B)
    yn = yc * lax.rsqrt(var + GN_EPS) * gnw_ref[...] + gnb_ref[...]
    ob = (yn + bonus_ref[0]) * gate_ref[0]
    merged = _sigmoid(g0_ref[0].astype(F32)) * oa_ref[0] + _sigmoid(g1_ref[0].astype(F32)) * ob
    out = _dot(merged, wout_ref[...])
    x1 = x_ref[0] + mg_ref[0] * _rms(out, gpost_ref[...])
    x1_ref[0] = x1
    h2 = _rms(x1, gpre_ref[...]) * (1.0 + sc_ref[0]) + sh_ref[0]
    h2_ref[0] = h2

    h_hi = h2.astype(BF16)
    h_lo = (h2 - h_hi.astype(F32)).astype(BF16)
    wr = wr_ref[...]
    w_hi = wr.astype(BF16)
    w_lo = (wr - w_hi.astype(F32)).astype(BF16)
    logits = (jnp.dot(h_hi, w_hi, preferred_element_type=F32) + jnp.dot(h_hi, w_lo, preferred_element_type=F32)
              + jnp.dot(h_lo, w_hi, preferred_element_type=F32)) + br_ref[...]
    lane = lax.broadcasted_iota(jnp.int32, logits.shape, 1)
    work = logits
    top = None
    picks = []
    for _ in range(TOP_K):
        mx = jnp.max(work, axis=-1, keepdims=True)
        idx = jnp.min(jnp.where(work == mx, lane, LANES), axis=-1, keepdims=True)
        if top is None:
            top = mx
        picks.append((idx, jnp.exp(mx - top)))
        work = jnp.where(lane == idx, -jnp.inf, work)
    denom = sum(e for _, e in picks)
    tm = logits.shape[0]
    onehot = jnp.zeros_like(logits)
    for idx, _ in picks:
        onehot = onehot + jnp.where(lane == idx, 1.0, 0.0)
    rr = lax.broadcasted_iota(jnp.int32, (tm, tm), 0)
    cc = lax.broadcasted_iota(jnp.int32, (tm, tm), 1)
    before = cnt_scr[...] + _dot(jnp.where(rr > cc, 1.0, 0.0), onehot)
    route = jnp.zeros(logits.shape, jnp.int32)
    ew = jnp.zeros_like(logits)
    for j, (idx, e) in enumerate(picks):
        route = jnp.where(lane == j, idx, route)
        ew = jnp.where(lane == j, e / denom, ew)
    route_ref[0] = route
    rank_ref[0] = jnp.where(onehot > 0.0, before, 0.0).astype(jnp.int32)
    ew_ref[0] = ew
    cnt_scr[...] = cnt_scr[...] + jnp.sum(onehot, axis=0, keepdims=True)
    cnt_ref[...] = cnt_scr[...]


def _merge_out(x, oa, y0, y1, bonus, gate, proj, mods, p, cnt0, tm):
    b, s, d = x.shape
    per_batch = mods[0].shape[0] > 1
    mod_map = (lambda bi, i: (bi, 0, 0)) if per_batch else (lambda bi, i: (0, 0, 0))
    tok = pl.BlockSpec((1, tm, d), lambda bi, i: (bi, i, 0))

    def full(a):
        return pl.BlockSpec(a.shape, lambda bi, i: (0,) * a.ndim)

    consts = [p['gn_w'], p['gn_b'], p['head_ind'], p['head_ind_t'], p['w_out'], p['g_post_mix'], p['g_pre_ffn'],
              p['w_router'], p['b_router']]
    in_specs = ([tok] * 6
                + [pl.BlockSpec((1, tm, d), lambda bi, i: (bi, i, COL_G0 // d)),
                   pl.BlockSpec((1, tm, d), lambda bi, i: (bi, i, COL_G1 // d))]
                + [pl.BlockSpec((1, 1, d), mod_map)] * 3
                + [full(a) for a in consts] + [full(cnt0)])
    return pl.pallas_call(
        _merge_kernel,
        grid=(b, s // tm),
        in_specs=in_specs,
        out_specs=[tok, tok] + [pl.BlockSpec((1, tm, LANES), lambda bi, i: (bi, i, 0))] * 3
                  + [pl.BlockSpec((1, LANES), lambda bi, i: (0, 0))],
        out_shape=[jax.ShapeDtypeStruct((b, s, d), F32),
                   jax.ShapeDtypeStruct((b, s, d), F32),
                   jax.ShapeDtypeStruct((b, s, LANES), jnp.int32),
                   jax.ShapeDtypeStruct((b, s, LANES), jnp.int32),
                   jax.ShapeDtypeStruct((b, s, LANES), F32),
                   jax.ShapeDtypeStruct((1, LANES), F32)],
        scratch_shapes=[pltpu.VMEM((1, LANES), F32)],
        compiler_params=pltpu.CompilerParams(
            dimension_semantics=("arbitrary", "arbitrary"),
            vmem_limit_bytes=VMEM_LIMIT),
        name="merge_out",
    )(x, oa, y0, y1, bonus, gate, proj, proj, *mods, *consts, cnt0)


def _route_metadata(eid, rank, counts, n_tiles):
    padded = (counts + MOE_TM - 1) // MOE_TM * MOE_TM
    ends = jnp.cumsum(padded)
    offs = ends - padded
    experts = jnp.arange(N_EXPERTS, dtype=jnp.int32)
    pos = jnp.sum(jnp.where(eid[:, :, None] == experts, (offs + rank)[:, None, :], 0), axis=-1)
    idx = jnp.arange(n_tiles, dtype=jnp.int32)
    valid = idx * MOE_TM < ends[-1]
    tile = jnp.where(valid, idx, ends[-1] // MOE_TM - 1)
    te = jnp.sum((ends[None, :] <= (tile * MOE_TM)[:, None]).astype(jnp.int32), axis=1)
    first = valid & jnp.concatenate([jnp.ones((1,), bool), te[1:] != te[:-1]])
    used = counts > 0
    slot_of = (jnp.cumsum(used.astype(jnp.int32)) - 1) % 2
    later = (experts[None, :] > experts[:, None]) & used[None, :]
    next_of = jnp.min(jnp.where(later, experts[None, :], N_EXPERTS), axis=1)
    next_of = jnp.where(next_of == N_EXPERTS, -1, next_of)
    is_te = te[:, None] == experts[None, :]
    pick = lambda table: jnp.sum(jnp.where(is_te, table[None, :], 0), axis=1)
    sched = (tile, te, valid.astype(jnp.int32), first.astype(jnp.int32), pick(slot_of), pick(next_of))
    fill = jnp.concatenate([jnp.where(padded > counts, ends - MOE_TM, -1),
                            jnp.where(valid, -1, idx * MOE_TM)[eid.size // MOE_TM:]])
    return pos, tuple(a.astype(jnp.int32) for a in sched), fill.astype(jnp.int32)


def _dispatch_kernel(fill_ref, pos_ref, hp_ref, hs_ref, xs_hbm, zbuf, sem, fsem, *, n_p_tiles, n_fill):
    i = pl.program_id(0)

    def fill_copy(e):
        start = pl.multiple_of(fill_ref[e], MOE_TM)
        return pltpu.make_async_copy(zbuf, xs_hbm.at[pl.ds(start, MOE_TM)], fsem)

    @pl.when(i == 0)
    def _():
        zbuf[...] = jnp.zeros_like(zbuf)
        for e in range(n_fill):
            @pl.when(fill_ref[e] >= 0)
            def _(e=e):
                fill_copy(e).start()
        for e in range(n_fill):
            @pl.when(fill_ref[e] >= 0)
            def _(e=e):
                fill_copy(e).wait()

    def scatter(h_ref):
        for t in range(COMB_TB):
            for j in range(TOP_K):
                pltpu.make_async_copy(h_ref.at[pl.ds(t, 1)],
                                      xs_hbm.at[pl.ds(pos_ref[0, 0, j * COMB_TB + t], 1)], sem).start()

    @pl.when(i < n_p_tiles)
    def _():
        scatter(hp_ref)

    @pl.when(i >= n_p_tiles)
    def _():
        scatter(hs_ref)

    for _ in range(TOP_K):
        pltpu.make_async_copy(hp_ref, xs_hbm.at[pl.ds(0, COMB_TB)], sem).wait()


def _dispatch(pos, fill, h2p, h2s, n_tiles):
    d = D_MODEL
    n_p, n_s = h2p.shape[0], h2s.shape[0]
    n = (n_p + n_s) // COMB_TB
    n_pt = n_p // COMB_TB
    pos_t = pos.reshape(n, COMB_TB, TOP_K).transpose(0, 2, 1).reshape(n, 1, TOP_K * COMB_TB)
    grid_spec = pltpu.PrefetchScalarGridSpec(
        num_scalar_prefetch=1,
        grid=(n,),
        in_specs=[pl.BlockSpec((1, 1, TOP_K * COMB_TB), lambda i, fill: (i, 0, 0), memory_space=pltpu.SMEM),
                  pl.BlockSpec((COMB_TB, d), lambda i, fill: (jnp.minimum(i, n_pt - 1), 0)),
                  pl.BlockSpec((COMB_TB, d), lambda i, fill: (jnp.maximum(i - n_pt, 0), 0))],
        out_specs=pl.BlockSpec(memory_space=pl.ANY),
        scratch_shapes=[pltpu.VMEM((MOE_TM, d), F32),
                        pltpu.SemaphoreType.DMA(()),
                        pltpu.SemaphoreType.DMA(())])
    return pl.pallas_call(
        functools.partial(_dispatch_kernel, n_p_tiles=n_pt, n_fill=fill.shape[0]),
        grid_spec=grid_spec,
        out_shape=jax.ShapeDtypeStruct((n_tiles * MOE_TM, d), F32),
        compiler_params=pltpu.CompilerParams(
            dimension_semantics=("arbitrary",), vmem_limit_bytes=VMEM_LIMIT),
        name="moe_dispatch",
    )(fill, pos_t, h2p, h2s)


def _experts_kernel(tile_ref, te_ref, tv_ref, tf_ref, ws_ref, nx_ref, x_ref,
                    wgu_hbm, bgu_ref, wd_hbm, bd_ref, ys_ref, wgu_buf, wd_buf, wgu_scr, wd_scr, sem):
    i = pl.program_id(0)

    def weight_copies(e, slot):
        return (pltpu.make_async_copy(wgu_hbm.at[e], wgu_buf.at[slot], sem.at[0, slot]),
                pltpu.make_async_copy(wd_hbm.at[e], wd_buf.at[slot], sem.at[1, slot]))

    @pl.when(i == 0)
    def _():
        for cp in weight_copies(te_ref[0], 0):
            cp.start()

    @pl.when(tf_ref[i] == 1)
    def _():
        slot = ws_ref[i]

        @pl.when(nx_ref[i] >= 0)
        def _():
            for cp in weight_copies(nx_ref[i], 1 - slot):
                cp.start()

        for cp in weight_copies(te_ref[i], slot):
            cp.wait()
        wgu_scr[...] = wgu_buf[slot].astype(BF16)
        wd_scr[...] = wd_buf[slot].astype(BF16)

    @pl.when(tv_ref[i] == 0)
    def _():
        ys_ref[...] = jnp.zeros_like(ys_ref)

    @pl.when(tv_ref[i] == 1)
    def _():
        gu = jnp.dot(x_ref[...].astype(BF16), wgu_scr[...], preferred_element_type=F32) + bgu_ref[0]
        gate = jnp.minimum(gu[:, :D_MODEL], SWIGLU_LIMIT)
        up = jnp.clip(gu[:, D_MODEL:], -SWIGLU_LIMIT, SWIGLU_LIMIT)
        act = (up + 1.0) * gate * _sigmoid(SWIGLU_ALPHA * gate)
        ys_ref[...] = jnp.dot(act.astype(BF16), wd_scr[...], preferred_element_type=F32) + bd_ref[0]


def _experts(xs, sched, p, n_tiles):
    d = D_MODEL
    bias = lambda w: pl.BlockSpec((1, 1, w), lambda i, tile, te, *_: (te[i], 0, 0))
    grid_spec = pltpu.PrefetchScalarGridSpec(
        num_scalar_prefetch=len(sched),
        grid=(n_tiles,),
        in_specs=[pl.BlockSpec((MOE_TM, d), lambda i, tile, *_: (tile[i], 0)),
                  pl.BlockSpec(memory_space=pl.ANY), bias(2 * d),
                  pl.BlockSpec(memory_space=pl.ANY), bias(d)],
        out_specs=pl.BlockSpec((MOE_TM, d), lambda i, *_: (i, 0)),
        scratch_shapes=[pltpu.VMEM((2, d, 2 * d), F32),
                        pltpu.VMEM((2, d, d), F32),
                        pltpu.VMEM((d, 2 * d), BF16),
                        pltpu.VMEM((d, d), BF16),
                        pltpu.SemaphoreType.DMA((2, 2))])
    return pl.pallas_call(
        _experts_kernel,
        grid_spec=grid_spec,
        out_shape=jax.ShapeDtypeStruct((n_tiles * MOE_TM, d), F32),
        compiler_params=pltpu.CompilerParams(
            dimension_semantics=("arbitrary",), vmem_limit_bytes=VMEM_LIMIT),
        name="moe_experts",
    )(*sched, xs, p['w_gate_up'], p['b_gate_up'], p['w_down'], p['b_down'])


def _combine_kernel(pos_ref, posn_ref, ys_hbm, ew_ref, x1_ref, mg_ref, gpost_ref, o_ref, buf, sem, *, n):
    i = pl.program_id(0)
    slot = i % 2

    def row_copy(row, j, t, slot_):
        return pltpu.make_async_copy(ys_hbm.at[pl.ds(row, 1)], buf.at[slot_, j, pl.ds(t, 1)], sem.at[slot_])

    def gather(src_ref, slot_):
        for t in range(COMB_TB):
            for j in range(TOP_K):
                row_copy(src_ref[0, 0, j * COMB_TB + t], j, t, slot_).start()

    @pl.when(i == 0)
    def _():
        gather(pos_ref, 0)

    @pl.when(i + 1 < n)
    def _():
        gather(posn_ref, 1 - slot)

    for j in range(TOP_K):
        pltpu.make_async_copy(ys_hbm.at[pl.ds(0, COMB_TB)], buf.at[slot, j], sem.at[slot]).wait()
    ew = ew_ref[...]
    f = sum(ew[:, j:j + 1] * buf[slot, j] for j in range(TOP_K))
    o_ref[...] = x1_ref[...] + mg_ref[0] * _rms(f, gpost_ref[...])


def _combine(ys, pos, ew, x1, mod_gate, p):
    b, s, d = x1.shape
    n_tok = b * s
    n_tiles = n_tok // COMB_TB
    per_batch = mod_gate.shape[0] > 1
    if per_batch:
        mod_gate = jnp.repeat(mod_gate, s // COMB_TB, axis=0)
    mod_map = (lambda i: (i, 0, 0)) if per_batch else (lambda i: (0, 0, 0))
    pos_t = pos.reshape(n_tiles, COMB_TB, TOP_K).transpose(0, 2, 1).reshape(n_tiles, 1, TOP_K * COMB_TB)
    smem_tile = lambda f: pl.BlockSpec((1, 1, TOP_K * COMB_TB), f, memory_space=pltpu.SMEM)
    out = pl.pallas_call(
        functools.partial(_combine_kernel, n=n_tiles),
        grid=(n_tiles,),
        in_specs=[smem_tile(lambda i: (i, 0, 0)),
                  smem_tile(lambda i: (jnp.minimum(i + 1, n_tiles - 1), 0, 0)),
                  pl.BlockSpec(memory_space=pl.ANY),
                  pl.BlockSpec((COMB_TB, LANES), lambda i: (i, 0)),
                  pl.BlockSpec((COMB_TB, d), lambda i: (i, 0)),
                  pl.BlockSpec((1, 1, d), mod_map),
                  pl.BlockSpec((1, d), lambda i: (0, 0))],
        out_specs=pl.BlockSpec((COMB_TB, d), lambda i: (i, 0)),
        out_shape=jax.ShapeDtypeStruct((n_tok, d), F32),
        scratch_shapes=[pltpu.VMEM((2, TOP_K, COMB_TB, d), F32),
                        pltpu.SemaphoreType.DMA((2,))],
        compiler_params=pltpu.CompilerParams(
            dimension_semantics=("arbitrary",), vmem_limit_bytes=VMEM_LIMIT),
        name="moe_combine",
    )(pos_t, pos_t, ys, ew.reshape(n_tok, LANES), x1.reshape(n_tok, d), mod_gate, p['g_post_ffn'])
    return out.reshape(b, s, d)


def _rope_tables(n_tokens):
    rows = n_tokens // GRID_W
    row = jnp.repeat(jnp.arange(rows, dtype=F32), GRID_W)
    col = jnp.tile(jnp.arange(GRID_W, dtype=F32), rows)
    inv = ROPE_THETA ** (-jnp.arange(ROPE_FREQS, dtype=F32) / ROPE_FREQS)
    ang_r = row[:, None] * inv[None, :]
    ang_c = col[:, None] * inv[None, :]
    ang = jnp.concatenate([ang_r, ang_r, ang_c, ang_c] * 2, axis=-1)
    return jnp.cos(ang), jnp.sin(ang)


def _stream(x, mods, proj_w, lam, p, rope, ctx, s0, cnt0, proj_dtype, tm_proj, tm_tok):
    b, s, d = x.shape
    xf = x if mods[0].shape[0] > 1 else x.reshape(1, b * s, d)
    proj = _norm_proj(xf, p['g_pre_mix'], mods[1], mods[0], proj_w, tm_proj, PROJ_TN, proj_dtype)
    proj = proj.reshape(b, s, C_IN)
    oa = _diff_attention(proj, lam, p['g_subln'], rope, ctx, tq=min(s, 4 * ATT_SUB),
                         hps=H_A if s <= ATT_SUB else 1)
    r, v, kkn, kd, ba, lw, bonus, gate = _rwkv_prep(proj, p, min(s, 256))
    y0, y1, sfin = _rwkv_scan(r, v, kkn, kd, ba, lw, s0)
    tb, tsq = xf.shape[0], xf.shape[1]
    x1, h2, route, rank, ew, cnt = _merge_out(
        *(a.reshape(tb, tsq, a.shape[-1]) for a in (x, oa, y0, y1, bonus, gate, proj)),
        (mods[2], mods[4], mods[3]), p, cnt0, tm_tok)
    n = b * s
    route = (route.reshape(n, LANES)[:, :TOP_K], rank.reshape(n, LANES)[:, :N_EXPERTS])
    return (x1.reshape(b, s, d), h2.reshape(n, d), route, ew.reshape(n, LANES), cnt,
            proj, sfin)


def kernel(x_prompt, x_sample, cache_k, cache_v, state_rwkv, c, c_ctx, w_ada, b_ada, g_pre_mix, g_post_mix, g_pre_ffn, g_post_ffn, w_in, mu_prev, mu_next, lam, g_subln, k_k, k_a, r_k, w0, w_up, a0, a_up, g_up, gn_w, gn_b, w_out, w_router, b_router, w_gate_up, b_gate_up, w_down, b_down):
    l = 0
    d = D_MODEL
    bp, sp, _ = x_prompt.shape
    bs, ss, _ = x_sample.shape

    n_cond = 1 + bs
    rows = -(-n_cond // SUBLANES) * SUBLANES
    cond = jnp.concatenate([c_ctx[None, :], c, jnp.zeros((rows - n_cond, d), F32)], axis=0)
    mod = _modulation(cond, w_ada[l], b_ada[l][None, :])
    mods_p = [mod[0:1, i * d:(i + 1) * d].reshape(1, 1, d) for i in range(N_MOD)]
    mods_s = [mod[1:1 + bs, i * d:(i + 1) * d].reshape(bs, 1, d) for i in range(N_MOD)]

    w = w_in[l]
    o_rwkv = 3 * d
    o_gate = o_rwkv + 3 * d + C_LORA
    w_perm = jnp.concatenate([w[:, :3 * d], w[:, o_gate:o_gate + 2 * d],
                              w[:, o_rwkv:o_rwkv + 3 * d], w[:, o_rwkv + 3 * d:o_gate]],
                             axis=1).astype(BF16)

    lq = lam[l]
    lam_val = (jnp.exp(jnp.sum(lq[0] * lq[1])) - jnp.exp(jnp.sum(lq[2] * lq[3])) + LAM_INIT).reshape(1)

    head = jnp.arange(d) // N_B
    wr = jnp.concatenate([w_router[l], jnp.zeros((d, LANES - N_EXPERTS), F32)], axis=1)
    br = jnp.concatenate([b_router[l], jnp.full((LANES - N_EXPERTS,), -jnp.inf, F32)])[None, :]
    mup, mun = mu_prev[l][None, :], mu_next[l][None, :]
    p = {
        'g_pre_mix': g_pre_mix[l][None, :], 'g_post_mix': g_post_mix[l][None, :],
        'g_pre_ffn': g_pre_ffn[l][None, :], 'g_post_ffn': g_post_ffn[l][None, :],
        'g_subln': g_subln[l][None, :],
        'mu_prev_main': mup[:, :3 * d], 'mu_next_main': mun[:, :3 * d],
        'mu_prev_lora': mup[:, 3 * d:], 'mu_next_lora': mun[:, 3 * d:],
        'k_k': k_k[l][None, :], 'k_a': k_a[l][None, :], 'r_k': r_k[l].reshape(1, d),
        'w0': w0[l].reshape(1, 2 * d),
        'w_up': jnp.concatenate([w_up[l, 0], w_up[l, 1]], axis=1).astype(BF16),
        'a0': a0[l].reshape(1, 2 * d),
        'a_up': jnp.concatenate([a_up[l, 0], a_up[l, 1]], axis=1).astype(BF16),
        'g_up': g_up[l].astype(BF16),
        'head_ind': (head[:, None] == jnp.arange(LANES)[None, :]).astype(BF16),
        'head_ind_t': (jnp.arange(LANES)[:, None] == head[None, :]).astype(BF16),
        'gn_w': gn_w[l][None, :], 'gn_b': gn_b[l][None, :],
        'w_out': w_out[l].astype(BF16),
        'w_router': wr, 'b_router': br,
        'w_gate_up': w_gate_up[l], 'b_gate_up': b_gate_up[l][:, None, :],
        'w_down': w_down[l], 'b_down': b_down[l][:, None, :],
    }

    rope = _rope_tables(ss)
    ctx = (cache_k[:, l].reshape(bs, -1, d), cache_v[:, l].reshape(bs, -1, d))

    cnt0 = jnp.zeros((1, LANES), F32)
    x1p, h2p, route_p, ewp, cnt_p, proj_p, sfin = _stream(
        x_prompt, mods_p, w_perm, lam_val, p, None, None, None, cnt0, F32, 1024, 512)
    x1s, h2s, route_s, ews, cnt_s, _, _ = _stream(
        x_sample, mods_s, w_perm, lam_val, p, rope, ctx, state_rwkv[:, l], cnt_p, BF16, 1024, 512)

    n_p, n_s = bp * sp, bs * ss
    n_tok = n_p + n_s
    eid, rank = (jnp.concatenate([a, b_], axis=0) for a, b_ in zip(route_p, route_s))
    counts = cnt_s[0, :N_EXPERTS].astype(jnp.int32)
    n_tiles = -(-(n_tok * TOP_K + N_EXPERTS * (MOE_TM - 1)) // MOE_TM)
    pos, sched, fill = _route_metadata(eid, rank, counts, n_tiles)
    xs = _dispatch(pos, fill, h2p, h2s, n_tiles)
    rows = _experts(xs, sched, p, n_tiles)
    yp = _combine(rows, pos[:n_p], ewp, x1p, mods_p[5], p)
    ys = _combine(rows, pos[n_p:], ews, x1s, mods_s[5], p)

    new_k = proj_p[:, :, COL_K:COL_K + d].reshape(bp, 1, sp, H_A, 2, HD_A)
    new_v = proj_p[:, :, COL_V:COL_V + d].reshape(bp, 1, sp, H_A, DV_A)
    return (yp, ys, new_k, new_v, sfin[:, None])
```

```python
import functools
import math

import jax
import jax.numpy as jnp
from jax import lax
from jax.experimental import pallas as pl
from jax.experimental.pallas import tpu as pltpu

F32 = jnp.float32
BF16 = jnp.bfloat16

D_MODEL = 1024
GRID_W = 64
HD_A = 64
DV_A = 2 * HD_A
H_A = D_MODEL // DV_A
N_B = 64
H_B = D_MODEL // N_B
LORA_W = 64
LORA_A = 64
LORA_G = 128
N_EXPERTS = 32
TOP_K = 4
SWIGLU_LIMIT = 7.0
SWIGLU_ALPHA = 1.702
ROPE_THETA = 10000.0
ROPE_FREQS = HD_A // 4
NORM_EPS = 1e-6
GN_EPS = 64e-5
L2_EPS = 1e-12
ATTN_SCALE = HD_A ** -0.5
N_MOD = 6
LAM_INIT = 0.8 - 0.6 * math.exp(-0.3 * 0)
DECAY_SCALE = -math.exp(-0.5)

LANES = 128
SUBLANES = 8
CHUNK = 64
MOE_TM = 512
PROJ_TN = 1408
COMB_TB = 256
ATT_SUB = 256
LOG2_E = math.log2(math.e)
VMEM_LIMIT = 56 * 1024 * 1024

C_LORA = LORA_W + LORA_A + LORA_G
COL_Q, COL_K, COL_V, COL_R, COL_KR, COL_VR, COL_LORA = (i * D_MODEL for i in range(7))
COL_G0 = COL_LORA + C_LORA
COL_G1 = COL_G0 + D_MODEL
C_IN = COL_G1 + D_MODEL
GATE_W = 256


def _sigmoid(x):
    return 1.0 / (1.0 + jnp.exp(-x))


def _dot(a, b):
    return jnp.dot(a.astype(BF16), b.astype(BF16), preferred_element_type=F32)


def _dot_nt(a, b):
    return lax.dot_general(a.astype(BF16), b.astype(BF16), (((1,), (1,)), ((), ())),
                           preferred_element_type=F32)


def _dot_tn(a, b):
    return lax.dot_general(a.astype(BF16), b.astype(BF16), (((0,), (0,)), ((), ())),
                           preferred_element_type=F32)


def _split3(x):
    hi = x.astype(BF16)
    r1 = x - hi.astype(F32)
    mid = r1.astype(BF16)
    lo = (r1 - mid.astype(F32)).astype(BF16)
    return hi, mid, lo


def _headsum(x, ind):
    e, et = ind
    s = jnp.dot(x.astype(BF16), e, preferred_element_type=F32)
    s_hi = s.astype(BF16)
    s_lo = (s - s_hi.astype(F32)).astype(BF16)
    return jnp.dot(s_hi, et, preferred_element_type=F32) + jnp.dot(s_lo, et, preferred_element_type=F32)


def _rms(x, g):
    return x * lax.rsqrt(jnp.mean(x * x, axis=-1, keepdims=True) + NORM_EPS) * g


def _mod_kernel(c_ref, w_ref, b_ref, o_ref):
    c = c_ref[...]
    s = c * _sigmoid(c)
    o_ref[...] = _dot(s, w_ref[...]) + b_ref[...]


def _modulation(cond, w_ada, b_ada):
    rows, d = cond.shape
    n = w_ada.shape[1]
    tn = 768
    return pl.pallas_call(
        _mod_kernel,
        grid=(n // tn,),
        in_specs=[pl.BlockSpec((rows, d), lambda j: (0, 0)),
                  pl.BlockSpec((d, tn), lambda j: (0, j)),
                  pl.BlockSpec((1, tn), lambda j: (0, j))],
        out_specs=pl.BlockSpec((rows, tn), lambda j: (0, j)),
        out_shape=jax.ShapeDtypeStruct((rows, n), F32),
        name="modulation",
    )(cond, w_ada, b_ada)


def _proj_kernel(x_ref, g_ref, sc_ref, sh_ref, w_ref, o_ref, h_scr):
    @pl.when(pl.program_id(2) == 0)
    def _():
        h = _rms(x_ref[0], g_ref[...]) * (1.0 + sc_ref[0]) + sh_ref[0]
        h_scr[...] = h.astype(BF16)

    o_ref[0] = jnp.dot(h_scr[...], w_ref[...], preferred_element_type=F32).astype(o_ref.dtype)


def _norm_proj(x, g, scale, shift, w, tm, tn, out_dtype):
    b, s, d = x.shape
    n = w.shape[1]
    per_batch = scale.shape[0] > 1
    mod_map = (lambda bi, i, j: (bi, 0, 0)) if per_batch else (lambda bi, i, j: (0, 0, 0))
    return pl.pallas_call(
        _proj_kernel,
        grid=(b, s // tm, n // tn),
        in_specs=[pl.BlockSpec((1, tm, d), lambda bi, i, j: (bi, i, 0)),
                  pl.BlockSpec((1, d), lambda bi, i, j: (0, 0)),
                  pl.BlockSpec((1, 1, d), mod_map),
                  pl.BlockSpec((1, 1, d), mod_map),
                  pl.BlockSpec((d, tn), lambda bi, i, j: (0, j))],
        out_specs=pl.BlockSpec((1, tm, tn), lambda bi, i, j: (bi, i, j)),
        out_shape=jax.ShapeDtypeStruct((b, s, n), out_dtype),
        scratch_shapes=[pltpu.VMEM((tm, d), BF16)],
        compiler_params=pltpu.CompilerParams(
            dimension_semantics=("arbitrary", "arbitrary", "arbitrary"),
            vmem_limit_bytes=VMEM_LIMIT),
        name="norm_proj",
    )(x, g, scale, shift, w)


def _rope(x, cos, sin):
    lane = lax.broadcasted_iota(jnp.int32, x.shape, 1)
    even = (lane // ROPE_FREQS) % 2 == 0
    rot = jnp.where(even, -pltpu.roll(x, LANES - ROPE_FREQS, 1), pltpu.roll(x, ROPE_FREQS, 1))
    return x * cos + rot * sin


def _attn_kernel(*refs, use_rope, use_ctx, s_new):
    it = iter(refs)
    lam_ref = next(it)
    q_ref, k_ref, v_ref = next(it), next(it), next(it)
    if use_rope:
        cq_ref, sq_ref, ck_ref, sk_ref = next(it), next(it), next(it), next(it)
    if use_ctx:
        ctxk_ref, ctxv_ref = next(it), next(it)
    g_ref = next(it)
    o_ref = next(it)
    k_scr, vt_scr = next(it), next(it)

    hps = q_ref.shape[-1] // DV_A
    head_lanes = [slice(hh * DV_A, (hh + 1) * DV_A) for hh in range(hps)]

    @pl.when(pl.program_id(2) == 0)
    def _():
        for hh, hl in enumerate(head_lanes):
            k = k_ref[0][:, hl].astype(F32)
            if use_rope:
                k = _rope(k, ck_ref[...], sk_ref[...])
            k_scr[hh, 0:s_new, :] = k.astype(BF16)
            vt_scr[hh, :, 0:s_new] = v_ref[0][:, hl].astype(F32).T.astype(BF16)
            if use_ctx:
                k_scr[hh, s_new:, :] = ctxk_ref[0][:, hl].astype(BF16)
                vt_scr[hh, :, s_new:] = ctxv_ref[0][:, hl].T.astype(BF16)

    lam = lam_ref[0]
    n_sub = q_ref.shape[1] // ATT_SUB
    lane = lax.broadcasted_iota(jnp.int32, (ATT_SUB, DV_A), 1)
    sts = []
    for hh, hl in enumerate(head_lanes):
        q = q_ref[0][:, hl].astype(F32)
        if use_rope:
            q = _rope(q, cq_ref[...], sq_ref[...])
        q = q * (ATTN_SCALE * LOG2_E)
        for t in range(n_sub):
            for m in range(2):
                in_map = (lane >= HD_A) if m else (lane < HD_A)
                qm = jnp.where(in_map, q[t * ATT_SUB:(t + 1) * ATT_SUB], 0.0)
                sts.append((hh, _dot_nt(k_scr[hh], qm)))
    outs = []
    for hh, st in sts:
        et = jnp.exp2(st - jnp.max(st, axis=0, keepdims=True))
        l = jnp.sum(et, axis=0, keepdims=True)
        ot = jnp.dot(vt_scr[hh], et.astype(BF16), preferred_element_type=F32)
        outs.append(ot / l)
    for hh, hl in enumerate(head_lanes):
        for t in range(n_sub):
            u = 2 * (hh * n_sub + t)
            o = (outs[u] - lam * outs[u + 1]).T
            o = o * lax.rsqrt(jnp.mean(o * o, axis=-1, keepdims=True) + NORM_EPS)
            o_ref[0, t * ATT_SUB:(t + 1) * ATT_SUB, hl] = o * g_ref[:, hl] * (1.0 - LAM_INIT)


def _diff_attention(proj, lam, g_subln, rope=None, ctx=None, tq=256, hps=1):
    b, s, _ = proj.shape
    use_rope, use_ctx = rope is not None, ctx is not None
    s_tot = s + (ctx[0].shape[1] if use_ctx else 0)
    w = hps * DV_A
    in_specs = [pl.BlockSpec(memory_space=pltpu.SMEM),
                pl.BlockSpec((1, tq, w), lambda bi, h, i: (bi, i, COL_Q // w + h)),
                pl.BlockSpec((1, s, w), lambda bi, h, i: (bi, 0, COL_K // w + h)),
                pl.BlockSpec((1, s, w), lambda bi, h, i: (bi, 0, COL_V // w + h))]
    args = [lam, proj, proj, proj]
    if use_rope:
        cos, sin = rope
        in_specs += [pl.BlockSpec((tq, DV_A), lambda bi, h, i: (i, 0)),
                     pl.BlockSpec((tq, DV_A), lambda bi, h, i: (i, 0)),
                     pl.BlockSpec((s, DV_A), lambda bi, h, i: (0, 0)),
                     pl.BlockSpec((s, DV_A), lambda bi, h, i: (0, 0))]
        args += [cos, sin, cos, sin]
    if use_ctx:
        p = ctx[0].shape[1]
        in_specs += [pl.BlockSpec((1, p, w), lambda bi, h, i: (bi, 0, h)),
                     pl.BlockSpec((1, p, w), lambda bi, h, i: (bi, 0, h))]
        args += [ctx[0], ctx[1]]
    in_specs.append(pl.BlockSpec((1, w), lambda bi, h, i: (0, h)))
    args.append(g_subln)
    return pl.pallas_call(
        functools.partial(_attn_kernel, use_rope=use_rope, use_ctx=use_ctx, s_new=s),
        grid=(b, H_A // hps, s // tq),
        in_specs=in_specs,
        out_specs=pl.BlockSpec((1, tq, w), lambda bi, h, i: (bi, i, h)),
        out_shape=jax.ShapeDtypeStruct((b, s, D_MODEL), F32),
        scratch_shapes=[pltpu.VMEM((hps, s_tot, DV_A), BF16), pltpu.VMEM((hps, DV_A, s_tot), BF16)],
        compiler_params=pltpu.CompilerParams(
            dimension_semantics=("arbitrary", "arbitrary", "arbitrary"),
            vmem_limit_bytes=VMEM_LIMIT),
        name="diff_attention",
    )(*args)


def _shifted(x_ref, p_ref, n_ref, mup, mun, first, last):
    x = x_ref[0].astype(F32)
    ts = x.shape[0]
    halo = p_ref.shape[1]
    row = lax.broadcasted_iota(jnp.int32, x.shape, 0)
    prev_row = p_ref[0][halo - 1:halo, :].astype(F32) * first
    next_row = n_ref[0][0:1, :].astype(F32) * last
    prev = jnp.where(row == 0, prev_row, pltpu.roll(x, 1, 0))
    nxt = jnp.where(row == ts - 1, next_row, pltpu.roll(x, ts - 1, 0))
    return x + mup * (prev - x) + mun * (nxt - x)


def _prep_kernel(r_ref, rp_ref, rn_ref, k_ref, kp_ref, kn_ref, v_ref, vp_ref, vn_ref,
                 l_ref, lp_ref, ln_ref, mup_ref, mun_ref, mupl_ref, munl_ref,
                 kk_ref, ka_ref, rk_ref, w0_ref, wup_ref, a0_ref, aup_ref, gup_ref, e_ref, et_ref,
                 ro_ref, vo_ref, kko_ref, kd_ref, ba_ref, lw_ref, bonus_ref, gate_ref, *, n_tiles):
    i = pl.program_id(1)
    first = (i > 0).astype(F32)
    last = (i < n_tiles - 1).astype(F32)
    mup, mun = mup_ref[...], mun_ref[...]
    d = D_MODEL
    r = _shifted(r_ref, rp_ref, rn_ref, mup[:, 0:d], mun[:, 0:d], first, last)
    k = _shifted(k_ref, kp_ref, kn_ref, mup[:, d:2 * d], mun[:, d:2 * d], first, last)
    v = _shifted(v_ref, vp_ref, vn_ref, mup[:, 2 * d:3 * d], mun[:, 2 * d:3 * d], first, last)
    lo = _shifted(l_ref, lp_ref, ln_ref, mupl_ref[...], munl_ref[...], first, last)
    xw = lo[:, 0:LORA_W]
    xa = lo[:, LORA_W:LORA_W + LORA_A]
    xg = lo[:, LORA_W + LORA_A:]
    g = (e_ref[...], et_ref[...])

    kk = k * kk_ref[...]
    nrm = jnp.sqrt(_headsum(kk * kk, g))
    kkn = kk / jnp.maximum(nrm, L2_EPS)
    ro_ref[0] = r
    vo_ref[0] = v
    kko_ref[0] = kkn
    gate_ref[0] = _dot(_sigmoid(xg), gup_ref[...])

    wlog = w0_ref[...] + _dot(jnp.tanh(xw), wup_ref[...])
    alog = a0_ref[...] + _dot(xa, aup_ref[...])
    ka = ka_ref[...]
    rrk = r * rk_ref[...]
    dots = None
    for dr in range(2):
        sl = slice(dr * d, (dr + 1) * d)
        lw_ref[dr, 0] = DECAY_SCALE * _sigmoid(wlog[:, sl])
        a = _sigmoid(alog[:, sl])
        kd = k * (1.0 + (a - 1.0) * ka)
        kd_ref[dr, 0] = kd
        ba_ref[dr, 0] = kkn * a
        t = rrk * kd
        dots = t if dots is None else dots + t
    bonus_ref[0] = _headsum(dots, g) * v


def _rwkv_prep(proj, p, ts):
    b, s, _ = proj.shape
    d = D_MODEL
    nt = s // ts
    halo = SUBLANES * (4 // proj.dtype.itemsize)
    hb = ts // halo
    nhb = s // halo

    def main(col, w):
        return pl.BlockSpec((1, ts, w), lambda bi, i: (bi, i, col // w))

    def prev(col, w):
        return pl.BlockSpec((1, halo, w), lambda bi, i: (bi, jnp.maximum(i * hb - 1, 0), col // w))

    def nxt(col, w):
        return pl.BlockSpec((1, halo, w),
                            lambda bi, i: (bi, jnp.minimum((i + 1) * hb, nhb - 1), col // w))

    def full(a):
        return pl.BlockSpec(a.shape, lambda bi, i: (0,) * a.ndim)

    in_specs, args = [], []
    for col, w in ((COL_R, d), (COL_KR, d), (COL_VR, d), (COL_LORA, C_LORA)):
        in_specs += [main(col, w), prev(col, w), nxt(col, w)]
        args += [proj, proj, proj]
    consts = [p['mu_prev_main'], p['mu_next_main'], p['mu_prev_lora'], p['mu_next_lora'],
              p['k_k'], p['k_a'], p['r_k'], p['w0'], p['w_up'], p['a0'], p['a_up'], p['g_up'],
              p['head_ind'], p['head_ind_t']]
    in_specs += [full(a) for a in consts]
    args += consts
    tok = pl.BlockSpec((1, ts, d), lambda bi, i: (bi, i, 0))
    tok2 = pl.BlockSpec((2, 1, ts, d), lambda bi, i: (0, bi, i, 0))
    one = jax.ShapeDtypeStruct((b, s, d), F32)
    two = jax.ShapeDtypeStruct((2, b, s, d), F32)
    return pl.pallas_call(
        functools.partial(_prep_kernel, n_tiles=nt),
        grid=(b, nt),
        in_specs=in_specs,
        out_specs=[tok, tok, tok, tok2, tok2, tok2, tok, tok],
        out_shape=[one, one, one, two, two, two, one, one],
        compiler_params=pltpu.CompilerParams(
            dimension_semantics=("arbitrary", "arbitrary"),
            vmem_limit_bytes=VMEM_LIMIT),
        name="rwkv_prep",
    )(*args)


def _scan_masks(reverse):
    n = CHUNK
    tt = lax.broadcasted_iota(jnp.int32, (n, n), 0)
    ss = lax.broadcasted_iota(jnp.int32, (n, n), 1)
    hi, lo = (ss, tt) if reverse else (tt, ss)
    tt2 = lax.broadcasted_iota(jnp.int32, (n, 2 * n), 0)
    col2 = lax.broadcasted_iota(jnp.int32, (n, 2 * n), 1)
    ss2 = col2 % n
    hi2, lo2 = (ss2, tt2) if reverse else (tt2, ss2)
    levels = []
    blk = 2
    while blk < n:
        levels.append((hi // (2 * blk) == lo // (2 * blk)) & ((hi // blk) % 2 == 1) & ((lo // blk) % 2 == 0))
        blk *= 2
    return dict(strict=hi > lo, incl=hi >= lo, eye=(tt == ss).astype(F32),
                first=(hi // 2 == lo // 2) & (hi > lo), levels=levels,
                strict_r=(hi2 > lo2) & (col2 >= n),
                incl2=hi2 >= lo2)


def _scan_operands(reverse, incl, lw, r, v, kkn, kd, ba):
    n = CHUNK
    tri = jnp.where(incl, 1.0, 0.0).astype(BF16)
    cum = sum(jnp.dot(tri, part, preferred_element_type=F32) for part in _split3(lw))
    tot = cum[0:1, :] if reverse else cum[n - 1:n, :]
    g_inv = jnp.exp(-cum)
    g_rest = jnp.exp(tot - cum)
    return dict(a=(kkn * jnp.exp(cum - lw)).astype(BF16), r=(r * jnp.exp(cum)).astype(BF16),
                b=(ba * g_inv).astype(BF16), k=(kd * g_inv).astype(BF16),
                bh=(ba * g_rest).astype(BF16), kh=(kd * g_rest).astype(BF16),
                v=v.astype(BF16), g_tot=jnp.exp(tot))


def _scan_kernel(*refs, has_s0, n_chunks):
    it = iter(refs)
    tok_refs = [[next(it) for _ in range(3)] for _ in range(2)]
    dir_refs = [[next(it) for _ in range(3)] for _ in range(2)]
    s0_ref = next(it) if has_s0 else None
    y_refs = [next(it), next(it)]
    sf_ref = next(it)
    s_scr = next(it)
    c = pl.program_id(1)

    @pl.when(c == 0)
    def _():
        if has_s0:
            s_scr[...] = s0_ref[0]
        else:
            s_scr[...] = jnp.zeros_like(s_scr)

    n = CHUNK
    masks = [_scan_masks(dr == 1) for dr in range(2)]
    ops = []
    for dr in range(2):
        r_ref, v_ref, kk_ref = tok_refs[dr]
        kd_ref, ba_ref, lw_ref = dir_refs[dr]
        ops.append(_scan_operands(dr == 1, masks[dr]['incl'], lw_ref[0, 0], r_ref[0], v_ref[0], kk_ref[0],
                                  kd_ref[0, 0], ba_ref[0, 0]))

    units = [(dr, h) for dr in range(2) for h in range(H_B)]
    idx = range(len(units))
    sls = [slice(h * N_B, (h + 1) * N_B) for _, h in units]
    op = lambda u, name: ops[units[u][0]][name][:, sls[u]]
    mk = lambda u, name: masks[units[u][0]][name]
    s_old = [s_scr[dr, h] for dr, h in units]
    ar = [jnp.concatenate([op(u, 'a'), op(u, 'r')], axis=0) for u in idx]
    bk = [jnp.concatenate([op(u, 'b'), op(u, 'k')], axis=0) for u in idx]
    bkh = [jnp.concatenate([op(u, 'bh'), op(u, 'kh')], axis=0) for u in idx]
    vh = [op(u, 'v') for u in idx]

    gram = [_dot_nt(ar[u], bk[u]) for u in idx]
    ars = [_dot_nt(ar[u], s_old[u]) for u in idx]
    nmat = [jnp.where(mk(u, 'strict'), gram[u][0:n, 0:n], 0.0) for u in idx]
    mak = [jnp.where(mk(u, 'strict_r'), gram[u][0:n, :], 0.0) for u in idx]
    pr = [jnp.where(mk(u, 'incl2'), gram[u][n:, :], 0.0) for u in idx]
    z = [ars[u][0:n] + _dot(mak[u], jnp.concatenate([vh[u], vh[u]], axis=0)) for u in idx]
    x = [mk(u, 'eye') - jnp.where(mk(u, 'first'), nmat[u], 0.0) for u in idx]
    for lvl in range(len(masks[0]['levels'])):
        xl = [_dot(x[u], jnp.where(mk(u, 'levels')[lvl], nmat[u], 0.0)) for u in idx]
        x = [x[u] - _dot(xl[u], x[u]) for u in idx]
    uu = [-_dot(x[u], z[u]) for u in idx]
    uv = [jnp.concatenate([uu[u].astype(BF16), vh[u]], axis=0) for u in idx]
    y = [ars[u][n:] + _dot(pr[u], uv[u]) for u in idx]
    s_new = [s_old[u] * ops[units[u][0]]['g_tot'][:, sls[u]] + _dot_tn(uv[u], bkh[u]) for u in idx]
    for u, (dr, h) in enumerate(units):
        y_refs[dr][0, :, sls[u]] = y[u]
    for u, (dr, h) in enumerate(units):
        s_scr[dr, h] = s_new[u]

    @pl.when(c == n_chunks - 1)
    def _():
        sf_ref[0] = s_scr[...]


def _rwkv_scan(r, v, kkn, kd, ba, lw, s0):
    b, s, d = r.shape
    nc = s // CHUNK
    fwd = pl.BlockSpec((1, CHUNK, d), lambda bi, c: (bi, c, 0))
    bwd = pl.BlockSpec((1, CHUNK, d), lambda bi, c: (bi, nc - 1 - c, 0))
    fwd2 = pl.BlockSpec((1, 1, CHUNK, d), lambda bi, c: (0, bi, c, 0))
    bwd2 = pl.BlockSpec((1, 1, CHUNK, d), lambda bi, c: (1, bi, nc - 1 - c, 0))
    state = pl.BlockSpec((1, 2, H_B, N_B, N_B), lambda bi, c: (bi, 0, 0, 0, 0))
    in_specs = [fwd] * 3 + [bwd] * 3 + [fwd2] * 3 + [bwd2] * 3
    args = [r, v, kkn, r, v, kkn, kd, ba, lw, kd, ba, lw]
    if s0 is not None:
        in_specs.append(state)
        args.append(s0)
    return pl.pallas_call(
        functools.partial(_scan_kernel, has_s0=s0 is not None, n_chunks=nc),
        grid=(b, nc),
        in_specs=in_specs,
        out_specs=[fwd, bwd, state],
        out_shape=[jax.ShapeDtypeStruct((b, s, d), F32),
                   jax.ShapeDtypeStruct((b, s, d), F32),
                   jax.ShapeDtypeStruct((b, 2, H_B, N_B, N_B), F32)],
        scratch_shapes=[pltpu.VMEM((2, H_B, N_B, N_B), F32)],
        compiler_params=pltpu.CompilerParams(
            dimension_semantics=("arbitrary", "arbitrary"),
            vmem_limit_bytes=VMEM_LIMIT),
        name="rwkv_scan",
    )(*args)


def _merge_kernel(x_ref, oa_ref, y0_ref, y1_ref, bonus_ref, gate_ref,
                  g0a_ref, g0b_ref, g0c_ref, g0d_ref, g1a_ref, g1b_ref, g1c_ref, g1d_ref,
                  mg_ref, sc_ref, sh_ref, gnw_ref, gnb_ref, e_ref, et_ref, wout_ref,
                  gpost_ref, gpre_ref, wr_ref, br_ref, cnt0_ref,
                  x1_ref, h2_ref, route_ref, rank_ref, ew_ref, cnt_ref, cnt_scr):
    @pl.when((pl.program_id(0) == 0) & (pl.program_id(1) == 0))
    def _():
        cnt_scr[...] = cnt0_ref[...]

    g = (e_ref[...], et_ref[...])
    y = y0_ref[0] + y1_ref[0]
    mu = _headsum(y, g) * (1.0 / N_B)
    yc = y - mu
    var = _headsum(yc * yc, g) * (1.0 / N_B)
    yn = yc * lax.rsqrt(var + GN_EPS) * gnw_ref[...] + gnb_ref[...]
    ob = (yn + bonus_ref[0]) * gate_ref[0]
    g0 = jnp.concatenate([r[0].astype(F32) for r in (g0a_ref, g0b_ref, g0c_ref, g0d_ref)], axis=1)
    g1 = jnp.concatenate([r[0].astype(F32) for r in (g1a_ref, g1b_ref, g1c_ref, g1d_ref)], axis=1)
    merged = _sigmoid(g0) * oa_ref[0] + _sigmoid(g1) * ob
    out = _dot(merged, wout_ref[...])
    x1 = x_ref[0] + mg_ref[0] * _rms(out, gpost_ref[...])
    x1_ref[0] = x1
    h2 = _rms(x1, gpre_ref[...]) * (1.0 + sc_ref[0]) + sh_ref[0]
    h2_ref[0] = h2

    h_hi = h2.astype(BF16)
    h_lo = (h2 - h_hi.astype(F32)).astype(BF16)
    wr = wr_ref[...]
    w_hi = wr.astype(BF16)
    w_lo = (wr - w_hi.astype(F32)).astype(BF16)
    logits = (jnp.dot(h_hi, w_hi, preferred_element_type=F32) + jnp.dot(h_hi, w_lo, preferred_element_type=F32)
              + jnp.dot(h_lo, w_hi, preferred_element_type=F32)) + br_ref[...]
    lane = lax.broadcasted_iota(jnp.int32, logits.shape, 1)
    work = logits
    top = None
    picks = []
    for _ in range(TOP_K):
        mx = jnp.max(work, axis=-1, keepdims=True)
        idx = jnp.min(jnp.where(work == mx, lane, LANES), axis=-1, keepdims=True)
        if top is None:
            top = mx
        picks.append((idx, jnp.exp(mx - top)))
        work = jnp.where(lane == idx, -jnp.inf, work)
    denom = sum(e for _, e in picks)
    tm = logits.shape[0]
    onehot = jnp.zeros_like(logits)
    for idx, _ in picks:
        onehot = onehot + jnp.where(lane == idx, 1.0, 0.0)
    rr = lax.broadcasted_iota(jnp.int32, (tm, tm), 0)
    cc = lax.broadcasted_iota(jnp.int32, (tm, tm), 1)
    before = cnt_scr[...] + _dot(jnp.where(rr > cc, 1.0, 0.0), onehot)
    route = jnp.zeros(logits.shape, jnp.int32)
    ew = jnp.zeros_like(logits)
    for j, (idx, e) in enumerate(picks):
        route = jnp.where(lane == j, idx, route)
        ew = jnp.where(lane == j, e / denom, ew)
    route_ref[0] = route
    rank_ref[0] = jnp.where(onehot > 0.0, before, 0.0).astype(jnp.int32)
    ew_ref[0] = ew
    cnt_scr[...] = cnt_scr[...] + jnp.sum(onehot, axis=0, keepdims=True)
    cnt_ref[...] = cnt_scr[...]


def _merge_out(x, oa, y0, y1, bonus, gate, proj, mods, p, cnt0, tm):
    b, s, d = x.shape
    per_batch = mods[0].shape[0] > 1
    mod_map = (lambda bi, i: (bi, 0, 0)) if per_batch else (lambda bi, i: (0, 0, 0))
    tok = pl.BlockSpec((1, tm, d), lambda bi, i: (bi, i, 0))

    def full(a):
        return pl.BlockSpec(a.shape, lambda bi, i: (0,) * a.ndim)

    consts = [p['gn_w'], p['gn_b'], p['head_ind'], p['head_ind_t'], p['w_out'], p['g_post_mix'], p['g_pre_ffn'],
              p['w_router'], p['b_router']]
    in_specs = ([tok] * 6
                + [pl.BlockSpec((1, tm, GATE_W), lambda bi, i, c=col // GATE_W + k: (bi, i, c))
                   for col in (COL_G0, COL_G1) for k in range(d // GATE_W)]
                + [pl.BlockSpec((1, 1, d), mod_map)] * 3
                + [full(a) for a in consts] + [full(cnt0)])
    return pl.pallas_call(
        _merge_kernel,
        grid=(b, s // tm),
        in_specs=in_specs,
        out_specs=[tok, tok] + [pl.BlockSpec((1, tm, LANES), lambda bi, i: (bi, i, 0))] * 3
                  + [pl.BlockSpec((1, LANES), lambda bi, i: (0, 0))],
        out_shape=[jax.ShapeDtypeStruct((b, s, d), F32),
                   jax.ShapeDtypeStruct((b, s, d), F32),
                   jax.ShapeDtypeStruct((b, s, LANES), jnp.int32),
                   jax.ShapeDtypeStruct((b, s, LANES), jnp.int32),
                   jax.ShapeDtypeStruct((b, s, LANES), F32),
                   jax.ShapeDtypeStruct((1, LANES), F32)],
        scratch_shapes=[pltpu.VMEM((1, LANES), F32)],
        compiler_params=pltpu.CompilerParams(
            dimension_semantics=("arbitrary", "arbitrary"),
            vmem_limit_bytes=VMEM_LIMIT),
        name="merge_out",
    )(x, oa, y0, y1, bonus, gate, *([proj] * (2 * d // GATE_W)), *mods, *consts, cnt0)


def _route_metadata(eid, rank, counts, n_tiles):
    padded = (counts + MOE_TM - 1) // MOE_TM * MOE_TM
    ends = jnp.cumsum(padded)
    offs = ends - padded
    experts = jnp.arange(N_EXPERTS, dtype=jnp.int32)
    pos = jnp.sum(jnp.where(eid[:, :, None] == experts, (offs + rank)[:, None, :], 0), axis=-1)
    idx = jnp.arange(n_tiles, dtype=jnp.int32)
    valid = idx * MOE_TM < ends[-1]
    tile = jnp.where(valid, idx, ends[-1] // MOE_TM - 1)
    te = jnp.sum((ends[None, :] <= (tile * MOE_TM)[:, None]).astype(jnp.int32), axis=1)
    first = valid & jnp.concatenate([jnp.ones((1,), bool), te[1:] != te[:-1]])
    used = counts > 0
    slot_of = (jnp.cumsum(used.astype(jnp.int32)) - 1) % 2
    later = (experts[None, :] > experts[:, None]) & used[None, :]
    next_of = jnp.min(jnp.where(later, experts[None, :], N_EXPERTS), axis=1)
    next_of = jnp.where(next_of == N_EXPERTS, -1, next_of)
    is_te = te[:, None] == experts[None, :]
    pick = lambda table: jnp.sum(jnp.where(is_te, table[None, :], 0), axis=1)
    sched = (tile, te, valid.astype(jnp.int32), first.astype(jnp.int32), pick(slot_of), pick(next_of))
    fill = jnp.concatenate([jnp.where(padded > counts, ends - MOE_TM, -1),
                            jnp.where(valid, -1, idx * MOE_TM)[eid.size // MOE_TM:]])
    return pos, tuple(a.astype(jnp.int32) for a in sched), fill.astype(jnp.int32)


def _dispatch_kernel(fill_ref, pos_ref, hp_ref, hs_ref, xs_hbm, zbuf, sem, fsem, *, n_p_tiles, n_fill):
    i = pl.program_id(0)

    def fill_copy(e):
        start = pl.multiple_of(fill_ref[e], MOE_TM)
        return pltpu.make_async_copy(zbuf, xs_hbm.at[pl.ds(start, MOE_TM)], fsem)

    @pl.when(i == 0)
    def _():
        zbuf[...] = jnp.zeros_like(zbuf)
        for e in range(n_fill):
            @pl.when(fill_ref[e] >= 0)
            def _(e=e):
                fill_copy(e).start()
        for e in range(n_fill):
            @pl.when(fill_ref[e] >= 0)
            def _(e=e):
                fill_copy(e).wait()

    def scatter(h_ref):
        for t in range(COMB_TB):
            for j in range(TOP_K):
                pltpu.make_async_copy(h_ref.at[pl.ds(t, 1)],
                                      xs_hbm.at[pl.ds(pos_ref[0, 0, j * COMB_TB + t], 1)], sem).start()

    @pl.when(i < n_p_tiles)
    def _():
        scatter(hp_ref)

    @pl.when(i >= n_p_tiles)
    def _():
        scatter(hs_ref)

    for _ in range(TOP_K):
        pltpu.make_async_copy(hp_ref, xs_hbm.at[pl.ds(0, COMB_TB)], sem).wait()


def _dispatch(pos, fill, h2p, h2s, n_tiles):
    d = D_MODEL
    n_p, n_s = h2p.shape[0], h2s.shape[0]
    n = (n_p + n_s) // COMB_TB
    n_pt = n_p // COMB_TB
    pos_t = pos.reshape(n, COMB_TB, TOP_K).transpose(0, 2, 1).reshape(n, 1, TOP_K * COMB_TB)
    grid_spec = pltpu.PrefetchScalarGridSpec(
        num_scalar_prefetch=1,
        grid=(n,),
        in_specs=[pl.BlockSpec((1, 1, TOP_K * COMB_TB), lambda i, fill: (i, 0, 0), memory_space=pltpu.SMEM),
                  pl.BlockSpec((COMB_TB, d), lambda i, fill: (jnp.minimum(i, n_pt - 1), 0)),
                  pl.BlockSpec((COMB_TB, d), lambda i, fill: (jnp.maximum(i - n_pt, 0), 0))],
        out_specs=pl.BlockSpec(memory_space=pl.ANY),
        scratch_shapes=[pltpu.VMEM((MOE_TM, d), F32),
                        pltpu.SemaphoreType.DMA(()),
                        pltpu.SemaphoreType.DMA(())])
    return pl.pallas_call(
        functools.partial(_dispatch_kernel, n_p_tiles=n_pt, n_fill=fill.shape[0]),
        grid_spec=grid_spec,
        out_shape=jax.ShapeDtypeStruct((n_tiles * MOE_TM, d), F32),
        compiler_params=pltpu.CompilerParams(
            dimension_semantics=("arbitrary",), vmem_limit_bytes=VMEM_LIMIT),
        name="moe_dispatch",
    )(fill, pos_t, h2p, h2s)


def _experts_kernel(tile_ref, te_ref, tv_ref, tf_ref, ws_ref, nx_ref, x_ref,
                    wgu_hbm, bgu_ref, wd_hbm, bd_ref, ys_ref, wgu_buf, wd_buf, wgu_scr, wd_scr, sem):
    i = pl.program_id(0)

    def weight_copies(e, slot):
        return (pltpu.make_async_copy(wgu_hbm.at[e], wgu_buf.at[slot], sem.at[0, slot]),
                pltpu.make_async_copy(wd_hbm.at[e], wd_buf.at[slot], sem.at[1, slot]))

    @pl.when(i == 0)
    def _():
        for cp in weight_copies(te_ref[0], 0):
            cp.start()

    @pl.when(tf_ref[i] == 1)
    def _():
        slot = ws_ref[i]

        @pl.when(nx_ref[i] >= 0)
        def _():
            for cp in weight_copies(nx_ref[i], 1 - slot):
                cp.start()

        for cp in weight_copies(te_ref[i], slot):
            cp.wait()
        wgu_scr[...] = wgu_buf[slot].astype(BF16)
        wd_scr[...] = wd_buf[slot].astype(BF16)

    @pl.when(tv_ref[i] == 0)
    def _():
        ys_ref[...] = jnp.zeros_like(ys_ref)

    @pl.when(tv_ref[i] == 1)
    def _():
        gu = jnp.dot(x_ref[...].astype(BF16), wgu_scr[...], preferred_element_type=F32) + bgu_ref[0]
        gate = jnp.minimum(gu[:, :D_MODEL], SWIGLU_LIMIT)
        up = jnp.clip(gu[:, D_MODEL:], -SWIGLU_LIMIT, SWIGLU_LIMIT)
        act = (up + 1.0) * gate * _sigmoid(SWIGLU_ALPHA * gate)
        ys_ref[...] = jnp.dot(act.astype(BF16), wd_scr[...], preferred_element_type=F32) + bd_ref[0]


def _experts(xs, sched, p, n_tiles):
    d = D_MODEL
    bias = lambda w: pl.BlockSpec((1, 1, w), lambda i, tile, te, *_: (te[i], 0, 0))
    grid_spec = pltpu.PrefetchScalarGridSpec(
        num_scalar_prefetch=len(sched),
        grid=(n_tiles,),
        in_specs=[pl.BlockSpec((MOE_TM, d), lambda i, tile, *_: (tile[i], 0)),
                  pl.BlockSpec(memory_space=pl.ANY), bias(2 * d),
                  pl.BlockSpec(memory_space=pl.ANY), bias(d)],
        out_specs=pl.BlockSpec((MOE_TM, d), lambda i, *_: (i, 0)),
        scratch_shapes=[pltpu.VMEM((2, d, 2 * d), F32),
                        pltpu.VMEM((2, d, d), F32),
                        pltpu.VMEM((d, 2 * d), BF16),
                        pltpu.VMEM((d, d), BF16),
                        pltpu.SemaphoreType.DMA((2, 2))])
    return pl.pallas_call(
        _experts_kernel,
        grid_spec=grid_spec,
        out_shape=jax.ShapeDtypeStruct((n_tiles * MOE_TM, d), F32),
        compiler_params=pltpu.CompilerParams(
            dimension_semantics=("arbitrary",), vmem_limit_bytes=VMEM_LIMIT),
        name="moe_experts",
    )(*sched, xs, p['w_gate_up'], p['b_gate_up'], p['w_down'], p['b_down'])


def _combine_kernel(pos_ref, posn_ref, ys_hbm, ew_ref, x1_ref, mg_ref, gpost_ref, o_ref, buf, sem, *, n):
    i = pl.program_id(0)
    slot = i % 2

    def row_copy(row, j, t, slot_):
        return pltpu.make_async_copy(ys_hbm.at[pl.ds(row, 1)], buf.at[slot_, j, pl.ds(t, 1)], sem.at[slot_])

    def gather(src_ref, slot_):
        for t in range(COMB_TB):
            for j in range(TOP_K):
                row_copy(src_ref[0, 0, j * COMB_TB + t], j, t, slot_).start()

    @pl.when(i == 0)
    def _():
        gather(pos_ref, 0)

    @pl.when(i + 1 < n)
    def _():
        gather(posn_ref, 1 - slot)

    for j in range(TOP_K):
        pltpu.make_async_copy(ys_hbm.at[pl.ds(0, COMB_TB)], buf.at[slot, j], sem.at[slot]).wait()
    ew = ew_ref[...]
    f = sum(ew[:, j:j + 1] * buf[slot, j] for j in range(TOP_K))
    o_ref[...] = x1_ref[...] + mg_ref[0] * _rms(f, gpost_ref[...])


def _combine(ys, pos, ew, x1, mod_gate, p):
    b, s, d = x1.shape
    n_tok = b * s
    n_tiles = n_tok // COMB_TB
    per_batch = mod_gate.shape[0] > 1
    if per_batch:
        mod_gate = jnp.repeat(mod_gate, s // COMB_TB, axis=0)
    mod_map = (lambda i: (i, 0, 0)) if per_batch else (lambda i: (0, 0, 0))
    pos_t = pos.reshape(n_tiles, COMB_TB, TOP_K).transpose(0, 2, 1).reshape(n_tiles, 1, TOP_K * COMB_TB)
    smem_tile = lambda f: pl.BlockSpec((1, 1, TOP_K * COMB_TB), f, memory_space=pltpu.SMEM)
    out = pl.pallas_call(
        functools.partial(_combine_kernel, n=n_tiles),
        grid=(n_tiles,),
        in_specs=[smem_tile(lambda i: (i, 0, 0)),
                  smem_tile(lambda i: (jnp.minimum(i + 1, n_tiles - 1), 0, 0)),
                  pl.BlockSpec(memory_space=pl.ANY),
                  pl.BlockSpec((COMB_TB, LANES), lambda i: (i, 0)),
                  pl.BlockSpec((COMB_TB, d), lambda i: (i, 0)),
                  pl.BlockSpec((1, 1, d), mod_map),
                  pl.BlockSpec((1, d), lambda i: (0, 0))],
        out_specs=pl.BlockSpec((COMB_TB, d), lambda i: (i, 0)),
        out_shape=jax.ShapeDtypeStruct((n_tok, d), F32),
        scratch_shapes=[pltpu.VMEM((2, TOP_K, COMB_TB, d), F32),
                        pltpu.SemaphoreType.DMA((2,))],
        compiler_params=pltpu.CompilerParams(
            dimension_semantics=("arbitrary",), vmem_limit_bytes=VMEM_LIMIT),
        name="moe_combine",
    )(pos_t, pos_t, ys, ew.reshape(n_tok, LANES), x1.reshape(n_tok, d), mod_gate, p['g_post_ffn'])
    return out.reshape(b, s, d)


def _rope_tables(n_tokens):
    rows = n_tokens // GRID_W
    row = jnp.repeat(jnp.arange(rows, dtype=F32), GRID_W)
    col = jnp.tile(jnp.arange(GRID_W, dtype=F32), rows)
    inv = ROPE_THETA ** (-jnp.arange(ROPE_FREQS, dtype=F32) / ROPE_FREQS)
    ang_r = row[:, None] * inv[None, :]
    ang_c = col[:, None] * inv[None, :]
    ang = jnp.concatenate([ang_r, ang_r, ang_c, ang_c] * 2, axis=-1)
    return jnp.cos(ang), jnp.sin(ang)


def _stream(x, mods, proj_w, lam, p, rope, ctx, s0, cnt0, proj_dtype, tm_proj, tm_tok):
    b, s, d = x.shape
    xf = x if mods[0].shape[0] > 1 else x.reshape(1, b * s, d)
    proj = _norm_proj(xf, p['g_pre_mix'], mods[1], mods[0], proj_w, tm_proj, PROJ_TN, proj_dtype)
    proj = proj.reshape(b, s, C_IN)
    oa = _diff_attention(proj, lam, p['g_subln'], rope, ctx, tq=min(s, 4 * ATT_SUB),
                         hps=H_A if s <= ATT_SUB else 1)
    r, v, kkn, kd, ba, lw, bonus, gate = _rwkv_prep(proj, p, min(s, 256))
    y0, y1, sfin = _rwkv_scan(r, v, kkn, kd, ba, lw, s0)
    tb, tsq = xf.shape[0], xf.shape[1]
    x1, h2, route, rank, ew, cnt = _merge_out(
        *(a.reshape(tb, tsq, a.shape[-1]) for a in (x, oa, y0, y1, bonus, gate, proj)),
        (mods[2], mods[4], mods[3]), p, cnt0, tm_tok)
    n = b * s
    route = (route.reshape(n, LANES)[:, :TOP_K], rank.reshape(n, LANES)[:, :N_EXPERTS])
    return (x1.reshape(b, s, d), h2.reshape(n, d), route, ew.reshape(n, LANES), cnt,
            proj, sfin)


def kernel(x_prompt, x_sample, cache_k, cache_v, state_rwkv, c, c_ctx, w_ada, b_ada, g_pre_mix, g_post_mix, g_pre_ffn, g_post_ffn, w_in, mu_prev, mu_next, lam, g_subln, k_k, k_a, r_k, w0, w_up, a0, a_up, g_up, gn_w, gn_b, w_out, w_router, b_router, w_gate_up, b_gate_up, w_down, b_down):
    l = 0
    d = D_MODEL
    bp, sp, _ = x_prompt.shape
    bs, ss, _ = x_sample.shape

    n_cond = 1 + bs
    rows = -(-n_cond // SUBLANES) * SUBLANES
    cond = jnp.concatenate([c_ctx[None, :], c, jnp.zeros((rows - n_cond, d), F32)], axis=0)
    mod = _modulation(cond, w_ada[l], b_ada[l][None, :])
    mods_p = [mod[0:1, i * d:(i + 1) * d].reshape(1, 1, d) for i in range(N_MOD)]
    mods_s = [mod[1:1 + bs, i * d:(i + 1) * d].reshape(bs, 1, d) for i in range(N_MOD)]

    w_proj = w_in[l].astype(BF16)

    lq = lam[l]
    lam_val = (jnp.exp(jnp.sum(lq[0] * lq[1])) - jnp.exp(jnp.sum(lq[2] * lq[3])) + LAM_INIT).reshape(1)

    head = jnp.arange(d) // N_B
    wr = jnp.concatenate([w_router[l], jnp.zeros((d, LANES - N_EXPERTS), F32)], axis=1)
    br = jnp.concatenate([b_router[l], jnp.full((LANES - N_EXPERTS,), -jnp.inf, F32)])[None, :]
    mup, mun = mu_prev[l][None, :], mu_next[l][None, :]
    p = {
        'g_pre_mix': g_pre_mix[l][None, :], 'g_post_mix': g_post_mix[l][None, :],
        'g_pre_ffn': g_pre_ffn[l][None, :], 'g_post_ffn': g_post_ffn[l][None, :],
        'g_subln': g_subln[l][None, :],
        'mu_prev_main': mup[:, :3 * d], 'mu_next_main': mun[:, :3 * d],
        'mu_prev_lora': mup[:, 3 * d:], 'mu_next_lora': mun[:, 3 * d:],
        'k_k': k_k[l][None, :], 'k_a': k_a[l][None, :], 'r_k': r_k[l].reshape(1, d),
        'w0': w0[l].reshape(1, 2 * d),
        'w_up': jnp.concatenate([w_up[l, 0], w_up[l, 1]], axis=1).astype(BF16),
        'a0': a0[l].reshape(1, 2 * d),
        'a_up': jnp.concatenate([a_up[l, 0], a_up[l, 1]], axis=1).astype(BF16),
        'g_up': g_up[l].astype(BF16),
        'head_ind': (head[:, None] == jnp.arange(LANES)[None, :]).astype(BF16),
        'head_ind_t': (jnp.arange(LANES)[:, None] == head[None, :]).astype(BF16),
        'gn_w': gn_w[l][None, :], 'gn_b': gn_b[l][None, :],
        'w_out': w_out[l].astype(BF16),
        'w_router': wr, 'b_router': br,
        'w_gate_up': w_gate_up[l], 'b_gate_up': b_gate_up[l][:, None, :],
        'w_down': w_down[l], 'b_down': b_down[l][:, None, :],
    }

    rope = _rope_tables(ss)
    ctx = (cache_k[:, l].reshape(bs, -1, d), cache_v[:, l].reshape(bs, -1, d))

    cnt0 = jnp.zeros((1, LANES), F32)
    x1p, h2p, route_p, ewp, cnt_p, proj_p, sfin = _stream(
        x_prompt, mods_p, w_proj, lam_val, p, None, None, None, cnt0, F32, 1024, 512)
    x1s, h2s, route_s, ews, cnt_s, _, _ = _stream(
        x_sample, mods_s, w_proj, lam_val, p, rope, ctx, state_rwkv[:, l], cnt_p, BF16, 1024, 512)

    n_p, n_s = bp * sp, bs * ss
    n_tok = n_p + n_s
    eid, rank = (jnp.concatenate([a, b_], axis=0) for a, b_ in zip(route_p, route_s))
    counts = cnt_s[0, :N_EXPERTS].astype(jnp.int32)
    n_tiles = -(-(n_tok * TOP_K + N_EXPERTS * (MOE_TM - 1)) // MOE_TM)
    pos, sched, fill = _route_metadata(eid, rank, counts, n_tiles)
    xs = _dispatch(pos, fill, h2p, h2s, n_tiles)
    rows = _experts(xs, sched, p, n_tiles)
    yp = _combine(rows, pos[:n_p], ewp, x1p, mods_p[5], p)
    ys = _combine(rows, pos[n_p:], ews, x1s, mods_s[5], p)

    new_k = proj_p[:, :, COL_K:COL_K + d].reshape(bp, 1, sp, H_A, 2, HD_A)
    new_v = proj_p[:, :, COL_V:COL_V + d].reshape(bp, 1, sp, H_A, DV_A)
    return (yp, ys, new_k, new_v, sfin[:, None])
```

```python
import functools
import math

import jax
import jax.numpy as jnp
from jax import lax
from jax.experimental import pallas as pl
from jax.experimental.pallas import tpu as pltpu

F32 = jnp.float32
BF16 = jnp.bfloat16

D_MODEL = 1024
GRID_W = 64
HD_A = 64
DV_A = 2 * HD_A
H_A = D_MODEL // DV_A
N_B = 64
H_B = D_MODEL // N_B
LORA_W = 64
LORA_A = 64
LORA_G = 128
N_EXPERTS = 32
TOP_K = 4
SWIGLU_LIMIT = 7.0
SWIGLU_ALPHA = 1.702
ROPE_THETA = 10000.0
ROPE_FREQS = HD_A // 4
NORM_EPS = 1e-6
GN_EPS = 64e-5
L2_EPS = 1e-12
ATTN_SCALE = HD_A ** -0.5
N_MOD = 6
LAM_INIT = 0.8 - 0.6 * math.exp(-0.3 * 0)
DECAY_SCALE = -math.exp(-0.5)

LANES = 128
SUBLANES = 8
CHUNK = 64
SCAN_NB = 1
MOE_TM = 512
PROJ_TN = 1408
COMB_TB = 256
ATT_SUB = 256
LOG2_E = math.log2(math.e)
VMEM_LIMIT = 56 * 1024 * 1024

C_LORA = LORA_W + LORA_A + LORA_G
COL_Q, COL_K, COL_V, COL_R, COL_KR, COL_VR, COL_LORA = (i * D_MODEL for i in range(7))
COL_G0 = COL_LORA + C_LORA
COL_G1 = COL_G0 + D_MODEL
C_IN = COL_G1 + D_MODEL
GATE_W = 256


def _sigmoid(x):
    return 1.0 / (1.0 + jnp.exp(-x))


def _dot(a, b):
    return jnp.dot(a.astype(BF16), b.astype(BF16), preferred_element_type=F32)


def _dot_nt(a, b):
    return lax.dot_general(a.astype(BF16), b.astype(BF16), (((1,), (1,)), ((), ())),
                           preferred_element_type=F32)


def _dot_tn(a, b):
    return lax.dot_general(a.astype(BF16), b.astype(BF16), (((0,), (0,)), ((), ())),
                           preferred_element_type=F32)


def _split3(x):
    hi = x.astype(BF16)
    r1 = x - hi.astype(F32)
    mid = r1.astype(BF16)
    lo = (r1 - mid.astype(F32)).astype(BF16)
    return hi, mid, lo


def _headsum(x, ind):
    e, et = ind
    s = jnp.dot(x.astype(BF16), e, preferred_element_type=F32)
    s_hi = s.astype(BF16)
    s_lo = (s - s_hi.astype(F32)).astype(BF16)
    return jnp.dot(s_hi, et, preferred_element_type=F32) + jnp.dot(s_lo, et, preferred_element_type=F32)


def _rms(x, g):
    return x * lax.rsqrt(jnp.mean(x * x, axis=-1, keepdims=True) + NORM_EPS) * g


def _mod_kernel(c_ref, w_ref, b_ref, o_ref):
    c = c_ref[...]
    s = c * _sigmoid(c)
    o_ref[...] = _dot(s, w_ref[...]) + b_ref[...]


def _modulation(cond, w_ada, b_ada):
    rows, d = cond.shape
    n = w_ada.shape[1]
    tn = 768
    return pl.pallas_call(
        _mod_kernel,
        grid=(n // tn,),
        in_specs=[pl.BlockSpec((rows, d), lambda j: (0, 0)),
                  pl.BlockSpec((d, tn), lambda j: (0, j)),
                  pl.BlockSpec((1, tn), lambda j: (0, j))],
        out_specs=pl.BlockSpec((rows, tn), lambda j: (0, j)),
        out_shape=jax.ShapeDtypeStruct((rows, n), F32),
        name="modulation",
    )(cond, w_ada, b_ada)


def _proj_kernel(x_ref, g_ref, sc_ref, sh_ref, w_ref, o_ref, h_scr):
    @pl.when(pl.program_id(2) == 0)
    def _():
        h = _rms(x_ref[0], g_ref[...]) * (1.0 + sc_ref[0]) + sh_ref[0]
        h_scr[...] = h.astype(BF16)

    o_ref[0] = jnp.dot(h_scr[...], w_ref[...], preferred_element_type=F32).astype(o_ref.dtype)


def _norm_proj(x, g, scale, shift, w, tm, tn, out_dtype):
    b, s, d = x.shape
    n = w.shape[1]
    per_batch = scale.shape[0] > 1
    mod_map = (lambda bi, i, j: (bi, 0, 0)) if per_batch else (lambda bi, i, j: (0, 0, 0))
    return pl.pallas_call(
        _proj_kernel,
        grid=(b, s // tm, n // tn),
        in_specs=[pl.BlockSpec((1, tm, d), lambda bi, i, j: (bi, i, 0)),
                  pl.BlockSpec((1, d), lambda bi, i, j: (0, 0)),
                  pl.BlockSpec((1, 1, d), mod_map),
                  pl.BlockSpec((1, 1, d), mod_map),
                  pl.BlockSpec((d, tn), lambda bi, i, j: (0, j))],
        out_specs=pl.BlockSpec((1, tm, tn), lambda bi, i, j: (bi, i, j)),
        out_shape=jax.ShapeDtypeStruct((b, s, n), out_dtype),
        scratch_shapes=[pltpu.VMEM((tm, d), BF16)],
        compiler_params=pltpu.CompilerParams(
            dimension_semantics=("arbitrary", "arbitrary", "arbitrary"),
            vmem_limit_bytes=VMEM_LIMIT),
        name="norm_proj",
    )(x, g, scale, shift, w)


def _rope(x, cos, sin):
    lane = lax.broadcasted_iota(jnp.int32, x.shape, 1)
    even = (lane // ROPE_FREQS) % 2 == 0
    rot = jnp.where(even, -pltpu.roll(x, LANES - ROPE_FREQS, 1), pltpu.roll(x, ROPE_FREQS, 1))
    return x * cos + rot * sin


def _attn_kernel(*refs, use_rope, use_ctx, s_new):
    it = iter(refs)
    lam_ref = next(it)
    q_ref, k_ref, v_ref = next(it), next(it), next(it)
    if use_rope:
        cq_ref, sq_ref, ck_ref, sk_ref = next(it), next(it), next(it), next(it)
    if use_ctx:
        ctxk_ref, ctxv_ref = next(it), next(it)
    g_ref = next(it)
    o_ref = next(it)
    k_scr, vt_scr = next(it), next(it)

    hps = q_ref.shape[-1] // DV_A
    head_lanes = [slice(hh * DV_A, (hh + 1) * DV_A) for hh in range(hps)]

    @pl.when(pl.program_id(2) == 0)
    def _():
        for hh, hl in enumerate(head_lanes):
            k = k_ref[0][:, hl].astype(F32)
            if use_rope:
                k = _rope(k, ck_ref[...], sk_ref[...])
            k_scr[hh, 0:s_new, :] = k.astype(BF16)
            vt_scr[hh, :, 0:s_new] = v_ref[0][:, hl].astype(F32).T.astype(BF16)
            if use_ctx:
                k_scr[hh, s_new:, :] = ctxk_ref[0][:, hl].astype(BF16)
                vt_scr[hh, :, s_new:] = ctxv_ref[0][:, hl].T.astype(BF16)

    lam = lam_ref[0]
    n_sub = q_ref.shape[1] // ATT_SUB
    lane = lax.broadcasted_iota(jnp.int32, (ATT_SUB, DV_A), 1)
    sts = []
    for hh, hl in enumerate(head_lanes):
        q = q_ref[0][:, hl].astype(F32)
        if use_rope:
            q = _rope(q, cq_ref[...], sq_ref[...])
        q = q * (ATTN_SCALE * LOG2_E)
        for t in range(n_sub):
            for m in range(2):
                in_map = (lane >= HD_A) if m else (lane < HD_A)
                qm = jnp.where(in_map, q[t * ATT_SUB:(t + 1) * ATT_SUB], 0.0)
                sts.append((hh, _dot_nt(k_scr[hh], qm)))
    outs = []
    for hh, st in sts:
        et = jnp.exp2(st - jnp.max(st, axis=0, keepdims=True))
        l = jnp.sum(et, axis=0, keepdims=True)
        ot = jnp.dot(vt_scr[hh], et.astype(BF16), preferred_element_type=F32)
        outs.append(ot / l)
    for hh, hl in enumerate(head_lanes):
        for t in range(n_sub):
            u = 2 * (hh * n_sub + t)
            o = (outs[u] - lam * outs[u + 1]).T
            o = o * lax.rsqrt(jnp.mean(o * o, axis=-1, keepdims=True) + NORM_EPS)
            o_ref[0, t * ATT_SUB:(t + 1) * ATT_SUB, hl] = o * g_ref[:, hl] * (1.0 - LAM_INIT)


def _diff_attention(proj, lam, g_subln, rope=None, ctx=None, tq=256, hps=1):
    b, s, _ = proj.shape
    use_rope, use_ctx = rope is not None, ctx is not None
    s_tot = s + (ctx[0].shape[1] if use_ctx else 0)
    w = hps * DV_A
    in_specs = [pl.BlockSpec(memory_space=pltpu.SMEM),
                pl.BlockSpec((1, tq, w), lambda bi, h, i: (bi, i, COL_Q // w + h)),
                pl.BlockSpec((1, s, w), lambda bi, h, i: (bi, 0, COL_K // w + h)),
                pl.BlockSpec((1, s, w), lambda bi, h, i: (bi, 0, COL_V // w + h))]
    args = [lam, proj, proj, proj]
    if use_rope:
        cos, sin = rope
        in_specs += [pl.BlockSpec((tq, DV_A), lambda bi, h, i: (i, 0)),
                     pl.BlockSpec((tq, DV_A), lambda bi, h, i: (i, 0)),
                     pl.BlockSpec((s, DV_A), lambda bi, h, i: (0, 0)),
                     pl.BlockSpec((s, DV_A), lambda bi, h, i: (0, 0))]
        args += [cos, sin, cos, sin]
    if use_ctx:
        p = ctx[0].shape[1]
        in_specs += [pl.BlockSpec((1, p, w), lambda bi, h, i: (bi, 0, h)),
                     pl.BlockSpec((1, p, w), lambda bi, h, i: (bi, 0, h))]
        args += [ctx[0], ctx[1]]
    in_specs.append(pl.BlockSpec((1, w), lambda bi, h, i: (0, h)))
    args.append(g_subln)
    return pl.pallas_call(
        functools.partial(_attn_kernel, use_rope=use_rope, use_ctx=use_ctx, s_new=s),
        grid=(b, H_A // hps, s // tq),
        in_specs=in_specs,
        out_specs=pl.BlockSpec((1, tq, w), lambda bi, h, i: (bi, i, h)),
        out_shape=jax.ShapeDtypeStruct((b, s, D_MODEL), F32),
        scratch_shapes=[pltpu.VMEM((hps, s_tot, DV_A), BF16), pltpu.VMEM((hps, DV_A, s_tot), BF16)],
        compiler_params=pltpu.CompilerParams(
            dimension_semantics=("arbitrary", "arbitrary", "arbitrary"),
            vmem_limit_bytes=VMEM_LIMIT),
        name="diff_attention",
    )(*args)


def _shifted(x_ref, p_ref, n_ref, mup, mun, first, last):
    x = x_ref[0].astype(F32)
    ts = x.shape[0]
    halo = p_ref.shape[1]
    row = lax.broadcasted_iota(jnp.int32, x.shape, 0)
    prev_row = p_ref[0][halo - 1:halo, :].astype(F32) * first
    next_row = n_ref[0][0:1, :].astype(F32) * last
    prev = jnp.where(row == 0, prev_row, pltpu.roll(x, 1, 0))
    nxt = jnp.where(row == ts - 1, next_row, pltpu.roll(x, ts - 1, 0))
    return x + mup * (prev - x) + mun * (nxt - x)


def _prep_kernel(r_ref, rp_ref, rn_ref, k_ref, kp_ref, kn_ref, v_ref, vp_ref, vn_ref,
                 l_ref, lp_ref, ln_ref, mup_ref, mun_ref, mupl_ref, munl_ref,
                 kk_ref, ka_ref, rk_ref, w0_ref, wup_ref, a0_ref, aup_ref, gup_ref, e_ref, et_ref,
                 ro_ref, vo_ref, kko_ref, kd_ref, ba_ref, lw_ref, bonus_ref, gate_ref, *, n_tiles):
    i = pl.program_id(1)
    first = (i > 0).astype(F32)
    last = (i < n_tiles - 1).astype(F32)
    mup, mun = mup_ref[...], mun_ref[...]
    d = D_MODEL
    r = _shifted(r_ref, rp_ref, rn_ref, mup[:, 0:d], mun[:, 0:d], first, last)
    k = _shifted(k_ref, kp_ref, kn_ref, mup[:, d:2 * d], mun[:, d:2 * d], first, last)
    v = _shifted(v_ref, vp_ref, vn_ref, mup[:, 2 * d:3 * d], mun[:, 2 * d:3 * d], first, last)
    lo = _shifted(l_ref, lp_ref, ln_ref, mupl_ref[...], munl_ref[...], first, last)
    xw = lo[:, 0:LORA_W]
    xa = lo[:, LORA_W:LORA_W + LORA_A]
    xg = lo[:, LORA_W + LORA_A:]
    g = (e_ref[...], et_ref[...])

    kk = k * kk_ref[...]
    nrm = jnp.sqrt(_headsum(kk * kk, g))
    kkn = kk / jnp.maximum(nrm, L2_EPS)
    ro_ref[0] = r
    vo_ref[0] = v
    kko_ref[0] = kkn
    gate_ref[0] = _dot(_sigmoid(xg), gup_ref[...])

    wlog = w0_ref[...] + _dot(jnp.tanh(xw), wup_ref[...])
    alog = a0_ref[...] + _dot(xa, aup_ref[...])
    ka = ka_ref[...]
    rrk = r * rk_ref[...]
    dots = None
    for dr in range(2):
        sl = slice(dr * d, (dr + 1) * d)
        lw_ref[dr, 0] = DECAY_SCALE * _sigmoid(wlog[:, sl])
        a = _sigmoid(alog[:, sl])
        kd = k * (1.0 + (a - 1.0) * ka)
        kd_ref[dr, 0] = kd
        ba_ref[dr, 0] = kkn * a
        t = rrk * kd
        dots = t if dots is None else dots + t
    bonus_ref[0] = _headsum(dots, g) * v


def _rwkv_prep(proj, p, ts):
    b, s, _ = proj.shape
    d = D_MODEL
    nt = s // ts
    halo = SUBLANES * (4 // proj.dtype.itemsize)
    hb = ts // halo
    nhb = s // halo

    def main(col, w):
        return pl.BlockSpec((1, ts, w), lambda bi, i: (bi, i, col // w))

    def prev(col, w):
        return pl.BlockSpec((1, halo, w), lambda bi, i: (bi, jnp.maximum(i * hb - 1, 0), col // w))

    def nxt(col, w):
        return pl.BlockSpec((1, halo, w),
                            lambda bi, i: (bi, jnp.minimum((i + 1) * hb, nhb - 1), col // w))

    def full(a):
        return pl.BlockSpec(a.shape, lambda bi, i: (0,) * a.ndim)

    in_specs, args = [], []
    for col, w in ((COL_R, d), (COL_KR, d), (COL_VR, d), (COL_LORA, C_LORA)):
        in_specs += [main(col, w), prev(col, w), nxt(col, w)]
        args += [proj, proj, proj]
    consts = [p['mu_prev_main'], p['mu_next_main'], p['mu_prev_lora'], p['mu_next_lora'],
              p['k_k'], p['k_a'], p['r_k'], p['w0'], p['w_up'], p['a0'], p['a_up'], p['g_up'],
              p['head_ind'], p['head_ind_t']]
    in_specs += [full(a) for a in consts]
    args += consts
    tok = pl.BlockSpec((1, ts, d), lambda bi, i: (bi, i, 0))
    tok2 = pl.BlockSpec((2, 1, ts, d), lambda bi, i: (0, bi, i, 0))
    one = jax.ShapeDtypeStruct((b, s, d), F32)
    two = jax.ShapeDtypeStruct((2, b, s, d), F32)
    return pl.pallas_call(
        functools.partial(_prep_kernel, n_tiles=nt),
        grid=(b, nt),
        in_specs=in_specs,
        out_specs=[tok, tok, tok, tok2, tok2, tok2, tok, tok],
        out_shape=[one, one, one, two, two, two, one, one],
        compiler_params=pltpu.CompilerParams(
            dimension_semantics=("arbitrary", "arbitrary"),
            vmem_limit_bytes=VMEM_LIMIT),
        name="rwkv_prep",
    )(*args)


def _scan_masks(reverse):
    n = CHUNK
    tt = lax.broadcasted_iota(jnp.int32, (n, n), 0)
    ss = lax.broadcasted_iota(jnp.int32, (n, n), 1)
    hi, lo = (ss, tt) if reverse else (tt, ss)
    tt2 = lax.broadcasted_iota(jnp.int32, (n, 2 * n), 0)
    col2 = lax.broadcasted_iota(jnp.int32, (n, 2 * n), 1)
    ss2 = col2 % n
    hi2, lo2 = (ss2, tt2) if reverse else (tt2, ss2)
    levels = []
    blk = 2
    while blk < n:
        levels.append((hi // (2 * blk) == lo // (2 * blk)) & ((hi // blk) % 2 == 1) & ((lo // blk) % 2 == 0))
        blk *= 2
    return dict(strict=hi > lo, incl=hi >= lo, eye=(tt == ss).astype(F32),
                first=(hi // 2 == lo // 2) & (hi > lo), levels=levels,
                strict_r=(hi2 > lo2) & (col2 >= n),
                incl2=hi2 >= lo2)


def _scan_operands(reverse, incl, lw, r, v, kkn, kd, ba):
    n = CHUNK
    tri = jnp.where(incl, 1.0, 0.0).astype(BF16)
    cum = sum(jnp.dot(tri, part, preferred_element_type=F32) for part in _split3(lw))
    tot = cum[0:1, :] if reverse else cum[n - 1:n, :]
    g_inv = jnp.exp(-cum)
    g_rest = jnp.exp(tot - cum)
    return dict(a=(kkn * jnp.exp(cum - lw)).astype(BF16), r=(r * jnp.exp(cum)).astype(BF16),
                b=(ba * g_inv).astype(BF16), k=(kd * g_inv).astype(BF16),
                bh=(ba * g_rest).astype(BF16), kh=(kd * g_rest).astype(BF16),
                v=v.astype(BF16), g_tot=jnp.exp(tot))


def _scan_kernel(*refs, has_s0, n_chunks):
    it = iter(refs)
    tok_refs = [[next(it) for _ in range(3)] for _ in range(2)]
    dir_refs = [[next(it) for _ in range(3)] for _ in range(2)]
    s0_ref = next(it) if has_s0 else None
    y_refs = [next(it), next(it)]
    sf_ref = next(it)
    s_scr = next(it)
    c = pl.program_id(1)

    @pl.when(c == 0)
    def _():
        if has_s0:
            s_scr[...] = s0_ref[...]
        else:
            s_scr[...] = jnp.zeros_like(s_scr)

    n = CHUNK
    nb = s_scr.shape[0]
    masks = [_scan_masks(dr == 1) for dr in range(2)]
    ops = {}
    for bb in range(nb):
        for dr in range(2):
            r_ref, v_ref, kk_ref = tok_refs[dr]
            kd_ref, ba_ref, lw_ref = dir_refs[dr]
            ops[bb, dr] = _scan_operands(dr == 1, masks[dr]['incl'], lw_ref[0, bb], r_ref[bb], v_ref[bb],
                                         kk_ref[bb], kd_ref[0, bb], ba_ref[0, bb])

    units = [(bb, dr, h) for bb in range(nb) for dr in range(2) for h in range(H_B)]
    idx = range(len(units))
    sls = [slice(h * N_B, (h + 1) * N_B) for _, _, h in units]
    op = lambda u, name: ops[units[u][:2]][name][:, sls[u]]
    mk = lambda u, name: masks[units[u][1]][name]
    s_old = [s_scr[bb, dr, h] for bb, dr, h in units]
    ar = [jnp.concatenate([op(u, 'a'), op(u, 'r')], axis=0) for u in idx]
    bk = [jnp.concatenate([op(u, 'b'), op(u, 'k')], axis=0) for u in idx]
    bkh = [jnp.concatenate([op(u, 'bh'), op(u, 'kh')], axis=0) for u in idx]
    vh = [op(u, 'v') for u in idx]

    gram = [_dot_nt(ar[u], bk[u]) for u in idx]
    ars = [_dot_nt(ar[u], s_old[u]) for u in idx]
    nmat = [jnp.where(mk(u, 'strict'), gram[u][0:n, 0:n], 0.0) for u in idx]
    mak = [jnp.where(mk(u, 'strict_r'), gram[u][0:n, :], 0.0) for u in idx]
    pr = [jnp.where(mk(u, 'incl2'), gram[u][n:, :], 0.0) for u in idx]
    z = [ars[u][0:n] + _dot(mak[u], jnp.concatenate([vh[u], vh[u]], axis=0)) for u in idx]
    x = [mk(u, 'eye') - jnp.where(mk(u, 'first'), nmat[u], 0.0) for u in idx]
    for lvl in range(len(masks[0]['levels'])):
        xl = [_dot(x[u], jnp.where(mk(u, 'levels')[lvl], nmat[u], 0.0)) for u in idx]
        x = [x[u] - _dot(xl[u], x[u]) for u in idx]
    uu = [-_dot(x[u], z[u]) for u in idx]
    uv = [jnp.concatenate([uu[u].astype(BF16), vh[u]], axis=0) for u in idx]
    y = [ars[u][n:] + _dot(pr[u], uv[u]) for u in idx]
    s_new = [s_old[u] * op(u, 'g_tot') + _dot_tn(uv[u], bkh[u]) for u in idx]
    for u, (bb, dr, h) in enumerate(units):
        y_refs[dr][bb, :, sls[u]] = y[u]
    for u, (bb, dr, h) in enumerate(units):
        s_scr[bb, dr, h] = s_new[u]

    @pl.when(c == n_chunks - 1)
    def _():
        sf_ref[...] = s_scr[...]


def _rwkv_scan(r, v, kkn, kd, ba, lw, s0):
    b, s, d = r.shape
    nc = s // CHUNK
    nb = SCAN_NB
    fwd = pl.BlockSpec((nb, CHUNK, d), lambda bi, c: (bi, c, 0))
    bwd = pl.BlockSpec((nb, CHUNK, d), lambda bi, c: (bi, nc - 1 - c, 0))
    fwd2 = pl.BlockSpec((1, nb, CHUNK, d), lambda bi, c: (0, bi, c, 0))
    bwd2 = pl.BlockSpec((1, nb, CHUNK, d), lambda bi, c: (1, bi, nc - 1 - c, 0))
    state = pl.BlockSpec((nb, 2, H_B, N_B, N_B), lambda bi, c: (bi, 0, 0, 0, 0))
    in_specs = [fwd] * 3 + [bwd] * 3 + [fwd2] * 3 + [bwd2] * 3
    args = [r, v, kkn, r, v, kkn, kd, ba, lw, kd, ba, lw]
    if s0 is not None:
        in_specs.append(state)
        args.append(s0)
    return pl.pallas_call(
        functools.partial(_scan_kernel, has_s0=s0 is not None, n_chunks=nc),
        grid=(b // nb, nc),
        in_specs=in_specs,
        out_specs=[fwd, bwd, state],
        out_shape=[jax.ShapeDtypeStruct((b, s, d), F32),
                   jax.ShapeDtypeStruct((b, s, d), F32),
                   jax.ShapeDtypeStruct((b, 2, H_B, N_B, N_B), F32)],
        scratch_shapes=[pltpu.VMEM((nb, 2, H_B, N_B, N_B), F32)],
        compiler_params=pltpu.CompilerParams(
            dimension_semantics=("arbitrary", "arbitrary"),
            vmem_limit_bytes=VMEM_LIMIT),
        name="rwkv_scan",
    )(*args)


def _merge_kernel(x_ref, oa_ref, y0_ref, y1_ref, bonus_ref, gate_ref,
                  g0a_ref, g0b_ref, g0c_ref, g0d_ref, g1a_ref, g1b_ref, g1c_ref, g1d_ref,
                  mg_ref, sc_ref, sh_ref, gnw_ref, gnb_ref, e_ref, et_ref, wout_ref,
                  gpost_ref, gpre_ref, wr_ref, br_ref, cnt0_ref,
                  x1_ref, h2_ref, route_ref, rank_ref, ew_ref, cnt_ref, cnt_scr):
    @pl.when((pl.program_id(0) == 0) & (pl.program_id(1) == 0))
    def _():
        cnt_scr[...] = cnt0_ref[...]

    g = (e_ref[...], et_ref[...])
    y = y0_ref[0] + y1_ref[0]
    mu = _headsum(y, g) * (1.0 / N_B)
    yc = y - mu
    var = _headsum(yc * yc, g) * (1.0 / N_B)
    yn = yc * lax.rsqrt(var + GN_EPS) * gnw_ref[...] + gnb_ref[...]
    ob = (yn + bonus_ref[0]) * gate_ref[0]
    g0 = jnp.concatenate([r[0].astype(F32) for r in (g0a_ref, g0b_ref, g0c_ref, g0d_ref)], axis=1)
    g1 = jnp.concatenate([r[0].astype(F32) for r in (g1a_ref, g1b_ref, g1c_ref, g1d_ref)], axis=1)
    merged = _sigmoid(g0) * oa_ref[0] + _sigmoid(g1) * ob
    out = _dot(merged, wout_ref[...])
    x1 = x_ref[0] + mg_ref[0] * _rms(out, gpost_ref[...])
    x1_ref[0] = x1
    h2 = _rms(x1, gpre_ref[...]) * (1.0 + sc_ref[0]) + sh_ref[0]
    h2_ref[0] = h2

    h_hi = h2.astype(BF16)
    h_lo = (h2 - h_hi.astype(F32)).astype(BF16)
    wr = wr_ref[...]
    w_hi = wr.astype(BF16)
    w_lo = (wr - w_hi.astype(F32)).astype(BF16)
    logits = (jnp.dot(h_hi, w_hi, preferred_element_type=F32) + jnp.dot(h_hi, w_lo, preferred_element_type=F32)
              + jnp.dot(h_lo, w_hi, preferred_element_type=F32)) + br_ref[...]
    lane = lax.broadcasted_iota(jnp.int32, logits.shape, 1)
    work = logits
    top = None
    picks = []
    for _ in range(TOP_K):
        mx = jnp.max(work, axis=-1, keepdims=True)
        idx = jnp.min(jnp.where(work == mx, lane, LANES), axis=-1, keepdims=True)
        if top is None:
            top = mx
        picks.append((idx, jnp.exp(mx - top)))
        work = jnp.where(lane == idx, -jnp.inf, work)
    denom = sum(e for _, e in picks)
    tm = logits.shape[0]
    onehot = jnp.zeros_like(logits)
    for idx, _ in picks:
        onehot = onehot + jnp.where(lane == idx, 1.0, 0.0)
    rr = lax.broadcasted_iota(jnp.int32, (tm, tm), 0)
    cc = lax.broadcasted_iota(jnp.int32, (tm, tm), 1)
    before = cnt_scr[...] + _dot(jnp.where(rr > cc, 1.0, 0.0), onehot)
    route = jnp.zeros(logits.shape, jnp.int32)
    ew = jnp.zeros_like(logits)
    for j, (idx, e) in enumerate(picks):
        route = jnp.where(lane == j, idx, route)
        ew = jnp.where(lane == j, e / denom, ew)
    route_ref[0] = route
    rank_ref[0] = jnp.where(onehot > 0.0, before, 0.0).astype(jnp.int32)
    ew_ref[0] = ew
    cnt_scr[...] = cnt_scr[...] + jnp.sum(onehot, axis=0, keepdims=True)
    cnt_ref[...] = cnt_scr[...]


def _merge_out(x, oa, y0, y1, bonus, gate, proj, mods, p, cnt0, tm):
    b, s, d = x.shape
    per_batch = mods[0].shape[0] > 1
    mod_map = (lambda bi, i: (bi, 0, 0)) if per_batch else (lambda bi, i: (0, 0, 0))
    tok = pl.BlockSpec((1, tm, d), lambda bi, i: (bi, i, 0))

    def full(a):
        return pl.BlockSpec(a.shape, lambda bi, i: (0,) * a.ndim)

    consts = [p['gn_w'], p['gn_b'], p['head_ind'], p['head_ind_t'], p['w_out'], p['g_post_mix'], p['g_pre_ffn'],
              p['w_router'], p['b_router']]
    in_specs = ([tok] * 6
                + [pl.BlockSpec((1, tm, GATE_W), lambda bi, i, c=col // GATE_W + k: (bi, i, c))
                   for col in (COL_G0, COL_G1) for k in range(d // GATE_W)]
                + [pl.BlockSpec((1, 1, d), mod_map)] * 3
                + [full(a) for a in consts] + [full(cnt0)])
    return pl.pallas_call(
        _merge_kernel,
        grid=(b, s // tm),
        in_specs=in_specs,
        out_specs=[tok, tok] + [pl.BlockSpec((1, tm, LANES), lambda bi, i: (bi, i, 0))] * 3
                  + [pl.BlockSpec((1, LANES), lambda bi, i: (0, 0))],
        out_shape=[jax.ShapeDtypeStruct((b, s, d), F32),
                   jax.ShapeDtypeStruct((b, s, d), F32),
                   jax.ShapeDtypeStruct((b, s, LANES), jnp.int32),
                   jax.ShapeDtypeStruct((b, s, LANES), jnp.int32),
                   jax.ShapeDtypeStruct((b, s, LANES), F32),
                   jax.ShapeDtypeStruct((1, LANES), F32)],
        scratch_shapes=[pltpu.VMEM((1, LANES), F32)],
        compiler_params=pltpu.CompilerParams(
            dimension_semantics=("arbitrary", "arbitrary"),
            vmem_limit_bytes=VMEM_LIMIT),
        name="merge_out",
    )(x, oa, y0, y1, bonus, gate, *([proj] * (2 * d // GATE_W)), *mods, *consts, cnt0)


def _route_metadata(eid, rank, counts, n_tiles):
    padded = (counts + MOE_TM - 1) // MOE_TM * MOE_TM
    ends = jnp.cumsum(padded)
    offs = ends - padded
    experts = jnp.arange(N_EXPERTS, dtype=jnp.int32)
    pos = jnp.sum(jnp.where(eid[:, :, None] == experts, (offs + rank)[:, None, :], 0), axis=-1)
    idx = jnp.arange(n_tiles, dtype=jnp.int32)
    valid = idx * MOE_TM < ends[-1]
    tile = jnp.where(valid, idx, ends[-1] // MOE_TM - 1)
    te = jnp.sum((ends[None, :] <= (tile * MOE_TM)[:, None]).astype(jnp.int32), axis=1)
    first = valid & jnp.concatenate([jnp.ones((1,), bool), te[1:] != te[:-1]])
    used = counts > 0
    slot_of = (jnp.cumsum(used.astype(jnp.int32)) - 1) % 2
    later = (experts[None, :] > experts[:, None]) & used[None, :]
    next_of = jnp.min(jnp.where(later, experts[None, :], N_EXPERTS), axis=1)
    next_of = jnp.where(next_of == N_EXPERTS, -1, next_of)
    is_te = te[:, None] == experts[None, :]
    pick = lambda table: jnp.sum(jnp.where(is_te, table[None, :], 0), axis=1)
    sched = (tile, te, valid.astype(jnp.int32), first.astype(jnp.int32), pick(slot_of), pick(next_of))
    fill = jnp.concatenate([jnp.where(padded > counts, ends - MOE_TM, -1),
                            jnp.where(valid, -1, idx * MOE_TM)[eid.size // MOE_TM:]])
    return pos, tuple(a.astype(jnp.int32) for a in sched), fill.astype(jnp.int32)


def _dispatch_kernel(fill_ref, pos_ref, hp_ref, hs_ref, xs_hbm, zbuf, sem, fsem, *, n_p_tiles, n_fill):
    i = pl.program_id(0)

    def fill_copy(e):
        start = pl.multiple_of(fill_ref[e], MOE_TM)
        return pltpu.make_async_copy(zbuf, xs_hbm.at[pl.ds(start, MOE_TM)], fsem)

    @pl.when(i == 0)
    def _():
        zbuf[...] = jnp.zeros_like(zbuf)
        for e in range(n_fill):
            @pl.when(fill_ref[e] >= 0)
            def _(e=e):
                fill_copy(e).start()
        for e in range(n_fill):
            @pl.when(fill_ref[e] >= 0)
            def _(e=e):
                fill_copy(e).wait()

    def scatter(h_ref):
        for t in range(COMB_TB):
            for j in range(TOP_K):
                pltpu.make_async_copy(h_ref.at[pl.ds(t, 1)],
                                      xs_hbm.at[pl.ds(pos_ref[0, 0, j * COMB_TB + t], 1)], sem).start()

    @pl.when(i < n_p_tiles)
    def _():
        scatter(hp_ref)

    @pl.when(i >= n_p_tiles)
    def _():
        scatter(hs_ref)

    for _ in range(TOP_K):
        pltpu.make_async_copy(hp_ref, xs_hbm.at[pl.ds(0, COMB_TB)], sem).wait()


def _dispatch(pos, fill, h2p, h2s, n_tiles):
    d = D_MODEL
    n_p, n_s = h2p.shape[0], h2s.shape[0]
    n = (n_p + n_s) // COMB_TB
    n_pt = n_p // COMB_TB
    pos_t = pos.reshape(n, COMB_TB, TOP_K).transpose(0, 2, 1).reshape(n, 1, TOP_K * COMB_TB)
    grid_spec = pltpu.PrefetchScalarGridSpec(
        num_scalar_prefetch=1,
        grid=(n,),
        in_specs=[pl.BlockSpec((1, 1, TOP_K * COMB_TB), lambda i, fill: (i, 0, 0), memory_space=pltpu.SMEM),
                  pl.BlockSpec((COMB_TB, d), lambda i, fill: (jnp.minimum(i, n_pt - 1), 0)),
                  pl.BlockSpec((COMB_TB, d), lambda i, fill: (jnp.maximum(i - n_pt, 0), 0))],
        out_specs=pl.BlockSpec(memory_space=pl.ANY),
        scratch_shapes=[pltpu.VMEM((MOE_TM, d), F32),
                        pltpu.SemaphoreType.DMA(()),
                        pltpu.SemaphoreType.DMA(())])
    return pl.pallas_call(
        functools.partial(_dispatch_kernel, n_p_tiles=n_pt, n_fill=fill.shape[0]),
        grid_spec=grid_spec,
        out_shape=jax.ShapeDtypeStruct((n_tiles * MOE_TM, d), F32),
        compiler_params=pltpu.CompilerParams(
            dimension_semantics=("arbitrary",), vmem_limit_bytes=VMEM_LIMIT),
        name="moe_dispatch",
    )(fill, pos_t, h2p, h2s)


def _experts_kernel(tile_ref, te_ref, tv_ref, tf_ref, ws_ref, nx_ref, x_ref,
                    wgu_hbm, bgu_ref, wd_hbm, bd_ref, ys_ref, wgu_buf, wd_buf, wgu_scr, wd_scr, sem):
    i = pl.program_id(0)

    def weight_copies(e, slot):
        return (pltpu.make_async_copy(wgu_hbm.at[e], wgu_buf.at[slot], sem.at[0, slot]),
                pltpu.make_async_copy(wd_hbm.at[e], wd_buf.at[slot], sem.at[1, slot]))

    @pl.when(i == 0)
    def _():
        for cp in weight_copies(te_ref[0], 0):
            cp.start()

    @pl.when(tf_ref[i] == 1)
    def _():
        slot = ws_ref[i]

        @pl.when(nx_ref[i] >= 0)
        def _():
            for cp in weight_copies(nx_ref[i], 1 - slot):
                cp.start()

        for cp in weight_copies(te_ref[i], slot):
            cp.wait()
        wgu_scr[...] = wgu_buf[slot].astype(BF16)
        wd_scr[...] = wd_buf[slot].astype(BF16)

    @pl.when(tv_ref[i] == 0)
    def _():
        ys_ref[...] = jnp.zeros_like(ys_ref)

    @pl.when(tv_ref[i] == 1)
    def _():
        gu = jnp.dot(x_ref[...].astype(BF16), wgu_scr[...], preferred_element_type=F32) + bgu_ref[0]
        gate = jnp.minimum(gu[:, :D_MODEL], SWIGLU_LIMIT)
        up = jnp.clip(gu[:, D_MODEL:], -SWIGLU_LIMIT, SWIGLU_LIMIT)
        act = (up + 1.0) * gate * _sigmoid(SWIGLU_ALPHA * gate)
        ys_ref[...] = jnp.dot(act.astype(BF16), wd_scr[...], preferred_element_type=F32) + bd_ref[0]


def _experts(xs, sched, p, n_tiles):
    d = D_MODEL
    bias = lambda w: pl.BlockSpec((1, 1, w), lambda i, tile, te, *_: (te[i], 0, 0))
    grid_spec = pltpu.PrefetchScalarGridSpec(
        num_scalar_prefetch=len(sched),
        grid=(n_tiles,),
        in_specs=[pl.BlockSpec((MOE_TM, d), lambda i, tile, *_: (tile[i], 0)),
                  pl.BlockSpec(memory_space=pl.ANY), bias(2 * d),
                  pl.BlockSpec(memory_space=pl.ANY), bias(d)],
        out_specs=pl.BlockSpec((MOE_TM, d), lambda i, *_: (i, 0)),
        scratch_shapes=[pltpu.VMEM((2, d, 2 * d), F32),
                        pltpu.VMEM((2, d, d), F32),
                        pltpu.VMEM((d, 2 * d), BF16),
                        pltpu.VMEM((d, d), BF16),
                        pltpu.SemaphoreType.DMA((2, 2))])
    return pl.pallas_call(
        _experts_kernel,
        grid_spec=grid_spec,
        out_shape=jax.ShapeDtypeStruct((n_tiles * MOE_TM, d), F32),
        compiler_params=pltpu.CompilerParams(
            dimension_semantics=("arbitrary",), vmem_limit_bytes=VMEM_LIMIT),
        name="moe_experts",
    )(*sched, xs, p['w_gate_up'], p['b_gate_up'], p['w_down'], p['b_down'])


def _combine_kernel(pos_ref, posn_ref, ys_hbm, ew_ref, x1_ref, mg_ref, gpost_ref, o_ref, buf, sem, *, n):
    i = pl.program_id(0)
    slot = i % 2

    def row_copy(row, j, t, slot_):
        return pltpu.make_async_copy(ys_hbm.at[pl.ds(row, 1)], buf.at[slot_, j, pl.ds(t, 1)], sem.at[slot_])

    def gather(src_ref, slot_):
        for t in range(COMB_TB):
            for j in range(TOP_K):
                row_copy(src_ref[0, 0, j * COMB_TB + t], j, t, slot_).start()

    @pl.when(i == 0)
    def _():
        gather(pos_ref, 0)

    @pl.when(i + 1 < n)
    def _():
        gather(posn_ref, 1 - slot)

    for j in range(TOP_K):
        pltpu.make_async_copy(ys_hbm.at[pl.ds(0, COMB_TB)], buf.at[slot, j], sem.at[slot]).wait()
    ew = ew_ref[...]
    f = sum(ew[:, j:j + 1] * buf[slot, j] for j in range(TOP_K))
    o_ref[...] = x1_ref[...] + mg_ref[0] * _rms(f, gpost_ref[...])


def _combine(ys, pos, ew, x1, mod_gate, p):
    b, s, d = x1.shape
    n_tok = b * s
    n_tiles = n_tok // COMB_TB
    per_batch = mod_gate.shape[0] > 1
    if per_batch:
        mod_gate = jnp.repeat(mod_gate, s // COMB_TB, axis=0)
    mod_map = (lambda i: (i, 0, 0)) if per_batch else (lambda i: (0, 0, 0))
    pos_t = pos.reshape(n_tiles, COMB_TB, TOP_K).transpose(0, 2, 1).reshape(n_tiles, 1, TOP_K * COMB_TB)
    smem_tile = lambda f: pl.BlockSpec((1, 1, TOP_K * COMB_TB), f, memory_space=pltpu.SMEM)
    out = pl.pallas_call(
        functools.partial(_combine_kernel, n=n_tiles),
        grid=(n_tiles,),
        in_specs=[smem_tile(lambda i: (i, 0, 0)),
                  smem_tile(lambda i: (jnp.minimum(i + 1, n_tiles - 1), 0, 0)),
                  pl.BlockSpec(memory_space=pl.ANY),
                  pl.BlockSpec((COMB_TB, LANES), lambda i: (i, 0)),
                  pl.BlockSpec((COMB_TB, d), lambda i: (i, 0)),
                  pl.BlockSpec((1, 1, d), mod_map),
                  pl.BlockSpec((1, d), lambda i: (0, 0))],
        out_specs=pl.BlockSpec((COMB_TB, d), lambda i: (i, 0)),
        out_shape=jax.ShapeDtypeStruct((n_tok, d), F32),
        scratch_shapes=[pltpu.VMEM((2, TOP_K, COMB_TB, d), F32),
                        pltpu.SemaphoreType.DMA((2,))],
        compiler_params=pltpu.CompilerParams(
            dimension_semantics=("arbitrary",), vmem_limit_bytes=VMEM_LIMIT),
        name="moe_combine",
    )(pos_t, pos_t, ys, ew.reshape(n_tok, LANES), x1.reshape(n_tok, d), mod_gate, p['g_post_ffn'])
    return out.reshape(b, s, d)


def _rope_tables(n_tokens):
    rows = n_tokens // GRID_W
    row = jnp.repeat(jnp.arange(rows, dtype=F32), GRID_W)
    col = jnp.tile(jnp.arange(GRID_W, dtype=F32), rows)
    inv = ROPE_THETA ** (-jnp.arange(ROPE_FREQS, dtype=F32) / ROPE_FREQS)
    ang_r = row[:, None] * inv[None, :]
    ang_c = col[:, None] * inv[None, :]
    ang = jnp.concatenate([ang_r, ang_r, ang_c, ang_c] * 2, axis=-1)
    return jnp.cos(ang), jnp.sin(ang)


def _stream(x, mods, proj_w, lam, p, rope, ctx, s0, cnt0, proj_dtype, tm_proj, tm_tok):
    b, s, d = x.shape
    xf = x if mods[0].shape[0] > 1 else x.reshape(1, b * s, d)
    proj = _norm_proj(xf, p['g_pre_mix'], mods[1], mods[0], proj_w, tm_proj, PROJ_TN, proj_dtype)
    proj = proj.reshape(b, s, C_IN)
    oa = _diff_attention(proj, lam, p['g_subln'], rope, ctx, tq=min(s, 4 * ATT_SUB),
                         hps=H_A if s <= ATT_SUB else 1)
    r, v, kkn, kd, ba, lw, bonus, gate = _rwkv_prep(proj, p, min(s, 256))
    y0, y1, sfin = _rwkv_scan(r, v, kkn, kd, ba, lw, s0)
    tb, tsq = xf.shape[0], xf.shape[1]
    x1, h2, route, rank, ew, cnt = _merge_out(
        *(a.reshape(tb, tsq, a.shape[-1]) for a in (x, oa, y0, y1, bonus, gate, proj)),
        (mods[2], mods[4], mods[3]), p, cnt0, tm_tok)
    n = b * s
    route = (route.reshape(n, LANES)[:, :TOP_K], rank.reshape(n, LANES)[:, :N_EXPERTS])
    return (x1.reshape(b, s, d), h2.reshape(n, d), route, ew.reshape(n, LANES), cnt,
            proj, sfin)


def kernel(x_prompt, x_sample, cache_k, cache_v, state_rwkv, c, c_ctx, w_ada, b_ada, g_pre_mix, g_post_mix, g_pre_ffn, g_post_ffn, w_in, mu_prev, mu_next, lam, g_subln, k_k, k_a, r_k, w0, w_up, a0, a_up, g_up, gn_w, gn_b, w_out, w_router, b_router, w_gate_up, b_gate_up, w_down, b_down):
    l = 0
    d = D_MODEL
    bp, sp, _ = x_prompt.shape
    bs, ss, _ = x_sample.shape

    n_cond = 1 + bs
    rows = -(-n_cond // SUBLANES) * SUBLANES
    cond = jnp.concatenate([c_ctx[None, :], c, jnp.zeros((rows - n_cond, d), F32)], axis=0)
    mod = _modulation(cond, w_ada[l], b_ada[l][None, :])
    mods_p = [mod[0:1, i * d:(i + 1) * d].reshape(1, 1, d) for i in range(N_MOD)]
    mods_s = [mod[1:1 + bs, i * d:(i + 1) * d].reshape(bs, 1, d) for i in range(N_MOD)]

    w_proj = w_in[l].astype(BF16)

    lq = lam[l]
    lam_val = (jnp.exp(jnp.sum(lq[0] * lq[1])) - jnp.exp(jnp.sum(lq[2] * lq[3])) + LAM_INIT).reshape(1)

    head = jnp.arange(d) // N_B
    wr = jnp.concatenate([w_router[l], jnp.zeros((d, LANES - N_EXPERTS), F32)], axis=1)
    br = jnp.concatenate([b_router[l], jnp.full((LANES - N_EXPERTS,), -jnp.inf, F32)])[None, :]
    mup, mun = mu_prev[l][None, :], mu_next[l][None, :]
    p = {
        'g_pre_mix': g_pre_mix[l][None, :], 'g_post_mix': g_post_mix[l][None, :],
        'g_pre_ffn': g_pre_ffn[l][None, :], 'g_post_ffn': g_post_ffn[l][None, :],
        'g_subln': g_subln[l][None, :],
        'mu_prev_main': mup[:, :3 * d], 'mu_next_main': mun[:, :3 * d],
        'mu_prev_lora': mup[:, 3 * d:], 'mu_next_lora': mun[:, 3 * d:],
        'k_k': k_k[l][None, :], 'k_a': k_a[l][None, :], 'r_k': r_k[l].reshape(1, d),
        'w0': w0[l].reshape(1, 2 * d),
        'w_up': jnp.concatenate([w_up[l, 0], w_up[l, 1]], axis=1).astype(BF16),
        'a0': a0[l].reshape(1, 2 * d),
        'a_up': jnp.concatenate([a_up[l, 0], a_up[l, 1]], axis=1).astype(BF16),
        'g_up': g_up[l].astype(BF16),
        'head_ind': (head[:, None] == jnp.arange(LANES)[None, :]).astype(BF16),
        'head_ind_t': (jnp.arange(LANES)[:, None] == head[None, :]).astype(BF16),
        'gn_w': gn_w[l][None, :], 'gn_b': gn_b[l][None, :],
        'w_out': w_out[l].astype(BF16),
        'w_router': wr, 'b_router': br,
        'w_gate_up': w_gate_up[l], 'b_gate_up': b_gate_up[l][:, None, :],
        'w_down': w_down[l], 'b_down': b_down[l][:, None, :],
    }

    rope = _rope_tables(ss)
    ctx = (cache_k[:, l].reshape(bs, -1, d), cache_v[:, l].reshape(bs, -1, d))

    cnt0 = jnp.zeros((1, LANES), F32)
    x1p, h2p, route_p, ewp, cnt_p, proj_p, sfin = _stream(
        x_prompt, mods_p, w_proj, lam_val, p, None, None, None, cnt0, F32, 2048, 512)
    x1s, h2s, route_s, ews, cnt_s, _, _ = _stream(
        x_sample, mods_s, w_proj, lam_val, p, rope, ctx, state_rwkv[:, l], cnt_p, BF16, 2048, 512)

    n_p, n_s = bp * sp, bs * ss
    n_tok = n_p + n_s
    eid, rank = (jnp.concatenate([a, b_], axis=0) for a, b_ in zip(route_p, route_s))
    counts = cnt_s[0, :N_EXPERTS].astype(jnp.int32)
    n_tiles = -(-(n_tok * TOP_K + N_EXPERTS * (MOE_TM - 1)) // MOE_TM)
    pos, sched, fill = _route_metadata(eid, rank, counts, n_tiles)
    xs = _dispatch(pos, fill, h2p, h2s, n_tiles)
    rows = _experts(xs, sched, p, n_tiles)
    yp = _combine(rows, pos[:n_p], ewp, x1p, mods_p[5], p)
    ys = _combine(rows, pos[n_p:], ews, x1s, mods_s[5], p)

    new_k = proj_p[:, :, COL_K:COL_K + d].reshape(bp, 1, sp, H_A, 2, HD_A)
    new_v = proj_p[:, :, COL_V:COL_V + d].reshape(bp, 1, sp, H_A, DV_A)
    return (yp, ys, new_k, new_v, sfin[:, None])
```

```python
import functools
import math

import jax
import jax.numpy as jnp
from jax import lax
from jax.experimental import pallas as pl
from jax.experimental.pallas import tpu as pltpu

F32 = jnp.float32
BF16 = jnp.bfloat16

D_MODEL = 1024
GRID_W = 64
HD_A = 64
DV_A = 2 * HD_A
H_A = D_MODEL // DV_A
N_B = 64
H_B = D_MODEL // N_B
LORA_W = 64
LORA_A = 64
LORA_G = 128
N_EXPERTS = 32
TOP_K = 4
SWIGLU_LIMIT = 7.0
SWIGLU_ALPHA = 1.702
ROPE_THETA = 10000.0
ROPE_FREQS = HD_A // 4
NORM_EPS = 1e-6
GN_EPS = 64e-5
L2_EPS = 1e-12
ATTN_SCALE = HD_A ** -0.5
N_MOD = 6
LAM_INIT = 0.8 - 0.6 * math.exp(-0.3 * 0)
DECAY_SCALE = -math.exp(-0.5)

LANES = 128
SUBLANES = 8
CHUNK = 64
SCAN_NB = 1
MOE_TM = 512
PROJ_TN = 1408
COMB_TB = 256
ATT_SUB = 256
LOG2_E = math.log2(math.e)
VMEM_LIMIT = 56 * 1024 * 1024

C_LORA = LORA_W + LORA_A + LORA_G
COL_Q, COL_K, COL_V, COL_R, COL_KR, COL_VR, COL_LORA = (i * D_MODEL for i in range(7))
COL_G0 = COL_LORA + C_LORA
COL_G1 = COL_G0 + D_MODEL
C_IN = COL_G1 + D_MODEL
GATE_W = 256


def _sigmoid(x):
    return 1.0 / (1.0 + jnp.exp(-x))


def _dot(a, b):
    return jnp.dot(a.astype(BF16), b.astype(BF16), preferred_element_type=F32)


def _dot_nt(a, b):
    return lax.dot_general(a.astype(BF16), b.astype(BF16), (((1,), (1,)), ((), ())),
                           preferred_element_type=F32)


def _dot_tn(a, b):
    return lax.dot_general(a.astype(BF16), b.astype(BF16), (((0,), (0,)), ((), ())),
                           preferred_element_type=F32)


def _split3(x):
    hi = x.astype(BF16)
    r1 = x - hi.astype(F32)
    mid = r1.astype(BF16)
    lo = (r1 - mid.astype(F32)).astype(BF16)
    return hi, mid, lo


def _headsum(x, ind):
    e, et = ind
    s = jnp.dot(x.astype(BF16), e, preferred_element_type=F32)
    s_hi = s.astype(BF16)
    s_lo = (s - s_hi.astype(F32)).astype(BF16)
    return jnp.dot(s_hi, et, preferred_element_type=F32) + jnp.dot(s_lo, et, preferred_element_type=F32)


def _rms(x, g):
    return x * lax.rsqrt(jnp.mean(x * x, axis=-1, keepdims=True) + NORM_EPS) * g


def _mod_kernel(c_ref, w_ref, b_ref, o_ref):
    c = c_ref[...]
    s = c * _sigmoid(c)
    o_ref[...] = _dot(s, w_ref[...]) + b_ref[...]


def _modulation(cond, w_ada, b_ada):
    rows, d = cond.shape
    n = w_ada.shape[1]
    tn = 768
    return pl.pallas_call(
        _mod_kernel,
        grid=(n // tn,),
        in_specs=[pl.BlockSpec((rows, d), lambda j: (0, 0)),
                  pl.BlockSpec((d, tn), lambda j: (0, j)),
                  pl.BlockSpec((1, tn), lambda j: (0, j))],
        out_specs=pl.BlockSpec((rows, tn), lambda j: (0, j)),
        out_shape=jax.ShapeDtypeStruct((rows, n), F32),
        name="modulation",
    )(cond, w_ada, b_ada)


def _proj_kernel(x_ref, g_ref, sc_ref, sh_ref, w_ref, o_ref, h_scr):
    @pl.when(pl.program_id(2) == 0)
    def _():
        h = _rms(x_ref[0], g_ref[...]) * (1.0 + sc_ref[0]) + sh_ref[0]
        h_scr[...] = h.astype(BF16)

    o_ref[0] = jnp.dot(h_scr[...], w_ref[...], preferred_element_type=F32).astype(o_ref.dtype)


def _norm_proj(x, g, scale, shift, w, tm, tn, out_dtype):
    b, s, d = x.shape
    n = w.shape[1]
    per_batch = scale.shape[0] > 1
    mod_map = (lambda bi, i, j: (bi, 0, 0)) if per_batch else (lambda bi, i, j: (0, 0, 0))
    return pl.pallas_call(
        _proj_kernel,
        grid=(b, s // tm, n // tn),
        in_specs=[pl.BlockSpec((1, tm, d), lambda bi, i, j: (bi, i, 0)),
                  pl.BlockSpec((1, d), lambda bi, i, j: (0, 0)),
                  pl.BlockSpec((1, 1, d), mod_map),
                  pl.BlockSpec((1, 1, d), mod_map),
                  pl.BlockSpec((d, tn), lambda bi, i, j: (0, j))],
        out_specs=pl.BlockSpec((1, tm, tn), lambda bi, i, j: (bi, i, j)),
        out_shape=jax.ShapeDtypeStruct((b, s, n), out_dtype),
        scratch_shapes=[pltpu.VMEM((tm, d), BF16)],
        compiler_params=pltpu.CompilerParams(
            dimension_semantics=("arbitrary", "arbitrary", "arbitrary"),
            vmem_limit_bytes=VMEM_LIMIT),
        name="norm_proj",
    )(x, g, scale, shift, w)


def _rope(x, cos, sin):
    lane = lax.broadcasted_iota(jnp.int32, x.shape, 1)
    even = (lane // ROPE_FREQS) % 2 == 0
    rot = jnp.where(even, -pltpu.roll(x, LANES - ROPE_FREQS, 1), pltpu.roll(x, ROPE_FREQS, 1))
    return x * cos + rot * sin


def _attn_kernel(*refs, use_rope, use_ctx, s_new):
    it = iter(refs)
    lam_ref = next(it)
    q_ref, k_ref, v_ref = next(it), next(it), next(it)
    if use_rope:
        cq_ref, sq_ref, ck_ref, sk_ref = next(it), next(it), next(it), next(it)
    if use_ctx:
        ctxk_ref, ctxv_ref = next(it), next(it)
    g_ref = next(it)
    o_ref = next(it)
    k_scr, vt_scr = next(it), next(it)

    hps = q_ref.shape[-1] // DV_A
    head_lanes = [slice(hh * DV_A, (hh + 1) * DV_A) for hh in range(hps)]

    @pl.when(pl.program_id(2) == 0)
    def _():
        for hh, hl in enumerate(head_lanes):
            k = k_ref[0][:, hl].astype(F32)
            if use_rope:
                k = _rope(k, ck_ref[...], sk_ref[...])
            k_scr[hh, 0:s_new, :] = k.astype(BF16)
            vt_scr[hh, :, 0:s_new] = v_ref[0][:, hl].astype(F32).T.astype(BF16)
            if use_ctx:
                k_scr[hh, s_new:, :] = ctxk_ref[0][:, hl].astype(BF16)
                vt_scr[hh, :, s_new:] = ctxv_ref[0][:, hl].T.astype(BF16)

    lam = lam_ref[0]
    n_sub = q_ref.shape[1] // ATT_SUB
    lane = lax.broadcasted_iota(jnp.int32, (ATT_SUB, DV_A), 1)
    sts = []
    for hh, hl in enumerate(head_lanes):
        q = q_ref[0][:, hl].astype(F32)
        if use_rope:
            q = _rope(q, cq_ref[...], sq_ref[...])
        q = q * (ATTN_SCALE * LOG2_E)
        for t in range(n_sub):
            for m in range(2):
                in_map = (lane >= HD_A) if m else (lane < HD_A)
                qm = jnp.where(in_map, q[t * ATT_SUB:(t + 1) * ATT_SUB], 0.0)
                sts.append((hh, _dot_nt(k_scr[hh], qm)))
    outs = []
    for hh, st in sts:
        et = jnp.exp2(st - jnp.max(st, axis=0, keepdims=True))
        l = jnp.sum(et, axis=0, keepdims=True)
        ot = jnp.dot(vt_scr[hh], et.astype(BF16), preferred_element_type=F32)
        outs.append(ot / l)
    for hh, hl in enumerate(head_lanes):
        for t in range(n_sub):
            u = 2 * (hh * n_sub + t)
            o = (outs[u] - lam * outs[u + 1]).T
            o = o * lax.rsqrt(jnp.mean(o * o, axis=-1, keepdims=True) + NORM_EPS)
            o_ref[0, t * ATT_SUB:(t + 1) * ATT_SUB, hl] = o * g_ref[:, hl] * (1.0 - LAM_INIT)


def _diff_attention(proj, lam, g_subln, rope=None, ctx=None, tq=256, hps=1):
    b, s, _ = proj.shape
    use_rope, use_ctx = rope is not None, ctx is not None
    s_tot = s + (ctx[0].shape[1] if use_ctx else 0)
    w = hps * DV_A
    in_specs = [pl.BlockSpec(memory_space=pltpu.SMEM),
                pl.BlockSpec((1, tq, w), lambda bi, h, i: (bi, i, COL_Q // w + h)),
                pl.BlockSpec((1, s, w), lambda bi, h, i: (bi, 0, COL_K // w + h)),
                pl.BlockSpec((1, s, w), lambda bi, h, i: (bi, 0, COL_V // w + h))]
    args = [lam, proj, proj, proj]
    if use_rope:
        cos, sin = rope
        in_specs += [pl.BlockSpec((tq, DV_A), lambda bi, h, i: (i, 0)),
                     pl.BlockSpec((tq, DV_A), lambda bi, h, i: (i, 0)),
                     pl.BlockSpec((s, DV_A), lambda bi, h, i: (0, 0)),
                     pl.BlockSpec((s, DV_A), lambda bi, h, i: (0, 0))]
        args += [cos, sin, cos, sin]
    if use_ctx:
        p = ctx[0].shape[1]
        in_specs += [pl.BlockSpec((1, p, w), lambda bi, h, i: (bi, 0, h)),
                     pl.BlockSpec((1, p, w), lambda bi, h, i: (bi, 0, h))]
        args += [ctx[0], ctx[1]]
    in_specs.append(pl.BlockSpec((1, w), lambda bi, h, i: (0, h)))
    args.append(g_subln)
    return pl.pallas_call(
        functools.partial(_attn_kernel, use_rope=use_rope, use_ctx=use_ctx, s_new=s),
        grid=(b, H_A // hps, s // tq),
        in_specs=in_specs,
        out_specs=pl.BlockSpec((1, tq, w), lambda bi, h, i: (bi, i, h)),
        out_shape=jax.ShapeDtypeStruct((b, s, D_MODEL), F32),
        scratch_shapes=[pltpu.VMEM((hps, s_tot, DV_A), BF16), pltpu.VMEM((hps, DV_A, s_tot), BF16)],
        compiler_params=pltpu.CompilerParams(
            dimension_semantics=("arbitrary", "arbitrary", "arbitrary"),
            vmem_limit_bytes=VMEM_LIMIT),
        name="diff_attention",
    )(*args)


def _shifted(x_ref, p_ref, n_ref, mup, mun, first, last):
    x = x_ref[0].astype(F32)
    ts = x.shape[0]
    halo = p_ref.shape[1]
    row = lax.broadcasted_iota(jnp.int32, x.shape, 0)
    prev_row = p_ref[0][halo - 1:halo, :].astype(F32) * first
    next_row = n_ref[0][0:1, :].astype(F32) * last
    prev = jnp.where(row == 0, prev_row, pltpu.roll(x, 1, 0))
    nxt = jnp.where(row == ts - 1, next_row, pltpu.roll(x, ts - 1, 0))
    return x + mup * (prev - x) + mun * (nxt - x)


def _prep_kernel(r_ref, rp_ref, rn_ref, k_ref, kp_ref, kn_ref, v_ref, vp_ref, vn_ref,
                 l_ref, lp_ref, ln_ref, mup_ref, mun_ref, mupl_ref, munl_ref,
                 kk_ref, ka_ref, rk_ref, w0_ref, wup_ref, a0_ref, aup_ref, gup_ref, e_ref, et_ref,
                 ro_ref, vo_ref, kko_ref, kd_ref, ba_ref, lw_ref, bonus_ref, gate_ref, *, n_tiles):
    i = pl.program_id(1)
    first = (i > 0).astype(F32)
    last = (i < n_tiles - 1).astype(F32)
    mup, mun = mup_ref[...], mun_ref[...]
    d = D_MODEL
    r = _shifted(r_ref, rp_ref, rn_ref, mup[:, 0:d], mun[:, 0:d], first, last)
    k = _shifted(k_ref, kp_ref, kn_ref, mup[:, d:2 * d], mun[:, d:2 * d], first, last)
    v = _shifted(v_ref, vp_ref, vn_ref, mup[:, 2 * d:3 * d], mun[:, 2 * d:3 * d], first, last)
    lo = _shifted(l_ref, lp_ref, ln_ref, mupl_ref[...], munl_ref[...], first, last)
    xw = lo[:, 0:LORA_W]
    xa = lo[:, LORA_W:LORA_W + LORA_A]
    xg = lo[:, LORA_W + LORA_A:]
    g = (e_ref[...], et_ref[...])

    kk = k * kk_ref[...]
    nrm = jnp.sqrt(_headsum(kk * kk, g))
    kkn = kk / jnp.maximum(nrm, L2_EPS)
    ro_ref[0] = r
    vo_ref[0] = v
    kko_ref[0] = kkn
    gate_ref[0] = _dot(_sigmoid(xg), gup_ref[...])

    wlog = w0_ref[...] + _dot(jnp.tanh(xw), wup_ref[...])
    alog = a0_ref[...] + _dot(xa, aup_ref[...])
    ka = ka_ref[...]
    rrk = r * rk_ref[...]
    dots = None
    for dr in range(2):
        sl = slice(dr * d, (dr + 1) * d)
        lw_ref[dr, 0] = DECAY_SCALE * _sigmoid(wlog[:, sl])
        a = _sigmoid(alog[:, sl])
        kd = k * (1.0 + (a - 1.0) * ka)
        kd_ref[dr, 0] = kd
        ba_ref[dr, 0] = kkn * a
        t = rrk * kd
        dots = t if dots is None else dots + t
    bonus_ref[0] = _headsum(dots, g) * v


def _rwkv_prep(proj, p, ts):
    b, s, _ = proj.shape
    d = D_MODEL
    nt = s // ts
    halo = SUBLANES * (4 // proj.dtype.itemsize)
    hb = ts // halo
    nhb = s // halo

    def main(col, w):
        return pl.BlockSpec((1, ts, w), lambda bi, i: (bi, i, col // w))

    def prev(col, w):
        return pl.BlockSpec((1, halo, w), lambda bi, i: (bi, jnp.maximum(i * hb - 1, 0), col // w))

    def nxt(col, w):
        return pl.BlockSpec((1, halo, w),
                            lambda bi, i: (bi, jnp.minimum((i + 1) * hb, nhb - 1), col // w))

    def full(a):
        return pl.BlockSpec(a.shape, lambda bi, i: (0,) * a.ndim)

    in_specs, args = [], []
    for col, w in ((COL_R, d), (COL_KR, d), (COL_VR, d), (COL_LORA, C_LORA)):
        in_specs += [main(col, w), prev(col, w), nxt(col, w)]
        args += [proj, proj, proj]
    consts = [p['mu_prev_main'], p['mu_next_main'], p['mu_prev_lora'], p['mu_next_lora'],
              p['k_k'], p['k_a'], p['r_k'], p['w0'], p['w_up'], p['a0'], p['a_up'], p['g_up'],
              p['head_ind'], p['head_ind_t']]
    in_specs += [full(a) for a in consts]
    args += consts
    tok = pl.BlockSpec((1, ts, d), lambda bi, i: (bi, i, 0))
    tok2 = pl.BlockSpec((2, 1, ts, d), lambda bi, i: (0, bi, i, 0))
    one = jax.ShapeDtypeStruct((b, s, d), F32)
    two = jax.ShapeDtypeStruct((2, b, s, d), F32)
    return pl.pallas_call(
        functools.partial(_prep_kernel, n_tiles=nt),
        grid=(b, nt),
        in_specs=in_specs,
        out_specs=[tok, tok, tok, tok2, tok2, tok2, tok, tok],
        out_shape=[one, one, one, two, two, two, one, one],
        compiler_params=pltpu.CompilerParams(
            dimension_semantics=("arbitrary", "arbitrary"),
            vmem_limit_bytes=VMEM_LIMIT),
        name="rwkv_prep",
    )(*args)


def _scan_masks(reverse):
    n = CHUNK
    tt = lax.broadcasted_iota(jnp.int32, (n, n), 0)
    ss = lax.broadcasted_iota(jnp.int32, (n, n), 1)
    hi, lo = (ss, tt) if reverse else (tt, ss)
    tt2 = lax.broadcasted_iota(jnp.int32, (n, 2 * n), 0)
    col2 = lax.broadcasted_iota(jnp.int32, (n, 2 * n), 1)
    ss2 = col2 % n
    hi2, lo2 = (ss2, tt2) if reverse else (tt2, ss2)
    levels = []
    blk = 2
    while blk < n:
        levels.append((hi // (2 * blk) == lo // (2 * blk)) & ((hi // blk) % 2 == 1) & ((lo // blk) % 2 == 0))
        blk *= 2
    return dict(strict=hi > lo, incl=hi >= lo, eye=(tt == ss).astype(F32),
                first=(hi // 2 == lo // 2) & (hi > lo), levels=levels,
                strict_r=(hi2 > lo2) & (col2 >= n),
                incl2=hi2 >= lo2)


def _scan_operands(reverse, incl, lw, r, v, kkn, kd, ba):
    n = CHUNK
    tri = jnp.where(incl, 1.0, 0.0).astype(BF16)
    cum = sum(jnp.dot(tri, part, preferred_element_type=F32) for part in _split3(lw))
    tot = cum[0:1, :] if reverse else cum[n - 1:n, :]
    g_inv = jnp.exp(-cum)
    g_rest = jnp.exp(tot - cum)
    return dict(a=(kkn * jnp.exp(cum - lw)).astype(BF16), r=(r * jnp.exp(cum)).astype(BF16),
                b=(ba * g_inv).astype(BF16), k=(kd * g_inv).astype(BF16),
                bh=(ba * g_rest).astype(BF16), kh=(kd * g_rest).astype(BF16),
                v=v.astype(BF16), g_tot=jnp.exp(tot))


def _scan_kernel(*refs, has_s0, n_chunks):
    it = iter(refs)
    tok_refs = [[next(it) for _ in range(3)] for _ in range(2)]
    dir_refs = [[next(it) for _ in range(3)] for _ in range(2)]
    s0_ref = next(it) if has_s0 else None
    y_refs = [next(it), next(it)]
    sf_ref = next(it)
    s_scr = next(it)
    c = pl.program_id(1)

    @pl.when(c == 0)
    def _():
        if has_s0:
            s_scr[...] = s0_ref[...]
        else:
            s_scr[...] = jnp.zeros_like(s_scr)

    n = CHUNK
    nb = s_scr.shape[0]
    masks = [_scan_masks(dr == 1) for dr in range(2)]
    ops = {}
    for bb in range(nb):
        for dr in range(2):
            r_ref, v_ref, kk_ref = tok_refs[dr]
            kd_ref, ba_ref, lw_ref = dir_refs[dr]
            ops[bb, dr] = _scan_operands(dr == 1, masks[dr]['incl'], lw_ref[0, bb], r_ref[bb], v_ref[bb],
                                         kk_ref[bb], kd_ref[0, bb], ba_ref[0, bb])

    units = [(bb, dr, h) for bb in range(nb) for dr in range(2) for h in range(H_B)]
    idx = range(len(units))
    sls = [slice(h * N_B, (h + 1) * N_B) for _, _, h in units]
    op = lambda u, name: ops[units[u][:2]][name][:, sls[u]]
    mk = lambda u, name: masks[units[u][1]][name]
    s_old = [s_scr[bb, dr, h] for bb, dr, h in units]
    ar = [jnp.concatenate([op(u, 'a'), op(u, 'r')], axis=0) for u in idx]
    bk = [jnp.concatenate([op(u, 'b'), op(u, 'k')], axis=0) for u in idx]
    bkh = [jnp.concatenate([op(u, 'bh'), op(u, 'kh')], axis=0) for u in idx]
    vh = [op(u, 'v') for u in idx]

    gram = [_dot_nt(ar[u], bk[u]) for u in idx]
    ars = [_dot_nt(ar[u], s_old[u]) for u in idx]
    nmat = [jnp.where(mk(u, 'strict'), gram[u][0:n, 0:n], 0.0) for u in idx]
    mak = [jnp.where(mk(u, 'strict_r'), gram[u][0:n, :], 0.0) for u in idx]
    pr = [jnp.where(mk(u, 'incl2'), gram[u][n:, :], 0.0) for u in idx]
    z = [ars[u][0:n] + _dot(mak[u], jnp.concatenate([vh[u], vh[u]], axis=0)) for u in idx]
    x = [mk(u, 'eye') - jnp.where(mk(u, 'first'), nmat[u], 0.0) for u in idx]
    for lvl in range(len(masks[0]['levels'])):
        xl = [_dot(x[u], jnp.where(mk(u, 'levels')[lvl], nmat[u], 0.0)) for u in idx]
        x = [x[u] - _dot(xl[u], x[u]) for u in idx]
    uu = [-_dot(x[u], z[u]) for u in idx]
    uv = [jnp.concatenate([uu[u].astype(BF16), vh[u]], axis=0) for u in idx]
    y = [ars[u][n:] + _dot(pr[u], uv[u]) for u in idx]
    s_new = [s_old[u] * op(u, 'g_tot') + _dot_tn(uv[u], bkh[u]) for u in idx]
    for u, (bb, dr, h) in enumerate(units):
        y_refs[dr][bb, :, sls[u]] = y[u]
    for u, (bb, dr, h) in enumerate(units):
        s_scr[bb, dr, h] = s_new[u]

    @pl.when(c == n_chunks - 1)
    def _():
        sf_ref[...] = s_scr[...]


def _rwkv_scan(r, v, kkn, kd, ba, lw, s0):
    b, s, d = r.shape
    nc = s // CHUNK
    nb = SCAN_NB
    fwd = pl.BlockSpec((nb, CHUNK, d), lambda bi, c: (bi, c, 0))
    bwd = pl.BlockSpec((nb, CHUNK, d), lambda bi, c: (bi, nc - 1 - c, 0))
    fwd2 = pl.BlockSpec((1, nb, CHUNK, d), lambda bi, c: (0, bi, c, 0))
    bwd2 = pl.BlockSpec((1, nb, CHUNK, d), lambda bi, c: (1, bi, nc - 1 - c, 0))
    state = pl.BlockSpec((nb, 2, H_B, N_B, N_B), lambda bi, c: (bi, 0, 0, 0, 0))
    in_specs = [fwd] * 3 + [bwd] * 3 + [fwd2] * 3 + [bwd2] * 3
    args = [r, v, kkn, r, v, kkn, kd, ba, lw, kd, ba, lw]
    if s0 is not None:
        in_specs.append(state)
        args.append(s0)
    return pl.pallas_call(
        functools.partial(_scan_kernel, has_s0=s0 is not None, n_chunks=nc),
        grid=(b // nb, nc),
        in_specs=in_specs,
        out_specs=[fwd, bwd, state],
        out_shape=[jax.ShapeDtypeStruct((b, s, d), F32),
                   jax.ShapeDtypeStruct((b, s, d), F32),
                   jax.ShapeDtypeStruct((b, 2, H_B, N_B, N_B), F32)],
        scratch_shapes=[pltpu.VMEM((nb, 2, H_B, N_B, N_B), F32)],
        compiler_params=pltpu.CompilerParams(
            dimension_semantics=("arbitrary", "arbitrary"),
            vmem_limit_bytes=VMEM_LIMIT),
        name="rwkv_scan",
    )(*args)


def _merge_kernel(x_ref, oa_ref, y0_ref, y1_ref, bonus_ref, gate_ref,
                  g0a_ref, g0b_ref, g0c_ref, g0d_ref, g1a_ref, g1b_ref, g1c_ref, g1d_ref,
                  mg_ref, sc_ref, sh_ref, gnw_ref, gnb_ref, e_ref, et_ref, wout_ref,
                  gpost_ref, gpre_ref, wr_ref, br_ref, cnt0_ref,
                  x1_ref, h2_ref, route_ref, rank_ref, ew_ref, cnt_ref, cnt_scr):
    @pl.when((pl.program_id(0) == 0) & (pl.program_id(1) == 0))
    def _():
        cnt_scr[...] = cnt0_ref[...]

    g = (e_ref[...], et_ref[...])
    y = y0_ref[0] + y1_ref[0]
    mu = _headsum(y, g) * (1.0 / N_B)
    yc = y - mu
    var = _headsum(yc * yc, g) * (1.0 / N_B)
    yn = yc * lax.rsqrt(var + GN_EPS) * gnw_ref[...] + gnb_ref[...]
    ob = (yn + bonus_ref[0]) * gate_ref[0]
    g0 = jnp.concatenate([r[0].astype(F32) for r in (g0a_ref, g0b_ref, g0c_ref, g0d_ref)], axis=1)
    g1 = jnp.concatenate([r[0].astype(F32) for r in (g1a_ref, g1b_ref, g1c_ref, g1d_ref)], axis=1)
    merged = _sigmoid(g0) * oa_ref[0] + _sigmoid(g1) * ob
    out = _dot(merged, wout_ref[...])
    x1 = x_ref[0] + mg_ref[0] * _rms(out, gpost_ref[...])
    x1_ref[0] = x1
    h2 = _rms(x1, gpre_ref[...]) * (1.0 + sc_ref[0]) + sh_ref[0]
    h2_ref[0] = h2

    h_hi = h2.astype(BF16)
    h_lo = (h2 - h_hi.astype(F32)).astype(BF16)
    wr = wr_ref[...]
    w_hi = wr.astype(BF16)
    w_lo = (wr - w_hi.astype(F32)).astype(BF16)
    logits = (jnp.dot(h_hi, w_hi, preferred_element_type=F32) + jnp.dot(h_hi, w_lo, preferred_element_type=F32)
              + jnp.dot(h_lo, w_hi, preferred_element_type=F32)) + br_ref[...]
    lane = lax.broadcasted_iota(jnp.int32, logits.shape, 1)
    work = logits
    top = None
    picks = []
    for _ in range(TOP_K):
        mx = jnp.max(work, axis=-1, keepdims=True)
        idx = jnp.min(jnp.where(work == mx, lane, LANES), axis=-1, keepdims=True)
        if top is None:
            top = mx
        picks.append((idx, jnp.exp(mx - top)))
        work = jnp.where(lane == idx, -jnp.inf, work)
    denom = sum(e for _, e in picks)
    tm = logits.shape[0]
    onehot = jnp.zeros_like(logits)
    for idx, _ in picks:
        onehot = onehot + jnp.where(lane == idx, 1.0, 0.0)
    rr = lax.broadcasted_iota(jnp.int32, (tm, tm), 0)
    cc = lax.broadcasted_iota(jnp.int32, (tm, tm), 1)
    before = cnt_scr[...] + _dot(jnp.where(rr > cc, 1.0, 0.0), onehot)
    route = jnp.zeros(logits.shape, jnp.int32)
    ew = jnp.zeros_like(logits)
    for j, (idx, e) in enumerate(picks):
        route = jnp.where(lane == j, idx, route)
        ew = jnp.where(lane == j, e / denom, ew)
    route_ref[0] = route
    rank_ref[0] = jnp.where(onehot > 0.0, before, 0.0).astype(jnp.int32)
    ew_ref[0] = ew
    cnt_scr[...] = cnt_scr[...] + jnp.sum(onehot, axis=0, keepdims=True)
    cnt_ref[...] = cnt_scr[...]


def _merge_out(x, oa, y0, y1, bonus, gate, proj, mods, p, cnt0, tm):
    b, s, d = x.shape
    per_batch = mods[0].shape[0] > 1
    mod_map = (lambda bi, i: (bi, 0, 0)) if per_batch else (lambda bi, i: (0, 0, 0))
    tok = pl.BlockSpec((1, tm, d), lambda bi, i: (bi, i, 0))

    def full(a):
        return pl.BlockSpec(a.shape, lambda bi, i: (0,) * a.ndim)

    consts = [p['gn_w'], p['gn_b'], p['head_ind'], p['head_ind_t'], p['w_out'], p['g_post_mix'], p['g_pre_ffn'],
              p['w_router'], p['b_router']]
    in_specs = ([tok] * 6
                + [pl.BlockSpec((1, tm, GATE_W), lambda bi, i, c=col // GATE_W + k: (bi, i, c))
                   for col in (COL_G0, COL_G1) for k in range(d // GATE_W)]
                + [pl.BlockSpec((1, 1, d), mod_map)] * 3
                + [full(a) for a in consts] + [full(cnt0)])
    return pl.pallas_call(
        _merge_kernel,
        grid=(b, s // tm),
        in_specs=in_specs,
        out_specs=[tok, tok] + [pl.BlockSpec((1, tm, LANES), lambda bi, i: (bi, i, 0))] * 3
                  + [pl.BlockSpec((1, LANES), lambda bi, i: (0, 0))],
        out_shape=[jax.ShapeDtypeStruct((b, s, d), F32),
                   jax.ShapeDtypeStruct((b, s, d), F32),
                   jax.ShapeDtypeStruct((b, s, LANES), jnp.int32),
                   jax.ShapeDtypeStruct((b, s, LANES), jnp.int32),
                   jax.ShapeDtypeStruct((b, s, LANES), F32),
                   jax.ShapeDtypeStruct((1, LANES), F32)],
        scratch_shapes=[pltpu.VMEM((1, LANES), F32)],
        compiler_params=pltpu.CompilerParams(
            dimension_semantics=("arbitrary", "arbitrary"),
            vmem_limit_bytes=VMEM_LIMIT),
        name="merge_out",
    )(x, oa, y0, y1, bonus, gate, *([proj] * (2 * d // GATE_W)), *mods, *consts, cnt0)


def _route_metadata(eid, rank, counts, n_tiles):
    padded = (counts + MOE_TM - 1) // MOE_TM * MOE_TM
    ends = jnp.cumsum(padded)
    offs = ends - padded
    experts = jnp.arange(N_EXPERTS, dtype=jnp.int32)
    pos = jnp.sum(jnp.where(eid[:, :, None] == experts, (offs + rank)[:, None, :], 0), axis=-1)
    idx = jnp.arange(n_tiles, dtype=jnp.int32)
    valid = idx * MOE_TM < ends[-1]
    tile = jnp.where(valid, idx, ends[-1] // MOE_TM - 1)
    te = jnp.sum((ends[None, :] <= (tile * MOE_TM)[:, None]).astype(jnp.int32), axis=1)
    first = valid & jnp.concatenate([jnp.ones((1,), bool), te[1:] != te[:-1]])
    used = counts > 0
    slot_of = (jnp.cumsum(used.astype(jnp.int32)) - 1) % 2
    later = (experts[None, :] > experts[:, None]) & used[None, :]
    next_of = jnp.min(jnp.where(later, experts[None, :], N_EXPERTS), axis=1)
    next_of = jnp.where(next_of == N_EXPERTS, -1, next_of)
    is_te = te[:, None] == experts[None, :]
    pick = lambda table: jnp.sum(jnp.where(is_te, table[None, :], 0), axis=1)
    sched = (tile, te, valid.astype(jnp.int32), first.astype(jnp.int32), pick(slot_of), pick(next_of))
    fill = jnp.concatenate([jnp.where(padded > counts, ends - MOE_TM, -1),
                            jnp.where(valid, -1, idx * MOE_TM)[eid.size // MOE_TM:]])
    return pos, tuple(a.astype(jnp.int32) for a in sched), fill.astype(jnp.int32)


def _dispatch_kernel(fill_ref, pos_ref, hp_ref, hs_ref, xs_hbm, zbuf, sem, fsem, *, n_p_tiles, n_fill):
    i = pl.program_id(0)

    def fill_copy(e):
        start = pl.multiple_of(fill_ref[e], MOE_TM)
        return pltpu.make_async_copy(zbuf, xs_hbm.at[pl.ds(start, MOE_TM)], fsem)

    @pl.when(i == 0)
    def _():
        zbuf[...] = jnp.zeros_like(zbuf)
        for e in range(n_fill):
            @pl.when(fill_ref[e] >= 0)
            def _(e=e):
                fill_copy(e).start()
        for e in range(n_fill):
            @pl.when(fill_ref[e] >= 0)
            def _(e=e):
                fill_copy(e).wait()

    def scatter(h_ref):
        for t in range(COMB_TB):
            for j in range(TOP_K):
                pltpu.make_async_copy(h_ref.at[pl.ds(t, 1)],
                                      xs_hbm.at[pl.ds(pos_ref[0, 0, j * COMB_TB + t], 1)], sem
                                      ).start(priority=j % 2)

    @pl.when(i < n_p_tiles)
    def _():
        scatter(hp_ref)

    @pl.when(i >= n_p_tiles)
    def _():
        scatter(hs_ref)

    for _ in range(TOP_K):
        pltpu.make_async_copy(hp_ref, xs_hbm.at[pl.ds(0, COMB_TB)], sem).wait()


def _dispatch(pos, fill, h2p, h2s, n_tiles):
    d = D_MODEL
    n_p, n_s = h2p.shape[0], h2s.shape[0]
    n = (n_p + n_s) // COMB_TB
    n_pt = n_p // COMB_TB
    pos_t = pos.reshape(n, COMB_TB, TOP_K).transpose(0, 2, 1).reshape(n, 1, TOP_K * COMB_TB)
    grid_spec = pltpu.PrefetchScalarGridSpec(
        num_scalar_prefetch=1,
        grid=(n,),
        in_specs=[pl.BlockSpec((1, 1, TOP_K * COMB_TB), lambda i, fill: (i, 0, 0), memory_space=pltpu.SMEM),
                  pl.BlockSpec((COMB_TB, d), lambda i, fill: (jnp.minimum(i, n_pt - 1), 0)),
                  pl.BlockSpec((COMB_TB, d), lambda i, fill: (jnp.maximum(i - n_pt, 0), 0))],
        out_specs=pl.BlockSpec(memory_space=pl.ANY),
        scratch_shapes=[pltpu.VMEM((MOE_TM, d), F32),
                        pltpu.SemaphoreType.DMA(()),
                        pltpu.SemaphoreType.DMA(())])
    return pl.pallas_call(
        functools.partial(_dispatch_kernel, n_p_tiles=n_pt, n_fill=fill.shape[0]),
        grid_spec=grid_spec,
        out_shape=jax.ShapeDtypeStruct((n_tiles * MOE_TM, d), F32),
        compiler_params=pltpu.CompilerParams(
            dimension_semantics=("arbitrary",), vmem_limit_bytes=VMEM_LIMIT),
        name="moe_dispatch",
    )(fill, pos_t, h2p, h2s)


def _experts_kernel(tile_ref, te_ref, tv_ref, tf_ref, ws_ref, nx_ref, x_ref,
                    wgu_hbm, bgu_ref, wd_hbm, bd_ref, ys_ref, wgu_buf, wd_buf, wgu_scr, wd_scr, sem):
    i = pl.program_id(0)

    def weight_copies(e, slot):
        return (pltpu.make_async_copy(wgu_hbm.at[e], wgu_buf.at[slot], sem.at[0, slot]),
                pltpu.make_async_copy(wd_hbm.at[e], wd_buf.at[slot], sem.at[1, slot]))

    @pl.when(i == 0)
    def _():
        for cp in weight_copies(te_ref[0], 0):
            cp.start()

    @pl.when(tf_ref[i] == 1)
    def _():
        slot = ws_ref[i]

        @pl.when(nx_ref[i] >= 0)
        def _():
            for cp in weight_copies(nx_ref[i], 1 - slot):
                cp.start()

        for cp in weight_copies(te_ref[i], slot):
            cp.wait()
        wgu_scr[...] = wgu_buf[slot].astype(BF16)
        wd_scr[...] = wd_buf[slot].astype(BF16)

    @pl.when(tv_ref[i] == 0)
    def _():
        ys_ref[...] = jnp.zeros_like(ys_ref)

    @pl.when(tv_ref[i] == 1)
    def _():
        gu = jnp.dot(x_ref[...].astype(BF16), wgu_scr[...], preferred_element_type=F32) + bgu_ref[0]
        gate = jnp.minimum(gu[:, :D_MODEL], SWIGLU_LIMIT)
        up = jnp.clip(gu[:, D_MODEL:], -SWIGLU_LIMIT, SWIGLU_LIMIT)
        act = (up + 1.0) * gate * _sigmoid(SWIGLU_ALPHA * gate)
        ys_ref[...] = jnp.dot(act.astype(BF16), wd_scr[...], preferred_element_type=F32) + bd_ref[0]


def _experts(xs, sched, p, n_tiles):
    d = D_MODEL
    bias = lambda w: pl.BlockSpec((1, 1, w), lambda i, tile, te, *_: (te[i], 0, 0))
    grid_spec = pltpu.PrefetchScalarGridSpec(
        num_scalar_prefetch=len(sched),
        grid=(n_tiles,),
        in_specs=[pl.BlockSpec((MOE_TM, d), lambda i, tile, *_: (tile[i], 0)),
                  pl.BlockSpec(memory_space=pl.ANY), bias(2 * d),
                  pl.BlockSpec(memory_space=pl.ANY), bias(d)],
        out_specs=pl.BlockSpec((MOE_TM, d), lambda i, *_: (i, 0)),
        scratch_shapes=[pltpu.VMEM((2, d, 2 * d), F32),
                        pltpu.VMEM((2, d, d), F32),
                        pltpu.VMEM((d, 2 * d), BF16),
                        pltpu.VMEM((d, d), BF16),
                        pltpu.SemaphoreType.DMA((2, 2))])
    return pl.pallas_call(
        _experts_kernel,
        grid_spec=grid_spec,
        out_shape=jax.ShapeDtypeStruct((n_tiles * MOE_TM, d), F32),
        compiler_params=pltpu.CompilerParams(
            dimension_semantics=("arbitrary",), vmem_limit_bytes=VMEM_LIMIT),
        name="moe_experts",
    )(*sched, xs, p['w_gate_up'], p['b_gate_up'], p['w_down'], p['b_down'])


def _combine_kernel(pos_ref, posn_ref, ys_hbm, ew_ref, x1_ref, mg_ref, gpost_ref, o_ref, buf, sem, *, n):
    i = pl.program_id(0)
    slot = i % 2

    def row_copy(row, j, t, slot_):
        return pltpu.make_async_copy(ys_hbm.at[pl.ds(row, 1)], buf.at[slot_, j, pl.ds(t, 1)], sem.at[slot_])

    def gather(src_ref, slot_):
        for t in range(COMB_TB):
            for j in range(TOP_K):
                row_copy(src_ref[0, 0, j * COMB_TB + t], j, t, slot_).start(priority=j % 2)

    @pl.when(i == 0)
    def _():
        gather(pos_ref, 0)

    @pl.when(i + 1 < n)
    def _():
        gather(posn_ref, 1 - slot)

    for j in range(TOP_K):
        pltpu.make_async_copy(ys_hbm.at[pl.ds(0, COMB_TB)], buf.at[slot, j], sem.at[slot]).wait()
    ew = ew_ref[...]
    f = sum(ew[:, j:j + 1] * buf[slot, j] for j in range(TOP_K))
    o_ref[...] = x1_ref[...] + mg_ref[0] * _rms(f, gpost_ref[...])


def _combine(ys, pos, ew, x1, mod_gate, p):
    b, s, d = x1.shape
    n_tok = b * s
    n_tiles = n_tok // COMB_TB
    per_batch = mod_gate.shape[0] > 1
    if per_batch:
        mod_gate = jnp.repeat(mod_gate, s // COMB_TB, axis=0)
    mod_map = (lambda i: (i, 0, 0)) if per_batch else (lambda i: (0, 0, 0))
    pos_t = pos.reshape(n_tiles, COMB_TB, TOP_K).transpose(0, 2, 1).reshape(n_tiles, 1, TOP_K * COMB_TB)
    smem_tile = lambda f: pl.BlockSpec((1, 1, TOP_K * COMB_TB), f, memory_space=pltpu.SMEM)
    out = pl.pallas_call(
        functools.partial(_combine_kernel, n=n_tiles),
        grid=(n_tiles,),
        in_specs=[smem_tile(lambda i: (i, 0, 0)),
                  smem_tile(lambda i: (jnp.minimum(i + 1, n_tiles - 1), 0, 0)),
                  pl.BlockSpec(memory_space=pl.ANY),
                  pl.BlockSpec((COMB_TB, LANES), lambda i: (i, 0)),
                  pl.BlockSpec((COMB_TB, d), lambda i: (i, 0)),
                  pl.BlockSpec((1, 1, d), mod_map),
                  pl.BlockSpec((1, d), lambda i: (0, 0))],
        out_specs=pl.BlockSpec((COMB_TB, d), lambda i: (i, 0)),
        out_shape=jax.ShapeDtypeStruct((n_tok, d), F32),
        scratch_shapes=[pltpu.VMEM((2, TOP_K, COMB_TB, d), F32),
                        pltpu.SemaphoreType.DMA((2,))],
        compiler_params=pltpu.CompilerParams(
            dimension_semantics=("arbitrary",), vmem_limit_bytes=VMEM_LIMIT),
        name="moe_combine",
    )(pos_t, pos_t, ys, ew.reshape(n_tok, LANES), x1.reshape(n_tok, d), mod_gate, p['g_post_ffn'])
    return out.reshape(b, s, d)


def _rope_tables(n_tokens):
    rows = n_tokens // GRID_W
    row = jnp.repeat(jnp.arange(rows, dtype=F32), GRID_W)
    col = jnp.tile(jnp.arange(GRID_W, dtype=F32), rows)
    inv = ROPE_THETA ** (-jnp.arange(ROPE_FREQS, dtype=F32) / ROPE_FREQS)
    ang_r = row[:, None] * inv[None, :]
    ang_c = col[:, None] * inv[None, :]
    ang = jnp.concatenate([ang_r, ang_r, ang_c, ang_c] * 2, axis=-1)
    return jnp.cos(ang), jnp.sin(ang)


def _stream(x, mods, proj_w, lam, p, rope, ctx, s0, cnt0, proj_dtype, tm_proj, tm_tok):
    b, s, d = x.shape
    xf = x if mods[0].shape[0] > 1 else x.reshape(1, b * s, d)
    proj = _norm_proj(xf, p['g_pre_mix'], mods[1], mods[0], proj_w, tm_proj, PROJ_TN, proj_dtype)
    proj = proj.reshape(b, s, C_IN)
    oa = _diff_attention(proj, lam, p['g_subln'], rope, ctx, tq=min(s, 4 * ATT_SUB),
                         hps=H_A if s <= ATT_SUB else 1)
    r, v, kkn, kd, ba, lw, bonus, gate = _rwkv_prep(proj, p, min(s, 256))
    y0, y1, sfin = _rwkv_scan(r, v, kkn, kd, ba, lw, s0)
    tb, tsq = xf.shape[0], xf.shape[1]
    x1, h2, route, rank, ew, cnt = _merge_out(
        *(a.reshape(tb, tsq, a.shape[-1]) for a in (x, oa, y0, y1, bonus, gate, proj)),
        (mods[2], mods[4], mods[3]), p, cnt0, tm_tok)
    n = b * s
    route = (route.reshape(n, LANES)[:, :TOP_K], rank.reshape(n, LANES)[:, :N_EXPERTS])
    return (x1.reshape(b, s, d), h2.reshape(n, d), route, ew.reshape(n, LANES), cnt,
            proj, sfin)


def kernel(x_prompt, x_sample, cache_k, cache_v, state_rwkv, c, c_ctx, w_ada, b_ada, g_pre_mix, g_post_mix, g_pre_ffn, g_post_ffn, w_in, mu_prev, mu_next, lam, g_subln, k_k, k_a, r_k, w0, w_up, a0, a_up, g_up, gn_w, gn_b, w_out, w_router, b_router, w_gate_up, b_gate_up, w_down, b_down):
    l = 0
    d = D_MODEL
    bp, sp, _ = x_prompt.shape
    bs, ss, _ = x_sample.shape

    n_cond = 1 + bs
    rows = -(-n_cond // SUBLANES) * SUBLANES
    cond = jnp.concatenate([c_ctx[None, :], c, jnp.zeros((rows - n_cond, d), F32)], axis=0)
    mod = _modulation(cond, w_ada[l], b_ada[l][None, :])
    mods_p = [mod[0:1, i * d:(i + 1) * d].reshape(1, 1, d) for i in range(N_MOD)]
    mods_s = [mod[1:1 + bs, i * d:(i + 1) * d].reshape(bs, 1, d) for i in range(N_MOD)]

    w_proj = w_in[l].astype(BF16)

    lq = lam[l]
    lam_val = (jnp.exp(jnp.sum(lq[0] * lq[1])) - jnp.exp(jnp.sum(lq[2] * lq[3])) + LAM_INIT).reshape(1)

    head = jnp.arange(d) // N_B
    wr = jnp.concatenate([w_router[l], jnp.zeros((d, LANES - N_EXPERTS), F32)], axis=1)
    br = jnp.concatenate([b_router[l], jnp.full((LANES - N_EXPERTS,), -jnp.inf, F32)])[None, :]
    mup, mun = mu_prev[l][None, :], mu_next[l][None, :]
    p = {
        'g_pre_mix': g_pre_mix[l][None, :], 'g_post_mix': g_post_mix[l][None, :],
        'g_pre_ffn': g_pre_ffn[l][None, :], 'g_post_ffn': g_post_ffn[l][None, :],
        'g_subln': g_subln[l][None, :],
        'mu_prev_main': mup[:, :3 * d], 'mu_next_main': mun[:, :3 * d],
        'mu_prev_lora': mup[:, 3 * d:], 'mu_next_lora': mun[:, 3 * d:],
        'k_k': k_k[l][None, :], 'k_a': k_a[l][None, :], 'r_k': r_k[l].reshape(1, d),
        'w0': w0[l].reshape(1, 2 * d),
        'w_up': jnp.concatenate([w_up[l, 0], w_up[l, 1]], axis=1).astype(BF16),
        'a0': a0[l].reshape(1, 2 * d),
        'a_up': jnp.concatenate([a_up[l, 0], a_up[l, 1]], axis=1).astype(BF16),
        'g_up': g_up[l].astype(BF16),
        'head_ind': (head[:, None] == jnp.arange(LANES)[None, :]).astype(BF16),
        'head_ind_t': (jnp.arange(LANES)[:, None] == head[None, :]).astype(BF16),
        'gn_w': gn_w[l][None, :], 'gn_b': gn_b[l][None, :],
        'w_out': w_out[l].astype(BF16),
        'w_router': wr, 'b_router': br,
        'w_gate_up': w_gate_up[l], 'b_gate_up': b_gate_up[l][:, None, :],
        'w_down': w_down[l], 'b_down': b_down[l][:, None, :],
    }

    rope = _rope_tables(ss)
    ctx = (cache_k[:, l].reshape(bs, -1, d), cache_v[:, l].reshape(bs, -1, d))

    cnt0 = jnp.zeros((1, LANES), F32)
    x1p, h2p, route_p, ewp, cnt_p, proj_p, sfin = _stream(
        x_prompt, mods_p, w_proj, lam_val, p, None, None, None, cnt0, F32, 2048, 512)
    x1s, h2s, route_s, ews, cnt_s, _, _ = _stream(
        x_sample, mods_s, w_proj, lam_val, p, rope, ctx, state_rwkv[:, l], cnt_p, BF16, 2048, 512)

    n_p, n_s = bp * sp, bs * ss
    n_tok = n_p + n_s
    eid, rank = (jnp.concatenate([a, b_], axis=0) for a, b_ in zip(route_p, route_s))
    counts = cnt_s[0, :N_EXPERTS].astype(jnp.int32)
    n_tiles = -(-(n_tok * TOP_K + N_EXPERTS * (MOE_TM - 1)) // MOE_TM)
    pos, sched, fill = _route_metadata(eid, rank, counts, n_tiles)
    xs = _dispatch(pos, fill, h2p, h2s, n_tiles)
    rows = _experts(xs, sched, p, n_tiles)
    yp = _combine(rows, pos[:n_p], ewp, x1p, mods_p[5], p)
    ys = _combine(rows, pos[n_p:], ews, x1s, mods_s[5], p)

    new_k = proj_p[:, :, COL_K:COL_K + d].reshape(bp, 1, sp, H_A, 2, HD_A)
    new_v = proj_p[:, :, COL_V:COL_V + d].reshape(bp, 1, sp, H_A, DV_A)
    return (yp, ys, new_k, new_v, sfin[:, None])
```

```python
import functools
import math

import jax
import jax.numpy as jnp
from jax import lax
from jax.experimental import pallas as pl
from jax.experimental.pallas import tpu as pltpu

F32 = jnp.float32
BF16 = jnp.bfloat16

D_MODEL = 1024
GRID_W = 64
HD_A = 64
DV_A = 2 * HD_A
H_A = D_MODEL // DV_A
N_B = 64
H_B = D_MODEL // N_B
LORA_W = 64
LORA_A = 64
LORA_G = 128
N_EXPERTS = 32
TOP_K = 4
SWIGLU_LIMIT = 7.0
SWIGLU_ALPHA = 1.702
ROPE_THETA = 10000.0
ROPE_FREQS = HD_A // 4
NORM_EPS = 1e-6
GN_EPS = 64e-5
L2_EPS = 1e-12
ATTN_SCALE = HD_A ** -0.5
N_MOD = 6
LAM_INIT = 0.8 - 0.6 * math.exp(-0.3 * 0)
DECAY_SCALE = -math.exp(-0.5)

LANES = 128
SUBLANES = 8
CHUNK = 64
SCAN_NB = 1
MOE_TM = 512
PROJ_TN = 1408
COMB_TB = 256
ATT_SUB = 256
LOG2_E = math.log2(math.e)
VMEM_LIMIT = 56 * 1024 * 1024

C_LORA = LORA_W + LORA_A + LORA_G
COL_Q, COL_K, COL_V, COL_R, COL_KR, COL_VR, COL_LORA = (i * D_MODEL for i in range(7))
COL_G0 = COL_LORA + C_LORA
COL_G1 = COL_G0 + D_MODEL
C_IN = COL_G1 + D_MODEL
GATE_W = 256


def _sigmoid(x):
    return 1.0 / (1.0 + jnp.exp(-x))


def _dot(a, b):
    return jnp.dot(a.astype(BF16), b.astype(BF16), preferred_element_type=F32)


def _dot_nt(a, b):
    return lax.dot_general(a.astype(BF16), b.astype(BF16), (((1,), (1,)), ((), ())),
                           preferred_element_type=F32)


def _dot_tn(a, b):
    return lax.dot_general(a.astype(BF16), b.astype(BF16), (((0,), (0,)), ((), ())),
                           preferred_element_type=F32)


def _split3(x):
    hi = x.astype(BF16)
    r1 = x - hi.astype(F32)
    mid = r1.astype(BF16)
    lo = (r1 - mid.astype(F32)).astype(BF16)
    return hi, mid, lo


def _headsum(x, ind):
    e, et = ind
    s = jnp.dot(x.astype(BF16), e, preferred_element_type=F32)
    s_hi = s.astype(BF16)
    s_lo = (s - s_hi.astype(F32)).astype(BF16)
    return jnp.dot(s_hi, et, preferred_element_type=F32) + jnp.dot(s_lo, et, preferred_element_type=F32)


def _rms(x, g):
    return x * lax.rsqrt(jnp.mean(x * x, axis=-1, keepdims=True) + NORM_EPS) * g


def _mod_kernel(c_ref, w_ref, b_ref, o_ref):
    c = c_ref[...]
    s = c * _sigmoid(c)
    o_ref[...] = _dot(s, w_ref[...]) + b_ref[...]


def _modulation(cond, w_ada, b_ada):
    rows, d = cond.shape
    n = w_ada.shape[1]
    tn = 768
    return pl.pallas_call(
        _mod_kernel,
        grid=(n // tn,),
        in_specs=[pl.BlockSpec((rows, d), lambda j: (0, 0)),
                  pl.BlockSpec((d, tn), lambda j: (0, j)),
                  pl.BlockSpec((1, tn), lambda j: (0, j))],
        out_specs=pl.BlockSpec((rows, tn), lambda j: (0, j)),
        out_shape=jax.ShapeDtypeStruct((rows, n), F32),
        name="modulation",
    )(cond, w_ada, b_ada)


def _proj_kernel(x_ref, g_ref, sc_ref, sh_ref, w_ref, o_ref, h_scr):
    @pl.when(pl.program_id(2) == 0)
    def _():
        h = _rms(x_ref[0], g_ref[...]) * (1.0 + sc_ref[0]) + sh_ref[0]
        h_scr[...] = h.astype(BF16)

    o_ref[0] = jnp.dot(h_scr[...], w_ref[...], preferred_element_type=F32).astype(o_ref.dtype)


def _norm_proj(x, g, scale, shift, w, tm, tn, out_dtype):
    b, s, d = x.shape
    n = w.shape[1]
    per_batch = scale.shape[0] > 1
    mod_map = (lambda bi, i, j: (bi, 0, 0)) if per_batch else (lambda bi, i, j: (0, 0, 0))
    return pl.pallas_call(
        _proj_kernel,
        grid=(b, s // tm, n // tn),
        in_specs=[pl.BlockSpec((1, tm, d), lambda bi, i, j: (bi, i, 0)),
                  pl.BlockSpec((1, d), lambda bi, i, j: (0, 0)),
                  pl.BlockSpec((1, 1, d), mod_map),
                  pl.BlockSpec((1, 1, d), mod_map),
                  pl.BlockSpec((d, tn), lambda bi, i, j: (0, j))],
        out_specs=pl.BlockSpec((1, tm, tn), lambda bi, i, j: (bi, i, j)),
        out_shape=jax.ShapeDtypeStruct((b, s, n), out_dtype),
        scratch_shapes=[pltpu.VMEM((tm, d), BF16)],
        compiler_params=pltpu.CompilerParams(
            dimension_semantics=("arbitrary", "arbitrary", "arbitrary"),
            vmem_limit_bytes=VMEM_LIMIT),
        name="norm_proj",
    )(x, g, scale, shift, w)


def _rope(x, cos, sin):
    lane = lax.broadcasted_iota(jnp.int32, x.shape, 1)
    even = (lane // ROPE_FREQS) % 2 == 0
    rot = jnp.where(even, -pltpu.roll(x, LANES - ROPE_FREQS, 1), pltpu.roll(x, ROPE_FREQS, 1))
    return x * cos + rot * sin


def _attn_kernel(*refs, use_rope, use_ctx, s_new):
    it = iter(refs)
    lam_ref = next(it)
    q_ref, k_ref, v_ref = next(it), next(it), next(it)
    if use_rope:
        cq_ref, sq_ref, ck_ref, sk_ref = next(it), next(it), next(it), next(it)
    if use_ctx:
        ctxk_ref, ctxv_ref = next(it), next(it)
    g_ref = next(it)
    o_ref = next(it)
    k_scr, vt_scr = next(it), next(it)

    hps = q_ref.shape[-1] // DV_A
    head_lanes = [slice(hh * DV_A, (hh + 1) * DV_A) for hh in range(hps)]

    @pl.when(pl.program_id(2) == 0)
    def _():
        for hh, hl in enumerate(head_lanes):
            k = k_ref[0][:, hl].astype(F32)
            if use_rope:
                k = _rope(k, ck_ref[...], sk_ref[...])
            k_scr[hh, 0:s_new, :] = k.astype(BF16)
            vt_scr[hh, :, 0:s_new] = v_ref[0][:, hl].astype(F32).T.astype(BF16)
            if use_ctx:
                k_scr[hh, s_new:, :] = ctxk_ref[0][:, hl].astype(BF16)
                vt_scr[hh, :, s_new:] = ctxv_ref[0][:, hl].T.astype(BF16)

    lam = lam_ref[0]
    n_sub = q_ref.shape[1] // ATT_SUB
    lane = lax.broadcasted_iota(jnp.int32, (ATT_SUB, DV_A), 1)
    sts = []
    for hh, hl in enumerate(head_lanes):
        q = q_ref[0][:, hl].astype(F32)
        if use_rope:
            q = _rope(q, cq_ref[...], sq_ref[...])
        q = q * (ATTN_SCALE * LOG2_E)
        for t in range(n_sub):
            for m in range(2):
                in_map = (lane >= HD_A) if m else (lane < HD_A)
                qm = jnp.where(in_map, q[t * ATT_SUB:(t + 1) * ATT_SUB], 0.0)
                sts.append((hh, _dot_nt(k_scr[hh], qm)))
    outs = []
    for hh, st in sts:
        et = jnp.exp2(st - jnp.max(st, axis=0, keepdims=True))
        l = jnp.sum(et, axis=0, keepdims=True)
        ot = jnp.dot(vt_scr[hh], et.astype(BF16), preferred_element_type=F32)
        outs.append(ot / l)
    for hh, hl in enumerate(head_lanes):
        for t in range(n_sub):
            u = 2 * (hh * n_sub + t)
            o = (outs[u] - lam * outs[u + 1]).T
            o = o * lax.rsqrt(jnp.mean(o * o, axis=-1, keepdims=True) + NORM_EPS)
            o_ref[0, t * ATT_SUB:(t + 1) * ATT_SUB, hl] = o * g_ref[:, hl] * (1.0 - LAM_INIT)


def _diff_attention(proj, lam, g_subln, rope=None, ctx=None, tq=256, hps=1):
    b, s, _ = proj.shape
    use_rope, use_ctx = rope is not None, ctx is not None
    s_tot = s + (ctx[0].shape[1] if use_ctx else 0)
    w = hps * DV_A
    in_specs = [pl.BlockSpec(memory_space=pltpu.SMEM),
                pl.BlockSpec((1, tq, w), lambda bi, h, i: (bi, i, COL_Q // w + h)),
                pl.BlockSpec((1, s, w), lambda bi, h, i: (bi, 0, COL_K // w + h)),
                pl.BlockSpec((1, s, w), lambda bi, h, i: (bi, 0, COL_V // w + h))]
    args = [lam, proj, proj, proj]
    if use_rope:
        cos, sin = rope
        in_specs += [pl.BlockSpec((tq, DV_A), lambda bi, h, i: (i, 0)),
                     pl.BlockSpec((tq, DV_A), lambda bi, h, i: (i, 0)),
                     pl.BlockSpec((s, DV_A), lambda bi, h, i: (0, 0)),
                     pl.BlockSpec((s, DV_A), lambda bi, h, i: (0, 0))]
        args += [cos, sin, cos, sin]
    if use_ctx:
        p = ctx[0].shape[1]
        in_specs += [pl.BlockSpec((1, p, w), lambda bi, h, i: (bi, 0, h)),
                     pl.BlockSpec((1, p, w), lambda bi, h, i: (bi, 0, h))]
        args += [ctx[0], ctx[1]]
    in_specs.append(pl.BlockSpec((1, w), lambda bi, h, i: (0, h)))
    args.append(g_subln)
    return pl.pallas_call(
        functools.partial(_attn_kernel, use_rope=use_rope, use_ctx=use_ctx, s_new=s),
        grid=(b, H_A // hps, s // tq),
        in_specs=in_specs,
        out_specs=pl.BlockSpec((1, tq, w), lambda bi, h, i: (bi, i, h)),
        out_shape=jax.ShapeDtypeStruct((b, s, D_MODEL), F32),
        scratch_shapes=[pltpu.VMEM((hps, s_tot, DV_A), BF16), pltpu.VMEM((hps, DV_A, s_tot), BF16)],
        compiler_params=pltpu.CompilerParams(
            dimension_semantics=("arbitrary", "arbitrary", "arbitrary"),
            vmem_limit_bytes=VMEM_LIMIT),
        name="diff_attention",
    )(*args)


def _shifted(x_ref, p_ref, n_ref, mup, mun, first, last):
    x = x_ref[0].astype(F32)
    ts = x.shape[0]
    halo = p_ref.shape[1]
    row = lax.broadcasted_iota(jnp.int32, x.shape, 0)
    prev_row = p_ref[0][halo - 1:halo, :].astype(F32) * first
    next_row = n_ref[0][0:1, :].astype(F32) * last
    prev = jnp.where(row == 0, prev_row, pltpu.roll(x, 1, 0))
    nxt = jnp.where(row == ts - 1, next_row, pltpu.roll(x, ts - 1, 0))
    return x + mup * (prev - x) + mun * (nxt - x)


def _prep_kernel(r_ref, rp_ref, rn_ref, k_ref, kp_ref, kn_ref, v_ref, vp_ref, vn_ref,
                 l_ref, lp_ref, ln_ref, mup_ref, mun_ref, mupl_ref, munl_ref,
                 kk_ref, ka_ref, rk_ref, w0_ref, wup_ref, a0_ref, aup_ref, gup_ref, e_ref, et_ref,
                 ro_ref, vo_ref, kko_ref, kd_ref, ba_ref, lw_ref, bonus_ref, gate_ref, *, n_tiles):
    i = pl.program_id(1)
    first = (i > 0).astype(F32)
    last = (i < n_tiles - 1).astype(F32)
    mup, mun = mup_ref[...], mun_ref[...]
    d = D_MODEL
    r = _shifted(r_ref, rp_ref, rn_ref, mup[:, 0:d], mun[:, 0:d], first, last)
    k = _shifted(k_ref, kp_ref, kn_ref, mup[:, d:2 * d], mun[:, d:2 * d], first, last)
    v = _shifted(v_ref, vp_ref, vn_ref, mup[:, 2 * d:3 * d], mun[:, 2 * d:3 * d], first, last)
    lo = _shifted(l_ref, lp_ref, ln_ref, mupl_ref[...], munl_ref[...], first, last)
    xw = lo[:, 0:LORA_W]
    xa = lo[:, LORA_W:LORA_W + LORA_A]
    xg = lo[:, LORA_W + LORA_A:]
    g = (e_ref[...], et_ref[...])

    kk = k * kk_ref[...]
    nrm = jnp.sqrt(_headsum(kk * kk, g))
    kkn = kk / jnp.maximum(nrm, L2_EPS)
    ro_ref[0] = r
    vo_ref[0] = v
    kko_ref[0] = kkn
    gate_ref[0] = _dot(_sigmoid(xg), gup_ref[...])

    wlog = w0_ref[...] + _dot(jnp.tanh(xw), wup_ref[...])
    alog = a0_ref[...] + _dot(xa, aup_ref[...])
    ka = ka_ref[...]
    rrk = r * rk_ref[...]
    dots = None
    for dr in range(2):
        sl = slice(dr * d, (dr + 1) * d)
        lw_ref[dr, 0] = DECAY_SCALE * _sigmoid(wlog[:, sl])
        a = _sigmoid(alog[:, sl])
        kd = k * (1.0 + (a - 1.0) * ka)
        kd_ref[dr, 0] = kd
        ba_ref[dr, 0] = kkn * a
        t = rrk * kd
        dots = t if dots is None else dots + t
    bonus_ref[0] = _headsum(dots, g) * v


def _rwkv_prep(proj, p, ts):
    b, s, _ = proj.shape
    d = D_MODEL
    nt = s // ts
    halo = SUBLANES * (4 // proj.dtype.itemsize)
    hb = ts // halo
    nhb = s // halo

    def main(col, w):
        return pl.BlockSpec((1, ts, w), lambda bi, i: (bi, i, col // w))

    def prev(col, w):
        return pl.BlockSpec((1, halo, w), lambda bi, i: (bi, jnp.maximum(i * hb - 1, 0), col // w))

    def nxt(col, w):
        return pl.BlockSpec((1, halo, w),
                            lambda bi, i: (bi, jnp.minimum((i + 1) * hb, nhb - 1), col // w))

    def full(a):
        return pl.BlockSpec(a.shape, lambda bi, i: (0,) * a.ndim)

    in_specs, args = [], []
    for col, w in ((COL_R, d), (COL_KR, d), (COL_VR, d), (COL_LORA, C_LORA)):
        in_specs += [main(col, w), prev(col, w), nxt(col, w)]
        args += [proj, proj, proj]
    consts = [p['mu_prev_main'], p['mu_next_main'], p['mu_prev_lora'], p['mu_next_lora'],
              p['k_k'], p['k_a'], p['r_k'], p['w0'], p['w_up'], p['a0'], p['a_up'], p['g_up'],
              p['head_ind'], p['head_ind_t']]
    in_specs += [full(a) for a in consts]
    args += consts
    tok = pl.BlockSpec((1, ts, d), lambda bi, i: (bi, i, 0))
    tok2 = pl.BlockSpec((2, 1, ts, d), lambda bi, i: (0, bi, i, 0))
    one = jax.ShapeDtypeStruct((b, s, d), F32)
    two = jax.ShapeDtypeStruct((2, b, s, d), F32)
    return pl.pallas_call(
        functools.partial(_prep_kernel, n_tiles=nt),
        grid=(b, nt),
        in_specs=in_specs,
        out_specs=[tok, tok, tok, tok2, tok2, tok2, tok, tok],
        out_shape=[one, one, one, two, two, two, one, one],
        compiler_params=pltpu.CompilerParams(
            dimension_semantics=("arbitrary", "arbitrary"),
            vmem_limit_bytes=VMEM_LIMIT),
        name="rwkv_prep",
    )(*args)


def _scan_masks(reverse):
    n = CHUNK
    tt = lax.broadcasted_iota(jnp.int32, (n, n), 0)
    ss = lax.broadcasted_iota(jnp.int32, (n, n), 1)
    hi, lo = (ss, tt) if reverse else (tt, ss)
    tt2 = lax.broadcasted_iota(jnp.int32, (n, 2 * n), 0)
    col2 = lax.broadcasted_iota(jnp.int32, (n, 2 * n), 1)
    ss2 = col2 % n
    hi2, lo2 = (ss2, tt2) if reverse else (tt2, ss2)
    levels = []
    blk = 2
    while blk < n:
        levels.append((hi // (2 * blk) == lo // (2 * blk)) & ((hi // blk) % 2 == 1) & ((lo // blk) % 2 == 0))
        blk *= 2
    return dict(strict=hi > lo, incl=hi >= lo, eye=(tt == ss).astype(F32),
                first=(hi // 2 == lo // 2) & (hi > lo), levels=levels,
                strict_r=(hi2 > lo2) & (col2 >= n),
                incl2=hi2 >= lo2)


def _scan_operands(reverse, incl, lw, r, v, kkn, kd, ba):
    n = CHUNK
    tri = jnp.where(incl, 1.0, 0.0).astype(BF16)
    cum = sum(jnp.dot(tri, part, preferred_element_type=F32) for part in _split3(lw))
    tot = cum[0:1, :] if reverse else cum[n - 1:n, :]
    g_inv = jnp.exp(-cum)
    g_rest = jnp.exp(tot - cum)
    return dict(a=(kkn * jnp.exp(cum - lw)).astype(BF16), r=(r * jnp.exp(cum)).astype(BF16),
                b=(ba * g_inv).astype(BF16), k=(kd * g_inv).astype(BF16),
                bh=(ba * g_rest).astype(BF16), kh=(kd * g_rest).astype(BF16),
                v=v.astype(BF16), g_tot=jnp.exp(tot))


def _scan_kernel(*refs, has_s0, n_chunks):
    it = iter(refs)
    tok_refs = [[next(it) for _ in range(3)] for _ in range(2)]
    dir_refs = [[next(it) for _ in range(3)] for _ in range(2)]
    s0_ref = next(it) if has_s0 else None
    y_refs = [next(it), next(it)]
    sf_ref = next(it)
    s_scr = next(it)
    c = pl.program_id(1)

    @pl.when(c == 0)
    def _():
        if has_s0:
            s_scr[...] = s0_ref[...]
        else:
            s_scr[...] = jnp.zeros_like(s_scr)

    n = CHUNK
    nb = s_scr.shape[0]
    masks = [_scan_masks(dr == 1) for dr in range(2)]
    ops = {}
    for bb in range(nb):
        for dr in range(2):
            r_ref, v_ref, kk_ref = tok_refs[dr]
            kd_ref, ba_ref, lw_ref = dir_refs[dr]
            ops[bb, dr] = _scan_operands(dr == 1, masks[dr]['incl'], lw_ref[0, bb], r_ref[bb], v_ref[bb],
                                         kk_ref[bb], kd_ref[0, bb], ba_ref[0, bb])

    units = [(bb, dr, h) for bb in range(nb) for dr in range(2) for h in range(H_B)]
    idx = range(len(units))
    sls = [slice(h * N_B, (h + 1) * N_B) for _, _, h in units]
    op = lambda u, name: ops[units[u][:2]][name][:, sls[u]]
    mk = lambda u, name: masks[units[u][1]][name]
    s_old = [s_scr[bb, dr, h] for bb, dr, h in units]
    ar = [jnp.concatenate([op(u, 'a'), op(u, 'r')], axis=0) for u in idx]
    bk = [jnp.concatenate([op(u, 'b'), op(u, 'k')], axis=0) for u in idx]
    bkh = [jnp.concatenate([op(u, 'bh'), op(u, 'kh')], axis=0) for u in idx]
    vh = [op(u, 'v') for u in idx]

    gram = [_dot_nt(ar[u], bk[u]) for u in idx]
    ars = [_dot_nt(ar[u], s_old[u]) for u in idx]
    nmat = [jnp.where(mk(u, 'strict'), gram[u][0:n, 0:n], 0.0) for u in idx]
    mak = [jnp.where(mk(u, 'strict_r'), gram[u][0:n, :], 0.0) for u in idx]
    pr = [jnp.where(mk(u, 'incl2'), gram[u][n:, :], 0.0) for u in idx]
    z = [ars[u][0:n] + _dot(mak[u], jnp.concatenate([vh[u], vh[u]], axis=0)) for u in idx]
    x = [mk(u, 'eye') - jnp.where(mk(u, 'first'), nmat[u], 0.0) for u in idx]
    for lvl in range(len(masks[0]['levels'])):
        xl = [_dot(x[u], jnp.where(mk(u, 'levels')[lvl], nmat[u], 0.0)) for u in idx]
        x = [x[u] - _dot(xl[u], x[u]) for u in idx]
    uu = [-_dot(x[u], z[u]) for u in idx]
    uv = [jnp.concatenate([uu[u].astype(BF16), vh[u]], axis=0) for u in idx]
    y = [ars[u][n:] + _dot(pr[u], uv[u]) for u in idx]
    s_new = [s_old[u] * op(u, 'g_tot') + _dot_tn(uv[u], bkh[u]) for u in idx]
    for u, (bb, dr, h) in enumerate(units):
        y_refs[dr][bb, :, sls[u]] = y[u]
    for u, (bb, dr, h) in enumerate(units):
        s_scr[bb, dr, h] = s_new[u]

    @pl.when(c == n_chunks - 1)
    def _():
        sf_ref[...] = s_scr[...]


def _rwkv_scan(r, v, kkn, kd, ba, lw, s0):
    b, s, d = r.shape
    nc = s // CHUNK
    nb = SCAN_NB
    fwd = pl.BlockSpec((nb, CHUNK, d), lambda bi, c: (bi, c, 0))
    bwd = pl.BlockSpec((nb, CHUNK, d), lambda bi, c: (bi, nc - 1 - c, 0))
    fwd2 = pl.BlockSpec((1, nb, CHUNK, d), lambda bi, c: (0, bi, c, 0))
    bwd2 = pl.BlockSpec((1, nb, CHUNK, d), lambda bi, c: (1, bi, nc - 1 - c, 0))
    state = pl.BlockSpec((nb, 2, H_B, N_B, N_B), lambda bi, c: (bi, 0, 0, 0, 0))
    in_specs = [fwd] * 3 + [bwd] * 3 + [fwd2] * 3 + [bwd2] * 3
    args = [r, v, kkn, r, v, kkn, kd, ba, lw, kd, ba, lw]
    if s0 is not None:
        in_specs.append(state)
        args.append(s0)
    return pl.pallas_call(
        functools.partial(_scan_kernel, has_s0=s0 is not None, n_chunks=nc),
        grid=(b // nb, nc),
        in_specs=in_specs,
        out_specs=[fwd, bwd, state],
        out_shape=[jax.ShapeDtypeStruct((b, s, d), F32),
                   jax.ShapeDtypeStruct((b, s, d), F32),
                   jax.ShapeDtypeStruct((b, 2, H_B, N_B, N_B), F32)],
        scratch_shapes=[pltpu.VMEM((nb, 2, H_B, N_B, N_B), F32)],
        compiler_params=pltpu.CompilerParams(
            dimension_semantics=("arbitrary", "arbitrary"),
            vmem_limit_bytes=VMEM_LIMIT),
        name="rwkv_scan",
    )(*args)


def _merge_kernel(x_ref, oa_ref, y0_ref, y1_ref, bonus_ref, gate_ref,
                  g0a_ref, g0b_ref, g0c_ref, g0d_ref, g1a_ref, g1b_ref, g1c_ref, g1d_ref,
                  mg_ref, sc_ref, sh_ref, gnw_ref, gnb_ref, e_ref, et_ref, wout_ref,
                  gpost_ref, gpre_ref, wr_ref, br_ref, cnt0_ref,
                  x1_ref, h2_ref, route_ref, rank_ref, ew_ref, cnt_ref, cnt_scr):
    @pl.when((pl.program_id(0) == 0) & (pl.program_id(1) == 0))
    def _():
        cnt_scr[...] = cnt0_ref[...]

    g = (e_ref[...], et_ref[...])
    y = y0_ref[0] + y1_ref[0]
    mu = _headsum(y, g) * (1.0 / N_B)
    yc = y - mu
    var = _headsum(yc * yc, g) * (1.0 / N_B)
    yn = yc * lax.rsqrt(var + GN_EPS) * gnw_ref[...] + gnb_ref[...]
    ob = (yn + bonus_ref[0]) * gate_ref[0]
    g0 = jnp.concatenate([r[0].astype(F32) for r in (g0a_ref, g0b_ref, g0c_ref, g0d_ref)], axis=1)
    g1 = jnp.concatenate([r[0].astype(F32) for r in (g1a_ref, g1b_ref, g1c_ref, g1d_ref)], axis=1)
    merged = _sigmoid(g0) * oa_ref[0] + _sigmoid(g1) * ob
    out = _dot(merged, wout_ref[...])
    x1 = x_ref[0] + mg_ref[0] * _rms(out, gpost_ref[...])
    x1_ref[0] = x1
    h2 = _rms(x1, gpre_ref[...]) * (1.0 + sc_ref[0]) + sh_ref[0]
    h2_ref[0] = h2

    h_hi = h2.astype(BF16)
    h_lo = (h2 - h_hi.astype(F32)).astype(BF16)
    wr = wr_ref[...]
    w_hi = wr.astype(BF16)
    w_lo = (wr - w_hi.astype(F32)).astype(BF16)
    logits = (jnp.dot(h_hi, w_hi, preferred_element_type=F32) + jnp.dot(h_hi, w_lo, preferred_element_type=F32)
              + jnp.dot(h_lo, w_hi, preferred_element_type=F32)) + br_ref[...]
    lane = lax.broadcasted_iota(jnp.int32, logits.shape, 1)
    work = logits
    top = None
    picks = []
    for _ in range(TOP_K):
        mx = jnp.max(work, axis=-1, keepdims=True)
        idx = jnp.min(jnp.where(work == mx, lane, LANES), axis=-1, keepdims=True)
        if top is None:
            top = mx
        picks.append((idx, jnp.exp(mx - top)))
        work = jnp.where(lane == idx, -jnp.inf, work)
    denom = sum(e for _, e in picks)
    tm = logits.shape[0]
    onehot = jnp.zeros_like(logits)
    for idx, _ in picks:
        onehot = onehot + jnp.where(lane == idx, 1.0, 0.0)
    rr = lax.broadcasted_iota(jnp.int32, (tm, tm), 0)
    cc = lax.broadcasted_iota(jnp.int32, (tm, tm), 1)
    before = cnt_scr[...] + _dot(jnp.where(rr > cc, 1.0, 0.0), onehot)
    route = jnp.zeros(logits.shape, jnp.int32)
    ew = jnp.zeros_like(logits)
    for j, (idx, e) in enumerate(picks):
        route = jnp.where(lane == j, idx, route)
        ew = jnp.where(lane == j, e / denom, ew)
    route_ref[0] = route
    rank_ref[0] = jnp.where(onehot > 0.0, before, 0.0).astype(jnp.int32)
    ew_ref[0] = ew
    cnt_scr[...] = cnt_scr[...] + jnp.sum(onehot, axis=0, keepdims=True)
    cnt_ref[...] = cnt_scr[...]


def _merge_out(x, oa, y0, y1, bonus, gate, proj, mods, p, cnt0, tm):
    b, s, d = x.shape
    per_batch = mods[0].shape[0] > 1
    mod_map = (lambda bi, i: (bi, 0, 0)) if per_batch else (lambda bi, i: (0, 0, 0))
    tok = pl.BlockSpec((1, tm, d), lambda bi, i: (bi, i, 0))

    def full(a):
        return pl.BlockSpec(a.shape, lambda bi, i: (0,) * a.ndim)

    consts = [p['gn_w'], p['gn_b'], p['head_ind'], p['head_ind_t'], p['w_out'], p['g_post_mix'], p['g_pre_ffn'],
              p['w_router'], p['b_router']]
    in_specs = ([tok] * 6
                + [pl.BlockSpec((1, tm, GATE_W), lambda bi, i, c=col // GATE_W + k: (bi, i, c))
                   for col in (COL_G0, COL_G1) for k in range(d // GATE_W)]
                + [pl.BlockSpec((1, 1, d), mod_map)] * 3
                + [full(a) for a in consts] + [full(cnt0)])
    return pl.pallas_call(
        _merge_kernel,
        grid=(b, s // tm),
        in_specs=in_specs,
        out_specs=[tok, tok] + [pl.BlockSpec((1, tm, LANES), lambda bi, i: (bi, i, 0))] * 3
                  + [pl.BlockSpec((1, LANES), lambda bi, i: (0, 0))],
        out_shape=[jax.ShapeDtypeStruct((b, s, d), F32),
                   jax.ShapeDtypeStruct((b, s, d), F32),
                   jax.ShapeDtypeStruct((b, s, LANES), jnp.int32),
                   jax.ShapeDtypeStruct((b, s, LANES), jnp.int32),
                   jax.ShapeDtypeStruct((b, s, LANES), F32),
                   jax.ShapeDtypeStruct((1, LANES), F32)],
        scratch_shapes=[pltpu.VMEM((1, LANES), F32)],
        compiler_params=pltpu.CompilerParams(
            dimension_semantics=("arbitrary", "arbitrary"),
            vmem_limit_bytes=VMEM_LIMIT),
        name="merge_out",
    )(x, oa, y0, y1, bonus, gate, *([proj] * (2 * d // GATE_W)), *mods, *consts, cnt0)


def _route_metadata(eid, rank, counts, n_tiles):
    padded = (counts + MOE_TM - 1) // MOE_TM * MOE_TM
    ends = jnp.cumsum(padded)
    offs = ends - padded
    experts = jnp.arange(N_EXPERTS, dtype=jnp.int32)
    pos = jnp.sum(jnp.where(eid[:, :, None] == experts, (offs + rank)[:, None, :], 0), axis=-1)
    idx = jnp.arange(n_tiles, dtype=jnp.int32)
    valid = idx * MOE_TM < ends[-1]
    tile = jnp.where(valid, idx, ends[-1] // MOE_TM - 1)
    te = jnp.sum((ends[None, :] <= (tile * MOE_TM)[:, None]).astype(jnp.int32), axis=1)
    first = valid & jnp.concatenate([jnp.ones((1,), bool), te[1:] != te[:-1]])
    used = counts > 0
    slot_of = (jnp.cumsum(used.astype(jnp.int32)) - 1) % 2
    later = (experts[None, :] > experts[:, None]) & used[None, :]
    next_of = jnp.min(jnp.where(later, experts[None, :], N_EXPERTS), axis=1)
    next_of = jnp.where(next_of == N_EXPERTS, -1, next_of)
    is_te = te[:, None] == experts[None, :]
    pick = lambda table: jnp.sum(jnp.where(is_te, table[None, :], 0), axis=1)
    rows_here = pick(offs + counts) - tile * MOE_TM
    state = jnp.where(valid, jnp.where(rows_here > MOE_TM // 2, 2, 1), 0)
    sched = (tile, te, state, first.astype(jnp.int32), pick(slot_of), pick(next_of))
    fill = jnp.concatenate([jnp.where(padded > counts, ends - MOE_TM, -1),
                            jnp.where(valid, -1, idx * MOE_TM)[eid.size // MOE_TM:]])
    return pos, tuple(a.astype(jnp.int32) for a in sched), fill.astype(jnp.int32)


def _dispatch_kernel(fill_ref, pos_ref, hp_ref, hs_ref, xs_hbm, zbuf, sem, fsem, *, n_p_tiles, n_fill):
    i = pl.program_id(0)

    def fill_copy(e):
        start = pl.multiple_of(fill_ref[e], MOE_TM)
        return pltpu.make_async_copy(zbuf, xs_hbm.at[pl.ds(start, MOE_TM)], fsem)

    @pl.when(i == 0)
    def _():
        zbuf[...] = jnp.zeros_like(zbuf)
        for e in range(n_fill):
            @pl.when(fill_ref[e] >= 0)
            def _(e=e):
                fill_copy(e).start()
        for e in range(n_fill):
            @pl.when(fill_ref[e] >= 0)
            def _(e=e):
                fill_copy(e).wait()

    def scatter(h_ref):
        for t in range(COMB_TB):
            for j in range(TOP_K):
                pltpu.make_async_copy(h_ref.at[pl.ds(t, 1)],
                                      xs_hbm.at[pl.ds(pos_ref[0, 0, j * COMB_TB + t], 1)], sem
                                      ).start(priority=j % 2)

    @pl.when(i < n_p_tiles)
    def _():
        scatter(hp_ref)

    @pl.when(i >= n_p_tiles)
    def _():
        scatter(hs_ref)

    for _ in range(TOP_K):
        pltpu.make_async_copy(hp_ref, xs_hbm.at[pl.ds(0, COMB_TB)], sem).wait()


def _dispatch(pos, fill, h2p, h2s, n_tiles):
    d = D_MODEL
    n_p, n_s = h2p.shape[0], h2s.shape[0]
    n = (n_p + n_s) // COMB_TB
    n_pt = n_p // COMB_TB
    pos_t = pos.reshape(n, COMB_TB, TOP_K).transpose(0, 2, 1).reshape(n, 1, TOP_K * COMB_TB)
    grid_spec = pltpu.PrefetchScalarGridSpec(
        num_scalar_prefetch=1,
        grid=(n,),
        in_specs=[pl.BlockSpec((1, 1, TOP_K * COMB_TB), lambda i, fill: (i, 0, 0), memory_space=pltpu.SMEM),
                  pl.BlockSpec((COMB_TB, d), lambda i, fill: (jnp.minimum(i, n_pt - 1), 0)),
                  pl.BlockSpec((COMB_TB, d), lambda i, fill: (jnp.maximum(i - n_pt, 0), 0))],
        out_specs=pl.BlockSpec(memory_space=pl.ANY),
        scratch_shapes=[pltpu.VMEM((MOE_TM, d), F32),
                        pltpu.SemaphoreType.DMA(()),
                        pltpu.SemaphoreType.DMA(())])
    return pl.pallas_call(
        functools.partial(_dispatch_kernel, n_p_tiles=n_pt, n_fill=fill.shape[0]),
        grid_spec=grid_spec,
        out_shape=jax.ShapeDtypeStruct((n_tiles * MOE_TM, d), F32),
        compiler_params=pltpu.CompilerParams(
            dimension_semantics=("arbitrary",), vmem_limit_bytes=VMEM_LIMIT),
        name="moe_dispatch",
    )(fill, pos_t, h2p, h2s)


def _experts_kernel(tile_ref, te_ref, tv_ref, tf_ref, ws_ref, nx_ref, x_ref,
                    wgu_hbm, bgu_ref, wd_hbm, bd_ref, ys_ref, wgu_buf, wd_buf, wgu_scr, wd_scr, sem):
    i = pl.program_id(0)

    def weight_copies(e, slot):
        return (pltpu.make_async_copy(wgu_hbm.at[e], wgu_buf.at[slot], sem.at[0, slot]),
                pltpu.make_async_copy(wd_hbm.at[e], wd_buf.at[slot], sem.at[1, slot]))

    @pl.when(i == 0)
    def _():
        for cp in weight_copies(te_ref[0], 0):
            cp.start()

    @pl.when(tf_ref[i] == 1)
    def _():
        slot = ws_ref[i]

        @pl.when(nx_ref[i] >= 0)
        def _():
            for cp in weight_copies(nx_ref[i], 1 - slot):
                cp.start()

        for cp in weight_copies(te_ref[i], slot):
            cp.wait()
        wgu_scr[...] = wgu_buf[slot].astype(BF16)
        wd_scr[...] = wd_buf[slot].astype(BF16)

    half = MOE_TM // 2

    def ffn(rows):
        gu = jnp.dot(x_ref[rows, :].astype(BF16), wgu_scr[...], preferred_element_type=F32) + bgu_ref[0]
        gate = jnp.minimum(gu[:, :D_MODEL], SWIGLU_LIMIT)
        up = jnp.clip(gu[:, D_MODEL:], -SWIGLU_LIMIT, SWIGLU_LIMIT)
        act = (up + 1.0) * gate * _sigmoid(SWIGLU_ALPHA * gate)
        ys_ref[rows, :] = jnp.dot(act.astype(BF16), wd_scr[...], preferred_element_type=F32) + bd_ref[0]

    @pl.when(tv_ref[i] >= 1)
    def _():
        ffn(slice(0, half))

    @pl.when(tv_ref[i] == 2)
    def _():
        ffn(slice(half, MOE_TM))

    @pl.when(tv_ref[i] == 0)
    def _():
        ys_ref[0:half, :] = jnp.zeros((half, D_MODEL), F32)

    @pl.when(tv_ref[i] <= 1)
    def _():
        ys_ref[half:, :] = jnp.zeros((half, D_MODEL), F32)


def _experts(xs, sched, p, n_tiles):
    d = D_MODEL
    bias = lambda w: pl.BlockSpec((1, 1, w), lambda i, tile, te, *_: (te[i], 0, 0))
    grid_spec = pltpu.PrefetchScalarGridSpec(
        num_scalar_prefetch=len(sched),
        grid=(n_tiles,),
        in_specs=[pl.BlockSpec((MOE_TM, d), lambda i, tile, *_: (tile[i], 0)),
                  pl.BlockSpec(memory_space=pl.ANY), bias(2 * d),
                  pl.BlockSpec(memory_space=pl.ANY), bias(d)],
        out_specs=pl.BlockSpec((MOE_TM, d), lambda i, *_: (i, 0)),
        scratch_shapes=[pltpu.VMEM((2, d, 2 * d), F32),
                        pltpu.VMEM((2, d, d), F32),
                        pltpu.VMEM((d, 2 * d), BF16),
                        pltpu.VMEM((d, d), BF16),
                        pltpu.SemaphoreType.DMA((2, 2))])
    return pl.pallas_call(
        _experts_kernel,
        grid_spec=grid_spec,
        out_shape=jax.ShapeDtypeStruct((n_tiles * MOE_TM, d), F32),
        compiler_params=pltpu.CompilerParams(
            dimension_semantics=("arbitrary",), vmem_limit_bytes=VMEM_LIMIT),
        name="moe_experts",
    )(*sched, xs, p['w_gate_up'], p['b_gate_up'], p['w_down'], p['b_down'])


def _combine_kernel(pos_ref, posn_ref, ys_hbm, ew_ref, x1_ref, mg_ref, gpost_ref, o_ref, buf, sem, *, n):
    i = pl.program_id(0)
    slot = i % 2

    def row_copy(row, j, t, slot_):
        return pltpu.make_async_copy(ys_hbm.at[pl.ds(row, 1)], buf.at[slot_, j, pl.ds(t, 1)], sem.at[slot_])

    def gather(src_ref, slot_):
        for t in range(COMB_TB):
            for j in range(TOP_K):
                row_copy(src_ref[0, 0, j * COMB_TB + t], j, t, slot_).start(priority=j % 2)

    @pl.when(i == 0)
    def _():
        gather(pos_ref, 0)

    for cur in range(2):
        @pl.when((i + 1 < n) & (slot == cur))
        def _(cur=cur):
            gather(posn_ref, 1 - cur)

    for j in range(TOP_K):
        pltpu.make_async_copy(ys_hbm.at[pl.ds(0, COMB_TB)], buf.at[slot, j], sem.at[slot]).wait()
    ew = ew_ref[...]
    f = sum(ew[:, j:j + 1] * buf[slot, j] for j in range(TOP_K))
    o_ref[...] = x1_ref[...] + mg_ref[0] * _rms(f, gpost_ref[...])


def _combine(ys, pos, ew, x1, mod_gate, p):
    b, s, d = x1.shape
    n_tok = b * s
    n_tiles = n_tok // COMB_TB
    per_batch = mod_gate.shape[0] > 1
    if per_batch:
        mod_gate = jnp.repeat(mod_gate, s // COMB_TB, axis=0)
    mod_map = (lambda i: (i, 0, 0)) if per_batch else (lambda i: (0, 0, 0))
    pos_t = pos.reshape(n_tiles, COMB_TB, TOP_K).transpose(0, 2, 1).reshape(n_tiles, 1, TOP_K * COMB_TB)
    smem_tile = lambda f: pl.BlockSpec((1, 1, TOP_K * COMB_TB), f, memory_space=pltpu.SMEM)
    out = pl.pallas_call(
        functools.partial(_combine_kernel, n=n_tiles),
        grid=(n_tiles,),
        in_specs=[smem_tile(lambda i: (i, 0, 0)),
                  smem_tile(lambda i: (jnp.minimum(i + 1, n_tiles - 1), 0, 0)),
                  pl.BlockSpec(memory_space=pl.ANY),
                  pl.BlockSpec((COMB_TB, LANES), lambda i: (i, 0)),
                  pl.BlockSpec((COMB_TB, d), lambda i: (i, 0)),
                  pl.BlockSpec((1, 1, d), mod_map),
                  pl.BlockSpec((1, d), lambda i: (0, 0))],
        out_specs=pl.BlockSpec((COMB_TB, d), lambda i: (i, 0)),
        out_shape=jax.ShapeDtypeStruct((n_tok, d), F32),
        scratch_shapes=[pltpu.VMEM((2, TOP_K, COMB_TB, d), F32),
                        pltpu.SemaphoreType.DMA((2,))],
        compiler_params=pltpu.CompilerParams(
            dimension_semantics=("arbitrary",), vmem_limit_bytes=VMEM_LIMIT),
        name="moe_combine",
    )(pos_t, pos_t, ys, ew.reshape(n_tok, LANES), x1.reshape(n_tok, d), mod_gate, p['g_post_ffn'])
    return out.reshape(b, s, d)


def _rope_tables(n_tokens):
    rows = n_tokens // GRID_W
    row = jnp.repeat(jnp.arange(rows, dtype=F32), GRID_W)
    col = jnp.tile(jnp.arange(GRID_W, dtype=F32), rows)
    inv = ROPE_THETA ** (-jnp.arange(ROPE_FREQS, dtype=F32) / ROPE_FREQS)
    ang_r = row[:, None] * inv[None, :]
    ang_c = col[:, None] * inv[None, :]
    ang = jnp.concatenate([ang_r, ang_r, ang_c, ang_c] * 2, axis=-1)
    return jnp.cos(ang), jnp.sin(ang)


def _stream(x, mods, proj_w, lam, p, rope, ctx, s0, cnt0, proj_dtype, tm_proj, tm_tok):
    b, s, d = x.shape
    xf = x if mods[0].shape[0] > 1 else x.reshape(1, b * s, d)
    proj = _norm_proj(xf, p['g_pre_mix'], mods[1], mods[0], proj_w, tm_proj, PROJ_TN, proj_dtype)
    proj = proj.reshape(b, s, C_IN)
    oa = _diff_attention(proj, lam, p['g_subln'], rope, ctx, tq=min(s, 4 * ATT_SUB),
                         hps=H_A if s <= ATT_SUB else 1)
    r, v, kkn, kd, ba, lw, bonus, gate = _rwkv_prep(proj, p, min(s, 256))
    y0, y1, sfin = _rwkv_scan(r, v, kkn, kd, ba, lw, s0)
    tb, tsq = xf.shape[0], xf.shape[1]
    x1, h2, route, rank, ew, cnt = _merge_out(
        *(a.reshape(tb, tsq, a.shape[-1]) for a in (x, oa, y0, y1, bonus, gate, proj)),
        (mods[2], mods[4], mods[3]), p, cnt0, tm_tok)
    n = b * s
    route = (route.reshape(n, LANES)[:, :TOP_K], rank.reshape(n, LANES)[:, :N_EXPERTS])
    return (x1.reshape(b, s, d), h2.reshape(n, d), route, ew.reshape(n, LANES), cnt,
            proj, sfin)


def kernel(x_prompt, x_sample, cache_k, cache_v, state_rwkv, c, c_ctx, w_ada, b_ada, g_pre_mix, g_post_mix, g_pre_ffn, g_post_ffn, w_in, mu_prev, mu_next, lam, g_subln, k_k, k_a, r_k, w0, w_up, a0, a_up, g_up, gn_w, gn_b, w_out, w_router, b_router, w_gate_up, b_gate_up, w_down, b_down):
    l = 0
    d = D_MODEL
    bp, sp, _ = x_prompt.shape
    bs, ss, _ = x_sample.shape

    n_cond = 1 + bs
    rows = -(-n_cond // SUBLANES) * SUBLANES
    cond = jnp.concatenate([c_ctx[None, :], c, jnp.zeros((rows - n_cond, d), F32)], axis=0)
    mod = _modulation(cond, w_ada[l], b_ada[l][None, :])
    mods_p = [mod[0:1, i * d:(i + 1) * d].reshape(1, 1, d) for i in range(N_MOD)]
    mods_s = [mod[1:1 + bs, i * d:(i + 1) * d].reshape(bs, 1, d) for i in range(N_MOD)]

    w_proj = w_in[l].astype(BF16)

    lq = lam[l]
    lam_val = (jnp.exp(jnp.sum(lq[0] * lq[1])) - jnp.exp(jnp.sum(lq[2] * lq[3])) + LAM_INIT).reshape(1)

    head = jnp.arange(d) // N_B
    wr = jnp.concatenate([w_router[l], jnp.zeros((d, LANES - N_EXPERTS), F32)], axis=1)
    br = jnp.concatenate([b_router[l], jnp.full((LANES - N_EXPERTS,), -jnp.inf, F32)])[None, :]
    mup, mun = mu_prev[l][None, :], mu_next[l][None, :]
    p = {
        'g_pre_mix': g_pre_mix[l][None, :], 'g_post_mix': g_post_mix[l][None, :],
        'g_pre_ffn': g_pre_ffn[l][None, :], 'g_post_ffn': g_post_ffn[l][None, :],
        'g_subln': g_subln[l][None, :],
        'mu_prev_main': mup[:, :3 * d], 'mu_next_main': mun[:, :3 * d],
        'mu_prev_lora': mup[:, 3 * d:], 'mu_next_lora': mun[:, 3 * d:],
        'k_k': k_k[l][None, :], 'k_a': k_a[l][None, :], 'r_k': r_k[l].reshape(1, d),
        'w0': w0[l].reshape(1, 2 * d),
        'w_up': jnp.concatenate([w_up[l, 0], w_up[l, 1]], axis=1).astype(BF16),
        'a0': a0[l].reshape(1, 2 * d),
        'a_up': jnp.concatenate([a_up[l, 0], a_up[l, 1]], axis=1).astype(BF16),
        'g_up': g_up[l].astype(BF16),
        'head_ind': (head[:, None] == jnp.arange(LANES)[None, :]).astype(BF16),
        'head_ind_t': (jnp.arange(LANES)[:, None] == head[None, :]).astype(BF16),
        'gn_w': gn_w[l][None, :], 'gn_b': gn_b[l][None, :],
        'w_out': w_out[l].astype(BF16),
        'w_router': wr, 'b_router': br,
        'w_gate_up': w_gate_up[l], 'b_gate_up': b_gate_up[l][:, None, :],
        'w_down': w_down[l], 'b_down': b_down[l][:, None, :],
    }

    rope = _rope_tables(ss)
    ctx = (cache_k[:, l].reshape(bs, -1, d), cache_v[:, l].reshape(bs, -1, d))

    cnt0 = jnp.zeros((1, LANES), F32)
    x1p, h2p, route_p, ewp, cnt_p, proj_p, sfin = _stream(
        x_prompt, mods_p, w_proj, lam_val, p, None, None, None, cnt0, F32, 2048, 512)
    x1s, h2s, route_s, ews, cnt_s, _, _ = _stream(
        x_sample, mods_s, w_proj, lam_val, p, rope, ctx, state_rwkv[:, l], cnt_p, BF16, 2048, 512)

    n_p, n_s = bp * sp, bs * ss
    n_tok = n_p + n_s
    eid, rank = (jnp.concatenate([a, b_], axis=0) for a, b_ in zip(route_p, route_s))
    counts = cnt_s[0, :N_EXPERTS].astype(jnp.int32)
    n_tiles = -(-(n_tok * TOP_K + N_EXPERTS * (MOE_TM - 1)) // MOE_TM)
    pos, sched, fill = _route_metadata(eid, rank, counts, n_tiles)
    xs = _dispatch(pos, fill, h2p, h2s, n_tiles)
    rows = _experts(xs, sched, p, n_tiles)
    yp = _combine(rows, pos[:n_p], ewp, x1p, mods_p[5], p)
    ys = _combine(rows, pos[n_p:], ews, x1s, mods_s[5], p)

    new_k = proj_p[:, :, COL_K:COL_K + d].reshape(bp, 1, sp, H_A, 2, HD_A)
    new_v = proj_p[:, :, COL_V:COL_V + d].reshape(bp, 1, sp, H_A, DV_A)
    return (yp, ys, new_k, new_v, sfin[:, None])
```

```python
import functools
import math

import jax
import jax.numpy as jnp
from jax import lax
from jax.experimental import pallas as pl
from jax.experimental.pallas import tpu as pltpu

F32 = jnp.float32
BF16 = jnp.bfloat16

D_MODEL = 1024
GRID_W = 64
HD_A = 64
DV_A = 2 * HD_A
H_A = D_MODEL // DV_A
N_B = 64
H_B = D_MODEL // N_B
LORA_W = 64
LORA_A = 64
LORA_G = 128
N_EXPERTS = 32
TOP_K = 4
SWIGLU_LIMIT = 7.0
SWIGLU_ALPHA = 1.702
ROPE_THETA = 10000.0
ROPE_FREQS = HD_A // 4
NORM_EPS = 1e-6
GN_EPS = 64e-5
L2_EPS = 1e-12
ATTN_SCALE = HD_A ** -0.5
N_MOD = 6
LAM_INIT = 0.8 - 0.6 * math.exp(-0.3 * 0)
DECAY_SCALE = -math.exp(-0.5)

LANES = 128
SUBLANES = 8
CHUNK = 64
SCAN_NB = 1
MOE_TM = 512
PROJ_TN = 1408
COMB_TB = 256
DISP_TB = 512
ATT_SUB = 256
LOG2_E = math.log2(math.e)
VMEM_LIMIT = 56 * 1024 * 1024

C_LORA = LORA_W + LORA_A + LORA_G
COL_Q, COL_K, COL_V, COL_R, COL_KR, COL_VR, COL_LORA = (i * D_MODEL for i in range(7))
COL_G0 = COL_LORA + C_LORA
COL_G1 = COL_G0 + D_MODEL
C_IN = COL_G1 + D_MODEL
GATE_W = 256


def _sigmoid(x):
    return 1.0 / (1.0 + jnp.exp(-x))


def _dot(a, b):
    return jnp.dot(a.astype(BF16), b.astype(BF16), preferred_element_type=F32)


def _dot_nt(a, b):
    return lax.dot_general(a.astype(BF16), b.astype(BF16), (((1,), (1,)), ((), ())),
                           preferred_element_type=F32)


def _dot_tn(a, b):
    return lax.dot_general(a.astype(BF16), b.astype(BF16), (((0,), (0,)), ((), ())),
                           preferred_element_type=F32)


def _split3(x):
    hi = x.astype(BF16)
    r1 = x - hi.astype(F32)
    mid = r1.astype(BF16)
    lo = (r1 - mid.astype(F32)).astype(BF16)
    return hi, mid, lo


def _headsum(x, ind):
    e, et = ind
    s = jnp.dot(x.astype(BF16), e, preferred_element_type=F32)
    s_hi = s.astype(BF16)
    s_lo = (s - s_hi.astype(F32)).astype(BF16)
    return jnp.dot(s_hi, et, preferred_element_type=F32) + jnp.dot(s_lo, et, preferred_element_type=F32)


def _rms(x, g):
    return x * lax.rsqrt(jnp.mean(x * x, axis=-1, keepdims=True) + NORM_EPS) * g


def _mod_kernel(c_ref, w_ref, b_ref, o_ref):
    c = c_ref[...]
    s = c * _sigmoid(c)
    o_ref[...] = _dot(s, w_ref[...]) + b_ref[...]


def _modulation(cond, w_ada, b_ada):
    rows, d = cond.shape
    n = w_ada.shape[1]
    tn = 768
    return pl.pallas_call(
        _mod_kernel,
        grid=(n // tn,),
        in_specs=[pl.BlockSpec((rows, d), lambda j: (0, 0)),
                  pl.BlockSpec((d, tn), lambda j: (0, j)),
                  pl.BlockSpec((1, tn), lambda j: (0, j))],
        out_specs=pl.BlockSpec((rows, tn), lambda j: (0, j)),
        out_shape=jax.ShapeDtypeStruct((rows, n), F32),
        name="modulation",
    )(cond, w_ada, b_ada)


def _proj_kernel(x_ref, g_ref, sc_ref, sh_ref, w_ref, o_ref, h_scr):
    @pl.when(pl.program_id(2) == 0)
    def _():
        h = _rms(x_ref[0], g_ref[...]) * (1.0 + sc_ref[0]) + sh_ref[0]
        h_scr[...] = h.astype(BF16)

    o_ref[0] = jnp.dot(h_scr[...], w_ref[...], preferred_element_type=F32).astype(o_ref.dtype)


def _norm_proj(x, g, scale, shift, w, tm, tn, out_dtype):
    b, s, d = x.shape
    n = w.shape[1]
    per_batch = scale.shape[0] > 1
    mod_map = (lambda bi, i, j: (bi, 0, 0)) if per_batch else (lambda bi, i, j: (0, 0, 0))
    return pl.pallas_call(
        _proj_kernel,
        grid=(b, s // tm, n // tn),
        in_specs=[pl.BlockSpec((1, tm, d), lambda bi, i, j: (bi, i, 0)),
                  pl.BlockSpec((1, d), lambda bi, i, j: (0, 0)),
                  pl.BlockSpec((1, 1, d), mod_map),
                  pl.BlockSpec((1, 1, d), mod_map),
                  pl.BlockSpec((d, tn), lambda bi, i, j: (0, j))],
        out_specs=pl.BlockSpec((1, tm, tn), lambda bi, i, j: (bi, i, j)),
        out_shape=jax.ShapeDtypeStruct((b, s, n), out_dtype),
        scratch_shapes=[pltpu.VMEM((tm, d), BF16)],
        compiler_params=pltpu.CompilerParams(
            dimension_semantics=("arbitrary", "arbitrary", "arbitrary"),
            vmem_limit_bytes=VMEM_LIMIT),
        name="norm_proj",
    )(x, g, scale, shift, w)


def _rope(x, cos, sin):
    lane = lax.broadcasted_iota(jnp.int32, x.shape, 1)
    even = (lane // ROPE_FREQS) % 2 == 0
    rot = jnp.where(even, -pltpu.roll(x, LANES - ROPE_FREQS, 1), pltpu.roll(x, ROPE_FREQS, 1))
    return x * cos + rot * sin


def _attn_kernel(*refs, use_rope, use_ctx, s_new):
    it = iter(refs)
    lam_ref = next(it)
    q_ref, k_ref, v_ref = next(it), next(it), next(it)
    if use_rope:
        cq_ref, sq_ref, ck_ref, sk_ref = next(it), next(it), next(it), next(it)
    if use_ctx:
        ctxk_ref, ctxv_ref = next(it), next(it)
    g_ref = next(it)
    o_ref = next(it)
    k_scr, vt_scr = next(it), next(it)

    hps = q_ref.shape[-1] // DV_A
    head_lanes = [slice(hh * DV_A, (hh + 1) * DV_A) for hh in range(hps)]

    @pl.when(pl.program_id(2) == 0)
    def _():
        for hh, hl in enumerate(head_lanes):
            k = k_ref[0][:, hl].astype(F32)
            if use_rope:
                k = _rope(k, ck_ref[...], sk_ref[...])
            k_scr[hh, 0:s_new, :] = k.astype(BF16)
            vt_scr[hh, :, 0:s_new] = v_ref[0][:, hl].astype(F32).T.astype(BF16)
            if use_ctx:
                k_scr[hh, s_new:, :] = ctxk_ref[0][:, hl].astype(BF16)
                vt_scr[hh, :, s_new:] = ctxv_ref[0][:, hl].T.astype(BF16)

    lam = lam_ref[0]
    n_sub = q_ref.shape[1] // ATT_SUB
    lane = lax.broadcasted_iota(jnp.int32, (ATT_SUB, DV_A), 1)
    sts = []
    for hh, hl in enumerate(head_lanes):
        q = q_ref[0][:, hl].astype(F32)
        if use_rope:
            q = _rope(q, cq_ref[...], sq_ref[...])
        q = q * (ATTN_SCALE * LOG2_E)
        for t in range(n_sub):
            for m in range(2):
                in_map = (lane >= HD_A) if m else (lane < HD_A)
                qm = jnp.where(in_map, q[t * ATT_SUB:(t + 1) * ATT_SUB], 0.0)
                sts.append((hh, _dot_nt(k_scr[hh], qm)))
    outs = []
    for hh, st in sts:
        et = jnp.exp2(st - jnp.max(st, axis=0, keepdims=True))
        l = jnp.sum(et, axis=0, keepdims=True)
        ot = jnp.dot(vt_scr[hh], et.astype(BF16), preferred_element_type=F32)
        outs.append(ot / l)
    for hh, hl in enumerate(head_lanes):
        for t in range(n_sub):
            u = 2 * (hh * n_sub + t)
            o = (outs[u] - lam * outs[u + 1]).T
            o = o * lax.rsqrt(jnp.mean(o * o, axis=-1, keepdims=True) + NORM_EPS)
            o_ref[0, t * ATT_SUB:(t + 1) * ATT_SUB, hl] = o * g_ref[:, hl] * (1.0 - LAM_INIT)


def _diff_attention(proj, lam, g_subln, rope=None, ctx=None, tq=256, hps=1):
    b, s, _ = proj.shape
    use_rope, use_ctx = rope is not None, ctx is not None
    s_tot = s + (ctx[0].shape[1] if use_ctx else 0)
    w = hps * DV_A
    in_specs = [pl.BlockSpec(memory_space=pltpu.SMEM),
                pl.BlockSpec((1, tq, w), lambda bi, h, i: (bi, i, COL_Q // w + h)),
                pl.BlockSpec((1, s, w), lambda bi, h, i: (bi, 0, COL_K // w + h)),
                pl.BlockSpec((1, s, w), lambda bi, h, i: (bi, 0, COL_V // w + h))]
    args = [lam, proj, proj, proj]
    if use_rope:
        cos, sin = rope
        in_specs += [pl.BlockSpec((tq, DV_A), lambda bi, h, i: (i, 0)),
                     pl.BlockSpec((tq, DV_A), lambda bi, h, i: (i, 0)),
                     pl.BlockSpec((s, DV_A), lambda bi, h, i: (0, 0)),
                     pl.BlockSpec((s, DV_A), lambda bi, h, i: (0, 0))]
        args += [cos, sin, cos, sin]
    if use_ctx:
        p = ctx[0].shape[1]
        in_specs += [pl.BlockSpec((1, p, w), lambda bi, h, i: (bi, 0, h)),
                     pl.BlockSpec((1, p, w), lambda bi, h, i: (bi, 0, h))]
        args += [ctx[0], ctx[1]]
    in_specs.append(pl.BlockSpec((1, w), lambda bi, h, i: (0, h)))
    args.append(g_subln)
    return pl.pallas_call(
        functools.partial(_attn_kernel, use_rope=use_rope, use_ctx=use_ctx, s_new=s),
        grid=(b, H_A // hps, s // tq),
        in_specs=in_specs,
        out_specs=pl.BlockSpec((1, tq, w), lambda bi, h, i: (bi, i, h)),
        out_shape=jax.ShapeDtypeStruct((b, s, D_MODEL), F32),
        scratch_shapes=[pltpu.VMEM((hps, s_tot, DV_A), BF16), pltpu.VMEM((hps, DV_A, s_tot), BF16)],
        compiler_params=pltpu.CompilerParams(
            dimension_semantics=("arbitrary", "arbitrary", "arbitrary"),
            vmem_limit_bytes=VMEM_LIMIT),
        name="diff_attention",
    )(*args)


def _shifted(x_ref, p_ref, n_ref, mup, mun, first, last):
    x = x_ref[0].astype(F32)
    ts = x.shape[0]
    halo = p_ref.shape[1]
    row = lax.broadcasted_iota(jnp.int32, x.shape, 0)
    prev_row = p_ref[0][halo - 1:halo, :].astype(F32) * first
    next_row = n_ref[0][0:1, :].astype(F32) * last
    prev = jnp.where(row == 0, prev_row, pltpu.roll(x, 1, 0))
    nxt = jnp.where(row == ts - 1, next_row, pltpu.roll(x, ts - 1, 0))
    return x + mup * (prev - x) + mun * (nxt - x)


def _prep_kernel(r_ref, rp_ref, rn_ref, k_ref, kp_ref, kn_ref, v_ref, vp_ref, vn_ref,
                 l_ref, lp_ref, ln_ref, mup_ref, mun_ref, mupl_ref, munl_ref,
                 kk_ref, ka_ref, rk_ref, w0_ref, wup_ref, a0_ref, aup_ref, gup_ref, e_ref, et_ref,
                 ro_ref, vo_ref, kko_ref, kd_ref, ba_ref, lw_ref, bonus_ref, gate_ref, *, n_tiles):
    i = pl.program_id(1)
    first = (i > 0).astype(F32)
    last = (i < n_tiles - 1).astype(F32)
    mup, mun = mup_ref[...], mun_ref[...]
    d = D_MODEL
    r = _shifted(r_ref, rp_ref, rn_ref, mup[:, 0:d], mun[:, 0:d], first, last)
    k = _shifted(k_ref, kp_ref, kn_ref, mup[:, d:2 * d], mun[:, d:2 * d], first, last)
    v = _shifted(v_ref, vp_ref, vn_ref, mup[:, 2 * d:3 * d], mun[:, 2 * d:3 * d], first, last)
    lo = _shifted(l_ref, lp_ref, ln_ref, mupl_ref[...], munl_ref[...], first, last)
    xw = lo[:, 0:LORA_W]
    xa = lo[:, LORA_W:LORA_W + LORA_A]
    xg = lo[:, LORA_W + LORA_A:]
    g = (e_ref[...], et_ref[...])

    kk = k * kk_ref[...]
    nrm = jnp.sqrt(_headsum(kk * kk, g))
    kkn = kk / jnp.maximum(nrm, L2_EPS)
    ro_ref[0] = r
    vo_ref[0] = v
    kko_ref[0] = kkn
    gate_ref[0] = _dot(_sigmoid(xg), gup_ref[...])

    wlog = w0_ref[...] + _dot(jnp.tanh(xw), wup_ref[...])
    alog = a0_ref[...] + _dot(xa, aup_ref[...])
    ka = ka_ref[...]
    rrk = r * rk_ref[...]
    dots = None
    for dr in range(2):
        sl = slice(dr * d, (dr + 1) * d)
        lw_ref[dr, 0] = DECAY_SCALE * _sigmoid(wlog[:, sl])
        a = _sigmoid(alog[:, sl])
        kd = k * (1.0 + (a - 1.0) * ka)
        kd_ref[dr, 0] = kd
        ba_ref[dr, 0] = kkn * a
        t = rrk * kd
        dots = t if dots is None else dots + t
    bonus_ref[0] = _headsum(dots, g) * v


def _rwkv_prep(proj, p, ts):
    b, s, _ = proj.shape
    d = D_MODEL
    nt = s // ts
    halo = SUBLANES * (4 // proj.dtype.itemsize)
    hb = ts // halo
    nhb = s // halo

    def main(col, w):
        return pl.BlockSpec((1, ts, w), lambda bi, i: (bi, i, col // w))

    def prev(col, w):
        return pl.BlockSpec((1, halo, w), lambda bi, i: (bi, jnp.maximum(i * hb - 1, 0), col // w))

    def nxt(col, w):
        return pl.BlockSpec((1, halo, w),
                            lambda bi, i: (bi, jnp.minimum((i + 1) * hb, nhb - 1), col // w))

    def full(a):
        return pl.BlockSpec(a.shape, lambda bi, i: (0,) * a.ndim)

    in_specs, args = [], []
    for col, w in ((COL_R, d), (COL_KR, d), (COL_VR, d), (COL_LORA, C_LORA)):
        in_specs += [main(col, w), prev(col, w), nxt(col, w)]
        args += [proj, proj, proj]
    consts = [p['mu_prev_main'], p['mu_next_main'], p['mu_prev_lora'], p['mu_next_lora'],
              p['k_k'], p['k_a'], p['r_k'], p['w0'], p['w_up'], p['a0'], p['a_up'], p['g_up'],
              p['head_ind'], p['head_ind_t']]
    in_specs += [full(a) for a in consts]
    args += consts
    tok = pl.BlockSpec((1, ts, d), lambda bi, i: (bi, i, 0))
    tok2 = pl.BlockSpec((2, 1, ts, d), lambda bi, i: (0, bi, i, 0))
    one = jax.ShapeDtypeStruct((b, s, d), F32)
    two = jax.ShapeDtypeStruct((2, b, s, d), F32)
    return pl.pallas_call(
        functools.partial(_prep_kernel, n_tiles=nt),
        grid=(b, nt),
        in_specs=in_specs,
        out_specs=[tok, tok, tok, tok2, tok2, tok2, tok, tok],
        out_shape=[one, one, one, two, two, two, one, one],
        compiler_params=pltpu.CompilerParams(
            dimension_semantics=("arbitrary", "arbitrary"),
            vmem_limit_bytes=VMEM_LIMIT),
        name="rwkv_prep",
    )(*args)


def _scan_masks(reverse):
    n = CHUNK
    tt = lax.broadcasted_iota(jnp.int32, (n, n), 0)
    ss = lax.broadcasted_iota(jnp.int32, (n, n), 1)
    hi, lo = (ss, tt) if reverse else (tt, ss)
    tt2 = lax.broadcasted_iota(jnp.int32, (n, 2 * n), 0)
    col2 = lax.broadcasted_iota(jnp.int32, (n, 2 * n), 1)
    ss2 = col2 % n
    hi2, lo2 = (ss2, tt2) if reverse else (tt2, ss2)
    levels = []
    blk = 2
    while blk < n:
        levels.append((hi // (2 * blk) == lo // (2 * blk)) & ((hi // blk) % 2 == 1) & ((lo // blk) % 2 == 0))
        blk *= 2
    return dict(strict=hi > lo, incl=hi >= lo, eye=(tt == ss).astype(F32),
                first=(hi // 2 == lo // 2) & (hi > lo), levels=levels,
                strict_r=(hi2 > lo2) & (col2 >= n),
                incl2=hi2 >= lo2)


def _scan_operands(reverse, incl, lw, r, v, kkn, kd, ba):
    n = CHUNK
    tri = jnp.where(incl, 1.0, 0.0).astype(BF16)
    cum = sum(jnp.dot(tri, part, preferred_element_type=F32) for part in _split3(lw))
    tot = cum[0:1, :] if reverse else cum[n - 1:n, :]
    g_inv = jnp.exp(-cum)
    g_rest = jnp.exp(tot - cum)
    return dict(a=(kkn * jnp.exp(cum - lw)).astype(BF16), r=(r * jnp.exp(cum)).astype(BF16),
                b=(ba * g_inv).astype(BF16), k=(kd * g_inv).astype(BF16),
                bh=(ba * g_rest).astype(BF16), kh=(kd * g_rest).astype(BF16),
                v=v.astype(BF16), g_tot=jnp.exp(tot))


def _scan_kernel(*refs, has_s0, n_chunks):
    it = iter(refs)
    tok_refs = [[next(it) for _ in range(3)] for _ in range(2)]
    dir_refs = [[next(it) for _ in range(3)] for _ in range(2)]
    s0_ref = next(it) if has_s0 else None
    y_refs = [next(it), next(it)]
    sf_ref = next(it)
    s_scr = next(it)
    c = pl.program_id(1)

    @pl.when(c == 0)
    def _():
        if has_s0:
            s_scr[...] = s0_ref[...]
        else:
            s_scr[...] = jnp.zeros_like(s_scr)

    n = CHUNK
    nb = s_scr.shape[0]
    masks = [_scan_masks(dr == 1) for dr in range(2)]
    ops = {}
    for bb in range(nb):
        for dr in range(2):
            r_ref, v_ref, kk_ref = tok_refs[dr]
            kd_ref, ba_ref, lw_ref = dir_refs[dr]
            ops[bb, dr] = _scan_operands(dr == 1, masks[dr]['incl'], lw_ref[0, bb], r_ref[bb], v_ref[bb],
                                         kk_ref[bb], kd_ref[0, bb], ba_ref[0, bb])

    units = [(bb, dr, h) for bb in range(nb) for dr in range(2) for h in range(H_B)]
    idx = range(len(units))
    sls = [slice(h * N_B, (h + 1) * N_B) for _, _, h in units]
    op = lambda u, name: ops[units[u][:2]][name][:, sls[u]]
    mk = lambda u, name: masks[units[u][1]][name]
    s_old = [s_scr[bb, dr, h] for bb, dr, h in units]
    ar = [jnp.concatenate([op(u, 'a'), op(u, 'r')], axis=0) for u in idx]
    bk = [jnp.concatenate([op(u, 'b'), op(u, 'k')], axis=0) for u in idx]
    bkh = [jnp.concatenate([op(u, 'bh'), op(u, 'kh')], axis=0) for u in idx]
    vh = [op(u, 'v') for u in idx]

    gram = [_dot_nt(ar[u], bk[u]) for u in idx]
    ars = [_dot_nt(ar[u], s_old[u]) for u in idx]
    nmat = [jnp.where(mk(u, 'strict'), gram[u][0:n, 0:n], 0.0) for u in idx]
    mak = [jnp.where(mk(u, 'strict_r'), gram[u][0:n, :], 0.0) for u in idx]
    pr = [jnp.where(mk(u, 'incl2'), gram[u][n:, :], 0.0) for u in idx]
    z = [ars[u][0:n] + _dot(mak[u], jnp.concatenate([vh[u], vh[u]], axis=0)) for u in idx]
    x = [mk(u, 'eye') - jnp.where(mk(u, 'first'), nmat[u], 0.0) for u in idx]
    for lvl in range(len(masks[0]['levels'])):
        xl = [_dot(x[u], jnp.where(mk(u, 'levels')[lvl], nmat[u], 0.0)) for u in idx]
        x = [x[u] - _dot(xl[u], x[u]) for u in idx]
    uu = [-_dot(x[u], z[u]) for u in idx]
    uv = [jnp.concatenate([uu[u].astype(BF16), vh[u]], axis=0) for u in idx]
    y = [ars[u][n:] + _dot(pr[u], uv[u]) for u in idx]
    s_new = [s_old[u] * op(u, 'g_tot') + _dot_tn(uv[u], bkh[u]) for u in idx]
    for u, (bb, dr, h) in enumerate(units):
        y_refs[dr][bb, :, sls[u]] = y[u]
    for u, (bb, dr, h) in enumerate(units):
        s_scr[bb, dr, h] = s_new[u]

    @pl.when(c == n_chunks - 1)
    def _():
        sf_ref[...] = s_scr[...]


def _rwkv_scan(r, v, kkn, kd, ba, lw, s0):
    b, s, d = r.shape
    nc = s // CHUNK
    nb = SCAN_NB
    fwd = pl.BlockSpec((nb, CHUNK, d), lambda bi, c: (bi, c, 0))
    bwd = pl.BlockSpec((nb, CHUNK, d), lambda bi, c: (bi, nc - 1 - c, 0))
    fwd2 = pl.BlockSpec((1, nb, CHUNK, d), lambda bi, c: (0, bi, c, 0))
    bwd2 = pl.BlockSpec((1, nb, CHUNK, d), lambda bi, c: (1, bi, nc - 1 - c, 0))
    state = pl.BlockSpec((nb, 2, H_B, N_B, N_B), lambda bi, c: (bi, 0, 0, 0, 0))
    in_specs = [fwd] * 3 + [bwd] * 3 + [fwd2] * 3 + [bwd2] * 3
    args = [r, v, kkn, r, v, kkn, kd, ba, lw, kd, ba, lw]
    if s0 is not None:
        in_specs.append(state)
        args.append(s0)
    return pl.pallas_call(
        functools.partial(_scan_kernel, has_s0=s0 is not None, n_chunks=nc),
        grid=(b // nb, nc),
        in_specs=in_specs,
        out_specs=[fwd, bwd, state],
        out_shape=[jax.ShapeDtypeStruct((b, s, d), F32),
                   jax.ShapeDtypeStruct((b, s, d), F32),
                   jax.ShapeDtypeStruct((b, 2, H_B, N_B, N_B), F32)],
        scratch_shapes=[pltpu.VMEM((nb, 2, H_B, N_B, N_B), F32)],
        compiler_params=pltpu.CompilerParams(
            dimension_semantics=("arbitrary", "arbitrary"),
            vmem_limit_bytes=VMEM_LIMIT),
        name="rwkv_scan",
    )(*args)


def _merge_kernel(x_ref, oa_ref, y0_ref, y1_ref, bonus_ref, gate_ref,
                  g0a_ref, g0b_ref, g0c_ref, g0d_ref, g1a_ref, g1b_ref, g1c_ref, g1d_ref,
                  mg_ref, sc_ref, sh_ref, gnw_ref, gnb_ref, e_ref, et_ref, wout_ref,
                  gpost_ref, gpre_ref, wr_ref, br_ref, cnt0_ref,
                  x1_ref, h2_ref, route_ref, rank_ref, ew_ref, cnt_ref, cnt_scr):
    @pl.when((pl.program_id(0) == 0) & (pl.program_id(1) == 0))
    def _():
        cnt_scr[...] = cnt0_ref[...]

    g = (e_ref[...], et_ref[...])
    y = y0_ref[0] + y1_ref[0]
    mu = _headsum(y, g) * (1.0 / N_B)
    yc = y - mu
    var = _headsum(yc * yc, g) * (1.0 / N_B)
    yn = yc * lax.rsqrt(var + GN_EPS) * gnw_ref[...] + gnb_ref[...]
    ob = (yn + bonus_ref[0]) * gate_ref[0]
    g0 = jnp.concatenate([r[0].astype(F32) for r in (g0a_ref, g0b_ref, g0c_ref, g0d_ref)], axis=1)
    g1 = jnp.concatenate([r[0].astype(F32) for r in (g1a_ref, g1b_ref, g1c_ref, g1d_ref)], axis=1)
    merged = _sigmoid(g0) * oa_ref[0] + _sigmoid(g1) * ob
    out = _dot(merged, wout_ref[...])
    x1 = x_ref[0] + mg_ref[0] * _rms(out, gpost_ref[...])
    x1_ref[0] = x1
    h2 = _rms(x1, gpre_ref[...]) * (1.0 + sc_ref[0]) + sh_ref[0]
    h2_ref[0] = h2

    h_hi = h2.astype(BF16)
    h_lo = (h2 - h_hi.astype(F32)).astype(BF16)
    wr = wr_ref[...]
    w_hi = wr.astype(BF16)
    w_lo = (wr - w_hi.astype(F32)).astype(BF16)
    logits = (jnp.dot(h_hi, w_hi, preferred_element_type=F32) + jnp.dot(h_hi, w_lo, preferred_element_type=F32)
              + jnp.dot(h_lo, w_hi, preferred_element_type=F32)) + br_ref[...]
    lane = lax.broadcasted_iota(jnp.int32, logits.shape, 1)
    work = logits
    top = None
    picks = []
    for _ in range(TOP_K):
        mx = jnp.max(work, axis=-1, keepdims=True)
        idx = jnp.min(jnp.where(work == mx, lane, LANES), axis=-1, keepdims=True)
        if top is None:
            top = mx
        picks.append((idx, jnp.exp(mx - top)))
        work = jnp.where(lane == idx, -jnp.inf, work)
    denom = sum(e for _, e in picks)
    tm = logits.shape[0]
    onehot = jnp.zeros_like(logits)
    for idx, _ in picks:
        onehot = onehot + jnp.where(lane == idx, 1.0, 0.0)
    rr = lax.broadcasted_iota(jnp.int32, (tm, tm), 0)
    cc = lax.broadcasted_iota(jnp.int32, (tm, tm), 1)
    before = cnt_scr[...] + _dot(jnp.where(rr > cc, 1.0, 0.0), onehot)
    route = jnp.zeros(logits.shape, jnp.int32)
    ew = jnp.zeros_like(logits)
    for j, (idx, e) in enumerate(picks):
        route = jnp.where(lane == j, idx, route)
        ew = jnp.where(lane == j, e / denom, ew)
    route_ref[0] = route
    rank_ref[0] = jnp.where(onehot > 0.0, before, 0.0).astype(jnp.int32)
    ew_ref[0] = ew
    cnt_scr[...] = cnt_scr[...] + jnp.sum(onehot, axis=0, keepdims=True)
    cnt_ref[...] = cnt_scr[...]


def _merge_out(x, oa, y0, y1, bonus, gate, proj, mods, p, cnt0, tm):
    b, s, d = x.shape
    per_batch = mods[0].shape[0] > 1
    mod_map = (lambda bi, i: (bi, 0, 0)) if per_batch else (lambda bi, i: (0, 0, 0))
    tok = pl.BlockSpec((1, tm, d), lambda bi, i: (bi, i, 0))

    def full(a):
        return pl.BlockSpec(a.shape, lambda bi, i: (0,) * a.ndim)

    consts = [p['gn_w'], p['gn_b'], p['head_ind'], p['head_ind_t'], p['w_out'], p['g_post_mix'], p['g_pre_ffn'],
              p['w_router'], p['b_router']]
    in_specs = ([tok] * 6
                + [pl.BlockSpec((1, tm, GATE_W), lambda bi, i, c=col // GATE_W + k: (bi, i, c))
                   for col in (COL_G0, COL_G1) for k in range(d // GATE_W)]
                + [pl.BlockSpec((1, 1, d), mod_map)] * 3
                + [full(a) for a in consts] + [full(cnt0)])
    return pl.pallas_call(
        _merge_kernel,
        grid=(b, s // tm),
        in_specs=in_specs,
        out_specs=[tok, tok] + [pl.BlockSpec((1, tm, LANES), lambda bi, i: (bi, i, 0))] * 3
                  + [pl.BlockSpec((1, LANES), lambda bi, i: (0, 0))],
        out_shape=[jax.ShapeDtypeStruct((b, s, d), F32),
                   jax.ShapeDtypeStruct((b, s, d), F32),
                   jax.ShapeDtypeStruct((b, s, LANES), jnp.int32),
                   jax.ShapeDtypeStruct((b, s, LANES), jnp.int32),
                   jax.ShapeDtypeStruct((b, s, LANES), F32),
                   jax.ShapeDtypeStruct((1, LANES), F32)],
        scratch_shapes=[pltpu.VMEM((1, LANES), F32)],
        compiler_params=pltpu.CompilerParams(
            dimension_semantics=("arbitrary", "arbitrary"),
            vmem_limit_bytes=VMEM_LIMIT),
        name="merge_out",
    )(x, oa, y0, y1, bonus, gate, *([proj] * (2 * d // GATE_W)), *mods, *consts, cnt0)


def _route_metadata(eid, rank, counts, n_tiles):
    padded = (counts + MOE_TM - 1) // MOE_TM * MOE_TM
    ends = jnp.cumsum(padded)
    offs = ends - padded
    experts = jnp.arange(N_EXPERTS, dtype=jnp.int32)
    pos = jnp.sum(jnp.where(eid[:, :, None] == experts, (offs + rank)[:, None, :], 0), axis=-1)
    idx = jnp.arange(n_tiles, dtype=jnp.int32)
    valid = idx * MOE_TM < ends[-1]
    tile = jnp.where(valid, idx, ends[-1] // MOE_TM - 1)
    te = jnp.sum((ends[None, :] <= (tile * MOE_TM)[:, None]).astype(jnp.int32), axis=1)
    first = valid & jnp.concatenate([jnp.ones((1,), bool), te[1:] != te[:-1]])
    used = counts > 0
    slot_of = (jnp.cumsum(used.astype(jnp.int32)) - 1) % 2
    later = (experts[None, :] > experts[:, None]) & used[None, :]
    next_of = jnp.min(jnp.where(later, experts[None, :], N_EXPERTS), axis=1)
    next_of = jnp.where(next_of == N_EXPERTS, -1, next_of)
    is_te = te[:, None] == experts[None, :]
    pick = lambda table: jnp.sum(jnp.where(is_te, table[None, :], 0), axis=1)
    rows_here = pick(offs + counts) - tile * MOE_TM
    state = jnp.where(valid, jnp.where(rows_here > MOE_TM // 2, 2, 1), 0)
    sched = (tile, te, state, first.astype(jnp.int32), pick(slot_of), pick(next_of))
    fill = jnp.concatenate([jnp.where(padded > counts, ends - MOE_TM, -1),
                            jnp.where(valid, -1, idx * MOE_TM)[eid.size // MOE_TM:]])
    return pos, tuple(a.astype(jnp.int32) for a in sched), fill.astype(jnp.int32)


def _dispatch_kernel(fill_ref, pos_ref, hp_ref, hs_ref, xs_hbm, zbuf, sem, fsem, *, n_p_tiles, n_fill):
    i = pl.program_id(0)

    def fill_copy(e):
        start = pl.multiple_of(fill_ref[e], MOE_TM)
        return pltpu.make_async_copy(zbuf, xs_hbm.at[pl.ds(start, MOE_TM)], fsem)

    @pl.when(i == 0)
    def _():
        zbuf[...] = jnp.zeros_like(zbuf)
        for e in range(n_fill):
            @pl.when(fill_ref[e] >= 0)
            def _(e=e):
                fill_copy(e).start()
        for e in range(n_fill):
            @pl.when(fill_ref[e] >= 0)
            def _(e=e):
                fill_copy(e).wait()

    def scatter(h_ref):
        for t in range(DISP_TB):
            for j in range(TOP_K):
                pltpu.make_async_copy(h_ref.at[pl.ds(t, 1)],
                                      xs_hbm.at[pl.ds(pos_ref[0, 0, j * DISP_TB + t], 1)], sem
                                      ).start(priority=j % 2)

    @pl.when(i < n_p_tiles)
    def _():
        scatter(hp_ref)

    @pl.when(i >= n_p_tiles)
    def _():
        scatter(hs_ref)

    for _ in range(TOP_K):
        pltpu.make_async_copy(hp_ref, xs_hbm.at[pl.ds(0, DISP_TB)], sem).wait()


def _dispatch(pos, fill, h2p, h2s, n_tiles):
    d = D_MODEL
    n_p, n_s = h2p.shape[0], h2s.shape[0]
    n = (n_p + n_s) // DISP_TB
    n_pt = n_p // DISP_TB
    pos_t = pos.reshape(n, DISP_TB, TOP_K).transpose(0, 2, 1).reshape(n, 1, TOP_K * DISP_TB)
    grid_spec = pltpu.PrefetchScalarGridSpec(
        num_scalar_prefetch=1,
        grid=(n,),
        in_specs=[pl.BlockSpec((1, 1, TOP_K * DISP_TB), lambda i, fill: (i, 0, 0), memory_space=pltpu.SMEM),
                  pl.BlockSpec((DISP_TB, d), lambda i, fill: (jnp.minimum(i, n_pt - 1), 0)),
                  pl.BlockSpec((DISP_TB, d), lambda i, fill: (jnp.maximum(i - n_pt, 0), 0))],
        out_specs=pl.BlockSpec(memory_space=pl.ANY),
        scratch_shapes=[pltpu.VMEM((MOE_TM, d), F32),
                        pltpu.SemaphoreType.DMA(()),
                        pltpu.SemaphoreType.DMA(())])
    return pl.pallas_call(
        functools.partial(_dispatch_kernel, n_p_tiles=n_pt, n_fill=fill.shape[0]),
        grid_spec=grid_spec,
        out_shape=jax.ShapeDtypeStruct((n_tiles * MOE_TM, d), F32),
        compiler_params=pltpu.CompilerParams(
            dimension_semantics=("arbitrary",), vmem_limit_bytes=VMEM_LIMIT),
        name="moe_dispatch",
    )(fill, pos_t, h2p, h2s)


def _experts_kernel(tile_ref, te_ref, tv_ref, tf_ref, ws_ref, nx_ref, x_ref,
                    wgu_hbm, bgu_ref, wd_hbm, bd_ref, ys_ref, wgu_buf, wd_buf, wgu_scr, wd_scr, sem):
    i = pl.program_id(0)

    def weight_copies(e, slot):
        return (pltpu.make_async_copy(wgu_hbm.at[e], wgu_buf.at[slot], sem.at[0, slot]),
                pltpu.make_async_copy(wd_hbm.at[e], wd_buf.at[slot], sem.at[1, slot]))

    @pl.when(i == 0)
    def _():
        for cp in weight_copies(te_ref[0], 0):
            cp.start()

    @pl.when(tf_ref[i] == 1)
    def _():
        slot = ws_ref[i]

        @pl.when(nx_ref[i] >= 0)
        def _():
            for cp in weight_copies(nx_ref[i], 1 - slot):
                cp.start()

        for cp in weight_copies(te_ref[i], slot):
            cp.wait()
        wgu_scr[...] = wgu_buf[slot].astype(BF16)
        wd_scr[...] = wd_buf[slot].astype(BF16)

    half = MOE_TM // 2

    def ffn(rows):
        gu = jnp.dot(x_ref[rows, :].astype(BF16), wgu_scr[...], preferred_element_type=F32) + bgu_ref[0]
        gate = jnp.minimum(gu[:, :D_MODEL], SWIGLU_LIMIT)
        up = jnp.clip(gu[:, D_MODEL:], -SWIGLU_LIMIT, SWIGLU_LIMIT)
        act = (up + 1.0) * gate * _sigmoid(SWIGLU_ALPHA * gate)
        ys_ref[rows, :] = jnp.dot(act.astype(BF16), wd_scr[...], preferred_element_type=F32) + bd_ref[0]

    @pl.when(tv_ref[i] >= 1)
    def _():
        ffn(slice(0, half))

    @pl.when(tv_ref[i] == 2)
    def _():
        ffn(slice(half, MOE_TM))

    @pl.when(tv_ref[i] == 0)
    def _():
        ys_ref[0:half, :] = jnp.zeros((half, D_MODEL), F32)

    @pl.when(tv_ref[i] <= 1)
    def _():
        ys_ref[half:, :] = jnp.zeros((half, D_MODEL), F32)


def _experts(xs, sched, p, n_tiles):
    d = D_MODEL
    bias = lambda w: pl.BlockSpec((1, 1, w), lambda i, tile, te, *_: (te[i], 0, 0))
    grid_spec = pltpu.PrefetchScalarGridSpec(
        num_scalar_prefetch=len(sched),
        grid=(n_tiles,),
        in_specs=[pl.BlockSpec((MOE_TM, d), lambda i, tile, *_: (tile[i], 0)),
                  pl.BlockSpec(memory_space=pl.ANY), bias(2 * d),
                  pl.BlockSpec(memory_space=pl.ANY), bias(d)],
        out_specs=pl.BlockSpec((MOE_TM, d), lambda i, *_: (i, 0)),
        scratch_shapes=[pltpu.VMEM((2, d, 2 * d), F32),
                        pltpu.VMEM((2, d, d), F32),
                        pltpu.VMEM((d, 2 * d), BF16),
                        pltpu.VMEM((d, d), BF16),
                        pltpu.SemaphoreType.DMA((2, 2))])
    return pl.pallas_call(
        _experts_kernel,
        grid_spec=grid_spec,
        out_shape=jax.ShapeDtypeStruct((n_tiles * MOE_TM, d), F32),
        compiler_params=pltpu.CompilerParams(
            dimension_semantics=("arbitrary",), vmem_limit_bytes=VMEM_LIMIT),
        name="moe_experts",
    )(*sched, xs, p['w_gate_up'], p['b_gate_up'], p['w_down'], p['b_down'])


def _combine_kernel(pos_ref, posn_ref, ys_hbm, ew_ref, x1_ref, mg_ref, gpost_ref, o_ref, buf, sem, *, n):
    i = pl.program_id(0)
    slot = i % 2

    def row_copy(row, j, t, slot_):
        return pltpu.make_async_copy(ys_hbm.at[pl.ds(row, 1)], buf.at[slot_, j, pl.ds(t, 1)], sem.at[slot_])

    def gather(src_ref, slot_):
        for t in range(COMB_TB):
            for j in range(TOP_K):
                row_copy(src_ref[0, 0, j * COMB_TB + t], j, t, slot_).start(priority=j % 2)

    @pl.when(i == 0)
    def _():
        gather(pos_ref, 0)

    for cur in range(2):
        @pl.when((i + 1 < n) & (slot == cur))
        def _(cur=cur):
            gather(posn_ref, 1 - cur)

    for j in range(TOP_K):
        pltpu.make_async_copy(ys_hbm.at[pl.ds(0, COMB_TB)], buf.at[slot, j], sem.at[slot]).wait()
    ew = ew_ref[...]
    f = sum(ew[:, j:j + 1] * buf[slot, j] for j in range(TOP_K))
    o_ref[...] = x1_ref[...] + mg_ref[0] * _rms(f, gpost_ref[...])


def _combine(ys, pos, ew, x1, mod_gate, p):
    b, s, d = x1.shape
    n_tok = b * s
    n_tiles = n_tok // COMB_TB
    per_batch = mod_gate.shape[0] > 1
    if per_batch:
        mod_gate = jnp.repeat(mod_gate, s // COMB_TB, axis=0)
    mod_map = (lambda i: (i, 0, 0)) if per_batch else (lambda i: (0, 0, 0))
    pos_t = pos.reshape(n_tiles, COMB_TB, TOP_K).transpose(0, 2, 1).reshape(n_tiles, 1, TOP_K * COMB_TB)
    smem_tile = lambda f: pl.BlockSpec((1, 1, TOP_K * COMB_TB), f, memory_space=pltpu.SMEM)
    out = pl.pallas_call(
        functools.partial(_combine_kernel, n=n_tiles),
        grid=(n_tiles,),
        in_specs=[smem_tile(lambda i: (i, 0, 0)),
                  smem_tile(lambda i: (jnp.minimum(i + 1, n_tiles - 1), 0, 0)),
                  pl.BlockSpec(memory_space=pl.ANY),
                  pl.BlockSpec((COMB_TB, LANES), lambda i: (i, 0)),
                  pl.BlockSpec((COMB_TB, d), lambda i: (i, 0)),
                  pl.BlockSpec((1, 1, d), mod_map),
                  pl.BlockSpec((1, d), lambda i: (0, 0))],
        out_specs=pl.BlockSpec((COMB_TB, d), lambda i: (i, 0)),
        out_shape=jax.ShapeDtypeStruct((n_tok, d), F32),
        scratch_shapes=[pltpu.VMEM((2, TOP_K, COMB_TB, d), F32),
                        pltpu.SemaphoreType.DMA((2,))],
        compiler_params=pltpu.CompilerParams(
            dimension_semantics=("arbitrary",), vmem_limit_bytes=VMEM_LIMIT),
        name="moe_combine",
    )(pos_t, pos_t, ys, ew.reshape(n_tok, LANES), x1.reshape(n_tok, d), mod_gate, p['g_post_ffn'])
    return out.reshape(b, s, d)


def _rope_tables(n_tokens):
    rows = n_tokens // GRID_W
    row = jnp.repeat(jnp.arange(rows, dtype=F32), GRID_W)
    col = jnp.tile(jnp.arange(GRID_W, dtype=F32), rows)
    inv = ROPE_THETA ** (-jnp.arange(ROPE_FREQS, dtype=F32) / ROPE_FREQS)
    ang_r = row[:, None] * inv[None, :]
    ang_c = col[:, None] * inv[None, :]
    ang = jnp.concatenate([ang_r, ang_r, ang_c, ang_c] * 2, axis=-1)
    return jnp.cos(ang), jnp.sin(ang)


def _stream(x, mods, proj_w, lam, p, rope, ctx, s0, cnt0, proj_dtype, tm_proj, tm_tok):
    b, s, d = x.shape
    xf = x if mods[0].shape[0] > 1 else x.reshape(1, b * s, d)
    proj = _norm_proj(xf, p['g_pre_mix'], mods[1], mods[0], proj_w, tm_proj, PROJ_TN, proj_dtype)
    proj = proj.reshape(b, s, C_IN)
    oa = _diff_attention(proj, lam, p['g_subln'], rope, ctx, tq=min(s, 4 * ATT_SUB),
                         hps=H_A if s <= ATT_SUB else 1)
    r, v, kkn, kd, ba, lw, bonus, gate = _rwkv_prep(proj, p, min(s, 256))
    y0, y1, sfin = _rwkv_scan(r, v, kkn, kd, ba, lw, s0)
    tb, tsq = xf.shape[0], xf.shape[1]
    x1, h2, route, rank, ew, cnt = _merge_out(
        *(a.reshape(tb, tsq, a.shape[-1]) for a in (x, oa, y0, y1, bonus, gate, proj)),
        (mods[2], mods[4], mods[3]), p, cnt0, tm_tok)
    n = b * s
    route = (route.reshape(n, LANES)[:, :TOP_K], rank.reshape(n, LANES)[:, :N_EXPERTS])
    return (x1.reshape(b, s, d), h2.reshape(n, d), route, ew.reshape(n, LANES), cnt,
            proj, sfin)


def kernel(x_prompt, x_sample, cache_k, cache_v, state_rwkv, c, c_ctx, w_ada, b_ada, g_pre_mix, g_post_mix, g_pre_ffn, g_post_ffn, w_in, mu_prev, mu_next, lam, g_subln, k_k, k_a, r_k, w0, w_up, a0, a_up, g_up, gn_w, gn_b, w_out, w_router, b_router, w_gate_up, b_gate_up, w_down, b_down):
    l = 0
    d = D_MODEL
    bp, sp, _ = x_prompt.shape
    bs, ss, _ = x_sample.shape

    n_cond = 1 + bs
    rows = -(-n_cond // SUBLANES) * SUBLANES
    cond = jnp.concatenate([c_ctx[None, :], c, jnp.zeros((rows - n_cond, d), F32)], axis=0)
    mod = _modulation(cond, w_ada[l], b_ada[l][None, :])
    mods_p = [mod[0:1, i * d:(i + 1) * d].reshape(1, 1, d) for i in range(N_MOD)]
    mods_s = [mod[1:1 + bs, i * d:(i + 1) * d].reshape(bs, 1, d) for i in range(N_MOD)]

    w_proj = w_in[l].astype(BF16)

    lq = lam[l]
    lam_val = (jnp.exp(jnp.sum(lq[0] * lq[1])) - jnp.exp(jnp.sum(lq[2] * lq[3])) + LAM_INIT).reshape(1)

    head = jnp.arange(d) // N_B
    wr = jnp.concatenate([w_router[l], jnp.zeros((d, LANES - N_EXPERTS), F32)], axis=1)
    br = jnp.concatenate([b_router[l], jnp.full((LANES - N_EXPERTS,), -jnp.inf, F32)])[None, :]
    mup, mun = mu_prev[l][None, :], mu_next[l][None, :]
    p = {
        'g_pre_mix': g_pre_mix[l][None, :], 'g_post_mix': g_post_mix[l][None, :],
        'g_pre_ffn': g_pre_ffn[l][None, :], 'g_post_ffn': g_post_ffn[l][None, :],
        'g_subln': g_subln[l][None, :],
        'mu_prev_main': mup[:, :3 * d], 'mu_next_main': mun[:, :3 * d],
        'mu_prev_lora': mup[:, 3 * d:], 'mu_next_lora': mun[:, 3 * d:],
        'k_k': k_k[l][None, :], 'k_a': k_a[l][None, :], 'r_k': r_k[l].reshape(1, d),
        'w0': w0[l].reshape(1, 2 * d),
        'w_up': jnp.concatenate([w_up[l, 0], w_up[l, 1]], axis=1).astype(BF16),
        'a0': a0[l].reshape(1, 2 * d),
        'a_up': jnp.concatenate([a_up[l, 0], a_up[l, 1]], axis=1).astype(BF16),
        'g_up': g_up[l].astype(BF16),
        'head_ind': (head[:, None] == jnp.arange(LANES)[None, :]).astype(BF16),
        'head_ind_t': (jnp.arange(LANES)[:, None] == head[None, :]).astype(BF16),
        'gn_w': gn_w[l][None, :], 'gn_b': gn_b[l][None, :],
        'w_out': w_out[l].astype(BF16),
        'w_router': wr, 'b_router': br,
        'w_gate_up': w_gate_up[l], 'b_gate_up': b_gate_up[l][:, None, :],
        'w_down': w_down[l], 'b_down': b_down[l][:, None, :],
    }

    rope = _rope_tables(ss)
    ctx = (cache_k[:, l].reshape(bs, -1, d), cache_v[:, l].reshape(bs, -1, d))

    cnt0 = jnp.zeros((1, LANES), F32)
    x1p, h2p, route_p, ewp, cnt_p, proj_p, sfin = _stream(
        x_prompt, mods_p, w_proj, lam_val, p, None, None, None, cnt0, F32, 2048, 512)
    x1s, h2s, route_s, ews, cnt_s, _, _ = _stream(
        x_sample, mods_s, w_proj, lam_val, p, rope, ctx, state_rwkv[:, l], cnt_p, BF16, 2048, 512)

    n_p, n_s = bp * sp, bs * ss
    n_tok = n_p + n_s
    eid, rank = (jnp.concatenate([a, b_], axis=0) for a, b_ in zip(route_p, route_s))
    counts = cnt_s[0, :N_EXPERTS].astype(jnp.int32)
    n_tiles = -(-(n_tok * TOP_K + N_EXPERTS * (MOE_TM - 1)) // MOE_TM)
    pos, sched, fill = _route_metadata(eid, rank, counts, n_tiles)
    xs = _dispatch(pos, fill, h2p, h2s, n_tiles)
    rows = _experts(xs, sched, p, n_tiles)
    yp = _combine(rows, pos[:n_p], ewp, x1p, mods_p[5], p)
    ys = _combine(rows, pos[n_p:], ews, x1s, mods_s[5], p)

    new_k = proj_p[:, :, COL_K:COL_K + d].reshape(bp, 1, sp, H_A, 2, HD_A)
    new_v = proj_p[:, :, COL_V:COL_V + d].reshape(bp, 1, sp, H_A, DV_A)
    return (yp, ys, new_k, new_v, sfin[:, None])
```

```python
import functools
import math

import jax
import jax.numpy as jnp
from jax import lax
from jax.experimental import pallas as pl
from jax.experimental.pallas import tpu as pltpu

F32 = jnp.float32
BF16 = jnp.bfloat16

D_MODEL = 1024
GRID_W = 64
HD_A = 64
DV_A = 2 * HD_A
H_A = D_MODEL // DV_A
N_B = 64
H_B = D_MODEL // N_B
LORA_W = 64
LORA_A = 64
LORA_G = 128
N_EXPERTS = 32
TOP_K = 4
SWIGLU_LIMIT = 7.0
SWIGLU_ALPHA = 1.702
ROPE_THETA = 10000.0
ROPE_FREQS = HD_A // 4
NORM_EPS = 1e-6
GN_EPS = 64e-5
L2_EPS = 1e-12
ATTN_SCALE = HD_A ** -0.5
N_MOD = 6
LAM_INIT = 0.8 - 0.6 * math.exp(-0.3 * 0)
DECAY_SCALE = -math.exp(-0.5)

LANES = 128
SUBLANES = 8
CHUNK = 64
SCAN_NB = 1
MOE_TM = 512
PROJ_TN = 1408
COMB_TB = 256
DISP_TB = 512
ATT_SUB = 256
LOG2_E = math.log2(math.e)
VMEM_LIMIT = 56 * 1024 * 1024

C_LORA = LORA_W + LORA_A + LORA_G
COL_Q, COL_K, COL_V, COL_R, COL_KR, COL_VR, COL_LORA = (i * D_MODEL for i in range(7))
COL_G0 = COL_LORA + C_LORA
COL_G1 = COL_G0 + D_MODEL
C_IN = COL_G1 + D_MODEL
GATE_W = 256


def _sigmoid(x):
    return 1.0 / (1.0 + jnp.exp(-x))


def _dot(a, b):
    return jnp.dot(a.astype(BF16), b.astype(BF16), preferred_element_type=F32)


def _dot_nt(a, b):
    return lax.dot_general(a.astype(BF16), b.astype(BF16), (((1,), (1,)), ((), ())),
                           preferred_element_type=F32)


def _dot_tn(a, b):
    return lax.dot_general(a.astype(BF16), b.astype(BF16), (((0,), (0,)), ((), ())),
                           preferred_element_type=F32)


def _split3(x):
    hi = x.astype(BF16)
    r1 = x - hi.astype(F32)
    mid = r1.astype(BF16)
    lo = (r1 - mid.astype(F32)).astype(BF16)
    return hi, mid, lo


def _headsum(x, ind):
    e, et = ind
    s = jnp.dot(x.astype(BF16), e, preferred_element_type=F32)
    s_hi = s.astype(BF16)
    s_lo = (s - s_hi.astype(F32)).astype(BF16)
    return jnp.dot(s_hi, et, preferred_element_type=F32) + jnp.dot(s_lo, et, preferred_element_type=F32)


def _rms(x, g):
    return x * lax.rsqrt(jnp.mean(x * x, axis=-1, keepdims=True) + NORM_EPS) * g


def _mod_kernel(c_ref, w_ref, b_ref, o_ref):
    c = c_ref[...]
    s = c * _sigmoid(c)
    o_ref[...] = _dot(s, w_ref[...]) + b_ref[...]


def _modulation(cond, w_ada, b_ada):
    rows, d = cond.shape
    n = w_ada.shape[1]
    tn = 768
    return pl.pallas_call(
        _mod_kernel,
        grid=(n // tn,),
        in_specs=[pl.BlockSpec((rows, d), lambda j: (0, 0)),
                  pl.BlockSpec((d, tn), lambda j: (0, j)),
                  pl.BlockSpec((1, tn), lambda j: (0, j))],
        out_specs=pl.BlockSpec((rows, tn), lambda j: (0, j)),
        out_shape=jax.ShapeDtypeStruct((rows, n), F32),
        name="modulation",
    )(cond, w_ada, b_ada)


def _proj_kernel(x_ref, g_ref, sc_ref, sh_ref, w_ref, o_ref, h_scr):
    @pl.when(pl.program_id(2) == 0)
    def _():
        h = _rms(x_ref[0], g_ref[...]) * (1.0 + sc_ref[0]) + sh_ref[0]
        h_scr[...] = h.astype(BF16)

    o_ref[0] = jnp.dot(h_scr[...], w_ref[...], preferred_element_type=F32).astype(o_ref.dtype)


def _norm_proj(x, g, scale, shift, w, tm, tn, out_dtype):
    b, s, d = x.shape
    n = w.shape[1]
    per_batch = scale.shape[0] > 1
    mod_map = (lambda bi, i, j: (bi, 0, 0)) if per_batch else (lambda bi, i, j: (0, 0, 0))
    return pl.pallas_call(
        _proj_kernel,
        grid=(b, s // tm, n // tn),
        in_specs=[pl.BlockSpec((1, tm, d), lambda bi, i, j: (bi, i, 0)),
                  pl.BlockSpec((1, d), lambda bi, i, j: (0, 0)),
                  pl.BlockSpec((1, 1, d), mod_map),
                  pl.BlockSpec((1, 1, d), mod_map),
                  pl.BlockSpec((d, tn), lambda bi, i, j: (0, j))],
        out_specs=pl.BlockSpec((1, tm, tn), lambda bi, i, j: (bi, i, j)),
        out_shape=jax.ShapeDtypeStruct((b, s, n), out_dtype),
        scratch_shapes=[pltpu.VMEM((tm, d), BF16)],
        compiler_params=pltpu.CompilerParams(
            dimension_semantics=("arbitrary", "arbitrary", "arbitrary"),
            vmem_limit_bytes=VMEM_LIMIT),
        name="norm_proj",
    )(x, g, scale, shift, w)


def _rope(x, cos, sin):
    lane = lax.broadcasted_iota(jnp.int32, x.shape, 1)
    even = (lane // ROPE_FREQS) % 2 == 0
    rot = jnp.where(even, -pltpu.roll(x, LANES - ROPE_FREQS, 1), pltpu.roll(x, ROPE_FREQS, 1))
    return x * cos + rot * sin


def _attn_kernel(*refs, use_rope, use_ctx, s_new):
    it = iter(refs)
    lam_ref = next(it)
    q_ref, k_ref, v_ref = next(it), next(it), next(it)
    if use_rope:
        cq_ref, sq_ref, ck_ref, sk_ref = next(it), next(it), next(it), next(it)
    if use_ctx:
        ctxk_ref, ctxv_ref = next(it), next(it)
    g_ref = next(it)
    o_ref = next(it)
    k_scr, vt_scr = next(it), next(it)

    hps = q_ref.shape[-1] // DV_A
    head_lanes = [slice(hh * DV_A, (hh + 1) * DV_A) for hh in range(hps)]

    @pl.when(pl.program_id(2) == 0)
    def _():
        for hh, hl in enumerate(head_lanes):
            k = k_ref[0][:, hl].astype(F32)
            if use_rope:
                k = _rope(k, ck_ref[...], sk_ref[...])
            k_scr[hh, 0:s_new, :] = k.astype(BF16)
            vt_scr[hh, :, 0:s_new] = v_ref[0][:, hl].astype(F32).T.astype(BF16)
            if use_ctx:
                k_scr[hh, s_new:, :] = ctxk_ref[0][:, hl].astype(BF16)
                vt_scr[hh, :, s_new:] = ctxv_ref[0][:, hl].T.astype(BF16)

    lam = lam_ref[0]
    n_sub = q_ref.shape[1] // ATT_SUB
    lane = lax.broadcasted_iota(jnp.int32, (ATT_SUB, DV_A), 1)
    sts = []
    for hh, hl in enumerate(head_lanes):
        q = q_ref[0][:, hl].astype(F32)
        if use_rope:
            q = _rope(q, cq_ref[...], sq_ref[...])
        q = q * (ATTN_SCALE * LOG2_E)
        for t in range(n_sub):
            for m in range(2):
                in_map = (lane >= HD_A) if m else (lane < HD_A)
                qm = jnp.where(in_map, q[t * ATT_SUB:(t + 1) * ATT_SUB], 0.0)
                sts.append((hh, _dot_nt(k_scr[hh], qm)))
    outs = []
    for hh, st in sts:
        et = jnp.exp2(st - jnp.max(st, axis=0, keepdims=True))
        l = jnp.sum(et, axis=0, keepdims=True)
        ot = jnp.dot(vt_scr[hh], et.astype(BF16), preferred_element_type=F32)
        outs.append(ot / l)
    for hh, hl in enumerate(head_lanes):
        for t in range(n_sub):
            u = 2 * (hh * n_sub + t)
            o = (outs[u] - lam * outs[u + 1]).T
            o = o * lax.rsqrt(jnp.mean(o * o, axis=-1, keepdims=True) + NORM_EPS)
            o_ref[0, t * ATT_SUB:(t + 1) * ATT_SUB, hl] = o * g_ref[:, hl] * (1.0 - LAM_INIT)


def _diff_attention(proj, lam, g_subln, rope=None, ctx=None, tq=256, hps=1):
    b, s, _ = proj.shape
    use_rope, use_ctx = rope is not None, ctx is not None
    s_tot = s + (ctx[0].shape[1] if use_ctx else 0)
    w = hps * DV_A
    in_specs = [pl.BlockSpec(memory_space=pltpu.SMEM),
                pl.BlockSpec((1, tq, w), lambda bi, h, i: (bi, i, COL_Q // w + h)),
                pl.BlockSpec((1, s, w), lambda bi, h, i: (bi, 0, COL_K // w + h)),
                pl.BlockSpec((1, s, w), lambda bi, h, i: (bi, 0, COL_V // w + h))]
    args = [lam, proj, proj, proj]
    if use_rope:
        cos, sin = rope
        in_specs += [pl.BlockSpec((tq, DV_A), lambda bi, h, i: (i, 0)),
                     pl.BlockSpec((tq, DV_A), lambda bi, h, i: (i, 0)),
                     pl.BlockSpec((s, DV_A), lambda bi, h, i: (0, 0)),
                     pl.BlockSpec((s, DV_A), lambda bi, h, i: (0, 0))]
        args += [cos, sin, cos, sin]
    if use_ctx:
        p = ctx[0].shape[1]
        in_specs += [pl.BlockSpec((1, p, w), lambda bi, h, i: (bi, 0, h)),
                     pl.BlockSpec((1, p, w), lambda bi, h, i: (bi, 0, h))]
        args += [ctx[0], ctx[1]]
    in_specs.append(pl.BlockSpec((1, w), lambda bi, h, i: (0, h)))
    args.append(g_subln)
    return pl.pallas_call(
        functools.partial(_attn_kernel, use_rope=use_rope, use_ctx=use_ctx, s_new=s),
        grid=(b, H_A // hps, s // tq),
        in_specs=in_specs,
        out_specs=pl.BlockSpec((1, tq, w), lambda bi, h, i: (bi, i, h)),
        out_shape=jax.ShapeDtypeStruct((b, s, D_MODEL), F32),
        scratch_shapes=[pltpu.VMEM((hps, s_tot, DV_A), BF16), pltpu.VMEM((hps, DV_A, s_tot), BF16)],
        compiler_params=pltpu.CompilerParams(
            dimension_semantics=("arbitrary", "arbitrary", "arbitrary"),
            vmem_limit_bytes=VMEM_LIMIT),
        name="diff_attention",
    )(*args)


def _shifted(x_ref, p_ref, n_ref, mup, mun, first, last):
    x = x_ref[0].astype(F32)
    ts = x.shape[0]
    halo = p_ref.shape[1]
    row = lax.broadcasted_iota(jnp.int32, x.shape, 0)
    prev_row = p_ref[0][halo - 1:halo, :].astype(F32) * first
    next_row = n_ref[0][0:1, :].astype(F32) * last
    prev = jnp.where(row == 0, prev_row, pltpu.roll(x, 1, 0))
    nxt = jnp.where(row == ts - 1, next_row, pltpu.roll(x, ts - 1, 0))
    return x + mup * (prev - x) + mun * (nxt - x)


def _prep_kernel(r_ref, rp_ref, rn_ref, k_ref, kp_ref, kn_ref, v_ref, vp_ref, vn_ref,
                 l_ref, lp_ref, ln_ref, mup_ref, mun_ref, mupl_ref, munl_ref,
                 kk_ref, ka_ref, rk_ref, w0_ref, wup_ref, a0_ref, aup_ref, gup_ref, e_ref, et_ref,
                 ro_ref, vo_ref, kko_ref, kd_ref, ba_ref, lw_ref, bonus_ref, gate_ref, *, n_tiles):
    i = pl.program_id(1)
    first = (i > 0).astype(F32)
    last = (i < n_tiles - 1).astype(F32)
    mup, mun = mup_ref[...], mun_ref[...]
    d = D_MODEL
    r = _shifted(r_ref, rp_ref, rn_ref, mup[:, 0:d], mun[:, 0:d], first, last)
    k = _shifted(k_ref, kp_ref, kn_ref, mup[:, d:2 * d], mun[:, d:2 * d], first, last)
    v = _shifted(v_ref, vp_ref, vn_ref, mup[:, 2 * d:3 * d], mun[:, 2 * d:3 * d], first, last)
    lo = _shifted(l_ref, lp_ref, ln_ref, mupl_ref[...], munl_ref[...], first, last)
    xw = lo[:, 0:LORA_W]
    xa = lo[:, LORA_W:LORA_W + LORA_A]
    xg = lo[:, LORA_W + LORA_A:]
    g = (e_ref[...], et_ref[...])

    kk = k * kk_ref[...]
    nrm = jnp.sqrt(_headsum(kk * kk, g))
    kkn = kk / jnp.maximum(nrm, L2_EPS)
    ro_ref[0] = r
    vo_ref[0] = v
    kko_ref[0] = kkn
    gate_ref[0] = _dot(_sigmoid(xg), gup_ref[...])

    wlog = w0_ref[...] + _dot(jnp.tanh(xw), wup_ref[...])
    alog = a0_ref[...] + _dot(xa, aup_ref[...])
    ka = ka_ref[...]
    rrk = r * rk_ref[...]
    dots = None
    for dr in range(2):
        sl = slice(dr * d, (dr + 1) * d)
        lw_ref[dr, 0] = DECAY_SCALE * _sigmoid(wlog[:, sl])
        a = _sigmoid(alog[:, sl])
        kd = k * (1.0 + (a - 1.0) * ka)
        kd_ref[dr, 0] = kd
        ba_ref[dr, 0] = kkn * a
        t = rrk * kd
        dots = t if dots is None else dots + t
    bonus_ref[0] = _headsum(dots, g) * v


def _rwkv_prep(proj, p, ts):
    b, s, _ = proj.shape
    d = D_MODEL
    nt = s // ts
    halo = SUBLANES * (4 // proj.dtype.itemsize)
    hb = ts // halo
    nhb = s // halo

    def main(col, w):
        return pl.BlockSpec((1, ts, w), lambda bi, i: (bi, i, col // w))

    def prev(col, w):
        return pl.BlockSpec((1, halo, w), lambda bi, i: (bi, jnp.maximum(i * hb - 1, 0), col // w))

    def nxt(col, w):
        return pl.BlockSpec((1, halo, w),
                            lambda bi, i: (bi, jnp.minimum((i + 1) * hb, nhb - 1), col // w))

    def full(a):
        return pl.BlockSpec(a.shape, lambda bi, i: (0,) * a.ndim)

    in_specs, args = [], []
    for col, w in ((COL_R, d), (COL_KR, d), (COL_VR, d), (COL_LORA, C_LORA)):
        in_specs += [main(col, w), prev(col, w), nxt(col, w)]
        args += [proj, proj, proj]
    consts = [p['mu_prev_main'], p['mu_next_main'], p['mu_prev_lora'], p['mu_next_lora'],
              p['k_k'], p['k_a'], p['r_k'], p['w0'], p['w_up'], p['a0'], p['a_up'], p['g_up'],
              p['head_ind'], p['head_ind_t']]
    in_specs += [full(a) for a in consts]
    args += consts
    tok = pl.BlockSpec((1, ts, d), lambda bi, i: (bi, i, 0))
    tok2 = pl.BlockSpec((2, 1, ts, d), lambda bi, i: (0, bi, i, 0))
    one = jax.ShapeDtypeStruct((b, s, d), F32)
    two = jax.ShapeDtypeStruct((2, b, s, d), F32)
    return pl.pallas_call(
        functools.partial(_prep_kernel, n_tiles=nt),
        grid=(b, nt),
        in_specs=in_specs,
        out_specs=[tok, tok, tok, tok2, tok2, tok2, tok, tok],
        out_shape=[one, one, one, two, two, two, one, one],
        compiler_params=pltpu.CompilerParams(
            dimension_semantics=("arbitrary", "arbitrary"),
            vmem_limit_bytes=VMEM_LIMIT),
        name="rwkv_prep",
    )(*args)


def _scan_masks(reverse):
    n = CHUNK
    tt = lax.broadcasted_iota(jnp.int32, (n, n), 0)
    ss = lax.broadcasted_iota(jnp.int32, (n, n), 1)
    hi, lo = (ss, tt) if reverse else (tt, ss)
    tt2 = lax.broadcasted_iota(jnp.int32, (n, 2 * n), 0)
    col2 = lax.broadcasted_iota(jnp.int32, (n, 2 * n), 1)
    ss2 = col2 % n
    hi2, lo2 = (ss2, tt2) if reverse else (tt2, ss2)
    levels = []
    blk = 2
    while blk < n:
        levels.append((hi // (2 * blk) == lo // (2 * blk)) & ((hi // blk) % 2 == 1) & ((lo // blk) % 2 == 0))
        blk *= 2
    return dict(strict=hi > lo, incl=hi >= lo, eye=(tt == ss).astype(F32),
                first=(hi // 2 == lo // 2) & (hi > lo), levels=levels,
                strict_r=(hi2 > lo2) & (col2 >= n),
                incl2=hi2 >= lo2)


def _scan_operands(reverse, incl, lw, r, v, kkn, kd, ba):
    n = CHUNK
    tri = jnp.where(incl, 1.0, 0.0).astype(BF16)
    cum = sum(jnp.dot(tri, part, preferred_element_type=F32) for part in _split3(lw))
    tot = cum[0:1, :] if reverse else cum[n - 1:n, :]
    g_inv = jnp.exp(-cum)
    g_rest = jnp.exp(tot - cum)
    return dict(a=(kkn * jnp.exp(cum - lw)).astype(BF16), r=(r * jnp.exp(cum)).astype(BF16),
                b=(ba * g_inv).astype(BF16), k=(kd * g_inv).astype(BF16),
                bh=(ba * g_rest).astype(BF16), kh=(kd * g_rest).astype(BF16),
                v=v.astype(BF16), g_tot=jnp.exp(tot))


def _scan_kernel(*refs, has_s0, n_chunks):
    it = iter(refs)
    tok_refs = [[next(it) for _ in range(3)] for _ in range(2)]
    dir_refs = [[next(it) for _ in range(3)] for _ in range(2)]
    s0_ref = next(it) if has_s0 else None
    y_refs = [next(it), next(it)]
    sf_ref = next(it)
    s_scr = next(it)
    c = pl.program_id(1)

    @pl.when(c == 0)
    def _():
        if has_s0:
            s_scr[...] = s0_ref[...]
        else:
            s_scr[...] = jnp.zeros_like(s_scr)

    n = CHUNK
    nb = s_scr.shape[0]
    masks = [_scan_masks(dr == 1) for dr in range(2)]
    ops = {}
    for bb in range(nb):
        for dr in range(2):
            r_ref, v_ref, kk_ref = tok_refs[dr]
            kd_ref, ba_ref, lw_ref = dir_refs[dr]
            ops[bb, dr] = _scan_operands(dr == 1, masks[dr]['incl'], lw_ref[0, bb], r_ref[bb], v_ref[bb],
                                         kk_ref[bb], kd_ref[0, bb], ba_ref[0, bb])

    units = [(bb, dr, h) for bb in range(nb) for dr in range(2) for h in range(H_B)]
    idx = range(len(units))
    sls = [slice(h * N_B, (h + 1) * N_B) for _, _, h in units]
    op = lambda u, name: ops[units[u][:2]][name][:, sls[u]]
    mk = lambda u, name: masks[units[u][1]][name]
    s_old = [s_scr[bb, dr, h] for bb, dr, h in units]
    ar = [jnp.concatenate([op(u, 'a'), op(u, 'r')], axis=0) for u in idx]
    bk = [jnp.concatenate([op(u, 'b'), op(u, 'k')], axis=0) for u in idx]
    bkh = [jnp.concatenate([op(u, 'bh'), op(u, 'kh')], axis=0) for u in idx]
    vh = [op(u, 'v') for u in idx]

    gram = [_dot_nt(ar[u], bk[u]) for u in idx]
    ars = [_dot_nt(ar[u], s_old[u]) for u in idx]
    nmat = [jnp.where(mk(u, 'strict'), gram[u][0:n, 0:n], 0.0) for u in idx]
    mak = [jnp.where(mk(u, 'strict_r'), gram[u][0:n, :], 0.0) for u in idx]
    pr = [jnp.where(mk(u, 'incl2'), gram[u][n:, :], 0.0) for u in idx]
    z = [ars[u][0:n] + _dot(mak[u], jnp.concatenate([vh[u], vh[u]], axis=0)) for u in idx]
    x = [mk(u, 'eye') - jnp.where(mk(u, 'first'), nmat[u], 0.0) for u in idx]
    for lvl in range(len(masks[0]['levels'])):
        xl = [_dot(x[u], jnp.where(mk(u, 'levels')[lvl], nmat[u], 0.0)) for u in idx]
        x = [x[u] - _dot(xl[u], x[u]) for u in idx]
    uu = [-_dot(x[u], z[u]) for u in idx]
    uv = [jnp.concatenate([uu[u].astype(BF16), vh[u]], axis=0) for u in idx]
    y = [ars[u][n:] + _dot(pr[u], uv[u]) for u in idx]
    s_new = [s_old[u] * op(u, 'g_tot') + _dot_tn(uv[u], bkh[u]) for u in idx]
    for u, (bb, dr, h) in enumerate(units):
        y_refs[dr][bb, :, sls[u]] = y[u]
    for u, (bb, dr, h) in enumerate(units):
        s_scr[bb, dr, h] = s_new[u]

    @pl.when(c == n_chunks - 1)
    def _():
        sf_ref[...] = s_scr[...]


def _rwkv_scan(r, v, kkn, kd, ba, lw, s0):
    b, s, d = r.shape
    nc = s // CHUNK
    nb = SCAN_NB
    fwd = pl.BlockSpec((nb, CHUNK, d), lambda bi, c: (bi, c, 0))
    bwd = pl.BlockSpec((nb, CHUNK, d), lambda bi, c: (bi, nc - 1 - c, 0))
    fwd2 = pl.BlockSpec((1, nb, CHUNK, d), lambda bi, c: (0, bi, c, 0))
    bwd2 = pl.BlockSpec((1, nb, CHUNK, d), lambda bi, c: (1, bi, nc - 1 - c, 0))
    state = pl.BlockSpec((nb, 2, H_B, N_B, N_B), lambda bi, c: (bi, 0, 0, 0, 0))
    in_specs = [fwd] * 3 + [bwd] * 3 + [fwd2] * 3 + [bwd2] * 3
    args = [r, v, kkn, r, v, kkn, kd, ba, lw, kd, ba, lw]
    if s0 is not None:
        in_specs.append(state)
        args.append(s0)
    return pl.pallas_call(
        functools.partial(_scan_kernel, has_s0=s0 is not None, n_chunks=nc),
        grid=(b // nb, nc),
        in_specs=in_specs,
        out_specs=[fwd, bwd, state],
        out_shape=[jax.ShapeDtypeStruct((b, s, d), F32),
                   jax.ShapeDtypeStruct((b, s, d), F32),
                   jax.ShapeDtypeStruct((b, 2, H_B, N_B, N_B), F32)],
        scratch_shapes=[pltpu.VMEM((nb, 2, H_B, N_B, N_B), F32)],
        compiler_params=pltpu.CompilerParams(
            dimension_semantics=("arbitrary", "arbitrary"),
            vmem_limit_bytes=VMEM_LIMIT),
        name="rwkv_scan",
    )(*args)


def _merge_kernel(x_ref, oa_ref, y0_ref, y1_ref, bonus_ref, gate_ref,
                  g0a_ref, g0b_ref, g0c_ref, g0d_ref, g1a_ref, g1b_ref, g1c_ref, g1d_ref,
                  mg_ref, sc_ref, sh_ref, gnw_ref, gnb_ref, e_ref, et_ref, wout_ref,
                  gpost_ref, gpre_ref, wr_ref, br_ref, cnt0_ref,
                  x1_ref, h2_ref, route_ref, rank_ref, ew_ref, cnt_ref, cnt_scr):
    @pl.when((pl.program_id(0) == 0) & (pl.program_id(1) == 0))
    def _():
        cnt_scr[...] = cnt0_ref[...]

    g = (e_ref[...], et_ref[...])
    y = y0_ref[0] + y1_ref[0]
    mu = _headsum(y, g) * (1.0 / N_B)
    yc = y - mu
    var = _headsum(yc * yc, g) * (1.0 / N_B)
    yn = yc * lax.rsqrt(var + GN_EPS) * gnw_ref[...] + gnb_ref[...]
    ob = (yn + bonus_ref[0]) * gate_ref[0]
    g0 = jnp.concatenate([r[0].astype(F32) for r in (g0a_ref, g0b_ref, g0c_ref, g0d_ref)], axis=1)
    g1 = jnp.concatenate([r[0].astype(F32) for r in (g1a_ref, g1b_ref, g1c_ref, g1d_ref)], axis=1)
    merged = _sigmoid(g0) * oa_ref[0] + _sigmoid(g1) * ob
    out = _dot(merged, wout_ref[...])
    x1 = x_ref[0] + mg_ref[0] * _rms(out, gpost_ref[...])
    x1_ref[0] = x1
    h2 = _rms(x1, gpre_ref[...]) * (1.0 + sc_ref[0]) + sh_ref[0]
    h2_ref[0] = h2

    h_hi = h2.astype(BF16)
    h_lo = (h2 - h_hi.astype(F32)).astype(BF16)
    wr = wr_ref[...]
    w_hi = wr.astype(BF16)
    w_lo = (wr - w_hi.astype(F32)).astype(BF16)
    logits = (jnp.dot(h_hi, w_hi, preferred_element_type=F32) + jnp.dot(h_hi, w_lo, preferred_element_type=F32)
              + jnp.dot(h_lo, w_hi, preferred_element_type=F32)) + br_ref[...]
    lane = lax.broadcasted_iota(jnp.int32, logits.shape, 1)
    work = logits
    top = None
    picks = []
    for _ in range(TOP_K):
        mx = jnp.max(work, axis=-1, keepdims=True)
        idx = jnp.min(jnp.where(work == mx, lane, LANES), axis=-1, keepdims=True)
        if top is None:
            top = mx
        picks.append((idx, jnp.exp(mx - top)))
        work = jnp.where(lane == idx, -jnp.inf, work)
    denom = sum(e for _, e in picks)
    tm = logits.shape[0]
    onehot = jnp.zeros_like(logits)
    for idx, _ in picks:
        onehot = onehot + jnp.where(lane == idx, 1.0, 0.0)
    rr = lax.broadcasted_iota(jnp.int32, (tm, tm), 0)
    cc = lax.broadcasted_iota(jnp.int32, (tm, tm), 1)
    before = cnt_scr[...] + _dot(jnp.where(rr > cc, 1.0, 0.0), onehot)
    route = jnp.zeros(logits.shape, jnp.int32)
    ew = jnp.zeros_like(logits)
    for j, (idx, e) in enumerate(picks):
        route = jnp.where(lane == j, idx, route)
        ew = jnp.where(lane == j, e / denom, ew)
    route_ref[0] = route
    rank_ref[0] = jnp.where(onehot > 0.0, before, 0.0).astype(jnp.int32)
    ew_ref[0] = ew
    cnt_scr[...] = cnt_scr[...] + jnp.sum(onehot, axis=0, keepdims=True)
    cnt_ref[...] = cnt_scr[...]


def _merge_out(x, oa, y0, y1, bonus, gate, proj, mods, p, cnt0, tm):
    b, s, d = x.shape
    per_batch = mods[0].shape[0] > 1
    mod_map = (lambda bi, i: (bi, 0, 0)) if per_batch else (lambda bi, i: (0, 0, 0))
    tok = pl.BlockSpec((1, tm, d), lambda bi, i: (bi, i, 0))

    def full(a):
        return pl.BlockSpec(a.shape, lambda bi, i: (0,) * a.ndim)

    consts = [p['gn_w'], p['gn_b'], p['head_ind'], p['head_ind_t'], p['w_out'], p['g_post_mix'], p['g_pre_ffn'],
              p['w_router'], p['b_router']]
    in_specs = ([tok] * 6
                + [pl.BlockSpec((1, tm, GATE_W), lambda bi, i, c=col // GATE_W + k: (bi, i, c))
                   for col in (COL_G0, COL_G1) for k in range(d // GATE_W)]
                + [pl.BlockSpec((1, 1, d), mod_map)] * 3
                + [full(a) for a in consts] + [full(cnt0)])
    return pl.pallas_call(
        _merge_kernel,
        grid=(b, s // tm),
        in_specs=in_specs,
        out_specs=[tok, tok] + [pl.BlockSpec((1, tm, LANES), lambda bi, i: (bi, i, 0))] * 3
                  + [pl.BlockSpec((1, LANES), lambda bi, i: (0, 0))],
        out_shape=[jax.ShapeDtypeStruct((b, s, d), F32),
                   jax.ShapeDtypeStruct((b, s, d), F32),
                   jax.ShapeDtypeStruct((b, s, LANES), jnp.int32),
                   jax.ShapeDtypeStruct((b, s, LANES), jnp.int32),
                   jax.ShapeDtypeStruct((b, s, LANES), F32),
                   jax.ShapeDtypeStruct((1, LANES), F32)],
        scratch_shapes=[pltpu.VMEM((1, LANES), F32)],
        compiler_params=pltpu.CompilerParams(
            dimension_semantics=("arbitrary", "arbitrary"),
            vmem_limit_bytes=VMEM_LIMIT),
        name="merge_out",
    )(x, oa, y0, y1, bonus, gate, *([proj] * (2 * d // GATE_W)), *mods, *consts, cnt0)


def _route_metadata(eid, rank, counts, n_tiles):
    padded = (counts + MOE_TM - 1) // MOE_TM * MOE_TM
    ends = jnp.cumsum(padded)
    offs = ends - padded
    experts = jnp.arange(N_EXPERTS, dtype=jnp.int32)
    pos = jnp.sum(jnp.where(eid[:, :, None] == experts, (offs + rank)[:, None, :], 0), axis=-1)
    idx = jnp.arange(n_tiles, dtype=jnp.int32)
    valid = idx * MOE_TM < ends[-1]
    tile = jnp.where(valid, idx, ends[-1] // MOE_TM - 1)
    te = jnp.sum((ends[None, :] <= (tile * MOE_TM)[:, None]).astype(jnp.int32), axis=1)
    first = valid & jnp.concatenate([jnp.ones((1,), bool), te[1:] != te[:-1]])
    used = counts > 0
    slot_of = (jnp.cumsum(used.astype(jnp.int32)) - 1) % 2
    later = (experts[None, :] > experts[:, None]) & used[None, :]
    next_of = jnp.min(jnp.where(later, experts[None, :], N_EXPERTS), axis=1)
    next_of = jnp.where(next_of == N_EXPERTS, -1, next_of)
    is_te = te[:, None] == experts[None, :]
    pick = lambda table: jnp.sum(jnp.where(is_te, table[None, :], 0), axis=1)
    rows_here = pick(offs + counts) - tile * MOE_TM
    state = jnp.where(valid, jnp.where(rows_here > MOE_TM // 2, 2, 1), 0)
    sched = (tile, te, state, first.astype(jnp.int32), pick(slot_of), pick(next_of))
    fill = jnp.concatenate([jnp.where(padded > counts, ends - MOE_TM, -1),
                            jnp.where(valid, -1, idx * MOE_TM)[eid.size // MOE_TM:]])
    return pos, tuple(a.astype(jnp.int32) for a in sched), fill.astype(jnp.int32)


def _dispatch_kernel(fill_ref, pos_ref, hp_hbm, hs_hbm, xs_hbm, zbuf, hbuf, sem, lsem, fsem,
                     *, n, n_p_tiles, n_fill):
    i = pl.program_id(0)
    slot = i % 3
    par = i % 2

    def load(tile, slot_):
        @pl.when(tile < n_p_tiles)
        def _():
            start = pl.multiple_of(tile * DISP_TB, DISP_TB)
            pltpu.make_async_copy(hp_hbm.at[pl.ds(start, DISP_TB)], hbuf.at[slot_], lsem.at[slot_]).start()

        @pl.when(tile >= n_p_tiles)
        def _():
            start = pl.multiple_of((tile - n_p_tiles) * DISP_TB, DISP_TB)
            pltpu.make_async_copy(hs_hbm.at[pl.ds(start, DISP_TB)], hbuf.at[slot_], lsem.at[slot_]).start()

    def wait_rows(parity):
        for _ in range(TOP_K):
            pltpu.make_async_copy(hbuf.at[0], xs_hbm.at[pl.ds(0, DISP_TB)], sem.at[parity]).wait()

    def fill_copy(e):
        start = pl.multiple_of(fill_ref[e], MOE_TM)
        return pltpu.make_async_copy(zbuf, xs_hbm.at[pl.ds(start, MOE_TM)], fsem)

    @pl.when(i == 0)
    def _():
        zbuf[...] = jnp.zeros_like(zbuf)
        for e in range(n_fill):
            @pl.when(fill_ref[e] >= 0)
            def _(e=e):
                fill_copy(e).start()
        for e in range(n_fill):
            @pl.when(fill_ref[e] >= 0)
            def _(e=e):
                fill_copy(e).wait()
        load(i, slot)

    @pl.when(i + 1 < n)
    def _():
        load(i + 1, (i + 1) % 3)

    pltpu.make_async_copy(hp_hbm.at[pl.ds(0, DISP_TB)], hbuf.at[slot], lsem.at[slot]).wait()
    for t in range(DISP_TB):
        for j in range(TOP_K):
            pltpu.make_async_copy(hbuf.at[slot, pl.ds(t, 1)],
                                  xs_hbm.at[pl.ds(pos_ref[0, 0, j * DISP_TB + t], 1)], sem.at[par]
                                  ).start(priority=j % 2)

    @pl.when(i > 0)
    def _():
        wait_rows(1 - par)

    @pl.when(i == n - 1)
    def _():
        wait_rows(par)


def _dispatch(pos, fill, h2p, h2s, n_tiles):
    d = D_MODEL
    n_p, n_s = h2p.shape[0], h2s.shape[0]
    n = (n_p + n_s) // DISP_TB
    n_pt = n_p // DISP_TB
    pos_t = pos.reshape(n, DISP_TB, TOP_K).transpose(0, 2, 1).reshape(n, 1, TOP_K * DISP_TB)
    grid_spec = pltpu.PrefetchScalarGridSpec(
        num_scalar_prefetch=1,
        grid=(n,),
        in_specs=[pl.BlockSpec((1, 1, TOP_K * DISP_TB), lambda i, fill: (i, 0, 0), memory_space=pltpu.SMEM),
                  pl.BlockSpec(memory_space=pl.ANY),
                  pl.BlockSpec(memory_space=pl.ANY)],
        out_specs=pl.BlockSpec(memory_space=pl.ANY),
        scratch_shapes=[pltpu.VMEM((MOE_TM, d), F32),
                        pltpu.VMEM((3, DISP_TB, d), F32),
                        pltpu.SemaphoreType.DMA((2,)),
                        pltpu.SemaphoreType.DMA((3,)),
                        pltpu.SemaphoreType.DMA(())])
    return pl.pallas_call(
        functools.partial(_dispatch_kernel, n=n, n_p_tiles=n_pt, n_fill=fill.shape[0]),
        grid_spec=grid_spec,
        out_shape=jax.ShapeDtypeStruct((n_tiles * MOE_TM, d), F32),
        compiler_params=pltpu.CompilerParams(
            dimension_semantics=("arbitrary",), vmem_limit_bytes=VMEM_LIMIT),
        name="moe_dispatch",
    )(fill, pos_t, h2p, h2s)


def _experts_kernel(tile_ref, te_ref, tv_ref, tf_ref, ws_ref, nx_ref, x_ref,
                    wgu_hbm, bgu_ref, wd_hbm, bd_ref, ys_ref, wgu_buf, wd_buf, wgu_scr, wd_scr, sem):
    i = pl.program_id(0)

    def weight_copies(e, slot):
        return (pltpu.make_async_copy(wgu_hbm.at[e], wgu_buf.at[slot], sem.at[0, slot]),
                pltpu.make_async_copy(wd_hbm.at[e], wd_buf.at[slot], sem.at[1, slot]))

    @pl.when(i == 0)
    def _():
        for cp in weight_copies(te_ref[0], 0):
            cp.start()

    @pl.when(tf_ref[i] == 1)
    def _():
        slot = ws_ref[i]

        @pl.when(nx_ref[i] >= 0)
        def _():
            for cp in weight_copies(nx_ref[i], 1 - slot):
                cp.start()

        for cp in weight_copies(te_ref[i], slot):
            cp.wait()
        wgu_scr[...] = wgu_buf[slot].astype(BF16)
        wd_scr[...] = wd_buf[slot].astype(BF16)

    half = MOE_TM // 2

    def ffn(rows):
        gu = jnp.dot(x_ref[rows, :].astype(BF16), wgu_scr[...], preferred_element_type=F32) + bgu_ref[0]
        gate = jnp.minimum(gu[:, :D_MODEL], SWIGLU_LIMIT)
        up = jnp.clip(gu[:, D_MODEL:], -SWIGLU_LIMIT, SWIGLU_LIMIT)
        act = (up + 1.0) * gate * _sigmoid(SWIGLU_ALPHA * gate)
        ys_ref[rows, :] = jnp.dot(act.astype(BF16), wd_scr[...], preferred_element_type=F32) + bd_ref[0]

    @pl.when(tv_ref[i] >= 1)
    def _():
        ffn(slice(0, half))

    @pl.when(tv_ref[i] == 2)
    def _():
        ffn(slice(half, MOE_TM))

    @pl.when(tv_ref[i] == 0)
    def _():
        ys_ref[0:half, :] = jnp.zeros((half, D_MODEL), F32)

    @pl.when(tv_ref[i] <= 1)
    def _():
        ys_ref[half:, :] = jnp.zeros((half, D_MODEL), F32)


def _experts(xs, sched, p, n_tiles):
    d = D_MODEL
    bias = lambda w: pl.BlockSpec((1, 1, w), lambda i, tile, te, *_: (te[i], 0, 0))
    grid_spec = pltpu.PrefetchScalarGridSpec(
        num_scalar_prefetch=len(sched),
        grid=(n_tiles,),
        in_specs=[pl.BlockSpec((MOE_TM, d), lambda i, tile, *_: (tile[i], 0)),
                  pl.BlockSpec(memory_space=pl.ANY), bias(2 * d),
                  pl.BlockSpec(memory_space=pl.ANY), bias(d)],
        out_specs=pl.BlockSpec((MOE_TM, d), lambda i, *_: (i, 0)),
        scratch_shapes=[pltpu.VMEM((2, d, 2 * d), F32),
                        pltpu.VMEM((2, d, d), F32),
                        pltpu.VMEM((d, 2 * d), BF16),
                        pltpu.VMEM((d, d), BF16),
                        pltpu.SemaphoreType.DMA((2, 2))])
    return pl.pallas_call(
        _experts_kernel,
        grid_spec=grid_spec,
        out_shape=jax.ShapeDtypeStruct((n_tiles * MOE_TM, d), F32),
        compiler_params=pltpu.CompilerParams(
            dimension_semantics=("arbitrary",), vmem_limit_bytes=VMEM_LIMIT),
        name="moe_experts",
    )(*sched, xs, p['w_gate_up'], p['b_gate_up'], p['w_down'], p['b_down'])


def _combine_kernel(pos_ref, posn_ref, ys_hbm, ew_ref, x1_ref, mg_ref, gpost_ref, o_ref, buf, sem, *, n):
    i = pl.program_id(0)
    slot = i % 2

    def row_copy(row, j, t, slot_):
        return pltpu.make_async_copy(ys_hbm.at[pl.ds(row, 1)], buf.at[slot_, j, pl.ds(t, 1)], sem.at[slot_])

    def gather(src_ref, slot_):
        for t in range(COMB_TB):
            for j in range(TOP_K):
                row_copy(src_ref[0, 0, j * COMB_TB + t], j, t, slot_).start(priority=j % 2)

    @pl.when(i == 0)
    def _():
        gather(pos_ref, 0)

    for cur in range(2):
        @pl.when((i + 1 < n) & (slot == cur))
        def _(cur=cur):
            gather(posn_ref, 1 - cur)

    for j in range(TOP_K):
        pltpu.make_async_copy(ys_hbm.at[pl.ds(0, COMB_TB)], buf.at[slot, j], sem.at[slot]).wait()
    ew = ew_ref[...]
    f = sum(ew[:, j:j + 1] * buf[slot, j] for j in range(TOP_K))
    o_ref[...] = x1_ref[...] + mg_ref[0] * _rms(f, gpost_ref[...])


def _combine(ys, pos, ew, x1, mod_gate, p):
    b, s, d = x1.shape
    n_tok = b * s
    n_tiles = n_tok // COMB_TB
    per_batch = mod_gate.shape[0] > 1
    if per_batch:
        mod_gate = jnp.repeat(mod_gate, s // COMB_TB, axis=0)
    mod_map = (lambda i: (i, 0, 0)) if per_batch else (lambda i: (0, 0, 0))
    pos_t = pos.reshape(n_tiles, COMB_TB, TOP_K).transpose(0, 2, 1).reshape(n_tiles, 1, TOP_K * COMB_TB)
    smem_tile = lambda f: pl.BlockSpec((1, 1, TOP_K * COMB_TB), f, memory_space=pltpu.SMEM)
    out = pl.pallas_call(
        functools.partial(_combine_kernel, n=n_tiles),
        grid=(n_tiles,),
        in_specs=[smem_tile(lambda i: (i, 0, 0)),
                  smem_tile(lambda i: (jnp.minimum(i + 1, n_tiles - 1), 0, 0)),
                  pl.BlockSpec(memory_space=pl.ANY),
                  pl.BlockSpec((COMB_TB, LANES), lambda i: (i, 0)),
                  pl.BlockSpec((COMB_TB, d), lambda i: (i, 0)),
                  pl.BlockSpec((1, 1, d), mod_map),
                  pl.BlockSpec((1, d), lambda i: (0, 0))],
        out_specs=pl.BlockSpec((COMB_TB, d), lambda i: (i, 0)),
        out_shape=jax.ShapeDtypeStruct((n_tok, d), F32),
        scratch_shapes=[pltpu.VMEM((2, TOP_K, COMB_TB, d), F32),
                        pltpu.SemaphoreType.DMA((2,))],
        compiler_params=pltpu.CompilerParams(
            dimension_semantics=("arbitrary",), vmem_limit_bytes=VMEM_LIMIT),
        name="moe_combine",
    )(pos_t, pos_t, ys, ew.reshape(n_tok, LANES), x1.reshape(n_tok, d), mod_gate, p['g_post_ffn'])
    return out.reshape(b, s, d)


def _rope_tables(n_tokens):
    rows = n_tokens // GRID_W
    row = jnp.repeat(jnp.arange(rows, dtype=F32), GRID_W)
    col = jnp.tile(jnp.arange(GRID_W, dtype=F32), rows)
    inv = ROPE_THETA ** (-jnp.arange(ROPE_FREQS, dtype=F32) / ROPE_FREQS)
    ang_r = row[:, None] * inv[None, :]
    ang_c = col[:, None] * inv[None, :]
    ang = jnp.concatenate([ang_r, ang_r, ang_c, ang_c] * 2, axis=-1)
    return jnp.cos(ang), jnp.sin(ang)


def _stream(x, mods, proj_w, lam, p, rope, ctx, s0, cnt0, proj_dtype, tm_proj, tm_tok):
    b, s, d = x.shape
    xf = x if mods[0].shape[0] > 1 else x.reshape(1, b * s, d)
    proj = _norm_proj(xf, p['g_pre_mix'], mods[1], mods[0], proj_w, tm_proj, PROJ_TN, proj_dtype)
    proj = proj.reshape(b, s, C_IN)
    oa = _diff_attention(proj, lam, p['g_subln'], rope, ctx, tq=min(s, 4 * ATT_SUB),
                         hps=H_A if s <= ATT_SUB else 1)
    r, v, kkn, kd, ba, lw, bonus, gate = _rwkv_prep(proj, p, min(s, 256))
    y0, y1, sfin = _rwkv_scan(r, v, kkn, kd, ba, lw, s0)
    tb, tsq = xf.shape[0], xf.shape[1]
    x1, h2, route, rank, ew, cnt = _merge_out(
        *(a.reshape(tb, tsq, a.shape[-1]) for a in (x, oa, y0, y1, bonus, gate, proj)),
        (mods[2], mods[4], mods[3]), p, cnt0, tm_tok)
    n = b * s
    route = (route.reshape(n, LANES)[:, :TOP_K], rank.reshape(n, LANES)[:, :N_EXPERTS])
    return (x1.reshape(b, s, d), h2.reshape(n, d), route, ew.reshape(n, LANES), cnt,
            proj, sfin)


def kernel(x_prompt, x_sample, cache_k, cache_v, state_rwkv, c, c_ctx, w_ada, b_ada, g_pre_mix, g_post_mix, g_pre_ffn, g_post_ffn, w_in, mu_prev, mu_next, lam, g_subln, k_k, k_a, r_k, w0, w_up, a0, a_up, g_up, gn_w, gn_b, w_out, w_router, b_router, w_gate_up, b_gate_up, w_down, b_down):
    l = 0
    d = D_MODEL
    bp, sp, _ = x_prompt.shape
    bs, ss, _ = x_sample.shape

    n_cond = 1 + bs
    rows = -(-n_cond // SUBLANES) * SUBLANES
    cond = jnp.concatenate([c_ctx[None, :], c, jnp.zeros((rows - n_cond, d), F32)], axis=0)
    mod = _modulation(cond, w_ada[l], b_ada[l][None, :])
    mods_p = [mod[0:1, i * d:(i + 1) * d].reshape(1, 1, d) for i in range(N_MOD)]
    mods_s = [mod[1:1 + bs, i * d:(i + 1) * d].reshape(bs, 1, d) for i in range(N_MOD)]

    w_proj = w_in[l].astype(BF16)

    lq = lam[l]
    lam_val = (jnp.exp(jnp.sum(lq[0] * lq[1])) - jnp.exp(jnp.sum(lq[2] * lq[3])) + LAM_INIT).reshape(1)

    head = jnp.arange(d) // N_B
    wr = jnp.concatenate([w_router[l], jnp.zeros((d, LANES - N_EXPERTS), F32)], axis=1)
    br = jnp.concatenate([b_router[l], jnp.full((LANES - N_EXPERTS,), -jnp.inf, F32)])[None, :]
    mup, mun = mu_prev[l][None, :], mu_next[l][None, :]
    p = {
        'g_pre_mix': g_pre_mix[l][None, :], 'g_post_mix': g_post_mix[l][None, :],
        'g_pre_ffn': g_pre_ffn[l][None, :], 'g_post_ffn': g_post_ffn[l][None, :],
        'g_subln': g_subln[l][None, :],
        'mu_prev_main': mup[:, :3 * d], 'mu_next_main': mun[:, :3 * d],
        'mu_prev_lora': mup[:, 3 * d:], 'mu_next_lora': mun[:, 3 * d:],
        'k_k': k_k[l][None, :], 'k_a': k_a[l][None, :], 'r_k': r_k[l].reshape(1, d),
        'w0': w0[l].reshape(1, 2 * d),
        'w_up': jnp.concatenate([w_up[l, 0], w_up[l, 1]], axis=1).astype(BF16),
        'a0': a0[l].reshape(1, 2 * d),
        'a_up': jnp.concatenate([a_up[l, 0], a_up[l, 1]], axis=1).astype(BF16),
        'g_up': g_up[l].astype(BF16),
        'head_ind': (head[:, None] == jnp.arange(LANES)[None, :]).astype(BF16),
        'head_ind_t': (jnp.arange(LANES)[:, None] == head[None, :]).astype(BF16),
        'gn_w': gn_w[l][None, :], 'gn_b': gn_b[l][None, :],
        'w_out': w_out[l].astype(BF16),
        'w_router': wr, 'b_router': br,
        'w_gate_up': w_gate_up[l], 'b_gate_up': b_gate_up[l][:, None, :],
        'w_down': w_down[l], 'b_down': b_down[l][:, None, :],
    }

    rope = _rope_tables(ss)
    ctx = (cache_k[:, l].reshape(bs, -1, d), cache_v[:, l].reshape(bs, -1, d))

    cnt0 = jnp.zeros((1, LANES), F32)
    x1p, h2p, route_p, ewp, cnt_p, proj_p, sfin = _stream(
        x_prompt, mods_p, w_proj, lam_val, p, None, None, None, cnt0, F32, 2048, 512)
    x1s, h2s, route_s, ews, cnt_s, _, _ = _stream(
        x_sample, mods_s, w_proj, lam_val, p, rope, ctx, state_rwkv[:, l], cnt_p, BF16, 2048, 512)

    n_p, n_s = bp * sp, bs * ss
    n_tok = n_p + n_s
    eid, rank = (jnp.concatenate([a, b_], axis=0) for a, b_ in zip(route_p, route_s))
    counts = cnt_s[0, :N_EXPERTS].astype(jnp.int32)
    n_tiles = -(-(n_tok * TOP_K + N_EXPERTS * (MOE_TM - 1)) // MOE_TM)
    pos, sched, fill = _route_metadata(eid, rank, counts, n_tiles)
    xs = _dispatch(pos, fill, h2p, h2s, n_tiles)
    rows = _experts(xs, sched, p, n_tiles)
    yp = _combine(rows, pos[:n_p], ewp, x1p, mods_p[5], p)
    ys = _combine(rows, pos[n_p:], ews, x1s, mods_s[5], p)

    new_k = proj_p[:, :, COL_K:COL_K + d].reshape(bp, 1, sp, H_A, 2, HD_A)
    new_v = proj_p[:, :, COL_V:COL_V + d].reshape(bp, 1, sp, H_A, DV_A)
    return (yp, ys, new_k, new_v, sfin[:, None])
```
